```python
import math
import jax, jax.numpy as jnp
from jax import lax
import numpy as np

D_MODEL = 1024
BATCH = 2
SEQ = 16384
DEPTH = 2

CHUNK = 64
Q_BLOCK = 128
A_HEADS = 8
A_HEAD_DIM = 64
A_WIDTH = A_HEADS * A_HEAD_DIM
IDX_HEADS = 8
IDX_DIM = 64
TOPK_MAX = 256
B_HEADS = 8
B_Q_LORA = 384
B_KV_LORA = 256
B_NOPE = 64
B_ROPE = 32
B_QK = B_NOPE + B_ROPE
B_V = 64
B_WIDTH = B_HEADS * B_V
ROPE_BASE = 10000.0
REL_BUCKETS = 32
REL_MAX_DIST = 128
D_FF = 2816
CONV_W = 3
EPS = 1e-6

IN_SIZES = (A_WIDTH, A_WIDTH, A_WIDTH,
            IDX_HEADS * IDX_DIM, IDX_DIM, IDX_HEADS,
            B_Q_LORA, B_KV_LORA, B_ROPE,
            D_MODEL, D_MODEL)
IN_COLS = 3 * A_WIDTH + IDX_HEADS * IDX_DIM + IDX_DIM + IDX_HEADS + B_Q_LORA + B_KV_LORA + B_ROPE + 2 * D_MODEL

kernel_name = "hybrid_dsa_mla_convffn_chunk_causal"


def rms_norm(x, g):
    xf = x.astype(jnp.float32)
    y = xf * lax.rsqrt(jnp.mean(xf * xf, axis=-1, keepdims=True) + EPS)
    return (y * g.astype(jnp.float32)).astype(x.dtype)


def split_cols(z):
    out, start = [], 0
    for n in IN_SIZES:
        out.append(z[..., start:start + n])
        start += n
    return out


def rope(x, pos):
    half = x.shape[-1] // 2
    inv = ROPE_BASE ** (-jnp.arange(half, dtype=jnp.float32) / half)
    ang = pos.astype(jnp.float32)[:, None] * inv[None, :]
    cos = jnp.cos(ang)[:, None, :]
    sin = jnp.sin(ang)[:, None, :]
    xf = x.astype(jnp.float32)
    x1, x2 = xf[..., :half], xf[..., half:]
    return jnp.concatenate([x1 * cos - x2 * sin, x1 * sin + x2 * cos], axis=-1).astype(x.dtype)


def t5_bucket(rel):
    nb = REL_BUCKETS // 2
    max_exact = nb // 2
    side = jnp.where(rel > 0, nb, 0)
    n = jnp.abs(rel)
    nf = jnp.maximum(n, 1).astype(jnp.float32)
    large = max_exact + (jnp.log(nf / max_exact) / math.log(REL_MAX_DIST / max_exact)
                         * (nb - max_exact)).astype(jnp.int32)
    large = jnp.minimum(large, nb - 1)
    return side + jnp.where(n < max_exact, n, large)


def sparse_indexed_attention(q, k, v, q_idx, k_idx, w_idx, rel_bias, topk):
    B, S = q.shape[0], q.shape[1]
    n_blk = S // Q_BLOCK
    key_chunk = jnp.arange(S, dtype=jnp.int32) // CHUNK
    bidx = jnp.arange(B)[:, None, None]
    scale = A_HEAD_DIM ** -0.5
    idx_scale = (IDX_HEADS ** -0.5) * (IDX_DIM ** -0.5)

    def block(i):
        t0 = i * Q_BLOCK
        tq = t0 + jnp.arange(Q_BLOCK, dtype=jnp.int32)
        qi = lax.dynamic_slice_in_dim(q_idx, t0, Q_BLOCK, axis=1)
        wi = lax.dynamic_slice_in_dim(w_idx, t0, Q_BLOCK, axis=1).astype(jnp.float32)
        rel = jax.nn.relu(jnp.einsum('bthd,bsd->bths', qi, k_idx).astype(jnp.float32))
        score = jnp.einsum('bths,bth->bts', rel, wi) * idx_scale
        admissible = key_chunk[None, :] <= (tq // CHUNK)[:, None]
        score = jnp.where(admissible[None], score, -jnp.inf)
        _, sel = lax.top_k(score, topk)
        valid = (sel // CHUNK) <= (tq // CHUNK)[None, :, None]
        k_sel = k[bidx, sel]
        v_sel = v[bidx, sel]
        qb = lax.dynamic_slice_in_dim(q, t0, Q_BLOCK, axis=1)
        logits = jnp.einsum('bthd,btkhd->bthk', qb, k_sel).astype(jnp.float32) * scale
        bias = rel_bias[t5_bucket(sel - tq[None, :, None])]
        logits = logits + jnp.moveaxis(bias, -1, 2).astype(jnp.float32)
        logits = jnp.where(valid[:, :, None, :], logits, -jnp.inf)
        p = jax.nn.softmax(logits, axis=-1).astype(v.dtype)
        return jnp.einsum('bthk,btkhd->bthd', p, v_sel)

    out = lax.map(block, jnp.arange(n_blk))
    return jnp.moveaxis(out, 0, 1).reshape(B, S, A_WIDTH)


def latent_attention(q, k, v):
    B, S = q.shape[0], q.shape[1]
    n_blk = S // Q_BLOCK
    key_chunk = jnp.arange(S, dtype=jnp.int32) // CHUNK
    scale = B_QK ** -0.5

    def block(i):
        t0 = i * Q_BLOCK
        tq = t0 + jnp.arange(Q_BLOCK, dtype=jnp.int32)
        qb = lax.dynamic_slice_in_dim(q, t0, Q_BLOCK, axis=1)
        logits = jnp.einsum('bthd,bshd->bhts', qb, k).astype(jnp.float32) * scale
        mask = key_chunk[None, :] <= (tq // CHUNK)[:, None]
        logits = jnp.where(mask[None, None], logits, -jnp.inf)
        p = jax.nn.softmax(logits, axis=-1).astype(v.dtype)
        return jnp.einsum('bhts,bshd->bthd', p, v)

    out = lax.map(block, jnp.arange(n_blk))
    return jnp.moveaxis(out, 0, 1).reshape(B, S, B_WIDTH)


def causal_dwconv(u, w, b):
    S = u.shape[1]
    up = jnp.pad(u, ((0, 0), (CONV_W - 1, 0), (0, 0)))
    acc = b
    for j in range(CONV_W):
        acc = acc + w[j] * up[:, j:j + S]
    return acc


def setup_inputs(seed: int = 0) -> dict:
    key = jax.random.key(seed)
    ks = jax.random.split(key, 22)
    f32 = jnp.float32
    L = DEPTH

    def lin(k, shape, fan_in):
        return jax.random.normal(k, shape, f32) * fan_in ** -0.5

    def gain(k, shape):
        return 1.0 + 0.02 * jax.random.normal(k, shape, f32)

    return {
        "x": jax.random.normal(ks[0], (BATCH, SEQ, D_MODEL), f32),
        "rel_bias": 0.1 * jax.random.normal(ks[1], (REL_BUCKETS, A_HEADS), f32),
        "norm_mix": gain(ks[2], (L, D_MODEL)),
        "w_in": lin(ks[3], (L, D_MODEL, IN_COLS), D_MODEL),
        "a_q_norm": gain(ks[4], (L, A_HEAD_DIM)),
        "a_k_norm": gain(ks[5], (L, A_HEAD_DIM)),
        "b_cq_norm": gain(ks[6], (L, B_Q_LORA)),
        "b_ckv_norm": gain(ks[7], (L, B_KV_LORA)),
        "b_w_uq": lin(ks[8], (L, B_Q_LORA, B_HEADS * B_QK), B_Q_LORA),
        "b_w_ukv": lin(ks[9], (L, B_KV_LORA, B_HEADS * (B_NOPE + B_V)), B_KV_LORA),
        "b_q_norm": gain(ks[10], (L, B_QK)),
        "b_k_norm": gain(ks[11], (L, B_QK)),
        "w_proj_a": lin(ks[12], (L, A_WIDTH, D_MODEL), A_WIDTH),
        "w_proj_b": lin(ks[13], (L, B_WIDTH, D_MODEL), B_WIDTH),
        "b_gate": 0.01 * jax.random.normal(ks[14], (L, 2 * D_MODEL), f32),
        "w_out": lin(ks[15], (L, D_MODEL, D_MODEL), D_MODEL),
        "norm_ffn": gain(ks[16], (L, D_MODEL)),
        "w_up": lin(ks[17], (L, D_MODEL, 2 * D_FF), D_MODEL),
        "conv_w": lin(ks[18], (L, CONV_W, 2 * D_FF), CONV_W),
        "conv_b": 0.01 * jax.random.normal(ks[19], (L, 2 * D_FF), f32),
        "w_down": lin(ks[20], (L, D_FF, D_MODEL), D_FF),
    }


def reference(x, rel_bias, norm_mix, w_in, a_q_norm, a_k_norm, b_cq_norm, b_ckv_norm,
              b_w_uq, b_w_ukv, b_q_norm, b_k_norm, w_proj_a, w_proj_b, b_gate, w_out,
              norm_ffn, w_up, conv_w, conv_b, w_down):
    B, S, _ = x.shape
    topk = min(TOPK_MAX, S // 4)
    pos = jnp.arange(S, dtype=jnp.int32)
    for l in range(DEPTH):
        h = rms_norm(x, norm_mix[l])
        z = h @ w_in[l]
        qa, ka, va, qi, ki, wi, cq, ckv, kr, ga, gb = split_cols(z)

        qa = rms_norm(qa.reshape(B, S, A_HEADS, A_HEAD_DIM), a_q_norm[l])
        ka = rms_norm(ka.reshape(B, S, A_HEADS, A_HEAD_DIM), a_k_norm[l])
        va = va.reshape(B, S, A_HEADS, A_HEAD_DIM)
        qi = qi.reshape(B, S, IDX_HEADS, IDX_DIM)
        y_a = sparse_indexed_attention(qa, ka, va, qi, ki, wi, rel_bias, topk)

        cq = rms_norm(cq, b_cq_norm[l])
        qb = (cq @ b_w_uq[l]).reshape(B, S, B_HEADS, B_QK)
        qb = jnp.concatenate([qb[..., :B_NOPE], rope(qb[..., B_NOPE:], pos)], axis=-1)
        ckv = rms_norm(ckv, b_ckv_norm[l])
        kv = (ckv @ b_w_ukv[l]).reshape(B, S, B_HEADS, B_NOPE + B_V)
        k_rope = jnp.broadcast_to(rope(kr[:, :, None, :], pos), (B, S, B_HEADS, B_ROPE))
        kb = jnp.concatenate([kv[..., :B_NOPE], k_rope], axis=-1)
        vb = kv[..., B_NOPE:]
        qb = rms_norm(qb, b_q_norm[l])
        kb = rms_norm(kb, b_k_norm[l])
        y_b = latent_attention(qb, kb, vb)

        gate_a = jax.nn.sigmoid(ga + b_gate[l, :D_MODEL])
        gate_b = jax.nn.sigmoid(gb + b_gate[l, D_MODEL:])
        merged = gate_a * (y_a @ w_proj_a[l]) + gate_b * (y_b @ w_proj_b[l])
        x = x + merged @ w_out[l]

        h = rms_norm(x, norm_ffn[l])
        u = causal_dwconv(h @ w_up[l], conv_w[l], conv_b[l])
        val, gat = u[..., :D_FF], u[..., D_FF:]
        x = x + (jax.nn.silu(gat) * val) @ w_down[l]
    return x
```

```python
import functools
import math

import numpy as np
import jax
import jax.numpy as jnp
from jax import lax
from jax.experimental import pallas as pl
from jax.experimental.pallas import tpu as pltpu

F32 = jnp.float32
BF16 = jnp.bfloat16
I32 = jnp.int32

D_MODEL = 1024
CHUNK = 64
CHUNK_SHIFT = 6
assert 1 << CHUNK_SHIFT == CHUNK
A_HEADS = 8
A_HEAD_DIM = 64
A_WIDTH = A_HEADS * A_HEAD_DIM
IDX_HEADS = 8
IDX_DIM = 64
TOPK_MAX = 256
B_HEADS = 8
B_Q_LORA = 384
B_KV_LORA = 256
B_NOPE = 64
B_ROPE = 32
B_QK = B_NOPE + B_ROPE
B_V = 64
B_WIDTH = B_HEADS * B_V
ROPE_BASE = 10000.0
REL_BUCKETS = 32
REL_MAX_DIST = 128
D_FF = 2816
CONV_W = 3
EPS = 1e-6

LANES = 128
SUBLANES = 8
VMEM_LIMIT = 56 * 1024 * 1024

TM_PREP = 256
TQ_SEL = 256
KU_SEL = 128
TQ_ATT = 512
TK_ATT = 512
TM_MERGE = 256
TM_FFN = 512
TF_FFN = 1408

HP = 128
NEG = -1e30
KEY_NEG_INF = -2139095041
INT_MIN = -2147483648


def _cparams(sem):
    return pltpu.CompilerParams(dimension_semantics=sem, vmem_limit_bytes=VMEM_LIMIT)


def _dot(a, b):
    return jnp.dot(a, b, preferred_element_type=F32)


_C_QA, _C_KA, _C_VA, _C_QI = 0, 512, 1024, 1536
_C_KI, _C_WI = 2048, 2176
_C_CQ = 2304
_C_CKV = _C_CQ + B_Q_LORA
_C_KR = _C_CKV + B_KV_LORA
_C_KRS = _C_KR + HP
_C_END = _C_KRS + HP


def _token_prep_kernel(x_ref, g_ref, w1_ref, wqm_ref, wqs_ref, wkm_ref, wv_ref,
                       gqa_ref, gka_ref, gcq_ref, gckv_ref, gqb_ref, gkb_ref,
                       cos_ref, sin_ref, hsum_ref, hexp_ref,
                       qaT_ref, ka_ref, vaT_ref, qiT_ref, ki_ref, wT_ref,
                       qbT_ref, kb_ref, vbT_ref):
    x = x_ref[...]
    h = (x * lax.rsqrt(jnp.mean(x * x, axis=-1, keepdims=True) + EPS) * g_ref[...]).astype(BF16)

    def proj(lo, hi):
        return _dot(h, w1_ref[:, lo:hi])

    def head_norm(z, gain):
        ss = jnp.dot(z * z, hsum_ref[...], preferred_element_type=F32,
                     precision=lax.Precision.HIGHEST)
        r = lax.rsqrt(ss * (1.0 / A_HEAD_DIM) + EPS)
        rf = jnp.dot(r, hexp_ref[...], preferred_element_type=F32,
                     precision=lax.Precision.HIGHEST)
        return z * rf * gain

    zero_half = jnp.zeros((A_HEAD_DIM, x.shape[0]), BF16)

    qa = head_norm(proj(_C_QA, _C_QA + A_WIDTH), gqa_ref[...]) * (A_HEAD_DIM ** -0.5)
    qaT = qa.T
    for hh in range(A_HEADS):
        real = HP * hh + A_HEAD_DIM * (hh % 2)
        pad = HP * hh + A_HEAD_DIM * (1 - hh % 2)
        qaT_ref[real:real + A_HEAD_DIM, :] = qaT[A_HEAD_DIM * hh:A_HEAD_DIM * (hh + 1), :].astype(BF16)
        qaT_ref[pad:pad + A_HEAD_DIM, :] = zero_half

    ka_ref[...] = head_norm(proj(_C_KA, _C_KA + A_WIDTH), gka_ref[...]).astype(BF16)
    vaT_ref[...] = proj(_C_VA, _C_VA + A_WIDTH).T.astype(BF16)

    qiT = proj(_C_QI, _C_QI + IDX_HEADS * IDX_DIM).T
    for hh in range(IDX_HEADS):
        qiT_ref[HP * hh:HP * hh + IDX_DIM, :] = qiT[IDX_DIM * hh:IDX_DIM * (hh + 1), :].astype(BF16)
        qiT_ref[HP * hh + IDX_DIM:HP * (hh + 1), :] = zero_half
    ki_ref[...] = proj(_C_KI, _C_KI + LANES).astype(BF16)
    wi = proj(_C_WI, _C_WI + LANES) * ((IDX_HEADS ** -0.5) * (IDX_DIM ** -0.5))
    wT_ref[...] = wi.T[0:IDX_HEADS, :]

    cos = cos_ref[...]
    sin = sin_ref[...]
    cq = proj(_C_CQ, _C_CQ + B_Q_LORA)
    cqn = (cq * lax.rsqrt(jnp.mean(cq * cq, axis=-1, keepdims=True) + EPS) * gcq_ref[...]).astype(BF16)
    qm = _dot(cqn, wqm_ref[...])
    qs = _dot(cqn, wqs_ref[...])
    gqb = gqb_ref[...] * (B_QK ** -0.5)
    for hh in range(B_HEADS):
        blk = slice(HP * hh, HP * (hh + 1))
        qh = qm[:, blk] * cos + qs[:, blk] * sin
        ss = jnp.sum(qh * qh, axis=-1, keepdims=True) * (1.0 / B_QK)
        qh = qh * lax.rsqrt(ss + EPS) * gqb
        qbT_ref[blk, :] = qh.T.astype(BF16)

    ckv = proj(_C_CKV, _C_CKV + B_KV_LORA)
    ckvn = (ckv * lax.rsqrt(jnp.mean(ckv * ckv, axis=-1, keepdims=True) + EPS) * gckv_ref[...]).astype(BF16)
    km = _dot(ckvn, wkm_ref[...])
    vbT_ref[...] = _dot(ckvn, wv_ref[...]).T.astype(BF16)
    krot = proj(_C_KR, _C_KR + HP) * cos + proj(_C_KRS, _C_KRS + HP) * sin
    gkb = gkb_ref[...]
    for hh in range(B_HEADS):
        blk = slice(HP * hh, HP * (hh + 1))
        kh = km[:, blk] + krot
        ss = jnp.sum(kh * kh, axis=-1, keepdims=True) * (1.0 / B_QK)
        kb_ref[:, blk] = (kh * lax.rsqrt(ss + EPS) * gkb).astype(BF16)


def _token_prep(x, g, w1, wqm, wqs, wkm, wv, gqa, gka, gcq, gckv, gqb, gkb, cos_t, sin_t, hsum, hexp):
    B, S, D = x.shape
    tm = TM_PREP
    nt = S // tm
    row3 = lambda w: pl.BlockSpec((None, tm, w), lambda b, i: (b, i, 0))
    colT = lambda r: pl.BlockSpec((None, r, tm), lambda b, i: (b, 0, i))
    full = lambda a: pl.BlockSpec(a.shape, lambda b, i: (0,) * a.ndim)
    tab = pl.BlockSpec((tm, LANES), lambda b, i: (i, 0))
    consts = (g, w1, wqm, wqs, wkm, wv, gqa, gka, gcq, gckv, gqb, gkb)
    out_shape = (
        jax.ShapeDtypeStruct((B, A_HEADS * HP, S), BF16),
        jax.ShapeDtypeStruct((B, S, A_WIDTH), BF16),
        jax.ShapeDtypeStruct((B, A_WIDTH, S), BF16),
        jax.ShapeDtypeStruct((B, IDX_HEADS * HP, S), BF16),
        jax.ShapeDtypeStruct((B, S, LANES), BF16),
        jax.ShapeDtypeStruct((B, IDX_HEADS, S), F32),
        jax.ShapeDtypeStruct((B, B_HEADS * HP, S), BF16),
        jax.ShapeDtypeStruct((B, S, B_HEADS * HP), BF16),
        jax.ShapeDtypeStruct((B, B_WIDTH, S), BF16),
    )
    out_specs = (colT(A_HEADS * HP), row3(A_WIDTH), colT(A_WIDTH), colT(IDX_HEADS * HP), row3(LANES),
                 colT(IDX_HEADS), colT(B_HEADS * HP), row3(B_HEADS * HP), colT(B_WIDTH))
    return pl.pallas_call(
        _token_prep_kernel,
        grid=(B, nt),
        in_specs=[row3(D)] + [full(a) for a in consts] + [tab, tab, full(hsum), full(hexp)],
        out_specs=out_specs,
        out_shape=out_shape,
        compiler_params=_cparams(("parallel", "parallel")),
        name="token_prep",
    )(x, *consts, cos_t, sin_t, hsum, hexp)


def _select_kernel(qiT_ref, wT_ref, ki_ref, mask_ref, keys_ref, *, topk):
    S = ki_ref.shape[0]
    tq = TQ_SEL
    i = pl.program_id(1)
    n_keys = (i + 1) * tq
    q_chunk = (i * tq + lax.broadcasted_iota(I32, (KU_SEL, tq), 1)) >> CHUNK_SHIFT
    k_off = lax.broadcasted_iota(I32, (KU_SEL, tq), 0)

    def score_unit(u, carry):
        k0 = pl.multiple_of(u * KU_SEL, KU_SEL)
        kblk = ki_ref[pl.ds(k0, KU_SEL), :]
        acc = jnp.zeros((KU_SEL, tq), F32)
        for hh in range(IDX_HEADS):
            d = _dot(kblk, qiT_ref[HP * hh:HP * (hh + 1), :])
            acc = acc + jnp.maximum(d, 0.0) * wT_ref[hh:hh + 1, :]
        acc = jnp.where(acc == 0.0, 0.0, acc)
        adm = ((k0 + k_off) >> CHUNK_SHIFT) <= q_chunk
        acc = jnp.where(adm, acc, -jnp.inf)
        bits = pltpu.bitcast(acc, I32)
        keys_ref[pl.ds(k0, KU_SEL), :] = bits ^ ((bits >> 31) & 0x7FFFFFFF)
        return carry

    lax.fori_loop(0, n_keys // KU_SEL, score_unit, 0)

    def count_ge(c):
        def body(r, acc):
            r0 = pl.multiple_of(r * tq, tq)
            blk = keys_ref[pl.ds(r0, tq), :]
            ind = (blk >= c).astype(I32)
            for s in range(tq // SUBLANES):
                acc = acc + ind[SUBLANES * s:SUBLANES * (s + 1), :]
            return acc
        acc = lax.fori_loop(0, i + 1, body, jnp.zeros((SUBLANES, tq), I32))
        return jnp.sum(acc, axis=0, keepdims=True)

    def bit_step(t, prefix):
        bit = lax.shift_left(jnp.int32(1), 31 - t)
        cand = prefix | bit
        cnt = count_ge(cand ^ INT_MIN)
        return jnp.where(cnt >= topk, cand, prefix)

    prefix = lax.fori_loop(0, 32, bit_step, jnp.zeros((1, tq), I32))
    thr = prefix ^ INT_MIN
    cnt_ge = count_ge(thr)
    cnt_gt = count_ge(jnp.where(thr == 2147483647, thr, thr + 1))
    cnt_gt = jnp.where(thr == 2147483647, 0, cnt_gt)
    n_take = (topk - cnt_gt).astype(F32)
    tied = jnp.logical_and(cnt_ge > topk, thr > KEY_NEG_INF)
    any_tie = jnp.max(tied.astype(I32)) > 0
    thr_adm = jnp.maximum(thr, KEY_NEG_INF + 1)

    @pl.when(jnp.logical_not(any_tie))
    def _():
        def body(r, carry):
            r0 = pl.multiple_of(r * tq, tq)
            blk = keys_ref[pl.ds(r0, tq), :]
            mask_ref[pl.ds(r0, tq), :] = jnp.where(blk >= thr_adm, 0.0, NEG).astype(BF16)
            return carry
        lax.fori_loop(0, i + 1, body, 0)

    @pl.when(any_tie)
    def _():
        ltri = (lax.broadcasted_iota(I32, (KU_SEL, KU_SEL), 0)
                >= lax.broadcasted_iota(I32, (KU_SEL, KU_SEL), 1)).astype(BF16)

        def body(u, seen):
            k0 = pl.multiple_of(u * KU_SEL, KU_SEL)
            blk = keys_ref[pl.ds(k0, KU_SEL), :]
            eq = blk == thr
            eqf = eq.astype(F32)
            rank = _dot(ltri, eqf.astype(BF16)) + seen
            sel = jnp.logical_or(blk > thr, jnp.logical_and(eq, rank <= n_take))
            sel = jnp.logical_and(sel, blk != KEY_NEG_INF)
            mask_ref[pl.ds(k0, KU_SEL), :] = jnp.where(sel, 0.0, NEG).astype(BF16)
            return seen + jnp.sum(eqf, axis=0, keepdims=True)
        lax.fori_loop(0, n_keys // KU_SEL, body, jnp.zeros((1, tq), F32))

    def fill(r, carry):
        r0 = pl.multiple_of(r * tq, tq)
        mask_ref[pl.ds(r0, tq), :] = jnp.full((tq, tq), NEG, BF16)
        return carry
    lax.fori_loop(i + 1, S // tq, fill, 0)


def _select_topk(qiT, wT, ki, topk):
    B, S, _ = ki.shape
    tq = TQ_SEL
    return pl.pallas_call(
        functools.partial(_select_kernel, topk=topk),
        grid=(B, S // tq),
        in_specs=[
            pl.BlockSpec((None, IDX_HEADS * HP, tq), lambda b, i: (b, 0, i)),
            pl.BlockSpec((None, IDX_HEADS, tq), lambda b, i: (b, 0, i)),
            pl.BlockSpec((None, S, LANES), lambda b, i: (b, 0, 0)),
        ],
        out_specs=pl.BlockSpec((None, S, tq), lambda b, i: (b, 0, i)),
        out_shape=jax.ShapeDtypeStruct((B, S, S), BF16),
        scratch_shapes=[pltpu.VMEM((S, tq), I32)],
        compiler_params=_cparams(("parallel", "parallel")),
        name="select_topk",
    )(qiT, wT, ki)


def _attend_kernel(ti_ref, tj_ref, *refs, heads, k_width, k_of_head, with_mask):
    if with_mask:
        qT_ref, k_ref, vT_ref, mask_ref, bias_ref, o_ref, m_ref, l_ref, acc_ref = refs
    else:
        qT_ref, k_ref, vT_ref, diag_ref, o_ref, m_ref, l_ref, acc_ref = refs
    s_idx = pl.program_id(1)
    i = ti_ref[s_idx]
    j = tj_ref[s_idx]
    dv = acc_ref.shape[0] // heads

    @pl.when(j == 0)
    def _():
        m_ref[...] = jnp.full(m_ref.shape, NEG, F32)
        l_ref[...] = jnp.zeros(l_ref.shape, F32)
        acc_ref[...] = jnp.zeros(acc_ref.shape, F32)

    def sweep(extra):
        for hh in range(heads):
            c0 = k_of_head(hh)
            s = _dot(k_ref[:, c0:c0 + k_width], qT_ref[HP * hh:HP * (hh + 1), :])
            e = extra(hh)
            if e is not None:
                s = s + e
            m_old = m_ref[hh:hh + 1, :]
            m_new = jnp.maximum(m_old, jnp.max(s, axis=0, keepdims=True))
            alpha = jnp.exp(m_old - m_new)
            p = jnp.exp(s - m_new)
            l_ref[hh:hh + 1, :] = alpha * l_ref[hh:hh + 1, :] + jnp.sum(p, axis=0, keepdims=True)
            m_ref[hh:hh + 1, :] = m_new
            pv = _dot(vT_ref[dv * hh:dv * (hh + 1), :], p.astype(BF16))
            acc_ref[dv * hh:dv * (hh + 1), :] = alpha * acc_ref[dv * hh:dv * (hh + 1), :] + pv

    if with_mask:
        @pl.when(j < i - 1)
        def _():
            neg = mask_ref[...].astype(F32)
            sweep(lambda hh: neg)

        @pl.when(j >= i - 1)
        def _():
            neg = mask_ref[...].astype(F32)
            sweep(lambda hh: neg + bias_ref[hh].astype(F32))
    else:
        @pl.when(j < i)
        def _():
            sweep(lambda hh: None)

        @pl.when(j == i)
        def _():
            dm = diag_ref[...]
            sweep(lambda hh: dm)

    @pl.when(j == i)
    def _():
        for hh in range(heads):
            inv = 1.0 / l_ref[hh:hh + 1, :]
            acc_ref[dv * hh:dv * (hh + 1), :] = acc_ref[dv * hh:dv * (hh + 1), :] * inv
        o_ref[...] = acc_ref[...].T.astype(o_ref.dtype)


def _pair_tables(S):
    n = S // TQ_ATT
    ti = np.array([i for i in range(n) for _ in range(i + 1)], np.int32)
    tj = np.array([j for i in range(n) for j in range(i + 1)], np.int32)
    return jnp.asarray(ti), jnp.asarray(tj)


def _attend_a(qaT, ka, vaT, mask, bias_tiles):
    B, S, _ = ka.shape
    tq, tk = TQ_ATT, TK_ATT
    ti, tj = _pair_tables(S)
    kern = functools.partial(_attend_kernel, heads=A_HEADS, k_width=HP,
                             k_of_head=lambda hh: HP * (hh // 2), with_mask=True)
    grid_spec = pltpu.PrefetchScalarGridSpec(
        num_scalar_prefetch=2,
        grid=(B, ti.shape[0]),
        in_specs=[
            pl.BlockSpec((None, A_HEADS * HP, tq), lambda b, s, ti, tj: (b, 0, ti[s])),
            pl.BlockSpec((None, tk, A_WIDTH), lambda b, s, ti, tj: (b, tj[s], 0)),
            pl.BlockSpec((None, A_WIDTH, tk), lambda b, s, ti, tj: (b, 0, tj[s])),
            pl.BlockSpec((None, tk, tq), lambda b, s, ti, tj: (b, tj[s], ti[s])),
            pl.BlockSpec((None, A_HEADS, tk, tq),
                         lambda b, s, ti, tj: (jnp.where(tj[s] == ti[s], 0, 1), 0, 0, 0)),
        ],
        out_specs=pl.BlockSpec((None, tq, A_WIDTH), lambda b, s, ti, tj: (b, ti[s], 0)),
        scratch_shapes=[pltpu.VMEM((A_HEADS, tq), F32), pltpu.VMEM((A_HEADS, tq), F32),
                        pltpu.VMEM((A_WIDTH, tq), F32)],
    )
    return pl.pallas_call(
        kern, grid_spec=grid_spec,
        out_shape=jax.ShapeDtypeStruct((B, S, A_WIDTH), BF16),
        compiler_params=_cparams(("parallel", "arbitrary")),
        name="attend_a",
    )(ti, tj, qaT, ka, vaT, mask, bias_tiles)


def _attend_b(qbT, kb, vbT, diag_tile):
    B, S, _ = kb.shape
    tq, tk = TQ_ATT, TK_ATT
    ti, tj = _pair_tables(S)
    kern = functools.partial(_attend_kernel, heads=B_HEADS, k_width=HP,
                             k_of_head=lambda hh: HP * hh, with_mask=False)
    grid_spec = pltpu.PrefetchScalarGridSpec(
        num_scalar_prefetch=2,
        grid=(B, ti.shape[0]),
        in_specs=[
            pl.BlockSpec((None, B_HEADS * HP, tq), lambda b, s, ti, tj: (b, 0, ti[s])),
            pl.BlockSpec((None, tk, B_HEADS * HP), lambda b, s, ti, tj: (b, tj[s], 0)),
            pl.BlockSpec((None, B_WIDTH, tk), lambda b, s, ti, tj: (b, 0, tj[s])),
            pl.BlockSpec((tk, tq), lambda b, s, ti, tj: (0, 0)),
        ],
        out_specs=pl.BlockSpec((None, tq, B_WIDTH), lambda b, s, ti, tj: (b, ti[s], 0)),
        scratch_shapes=[pltpu.VMEM((B_HEADS, tq), F32), pltpu.VMEM((B_HEADS, tq), F32),
                        pltpu.VMEM((B_WIDTH, tq), F32)],
    )
    return pl.pallas_call(
        kern, grid_spec=grid_spec,
        out_shape=jax.ShapeDtypeStruct((B, S, B_WIDTH), BF16),
        compiler_params=_cparams(("parallel", "arbitrary")),
        name="attend_b",
    )(ti, tj, qbT, kb, vbT, diag_tile)


def _merge_kernel(x_ref, ya_ref, yb_ref, g_ref, wga_ref, wgb_ref, bga_ref, bgb_ref,
                  wpa_ref, wpb_ref, wo_ref, o_ref):
    x = x_ref[...]
    h = (x * lax.rsqrt(jnp.mean(x * x, axis=-1, keepdims=True) + EPS) * g_ref[...]).astype(BF16)
    gate_a = jax.nn.sigmoid(_dot(h, wga_ref[...]) + bga_ref[...])
    gate_b = jax.nn.sigmoid(_dot(h, wgb_ref[...]) + bgb_ref[...])
    merged = gate_a * _dot(ya_ref[...], wpa_ref[...]) + gate_b * _dot(yb_ref[...], wpb_ref[...])
    o_ref[...] = x + _dot(merged.astype(BF16), wo_ref[...])


def _merge_out(x2, ya2, yb2, g, wga, wgb, bga, bgb, wpa, wpb, wo):
    R, D = x2.shape
    tm = TM_MERGE
    row = lambda w: pl.BlockSpec((tm, w), lambda i: (i, 0))
    full = lambda a: pl.BlockSpec(a.shape, lambda i: (0,) * a.ndim)
    consts = (g, wga, wgb, bga, bgb, wpa, wpb, wo)
    return pl.pallas_call(
        _merge_kernel,
        grid=(R // tm,),
        in_specs=[row(D), row(A_WIDTH), row(B_WIDTH)] + [full(a) for a in consts],
        out_specs=row(D),
        out_shape=jax.ShapeDtypeStruct((R, D), F32),
        compiler_params=_cparams(("parallel",)),
        name="merge_out",
    )(x2, ya2, yb2, *consts)


def _ffn_kernel(x_ref, xp_ref, g_ref, wuv_ref, wug_ref, cwv_ref, cwg_ref, cbv_ref, cbg_ref, wd_ref,
                o_ref, uv_ref, ug_ref, acc_ref, *, tiles_per_seq):
    i = pl.program_id(0)
    f = pl.program_id(1)
    tm = x_ref.shape[0]
    halo = SUBLANES
    g = g_ref[...]

    def normed(v):
        return (v * lax.rsqrt(jnp.mean(v * v, axis=-1, keepdims=True) + EPS) * g).astype(BF16)

    h = normed(x_ref[...])
    keep = jnp.where(i % tiles_per_seq == 0, 0.0, 1.0)
    hp = normed(xp_ref[...])

    def conv(u_ref, w_ref, cw_ref, cb_ref):
        u_ref[0:halo, :] = _dot(hp, w_ref[...]) * keep
        u_ref[halo:halo + tm, :] = _dot(h, w_ref[...])
        out = cb_ref[...]
        for t in range(CONV_W):
            lo = halo - (CONV_W - 1) + t
            out = out + cw_ref[t:t + 1, :] * u_ref[lo:lo + tm, :]
        return out

    val = conv(uv_ref, wuv_ref, cwv_ref, cbv_ref)
    gat = conv(ug_ref, wug_ref, cwg_ref, cbg_ref)
    act = (gat * jax.nn.sigmoid(gat) * val).astype(BF16)
    part = _dot(act, wd_ref[...])

    @pl.when(f == 0)
    def _():
        acc_ref[...] = part

    @pl.when(f > 0)
    def _():
        acc_ref[...] = acc_ref[...] + part

    @pl.when(f == pl.num_programs(1) - 1)
    def _():
        o_ref[...] = x_ref[...] + acc_ref[...]


def _conv_ffn(x2, S, g, wu, cw, cb, wd):
    R, D = x2.shape
    tm, tf = TM_FFN, TF_FFN
    nf = D_FF // tf
    halo_blocks = tm // SUBLANES
    kern = functools.partial(_ffn_kernel, tiles_per_seq=S // tm)
    return pl.pallas_call(
        kern,
        grid=(R // tm, nf),
        in_specs=[
            pl.BlockSpec((tm, D), lambda i, f: (i, 0)),
            pl.BlockSpec((SUBLANES, D), lambda i, f: (jnp.maximum(i * halo_blocks - 1, 0), 0)),
            pl.BlockSpec((1, D), lambda i, f: (0, 0)),
            pl.BlockSpec((D, tf), lambda i, f: (0, f)),
            pl.BlockSpec((D, tf), lambda i, f: (0, nf + f)),
            pl.BlockSpec((CONV_W, tf), lambda i, f: (0, f)),
            pl.BlockSpec((CONV_W, tf), lambda i, f: (0, nf + f)),
            pl.BlockSpec((1, tf), lambda i, f: (0, f)),
            pl.BlockSpec((1, tf), lambda i, f: (0, nf + f)),
            pl.BlockSpec((tf, D), lambda i, f: (f, 0)),
        ],
        out_specs=pl.BlockSpec((tm, D), lambda i, f: (i, 0)),
        out_shape=jax.ShapeDtypeStruct((R, D), F32),
        scratch_shapes=[pltpu.VMEM((tm + SUBLANES, tf), F32), pltpu.VMEM((tm + SUBLANES, tf), F32),
                        pltpu.VMEM((tm, D), F32)],
        compiler_params=_cparams(("parallel", "arbitrary")),
        name="conv_ffn",
    )(x2, x2, g, wu, wu, cw, cw, cb, cb, wd)


def _t5_bucket(rel):
    nb = REL_BUCKETS // 2
    max_exact = nb // 2
    side = jnp.where(rel > 0, nb, 0)
    n = jnp.abs(rel)
    nf = jnp.maximum(n, 1).astype(F32)
    large = max_exact + (jnp.log(nf / max_exact) / math.log(REL_MAX_DIST / max_exact)
                         * (nb - max_exact)).astype(I32)
    large = jnp.minimum(large, nb - 1)
    return side + jnp.where(n < max_exact, n, large)


def _bias_tiles(rel_bias):
    tq, tk = TQ_ATT, TK_ATT
    assert tk >= REL_MAX_DIST
    kk = jnp.arange(tk, dtype=I32)[:, None]
    qq = jnp.arange(tq, dtype=I32)[None, :]
    rel = jnp.stack([kk - qq, kk - qq - tk])
    far = rel_bias[_t5_bucket(jnp.asarray(-REL_MAX_DIST, I32))]
    tiles = rel_bias[_t5_bucket(rel)] - far
    return jnp.moveaxis(tiles, -1, 1).astype(BF16)


def _diag_tile():
    kk = np.arange(TK_ATT)[:, None] // CHUNK
    qq = np.arange(TQ_ATT)[None, :] // CHUNK
    return jnp.asarray(np.where(kk <= qq, 0.0, NEG).astype(np.float32))


def _rope_tables(S):
    half = B_ROPE // 2
    inv = ROPE_BASE ** (-jnp.arange(half, dtype=F32) / half)
    ang = jnp.arange(S, dtype=I32).astype(F32)[:, None] * inv[None, :]
    cos, sin = jnp.cos(ang), jnp.sin(ang)
    ones = jnp.ones((S, B_NOPE), F32)
    zeros_n = jnp.zeros((S, B_NOPE), F32)
    zeros_p = jnp.zeros((S, HP - B_QK), F32)
    cos_t = jnp.concatenate([ones, cos, cos, zeros_p], axis=1)
    sin_t = jnp.concatenate([zeros_n, -sin, sin, zeros_p], axis=1)
    return cos_t, sin_t


def _pad_cols(w, width):
    return jnp.pad(w, ((0, 0), (0, width - w.shape[1])))


def _swap_halves(w):
    half = w.shape[1] // 2
    return jnp.concatenate([w[:, half:], w[:, :half]], axis=1)


def _layer_weights(l, w_in, b_w_uq, b_w_ukv, b_q_norm, b_k_norm, a_q_norm, a_k_norm):
    w = w_in[l]
    o = np.cumsum([0, A_WIDTH, A_WIDTH, A_WIDTH, IDX_HEADS * IDX_DIM, IDX_DIM, IDX_HEADS,
                   B_Q_LORA, B_KV_LORA, B_ROPE, D_MODEL, D_MODEL])
    seg = [w[:, o[t]:o[t + 1]] for t in range(11)]
    zn = jnp.zeros((D_MODEL, B_NOPE), F32)
    kr = seg[8]
    w1 = jnp.concatenate([
        seg[0], seg[1], seg[2], seg[3], _pad_cols(seg[4], LANES), _pad_cols(seg[5], LANES),
        seg[6], seg[7],
        _pad_cols(jnp.concatenate([zn, kr], axis=1), HP),
        _pad_cols(jnp.concatenate([zn, _swap_halves(kr)], axis=1), HP),
    ], axis=1).astype(BF16)
    assert w1.shape[1] == _C_END

    uq = b_w_uq[l].reshape(B_Q_LORA, B_HEADS, B_QK)
    zq = jnp.zeros((B_Q_LORA, B_HEADS, B_NOPE), F32)
    pq = jnp.zeros((B_Q_LORA, B_HEADS, HP - B_QK), F32)
    rope_part = uq[..., B_NOPE:]
    half = B_ROPE // 2
    rope_swapped = jnp.concatenate([rope_part[..., half:], rope_part[..., :half]], axis=-1)
    wqm = jnp.concatenate([uq, pq], axis=-1).reshape(B_Q_LORA, B_HEADS * HP).astype(BF16)
    wqs = jnp.concatenate([zq, rope_swapped, pq], axis=-1).reshape(B_Q_LORA, B_HEADS * HP).astype(BF16)

    ukv = b_w_ukv[l].reshape(B_KV_LORA, B_HEADS, B_NOPE + B_V)
    pk = jnp.zeros((B_KV_LORA, B_HEADS, HP - B_NOPE), F32)
    wkm = jnp.concatenate([ukv[..., :B_NOPE], pk], axis=-1).reshape(B_KV_LORA, B_HEADS * HP).astype(BF16)
    wv = ukv[..., B_NOPE:].reshape(B_KV_LORA, B_WIDTH).astype(BF16)

    gqa = jnp.tile(a_q_norm[l], A_HEADS)[None, :]
    gka = jnp.tile(a_k_norm[l], A_HEADS)[None, :]
    gqb = _pad_cols(b_q_norm[l][None, :], HP)
    gkb = _pad_cols(b_k_norm[l][None, :], HP)
    return w1, wqm, wqs, wkm, wv, gqa, gka, gqb, gkb, seg[9].astype(BF16), seg[10].astype(BF16)


def kernel(x, rel_bias, norm_mix, w_in, a_q_norm, a_k_norm, b_cq_norm, b_ckv_norm, b_w_uq, b_w_ukv,
           b_q_norm, b_k_norm, w_proj_a, w_proj_b, b_gate, w_out, norm_ffn, w_up, conv_w, conv_b, w_down):
    B, S, D = x.shape
    assert D == D_MODEL and S % TQ_ATT == 0 and S % TM_FFN == 0 and S >= 4 * TQ_SEL
    topk = min(TOPK_MAX, S // 4)
    depth = w_in.shape[0]

    cos_t, sin_t = _rope_tables(S)
    bias_tiles = _bias_tiles(rel_bias)
    diag_tile = _diag_tile()
    head_of_lane = np.arange(A_WIDTH) // A_HEAD_DIM
    hsum = jnp.asarray((head_of_lane[:, None] == np.arange(LANES)[None, :]).astype(np.float32))
    hexp = jnp.asarray((np.arange(LANES)[:, None] == head_of_lane[None, :]).astype(np.float32))

    for l in range(depth):
        (w1, wqm, wqs, wkm, wv, gqa, gka, gqb, gkb, wga, wgb) = _layer_weights(
            l, w_in, b_w_uq, b_w_ukv, b_q_norm, b_k_norm, a_q_norm, a_k_norm)
        qaT, ka, vaT, qiT, ki, wT, qbT, kb, vbT = _token_prep(
            x, norm_mix[l][None, :], w1, wqm, wqs, wkm, wv, gqa, gka,
            b_cq_norm[l][None, :], b_ckv_norm[l][None, :], gqb, gkb, cos_t, sin_t, hsum, hexp)
        mask = _select_topk(qiT, wT, ki, topk)
        y_a = _attend_a(qaT, ka, vaT, mask, bias_tiles)
        y_b = _attend_b(qbT, kb, vbT, diag_tile)
        x2 = _merge_out(
            x.reshape(B * S, D), y_a.reshape(B * S, A_WIDTH), y_b.reshape(B * S, B_WIDTH),
            norm_mix[l][None, :], wga, wgb, b_gate[l][None, :D_MODEL], b_gate[l][None, D_MODEL:],
            w_proj_a[l].astype(BF16), w_proj_b[l].astype(BF16), w_out[l].astype(BF16))
        x2 = _conv_ffn(x2, S, norm_ffn[l][None, :], w_up[l].astype(BF16), conv_w[l], conv_b[l][None, :],
                       w_down[l].astype(BF16))
        x = x2.reshape(B, S, D)
    return x
```

```python
import functools
import math

import numpy as np
import jax
import jax.numpy as jnp
from jax import lax
from jax.experimental import pallas as pl
from jax.experimental.pallas import tpu as pltpu

F32 = jnp.float32
BF16 = jnp.bfloat16
I32 = jnp.int32

D_MODEL = 1024
CHUNK = 64
CHUNK_SHIFT = 6
assert 1 << CHUNK_SHIFT == CHUNK
A_HEADS = 8
A_HEAD_DIM = 64
A_WIDTH = A_HEADS * A_HEAD_DIM
IDX_HEADS = 8
IDX_DIM = 64
TOPK_MAX = 256
B_HEADS = 8
B_Q_LORA = 384
B_KV_LORA = 256
B_NOPE = 64
B_ROPE = 32
B_QK = B_NOPE + B_ROPE
B_V = 64
B_WIDTH = B_HEADS * B_V
ROPE_BASE = 10000.0
REL_BUCKETS = 32
REL_MAX_DIST = 128
D_FF = 2816
CONV_W = 3
EPS = 1e-6

LANES = 128
SUBLANES = 8
BF16_ROWS = 16
VMEM_LIMIT = 56 * 1024 * 1024

TM_PREP = 256
TQ_SEL = 256
KU_SEL = 128
TQ_ATT = 512
TK_ATT = 512
CH_ATT = 64
TM_MERGE = 256
TM_FFN = 512
TF_FFN = 1408

HP = 128
VP = B_V + BF16_ROWS
NEG = -1e30
KEY_NEG_INF = -2139095041
INT_MIN = -2147483648
LOG2E = 1.4426950408889634
BOUND_SLACK = 1.01
MAX_LOG2_RANGE = 60.0


def _cparams(sem):
    return pltpu.CompilerParams(dimension_semantics=sem, vmem_limit_bytes=VMEM_LIMIT)


def _dot(a, b):
    return jnp.dot(a, b, preferred_element_type=F32)


def _dot_exact(a, b):
    return jnp.dot(a, b, preferred_element_type=F32, precision=lax.Precision.HIGHEST)


_C_QA = 0
_C_KA = _C_QA + A_HEADS * HP
_C_VA = _C_KA + A_HEADS * HP
_C_QI = _C_VA + A_WIDTH
_C_KI = _C_QI + IDX_HEADS * IDX_DIM
_C_WI = _C_KI + LANES
_C_CQ = _C_WI + LANES
_C_CKV = _C_CQ + B_Q_LORA
_C_KR = _C_CKV + B_KV_LORA
_C_KRS = _C_KR + HP
_C_END = _C_KRS + HP


def _token_prep_kernel(x_ref, g_ref, w1_ref, wqm_ref, wqs_ref, wkm_ref, wv_ref,
                       gqa_ref, gka_ref, gcq_ref, gckv_ref, gqb_ref, gkb_ref, bnd_ref,
                       cos_ref, sin_ref, hsum_ref, hexp_ref, mexp_ref, onea_ref,
                       qaT_ref, ka_ref, vaT_ref, qiT_ref, ki_ref, wT_ref,
                       qbT_ref, kb_ref, vbT_ref):
    x = x_ref[...]
    tm = x.shape[0]
    h = (x * lax.rsqrt(jnp.mean(x * x, axis=-1, keepdims=True) + EPS) * g_ref[...]).astype(BF16)

    def proj(lo, hi):
        return _dot(h, w1_ref[:, lo:hi])

    def head_norm(z, gain):
        ss = _dot_exact(z * z, hsum_ref[...])
        r = lax.rsqrt(ss * (1.0 / A_HEAD_DIM) + EPS)
        return z * _dot_exact(r, hexp_ref[...]) * gain

    ones_rows = jnp.where(lax.broadcasted_iota(I32, (BF16_ROWS, tm), 0) == 0, 1.0, 0.0).astype(BF16)

    def store_vT(ref, v):
        vT = v.T
        for hh in range(A_HEADS):
            ref[VP * hh:VP * hh + B_V, :] = vT[B_V * hh:B_V * (hh + 1), :].astype(BF16)
            ref[VP * hh + B_V:VP * (hh + 1), :] = ones_rows

    qa = head_norm(proj(_C_QA, _C_KA), gqa_ref[...]) * (A_HEAD_DIM ** -0.5 * LOG2E)
    qnorm = jnp.sqrt(_dot_exact(qa * qa, hsum_ref[...]))
    m_a = (qnorm * bnd_ref[0:1, :] + bnd_ref[2:3, :]) * BOUND_SLACK
    qaT_ref[...] = (qa - _dot_exact(m_a, mexp_ref[...])).T.astype(BF16)
    ka_ref[...] = (head_norm(proj(_C_KA, _C_VA), gka_ref[...]) + onea_ref[...]).astype(BF16)
    store_vT(vaT_ref, proj(_C_VA, _C_QI))

    zero_half = jnp.zeros((IDX_DIM, tm), BF16)
    qiT = proj(_C_QI, _C_KI).T
    for hh in range(IDX_HEADS):
        qiT_ref[HP * hh:HP * hh + IDX_DIM, :] = qiT[IDX_DIM * hh:IDX_DIM * (hh + 1), :].astype(BF16)
        qiT_ref[HP * hh + IDX_DIM:HP * (hh + 1), :] = zero_half
    ki_ref[...] = proj(_C_KI, _C_WI).astype(BF16)
    wi = proj(_C_WI, _C_CQ) * ((IDX_HEADS ** -0.5) * (IDX_DIM ** -0.5))
    wT_ref[...] = wi.T[0:IDX_HEADS, :]

    cos = cos_ref[...]
    sin = sin_ref[...]
    spare = jnp.where(lax.broadcasted_iota(I32, (1, HP), 1) == B_QK, 1.0, 0.0)
    cq = proj(_C_CQ, _C_CKV)
    cqn = (cq * lax.rsqrt(jnp.mean(cq * cq, axis=-1, keepdims=True) + EPS) * gcq_ref[...]).astype(BF16)
    qm = _dot(cqn, wqm_ref[...])
    qs = _dot(cqn, wqs_ref[...])
    gqb = gqb_ref[...] * (B_QK ** -0.5 * LOG2E)
    kbound = bnd_ref[1:2, :]
    for hh in range(B_HEADS):
        blk = slice(HP * hh, HP * (hh + 1))
        qh = qm[:, blk] * cos + qs[:, blk] * sin
        ss = jnp.sum(qh * qh, axis=-1, keepdims=True) * (1.0 / B_QK)
        qh = qh * lax.rsqrt(ss + EPS) * gqb
        m_b = jnp.sqrt(jnp.sum(qh * qh, axis=-1, keepdims=True)) * kbound * BOUND_SLACK
        qbT_ref[blk, :] = (qh - m_b * spare).T.astype(BF16)

    ckv = proj(_C_CKV, _C_KR)
    ckvn = (ckv * lax.rsqrt(jnp.mean(ckv * ckv, axis=-1, keepdims=True) + EPS) * gckv_ref[...]).astype(BF16)
    km = _dot(ckvn, wkm_ref[...])
    store_vT(vbT_ref, _dot(ckvn, wv_ref[...]))
    krot = proj(_C_KR, _C_KRS) * cos + proj(_C_KRS, _C_END) * sin
    gkb = gkb_ref[...]
    for hh in range(B_HEADS):
        blk = slice(HP * hh, HP * (hh + 1))
        kh = km[:, blk] + krot
        ss = jnp.sum(kh * kh, axis=-1, keepdims=True) * (1.0 / B_QK)
        kb_ref[:, blk] = (kh * lax.rsqrt(ss + EPS) * gkb + spare).astype(BF16)


def _token_prep(x, g, w1, wqm, wqs, wkm, wv, gqa, gka, gcq, gckv, gqb, gkb, bnd,
                cos_t, sin_t, hsum, hexp, mexp, onea):
    B, S, D = x.shape
    tm = TM_PREP
    nt = S // tm
    row3 = lambda w: pl.BlockSpec((None, tm, w), lambda b, i: (b, i, 0))
    colT = lambda r: pl.BlockSpec((None, r, tm), lambda b, i: (b, 0, i))
    full = lambda a: pl.BlockSpec(a.shape, lambda b, i: (0,) * a.ndim)
    tab = pl.BlockSpec((tm, LANES), lambda b, i: (i, 0))
    consts = (g, w1, wqm, wqs, wkm, wv, gqa, gka, gcq, gckv, gqb, gkb, bnd)
    tail = (hsum, hexp, mexp, onea)
    out_shape = (
        jax.ShapeDtypeStruct((B, A_HEADS * HP, S), BF16),
        jax.ShapeDtypeStruct((B, S, A_HEADS * HP), BF16),
        jax.ShapeDtypeStruct((B, A_HEADS * VP, S), BF16),
        jax.ShapeDtypeStruct((B, IDX_HEADS * HP, S), BF16),
        jax.ShapeDtypeStruct((B, S, LANES), BF16),
        jax.ShapeDtypeStruct((B, IDX_HEADS, S), F32),
        jax.ShapeDtypeStruct((B, B_HEADS * HP, S), BF16),
        jax.ShapeDtypeStruct((B, S, B_HEADS * HP), BF16),
        jax.ShapeDtypeStruct((B, B_HEADS * VP, S), BF16),
    )
    out_specs = (colT(A_HEADS * HP), row3(A_HEADS * HP), colT(A_HEADS * VP), colT(IDX_HEADS * HP),
                 row3(LANES), colT(IDX_HEADS), colT(B_HEADS * HP), row3(B_HEADS * HP), colT(B_HEADS * VP))
    return pl.pallas_call(
        _token_prep_kernel,
        grid=(B, nt),
        in_specs=[row3(D)] + [full(a) for a in consts] + [tab, tab] + [full(a) for a in tail],
        out_specs=out_specs,
        out_shape=out_shape,
        compiler_params=_cparams(("parallel", "parallel")),
        name="token_prep",
    )(x, *consts, cos_t, sin_t, *tail)


def _select_kernel(qiT_ref, wT_ref, ki_ref, mask_ref, keys_ref, *, topk):
    S = ki_ref.shape[0]
    tq = TQ_SEL
    i = pl.program_id(1)
    n_keys = (i + 1) * tq
    q_chunk = (i * tq + lax.broadcasted_iota(I32, (KU_SEL, tq), 1)) >> CHUNK_SHIFT
    k_off = lax.broadcasted_iota(I32, (KU_SEL, tq), 0)

    def score_unit(u, carry):
        k0 = pl.multiple_of(u * KU_SEL, KU_SEL)
        kblk = ki_ref[pl.ds(k0, KU_SEL), :]
        acc = jnp.zeros((KU_SEL, tq), F32)
        for hh in range(IDX_HEADS):
            d = _dot(kblk, qiT_ref[HP * hh:HP * (hh + 1), :])
            acc = acc + jnp.maximum(d, 0.0) * wT_ref[hh:hh + 1, :]
        acc = jnp.where(acc == 0.0, 0.0, acc)
        adm = ((k0 + k_off) >> CHUNK_SHIFT) <= q_chunk
        acc = jnp.where(adm, acc, -jnp.inf)
        bits = pltpu.bitcast(acc, I32)
        keys_ref[pl.ds(k0, KU_SEL), :] = bits ^ ((bits >> 31) & 0x7FFFFFFF)
        return carry

    lax.fori_loop(0, n_keys // KU_SEL, score_unit, 0)

    def count_ge(c):
        def body(r, acc):
            r0 = pl.multiple_of(r * tq, tq)
            blk = keys_ref[pl.ds(r0, tq), :]
            ind = (blk >= c).astype(I32)
            for s in range(tq // SUBLANES):
                acc = acc + ind[SUBLANES * s:SUBLANES * (s + 1), :]
            return acc
        acc = lax.fori_loop(0, i + 1, body, jnp.zeros((SUBLANES, tq), I32))
        return jnp.sum(acc, axis=0, keepdims=True)

    def bit_step(t, prefix):
        bit = lax.shift_left(jnp.int32(1), 31 - t)
        cand = prefix | bit
        cnt = count_ge(cand ^ INT_MIN)
        return jnp.where(cnt >= topk, cand, prefix)

    prefix = lax.fori_loop(0, 32, bit_step, jnp.zeros((1, tq), I32))
    thr = prefix ^ INT_MIN
    cnt_ge = count_ge(thr)
    cnt_gt = count_ge(jnp.where(thr == 2147483647, thr, thr + 1))
    cnt_gt = jnp.where(thr == 2147483647, 0, cnt_gt)
    n_take = (topk - cnt_gt).astype(F32)
    tied = jnp.logical_and(cnt_ge > topk, thr > KEY_NEG_INF)
    any_tie = jnp.max(tied.astype(I32)) > 0
    thr_adm = jnp.maximum(thr, KEY_NEG_INF + 1)

    @pl.when(jnp.logical_not(any_tie))
    def _():
        def body(r, carry):
            r0 = pl.multiple_of(r * tq, tq)
            blk = keys_ref[pl.ds(r0, tq), :]
            mask_ref[pl.ds(r0, tq), :] = jnp.where(blk >= thr_adm, 0.0, NEG).astype(BF16)
            return carry
        lax.fori_loop(0, i + 1, body, 0)

    @pl.when(any_tie)
    def _():
        ltri = (lax.broadcasted_iota(I32, (KU_SEL, KU_SEL), 0)
                >= lax.broadcasted_iota(I32, (KU_SEL, KU_SEL), 1)).astype(BF16)

        def body(u, seen):
            k0 = pl.multiple_of(u * KU_SEL, KU_SEL)
            blk = keys_ref[pl.ds(k0, KU_SEL), :]
            eq = blk == thr
            eqf = eq.astype(F32)
            rank = _dot(ltri, eqf.astype(BF16)) + seen
            sel = jnp.logical_or(blk > thr, jnp.logical_and(eq, rank <= n_take))
            sel = jnp.logical_and(sel, blk != KEY_NEG_INF)
            mask_ref[pl.ds(k0, KU_SEL), :] = jnp.where(sel, 0.0, NEG).astype(BF16)
            return seen + jnp.sum(eqf, axis=0, keepdims=True)
        lax.fori_loop(0, n_keys // KU_SEL, body, jnp.zeros((1, tq), F32))

    def fill(r, carry):
        r0 = pl.multiple_of(r * tq, tq)
        mask_ref[pl.ds(r0, tq), :] = jnp.full((tq, tq), NEG, BF16)
        return carry
    lax.fori_loop(i + 1, S // tq, fill, 0)


def _select_topk(qiT, wT, ki, topk):
    B, S, _ = ki.shape
    tq = TQ_SEL
    return pl.pallas_call(
        functools.partial(_select_kernel, topk=topk),
        grid=(B, S // tq),
        in_specs=[
            pl.BlockSpec((None, IDX_HEADS * HP, tq), lambda b, i: (b, 0, i)),
            pl.BlockSpec((None, IDX_HEADS, tq), lambda b, i: (b, 0, i)),
            pl.BlockSpec((None, S, LANES), lambda b, i: (b, 0, 0)),
        ],
        out_specs=pl.BlockSpec((None, S, tq), lambda b, i: (b, 0, i)),
        out_shape=jax.ShapeDtypeStruct((B, S, S), BF16),
        scratch_shapes=[pltpu.VMEM((S, tq), I32)],
        compiler_params=_cparams(("parallel", "parallel")),
        name="select_topk",
    )(qiT, wT, ki)


def _attend_kernel(ti_ref, tj_ref, safe_ref, *refs, heads, with_mask):
    if with_mask:
        qT_ref, k_ref, vT_ref, mask_ref, bias_ref, o_ref, acc_ref, s_ref, p_ref, m_ref, add_ref = refs
    else:
        qT_ref, k_ref, vT_ref, diag_ref, o_ref, acc_ref, s_ref, p_ref, m_ref = refs
    s_idx = pl.program_id(1)
    i = ti_ref[s_idx]
    j = tj_ref[s_idx]
    bounded = safe_ref[0] == 1
    tk = k_ref.shape[0]
    n_ch = tk // CH_ATT

    @pl.when(j == 0)
    def _():
        acc_ref[...] = jnp.zeros(acc_ref.shape, F32)
        m_ref[...] = jnp.full(m_ref.shape, NEG, F32)

    def logits(hh):
        s_ref[...] = _dot(k_ref[:, HP * hh:HP * (hh + 1)], qT_ref[HP * hh:HP * (hh + 1), :])

    def chunks():
        return [slice(CH_ATT * c, CH_ATT * (c + 1)) for c in range(n_ch)]

    def sweep_bounded(extra):
        for hh in range(heads):
            logits(hh)
            for rows in chunks():
                t = s_ref[rows, :]
                e = extra(hh, rows)
                if e is not None:
                    t = t + e
                p_ref[rows, :] = jnp.exp2(t).astype(BF16)
            acc_ref[VP * hh:VP * (hh + 1), :] += _dot(vT_ref[VP * hh:VP * (hh + 1), :], p_ref[...])

    def sweep_running_max(extra):
        for hh in range(heads):
            logits(hh)
            m_blk = jnp.full((SUBLANES, s_ref.shape[1]), NEG, F32)
            for rows in chunks():
                t = s_ref[rows, :]
                e = extra(hh, rows)
                if e is not None:
                    t = t + e
                    s_ref[rows, :] = t
                for r in range(CH_ATT // SUBLANES):
                    m_blk = jnp.maximum(m_blk, t[SUBLANES * r:SUBLANES * (r + 1), :])
            m_old = m_ref[hh:hh + 1, :]
            m_new = jnp.maximum(m_old, jnp.max(m_blk, axis=0, keepdims=True))
            m_ref[hh:hh + 1, :] = m_new
            for rows in chunks():
                p_ref[rows, :] = jnp.exp2(s_ref[rows, :] - m_new).astype(BF16)
            pv = _dot(vT_ref[VP * hh:VP * (hh + 1), :], p_ref[...])
            acc_ref[VP * hh:VP * (hh + 1), :] = jnp.exp2(m_old - m_new) * acc_ref[VP * hh:VP * (hh + 1), :] + pv

    def both(extra):
        @pl.when(bounded)
        def _():
            sweep_bounded(extra)

        @pl.when(jnp.logical_not(bounded))
        def _():
            sweep_running_max(extra)

    if with_mask:
        add_ref[...] = mask_ref[...].astype(F32)

        @pl.when(j < i - 1)
        def _():
            both(lambda hh, rows: add_ref[rows, :])

        @pl.when(j >= i - 1)
        def _():
            both(lambda hh, rows: add_ref[rows, :] + bias_ref[hh, rows, :].astype(F32))
    else:
        @pl.when(j < i)
        def _():
            both(lambda hh, rows: None)

        @pl.when(j == i)
        def _():
            both(lambda hh, rows: diag_ref[rows, :])

    @pl.when(j == i)
    def _():
        dv = VP - BF16_ROWS
        for hh in range(heads):
            inv = 1.0 / acc_ref[VP * hh + dv:VP * hh + dv + 1, :]
            s_ref[dv * hh:dv * (hh + 1), :] = acc_ref[VP * hh:VP * hh + dv, :] * inv
        o_ref[...] = s_ref[0:dv * heads, :].T.astype(o_ref.dtype)


def _pair_tables(S):
    n = S // TQ_ATT
    ti = np.array([i for i in range(n) for _ in range(i + 1)], np.int32)
    tj = np.array([j for i in range(n) for j in range(i + 1)], np.int32)
    return jnp.asarray(ti), jnp.asarray(tj)


def _attend(qT, k, vT, safe, extra_inputs, extra_specs, extra_scratch, with_mask, name):
    B, S, _ = k.shape
    tq, tk = TQ_ATT, TK_ATT
    assert tq == tk and tk >= B_V * A_HEADS
    ti, tj = _pair_tables(S)
    heads = A_HEADS
    kern = functools.partial(_attend_kernel, heads=heads, with_mask=with_mask)
    grid_spec = pltpu.PrefetchScalarGridSpec(
        num_scalar_prefetch=3,
        grid=(B, ti.shape[0]),
        in_specs=[
            pl.BlockSpec((None, heads * HP, tq), lambda b, s, ti, tj, sf: (b, 0, ti[s])),
            pl.BlockSpec((None, tk, heads * HP), lambda b, s, ti, tj, sf: (b, tj[s], 0)),
            pl.BlockSpec((None, heads * VP, tk), lambda b, s, ti, tj, sf: (b, 0, tj[s])),
        ] + extra_specs,
        out_specs=pl.BlockSpec((None, tq, heads * B_V), lambda b, s, ti, tj, sf: (b, ti[s], 0)),
        scratch_shapes=[pltpu.VMEM((heads * VP, tq), F32),
                        pltpu.VMEM((tk, tq), F32),
                        pltpu.VMEM((tk, tq), BF16),
                        pltpu.VMEM((heads, tq), F32)] + extra_scratch,
    )
    return pl.pallas_call(
        kern, grid_spec=grid_spec,
        out_shape=jax.ShapeDtypeStruct((B, S, heads * B_V), BF16),
        compiler_params=_cparams(("parallel", "arbitrary")),
        name=name,
    )(ti, tj, safe, qT, k, vT, *extra_inputs)


def _attend_a(qaT, ka, vaT, mask, bias_tiles, safe):
    tq, tk = TQ_ATT, TK_ATT
    specs = [
        pl.BlockSpec((None, tk, tq), lambda b, s, ti, tj, sf: (b, tj[s], ti[s])),
        pl.BlockSpec((None, A_HEADS, tk, tq),
                     lambda b, s, ti, tj, sf: (jnp.where(tj[s] == ti[s], 0, 1), 0, 0, 0)),
    ]
    return _attend(qaT, ka, vaT, safe, (mask, bias_tiles), specs, [pltpu.VMEM((tk, tq), F32)],
                   True, "attend_a")


def _attend_b(qbT, kb, vbT, diag_tile, safe):
    tq, tk = TQ_ATT, TK_ATT
    specs = [pl.BlockSpec((tk, tq), lambda b, s, ti, tj, sf: (0, 0))]
    return _attend(qbT, kb, vbT, safe, (diag_tile,), specs, [], False, "attend_b")


def _merge_kernel(x_ref, ya_ref, yb_ref, g_ref, wga_ref, wgb_ref, bga_ref, bgb_ref,
                  wpa_ref, wpb_ref, wo_ref, o_ref):
    x = x_ref[...]
    h = (x * lax.rsqrt(jnp.mean(x * x, axis=-1, keepdims=True) + EPS) * g_ref[...]).astype(BF16)
    gate_a = jax.nn.sigmoid(_dot(h, wga_ref[...]) + bga_ref[...])
    gate_b = jax.nn.sigmoid(_dot(h, wgb_ref[...]) + bgb_ref[...])
    merged = gate_a * _dot(ya_ref[...], wpa_ref[...]) + gate_b * _dot(yb_ref[...], wpb_ref[...])
    o_ref[...] = x + _dot(merged.astype(BF16), wo_ref[...])


def _merge_out(x2, ya2, yb2, g, wga, wgb, bga, bgb, wpa, wpb, wo):
    R, D = x2.shape
    tm = TM_MERGE
    row = lambda w: pl.BlockSpec((tm, w), lambda i: (i, 0))
    full = lambda a: pl.BlockSpec(a.shape, lambda i: (0,) * a.ndim)
    consts = (g, wga, wgb, bga, bgb, wpa, wpb, wo)
    return pl.pallas_call(
        _merge_kernel,
        grid=(R // tm,),
        in_specs=[row(D), row(A_WIDTH), row(B_WIDTH)] + [full(a) for a in consts],
        out_specs=row(D),
        out_shape=jax.ShapeDtypeStruct((R, D), F32),
        compiler_params=_cparams(("parallel",)),
        name="merge_out",
    )(x2, ya2, yb2, *consts)


def _ffn_kernel(x_ref, xp_ref, g_ref, wuv_ref, wug_ref, cwv_ref, cwg_ref, cbv_ref, cbg_ref, wd_ref,
                o_ref, uv_ref, ug_ref, acc_ref, *, tiles_per_seq):
    i = pl.program_id(0)
    f = pl.program_id(1)
    tm = x_ref.shape[0]
    halo = SUBLANES
    g = g_ref[...]

    def normed(v):
        return (v * lax.rsqrt(jnp.mean(v * v, axis=-1, keepdims=True) + EPS) * g).astype(BF16)

    h = normed(x_ref[...])
    keep = jnp.where(i % tiles_per_seq == 0, 0.0, 1.0)
    hp = normed(xp_ref[...])

    def conv(u_ref, w_ref, cw_ref, cb_ref):
        u_ref[0:halo, :] = _dot(hp, w_ref[...]) * keep
        u_ref[halo:halo + tm, :] = _dot(h, w_ref[...])
        out = cb_ref[...]
        for t in range(CONV_W):
            lo = halo - (CONV_W - 1) + t
            out = out + cw_ref[t:t + 1, :] * u_ref[lo:lo + tm, :]
        return out

    val = conv(uv_ref, wuv_ref, cwv_ref, cbv_ref)
    gat = conv(ug_ref, wug_ref, cwg_ref, cbg_ref)
    act = (gat * jax.nn.sigmoid(gat) * val).astype(BF16)
    part = _dot(act, wd_ref[...])

    @pl.when(f == 0)
    def _():
        acc_ref[...] = part

    @pl.when(f > 0)
    def _():
        acc_ref[...] = acc_ref[...] + part

    @pl.when(f == pl.num_programs(1) - 1)
    def _():
        o_ref[...] = x_ref[...] + acc_ref[...]


def _conv_ffn(x2, S, g, wu, cw, cb, wd):
    R, D = x2.shape
    tm, tf = TM_FFN, TF_FFN
    nf = D_FF // tf
    halo_blocks = tm // SUBLANES
    kern = functools.partial(_ffn_kernel, tiles_per_seq=S // tm)
    return pl.pallas_call(
        kern,
        grid=(R // tm, nf),
        in_specs=[
            pl.BlockSpec((tm, D), lambda i, f: (i, 0)),
            pl.BlockSpec((SUBLANES, D), lambda i, f: (jnp.maximum(i * halo_blocks - 1, 0), 0)),
            pl.BlockSpec((1, D), lambda i, f: (0, 0)),
            pl.BlockSpec((D, tf), lambda i, f: (0, f)),
            pl.BlockSpec((D, tf), lambda i, f: (0, nf + f)),
            pl.BlockSpec((CONV_W, tf), lambda i, f: (0, f)),
            pl.BlockSpec((CONV_W, tf), lambda i, f: (0, nf + f)),
            pl.BlockSpec((1, tf), lambda i, f: (0, f)),
            pl.BlockSpec((1, tf), lambda i, f: (0, nf + f)),
            pl.BlockSpec((tf, D), lambda i, f: (f, 0)),
        ],
        out_specs=pl.BlockSpec((tm, D), lambda i, f: (i, 0)),
        out_shape=jax.ShapeDtypeStruct((R, D), F32),
        scratch_shapes=[pltpu.VMEM((tm + SUBLANES, tf), F32), pltpu.VMEM((tm + SUBLANES, tf), F32),
                        pltpu.VMEM((tm, D), F32)],
        compiler_params=_cparams(("parallel", "arbitrary")),
        name="conv_ffn",
    )(x2, x2, g, wu, wu, cw, cw, cb, cb, wd)


def _t5_bucket(rel):
    nb = REL_BUCKETS // 2
    max_exact = nb // 2
    side = jnp.where(rel > 0, nb, 0)
    n = jnp.abs(rel)
    nf = jnp.maximum(n, 1).astype(F32)
    large = max_exact + (jnp.log(nf / max_exact) / math.log(REL_MAX_DIST / max_exact)
                         * (nb - max_exact)).astype(I32)
    large = jnp.minimum(large, nb - 1)
    return side + jnp.where(n < max_exact, n, large)


def _bias_tiles(rel_bias):
    tq, tk = TQ_ATT, TK_ATT
    assert tq == tk and tk >= REL_MAX_DIST
    period = 2 * tk + tq
    e = jnp.arange(period, dtype=I32)
    d = jnp.where(e < tq, e, e - period)
    rel = -d - tk
    onehot = (_t5_bucket(rel)[:, None] == jnp.arange(REL_BUCKETS, dtype=I32)[None, :]).astype(F32)
    table = jnp.dot(onehot, rel_bias, precision=lax.Precision.HIGHEST)
    far = rel_bias[_t5_bucket(jnp.asarray(-REL_MAX_DIST, I32))]
    table = ((table - far[None, :]) * LOG2E).T
    skew = jnp.tile(table, (1, 2 * tk))[:, :2 * tk * (period - 1)]
    skew = skew.reshape(A_HEADS, 2 * tk, period - 1)[:, :, :tq]
    tiles = jnp.stack([skew[:, tk:], skew[:, :tk]])
    return tiles.astype(BF16), jnp.max(jnp.abs(table), axis=1)


def _diag_tile():
    kk = np.arange(TK_ATT)[:, None] // CHUNK
    qq = np.arange(TQ_ATT)[None, :] // CHUNK
    return jnp.asarray(np.where(kk <= qq, 0.0, NEG).astype(np.float32))


def _rope_tables(S):
    half = B_ROPE // 2
    inv = ROPE_BASE ** (-jnp.arange(half, dtype=F32) / half)
    ang = jnp.arange(S, dtype=I32).astype(F32)[:, None] * inv[None, :]
    cos, sin = jnp.cos(ang), jnp.sin(ang)
    ones = jnp.ones((S, B_NOPE), F32)
    zeros_n = jnp.zeros((S, B_NOPE), F32)
    zeros_p = jnp.zeros((S, HP - B_QK), F32)
    cos_t = jnp.concatenate([ones, cos, cos, zeros_p], axis=1)
    sin_t = jnp.concatenate([zeros_n, -sin, sin, zeros_p], axis=1)
    return cos_t, sin_t


def _pad_cols(w, width):
    return jnp.pad(w, ((0, 0), (0, width - w.shape[1])))


def _pad_heads(w, heads, width):
    rows = w.shape[0]
    w = w.reshape(rows, heads, -1)
    return jnp.pad(w, ((0, 0), (0, 0), (0, width - w.shape[2]))).reshape(rows, heads * width)


def _swap_halves(w):
    half = w.shape[-1] // 2
    return jnp.concatenate([w[..., half:], w[..., :half]], axis=-1)


def _layer_weights(l, w_in, b_w_uq, b_w_ukv, b_q_norm, b_k_norm, a_q_norm, a_k_norm, bias_max):
    w = w_in[l]
    o = np.cumsum([0, A_WIDTH, A_WIDTH, A_WIDTH, IDX_HEADS * IDX_DIM, IDX_DIM, IDX_HEADS,
                   B_Q_LORA, B_KV_LORA, B_ROPE, D_MODEL, D_MODEL])
    seg = [w[:, o[t]:o[t + 1]] for t in range(11)]
    zn = jnp.zeros((D_MODEL, B_NOPE), F32)
    kr = seg[8]
    w1 = jnp.concatenate([
        _pad_heads(seg[0], A_HEADS, HP), _pad_heads(seg[1], A_HEADS, HP), seg[2], seg[3],
        _pad_cols(seg[4], LANES), _pad_cols(seg[5], LANES), seg[6], seg[7],
        _pad_cols(jnp.concatenate([zn, kr], axis=1), HP),
        _pad_cols(jnp.concatenate([zn, _swap_halves(kr)], axis=1), HP),
    ], axis=1).astype(BF16)
    assert w1.shape[1] == _C_END

    uq = b_w_uq[l].reshape(B_Q_LORA, B_HEADS, B_QK)
    zq = jnp.zeros((B_Q_LORA, B_HEADS, B_NOPE), F32)
    pq = jnp.zeros((B_Q_LORA, B_HEADS, HP - B_QK), F32)
    wqm = jnp.concatenate([uq, pq], axis=-1).reshape(B_Q_LORA, B_HEADS * HP).astype(BF16)
    wqs = jnp.concatenate([zq, _swap_halves(uq[..., B_NOPE:]), pq], axis=-1)
    wqs = wqs.reshape(B_Q_LORA, B_HEADS * HP).astype(BF16)

    ukv = b_w_ukv[l].reshape(B_KV_LORA, B_HEADS, B_NOPE + B_V)
    pk = jnp.zeros((B_KV_LORA, B_HEADS, HP - B_NOPE), F32)
    wkm = jnp.concatenate([ukv[..., :B_NOPE], pk], axis=-1).reshape(B_KV_LORA, B_HEADS * HP).astype(BF16)
    wv = ukv[..., B_NOPE:].reshape(B_KV_LORA, B_WIDTH).astype(BF16)

    gqa = _pad_heads(jnp.tile(a_q_norm[l], A_HEADS)[None, :], A_HEADS, HP)
    gka = _pad_heads(jnp.tile(a_k_norm[l], A_HEADS)[None, :], A_HEADS, HP)
    gqb = _pad_cols(b_q_norm[l][None, :], HP)
    gkb = _pad_cols(b_k_norm[l][None, :], HP)

    ka_bound = math.sqrt(A_HEAD_DIM) * jnp.max(jnp.abs(a_k_norm[l]))
    kb_bound = math.sqrt(B_QK) * jnp.max(jnp.abs(b_k_norm[l]))
    bnd = jnp.zeros((SUBLANES, LANES), F32)
    bnd = bnd.at[0, :].set(ka_bound).at[1, :].set(kb_bound).at[2, :A_HEADS].set(bias_max)
    qa_bound = math.sqrt(A_HEAD_DIM) * jnp.max(jnp.abs(a_q_norm[l])) * (A_HEAD_DIM ** -0.5 * LOG2E)
    qb_bound = math.sqrt(B_QK) * jnp.max(jnp.abs(b_q_norm[l])) * (B_QK ** -0.5 * LOG2E)
    range_a = (qa_bound * ka_bound + jnp.max(bias_max)) * BOUND_SLACK + jnp.max(bias_max)
    range_b = qb_bound * kb_bound * BOUND_SLACK
    safe_a = (range_a <= MAX_LOG2_RANGE).astype(I32).reshape(1)
    safe_b = (range_b <= MAX_LOG2_RANGE).astype(I32).reshape(1)
    return (w1, wqm, wqs, wkm, wv, gqa, gka, gqb, gkb, bnd, safe_a, safe_b,
            seg[9].astype(BF16), seg[10].astype(BF16))


def kernel(x, rel_bias, norm_mix, w_in, a_q_norm, a_k_norm, b_cq_norm, b_ckv_norm, b_w_uq, b_w_ukv,
           b_q_norm, b_k_norm, w_proj_a, w_proj_b, b_gate, w_out, norm_ffn, w_up, conv_w, conv_b, w_down):
    B, S, D = x.shape
    assert D == D_MODEL and S % TQ_ATT == 0 and S % TM_FFN == 0
    topk = min(TOPK_MAX, S // 4)
    assert TQ_SEL >= topk
    depth = w_in.shape[0]

    cos_t, sin_t = _rope_tables(S)
    bias_tiles, bias_max = _bias_tiles(rel_bias)
    diag_tile = _diag_tile()
    lane = np.arange(A_HEADS * HP)
    real = (lane % HP) < A_HEAD_DIM
    col = np.arange(LANES)
    hsum = jnp.asarray(((lane[:, None] // HP == col[None, :]) & real[:, None]).astype(np.float32))
    hexp = jnp.asarray(((col[:, None] == lane[None, :] // HP) & real[None, :]).astype(np.float32))
    spare = (lane % HP) == A_HEAD_DIM
    mexp = jnp.asarray(((col[:, None] == lane[None, :] // HP) & spare[None, :]).astype(np.float32))
    onea = jnp.asarray(spare.astype(np.float32)[None, :])

    for l in range(depth):
        (w1, wqm, wqs, wkm, wv, gqa, gka, gqb, gkb, bnd, safe_a, safe_b, wga, wgb) = _layer_weights(
            l, w_in, b_w_uq, b_w_ukv, b_q_norm, b_k_norm, a_q_norm, a_k_norm, bias_max)
        qaT, ka, vaT, qiT, ki, wT, qbT, kb, vbT = _token_prep(
            x, norm_mix[l][None, :], w1, wqm, wqs, wkm, wv, gqa, gka,
            b_cq_norm[l][None, :], b_ckv_norm[l][None, :], gqb, gkb, bnd,
            cos_t, sin_t, hsum, hexp, mexp, onea)
        mask = _select_topk(qiT, wT, ki, topk)
        y_a = _attend_a(qaT, ka, vaT, mask, bias_tiles, safe_a)
        y_b = _attend_b(qbT, kb, vbT, diag_tile, safe_b)
        x2 = _merge_out(
            x.reshape(B * S, D), y_a.reshape(B * S, A_WIDTH), y_b.reshape(B * S, B_WIDTH),
            norm_mix[l][None, :], wga, wgb, b_gate[l][None, :D_MODEL], b_gate[l][None, D_MODEL:],
            w_proj_a[l].astype(BF16), w_proj_b[l].astype(BF16), w_out[l].astype(BF16))
        x2 = _conv_ffn(x2, S, norm_ffn[l][None, :], w_up[l].astype(BF16), conv_w[l], conv_b[l][None, :],
                       w_down[l].astype(BF16))
        x = x2.reshape(B, S, D)
    return x
```

```python
import functools
import math

import numpy as np
import jax
import jax.numpy as jnp
from jax import lax
from jax.experimental import pallas as pl
from jax.experimental.pallas import tpu as pltpu

F32 = jnp.float32
BF16 = jnp.bfloat16
I32 = jnp.int32
I16 = jnp.int16

D_MODEL = 1024
CHUNK = 64
CHUNK_SHIFT = 6
assert 1 << CHUNK_SHIFT == CHUNK
A_HEADS = 8
A_HEAD_DIM = 64
A_WIDTH = A_HEADS * A_HEAD_DIM
IDX_HEADS = 8
IDX_DIM = 64
TOPK_MAX = 256
B_HEADS = 8
B_Q_LORA = 384
B_KV_LORA = 256
B_NOPE = 64
B_ROPE = 32
B_QK = B_NOPE + B_ROPE
B_V = 64
B_WIDTH = B_HEADS * B_V
ROPE_BASE = 10000.0
REL_BUCKETS = 32
REL_MAX_DIST = 128
D_FF = 2816
CONV_W = 3
EPS = 1e-6

LANES = 128
SUBLANES = 8
BF16_ROWS = 16
VMEM_LIMIT = 56 * 1024 * 1024

TM_PREP = 256
TQ_SEL = 256
KU_SEL = 256
CH_SEL = 64
TQ_ATT = 512
TK_ATT = 512
CH_ATT = 64
TM_MERGE = 256
TM_FFN = 512
TF_FFN = 1408

HP = 128
VP = B_V + BF16_ROWS
NEG = -1e30
KEY_NEG_INF = -2139095041
INT_MIN = -2147483648
LOG2E = 1.4426950408889634
BOUND_SLACK = 1.01
MAX_LOG2_RANGE = 60.0


def _cparams(sem):
    return pltpu.CompilerParams(dimension_semantics=sem, vmem_limit_bytes=VMEM_LIMIT)


def _dot(a, b):
    return jnp.dot(a, b, preferred_element_type=F32)


def _dot_exact(a, b):
    return jnp.dot(a, b, preferred_element_type=F32, precision=lax.Precision.HIGHEST)


_C_QA = 0
_C_KA = _C_QA + A_HEADS * HP
_C_VA = _C_KA + A_HEADS * HP
_C_QI = _C_VA + A_WIDTH
_C_KI = _C_QI + IDX_HEADS * IDX_DIM
_C_WI = _C_KI + LANES
_C_CQ = _C_WI + LANES
_C_CKV = _C_CQ + B_Q_LORA
_C_KR = _C_CKV + B_KV_LORA
_C_KRS = _C_KR + HP
_C_END = _C_KRS + HP


def _token_prep_kernel(x_ref, g_ref, w1_ref, wqm_ref, wqs_ref, wkm_ref, wv_ref,
                       gqa_ref, gka_ref, gcq_ref, gckv_ref, gqb_ref, gkb_ref, bnd_ref,
                       cos_ref, sin_ref,
                       qaT_ref, ka_ref, vaT_ref, qiT_ref, ki_ref, wT_ref,
                       qbT_ref, kb_ref, vbT_ref):
    x = x_ref[...]
    tm = x.shape[0]
    h = (x * lax.rsqrt(jnp.mean(x * x, axis=-1, keepdims=True) + EPS) * g_ref[...]).astype(BF16)

    def proj(lo, hi):
        return _dot(h, w1_ref[:, lo:hi])

    def head_norm(zh, gain):
        ss = jnp.sum(zh * zh, axis=-1, keepdims=True) * (1.0 / A_HEAD_DIM)
        return zh * lax.rsqrt(ss + EPS) * gain

    ones_rows = jnp.where(lax.broadcasted_iota(I32, (BF16_ROWS, tm), 0) == 0, 1.0, 0.0).astype(BF16)

    def store_vT(ref, v):
        vT = v.T
        for hh in range(A_HEADS):
            ref[VP * hh:VP * hh + B_V, :] = vT[B_V * hh:B_V * (hh + 1), :].astype(BF16)
            ref[VP * hh + B_V:VP * (hh + 1), :] = ones_rows

    spare_a = jnp.where(lax.broadcasted_iota(I32, (1, HP), 1) == A_HEAD_DIM, 1.0, 0.0)
    qa = proj(_C_QA, _C_KA)
    ka = proj(_C_KA, _C_VA)
    gqa = gqa_ref[...] * (A_HEAD_DIM ** -0.5 * LOG2E)
    gka = gka_ref[...]
    for hh in range(A_HEADS):
        blk = slice(HP * hh, HP * (hh + 1))
        qh = head_norm(qa[:, blk], gqa)
        qnorm = jnp.sqrt(jnp.sum(qh * qh, axis=-1, keepdims=True))
        m_a = (qnorm * bnd_ref[0:1, :] + bnd_ref[2:3, hh:hh + 1]) * BOUND_SLACK
        qaT_ref[blk, :] = (qh - m_a * spare_a).T.astype(BF16)
        ka_ref[:, blk] = (head_norm(ka[:, blk], gka) + spare_a).astype(BF16)
    store_vT(vaT_ref, proj(_C_VA, _C_QI))

    zero_half = jnp.zeros((IDX_DIM, tm), BF16)
    qiT = proj(_C_QI, _C_KI).T
    for hh in range(IDX_HEADS):
        qiT_ref[HP * hh:HP * hh + IDX_DIM, :] = qiT[IDX_DIM * hh:IDX_DIM * (hh + 1), :].astype(BF16)
        qiT_ref[HP * hh + IDX_DIM:HP * (hh + 1), :] = zero_half
    ki_ref[...] = proj(_C_KI, _C_WI).astype(BF16)
    wi = proj(_C_WI, _C_CQ) * ((IDX_HEADS ** -0.5) * (IDX_DIM ** -0.5))
    wT_ref[...] = wi.T[0:IDX_HEADS, :]

    cos = cos_ref[...]
    sin = sin_ref[...]
    spare = jnp.where(lax.broadcasted_iota(I32, (1, HP), 1) == B_QK, 1.0, 0.0)
    cq = proj(_C_CQ, _C_CKV)
    cqn = (cq * lax.rsqrt(jnp.mean(cq * cq, axis=-1, keepdims=True) + EPS) * gcq_ref[...]).astype(BF16)
    qm = _dot(cqn, wqm_ref[...])
    qs = _dot(cqn, wqs_ref[...])
    gqb = gqb_ref[...] * (B_QK ** -0.5 * LOG2E)
    kbound = bnd_ref[1:2, :]
    for hh in range(B_HEADS):
        blk = slice(HP * hh, HP * (hh + 1))
        qh = qm[:, blk] * cos + qs[:, blk] * sin
        ss = jnp.sum(qh * qh, axis=-1, keepdims=True) * (1.0 / B_QK)
        qh = qh * lax.rsqrt(ss + EPS) * gqb
        m_b = jnp.sqrt(jnp.sum(qh * qh, axis=-1, keepdims=True)) * kbound * BOUND_SLACK
        qbT_ref[blk, :] = (qh - m_b * spare).T.astype(BF16)

    ckv = proj(_C_CKV, _C_KR)
    ckvn = (ckv * lax.rsqrt(jnp.mean(ckv * ckv, axis=-1, keepdims=True) + EPS) * gckv_ref[...]).astype(BF16)
    km = _dot(ckvn, wkm_ref[...])
    store_vT(vbT_ref, _dot(ckvn, wv_ref[...]))
    krot = proj(_C_KR, _C_KRS) * cos + proj(_C_KRS, _C_END) * sin
    gkb = gkb_ref[...]
    for hh in range(B_HEADS):
        blk = slice(HP * hh, HP * (hh + 1))
        kh = km[:, blk] + krot
        ss = jnp.sum(kh * kh, axis=-1, keepdims=True) * (1.0 / B_QK)
        kb_ref[:, blk] = (kh * lax.rsqrt(ss + EPS) * gkb + spare).astype(BF16)


def _token_prep(x, g, w1, wqm, wqs, wkm, wv, gqa, gka, gcq, gckv, gqb, gkb, bnd,
                cos_t, sin_t):
    B, S, D = x.shape
    tm = TM_PREP
    nt = S // tm
    row3 = lambda w: pl.BlockSpec((None, tm, w), lambda b, i: (b, i, 0))
    colT = lambda r: pl.BlockSpec((None, r, tm), lambda b, i: (b, 0, i))
    full = lambda a: pl.BlockSpec(a.shape, lambda b, i: (0,) * a.ndim)
    tab = pl.BlockSpec((tm, LANES), lambda b, i: (i, 0))
    consts = (g, w1, wqm, wqs, wkm, wv, gqa, gka, gcq, gckv, gqb, gkb, bnd)
    out_shape = (
        jax.ShapeDtypeStruct((B, A_HEADS * HP, S), BF16),
        jax.ShapeDtypeStruct((B, S, A_HEADS * HP), BF16),
        jax.ShapeDtypeStruct((B, A_HEADS * VP, S), BF16),
        jax.ShapeDtypeStruct((B, IDX_HEADS * HP, S), BF16),
        jax.ShapeDtypeStruct((B, S, LANES), BF16),
        jax.ShapeDtypeStruct((B, IDX_HEADS, S), F32),
        jax.ShapeDtypeStruct((B, B_HEADS * HP, S), BF16),
        jax.ShapeDtypeStruct((B, S, B_HEADS * HP), BF16),
        jax.ShapeDtypeStruct((B, B_HEADS * VP, S), BF16),
    )
    out_specs = (colT(A_HEADS * HP), row3(A_HEADS * HP), colT(A_HEADS * VP), colT(IDX_HEADS * HP),
                 row3(LANES), colT(IDX_HEADS), colT(B_HEADS * HP), row3(B_HEADS * HP), colT(B_HEADS * VP))
    return pl.pallas_call(
        _token_prep_kernel,
        grid=(B, nt),
        in_specs=[row3(D)] + [full(a) for a in consts] + [tab, tab],
        out_specs=out_specs,
        out_shape=out_shape,
        compiler_params=_cparams(("parallel", "parallel")),
        name="token_prep",
    )(x, *consts, cos_t, sin_t)


def _tree_sum(parts):
    while len(parts) > 1:
        nxt = [parts[t] + parts[t + 1] for t in range(0, len(parts) - 1, 2)]
        if len(parts) % 2:
            nxt.append(parts[-1])
        parts = nxt
    return parts[0]


def _select_kernel(qiT_ref, wT_ref, ki_ref, mask_ref, hi_ref, lo_ref, acc_ref, *, topk):
    S = ki_ref.shape[0]
    tq = TQ_SEL
    i = pl.program_id(1)
    n_keys = (i + 1) * tq
    q_chunk = (i * tq + lax.broadcasted_iota(I32, (CH_SEL, tq), 1)) >> CHUNK_SHIFT
    k_off = lax.broadcasted_iota(I32, (CH_SEL, tq), 0)

    def score_unit(u, carry):
        k0 = pl.multiple_of(u * KU_SEL, KU_SEL)
        kblk = ki_ref[pl.ds(k0, KU_SEL), :]
        for hh in range(IDX_HEADS):
            d = _dot(kblk, qiT_ref[HP * hh:HP * (hh + 1), :])
            w = wT_ref[hh:hh + 1, :]
            for c in range(KU_SEL // CH_SEL):
                rows = slice(CH_SEL * c, CH_SEL * (c + 1))
                t = jnp.maximum(d[rows, :], 0.0) * w
                if hh == 0:
                    acc_ref[rows, :] = t
                elif hh < IDX_HEADS - 1:
                    acc_ref[rows, :] += t
                else:
                    acc = acc_ref[rows, :] + t
                    acc = jnp.where(acc == 0.0, 0.0, acc)
                    adm = ((k0 + CH_SEL * c + k_off) >> CHUNK_SHIFT) <= q_chunk
                    acc = jnp.where(adm, acc, -jnp.inf)
                    bits = pltpu.bitcast(acc, I32)
                    key = bits ^ ((bits >> 31) & 0x7FFFFFFF)
                    r0 = k0 + CH_SEL * c
                    hi_ref[pl.ds(r0, CH_SEL), :] = (key >> 16).astype(I16)
                    lo_ref[pl.ds(r0, CH_SEL), :] = (key ^ 0x8000).astype(I16)
        return carry

    lax.fori_loop(0, n_keys // KU_SEL, score_unit, 0)

    one16 = jnp.ones((tq, tq), I16)
    zero16 = jnp.zeros((tq, tq), I16)
    rows16 = tq // BF16_ROWS

    def count(pred):
        def body(r, acc):
            r0 = pl.multiple_of(r * tq, tq)
            ind = jnp.where(pred(lambda: hi_ref[pl.ds(r0, tq), :], lambda: lo_ref[pl.ds(r0, tq), :]),
                            one16, zero16)
            return acc + _tree_sum([ind[BF16_ROWS * s:BF16_ROWS * (s + 1), :] for s in range(rows16)])
        acc = lax.fori_loop(0, i + 1, body, jnp.zeros((BF16_ROWS, tq), I16))
        return jnp.sum(acc.astype(I32), axis=0, keepdims=True)

    def radix16(pred_of, need, count_at_zero):
        def step(t, carry):
            prefix, cnt_acc = carry
            cand = prefix | lax.shift_left(jnp.int32(1), 15 - t)
            cnt = count(pred_of((cand - 32768).astype(I16)))
            ok = cnt >= need
            return jnp.where(ok, cand, prefix), jnp.where(ok, cnt, cnt_acc)
        prefix, cnt = lax.fori_loop(0, 16, step, (jnp.zeros((1, tq), I32), count_at_zero))
        return prefix - 32768, cnt

    t_hi32, cnt_hi_ge = radix16(lambda c: (lambda hi, lo: hi() >= c), topk,
                                jnp.full((1, tq), n_keys, I32))
    t_hi = t_hi32.astype(I16)
    lowest16 = jnp.full((tq, tq), -32768, I16)

    def above_and_park(r, acc):
        r0 = pl.multiple_of(r * tq, tq)
        blk_hi = hi_ref[pl.ds(r0, tq), :]
        lo_ref[pl.ds(r0, tq), :] = jnp.where(blk_hi == t_hi, lo_ref[pl.ds(r0, tq), :], lowest16)
        ind = jnp.where(blk_hi > t_hi, one16, zero16)
        return acc + _tree_sum([ind[BF16_ROWS * s:BF16_ROWS * (s + 1), :] for s in range(rows16)])

    cnt_above = jnp.sum(lax.fori_loop(0, i + 1, above_and_park, jnp.zeros((BF16_ROWS, tq), I16)).astype(I32),
                        axis=0, keepdims=True)
    t_lo32, cnt_lo_ge = radix16(lambda c: (lambda hi, lo: lo() >= c),
                                topk - cnt_above, cnt_hi_ge - cnt_above)
    t_lo = t_lo32.astype(I16)
    thr = (t_hi32 << 16) + (t_lo32 + 32768)
    cnt_ge = cnt_above + cnt_lo_ge
    tied = jnp.logical_and(cnt_ge > topk, thr > KEY_NEG_INF)
    any_tie = jnp.max(tied.astype(I32)) > 0
    thr_adm = jnp.maximum(thr, KEY_NEG_INF + 1)
    a_hi = (thr_adm >> 16).astype(I16)
    a_lo = ((thr_adm & 0xFFFF) - 32768).astype(I16)
    pass16 = jnp.zeros((tq, tq), BF16)
    drop16 = jnp.full((tq, tq), NEG, BF16)

    @pl.when(jnp.logical_not(any_tie))
    def _():
        def body(r, carry):
            r0 = pl.multiple_of(r * tq, tq)
            hi = hi_ref[pl.ds(r0, tq), :]
            lo = lo_ref[pl.ds(r0, tq), :]
            sel = jnp.logical_or(hi > a_hi, jnp.logical_and(hi == a_hi, lo >= a_lo))
            mask_ref[pl.ds(r0, tq), :] = jnp.where(sel, pass16, drop16)
            return carry
        lax.fori_loop(0, i + 1, body, 0)

    @pl.when(any_tie)
    def _():
        cnt_gt = cnt_above + count(lambda hi, lo: lo() > t_lo)
        n_take = (topk - cnt_gt).astype(F32)
        ltri = (lax.broadcasted_iota(I32, (KU_SEL, KU_SEL), 0)
                >= lax.broadcasted_iota(I32, (KU_SEL, KU_SEL), 1)).astype(BF16)

        def body(u, seen):
            k0 = pl.multiple_of(u * KU_SEL, KU_SEL)
            blk = ((hi_ref[pl.ds(k0, KU_SEL), :].astype(I32) << 16)
                   + (lo_ref[pl.ds(k0, KU_SEL), :].astype(I32) + 32768))
            eq = blk == thr
            eqf = eq.astype(F32)
            rank = _dot(ltri, eqf.astype(BF16)) + seen
            sel = jnp.logical_or(blk > thr, jnp.logical_and(eq, rank <= n_take))
            sel = jnp.logical_and(sel, blk != KEY_NEG_INF)
            mask_ref[pl.ds(k0, KU_SEL), :] = jnp.where(sel, 0.0, NEG).astype(BF16)
            return seen + jnp.sum(eqf, axis=0, keepdims=True)
        lax.fori_loop(0, n_keys // KU_SEL, body, jnp.zeros((1, tq), F32))

    def fill(r, carry):
        r0 = pl.multiple_of(r * tq, tq)
        mask_ref[pl.ds(r0, tq), :] = drop16
        return carry
    lax.fori_loop(i + 1, S // tq, fill, 0)


def _select_topk(qiT, wT, ki, topk):
    B, S, _ = ki.shape
    tq = TQ_SEL
    return pl.pallas_call(
        functools.partial(_select_kernel, topk=topk),
        grid=(B, S // tq),
        in_specs=[
            pl.BlockSpec((None, IDX_HEADS * HP, tq), lambda b, i: (b, 0, i)),
            pl.BlockSpec((None, IDX_HEADS, tq), lambda b, i: (b, 0, i)),
            pl.BlockSpec((None, S, LANES), lambda b, i: (b, 0, 0)),
        ],
        out_specs=pl.BlockSpec((None, S, tq), lambda b, i: (b, 0, i)),
        out_shape=jax.ShapeDtypeStruct((B, S, S), BF16),
        scratch_shapes=[pltpu.VMEM((S, tq), I16), pltpu.VMEM((S, tq), I16),
                        pltpu.VMEM((KU_SEL, tq), F32)],
        compiler_params=_cparams(("parallel", "parallel")),
        name="select_topk",
    )(qiT, wT, ki)


def _attend_kernel(ti_ref, tj_ref, safe_ref, *refs, heads, with_mask):
    if with_mask:
        qT_ref, k_ref, vT_ref, mask_ref, bias_ref, o_ref, acc_ref, s_ref, p_ref, m_ref, add_ref = refs
    else:
        qT_ref, k_ref, vT_ref, diag_ref, o_ref, acc_ref, s_ref, p_ref, m_ref = refs
    s_idx = pl.program_id(1)
    i = ti_ref[s_idx]
    j = tj_ref[s_idx]
    bounded = safe_ref[0] == 1
    tk = k_ref.shape[0]
    n_ch = tk // CH_ATT

    @pl.when(j == 0)
    def _():
        acc_ref[...] = jnp.zeros(acc_ref.shape, F32)
        m_ref[...] = jnp.full(m_ref.shape, NEG, F32)

    def logits(hh):
        s_ref[...] = _dot(k_ref[:, HP * hh:HP * (hh + 1)], qT_ref[HP * hh:HP * (hh + 1), :])

    def chunks():
        return [slice(CH_ATT * c, CH_ATT * (c + 1)) for c in range(n_ch)]

    def sweep_bounded(extra):
        for hh in range(heads):
            logits(hh)
            for rows in chunks():
                t = s_ref[rows, :]
                e = extra(hh, rows)
                if e is not None:
                    t = t + e
                p_ref[rows, :] = jnp.exp2(t).astype(BF16)
            acc_ref[VP * hh:VP * (hh + 1), :] += _dot(vT_ref[VP * hh:VP * (hh + 1), :], p_ref[...])

    def sweep_running_max(extra):
        for hh in range(heads):
            logits(hh)
            m_blk = jnp.full((SUBLANES, s_ref.shape[1]), NEG, F32)
            for rows in chunks():
                t = s_ref[rows, :]
                e = extra(hh, rows)
                if e is not None:
                    t = t + e
                    s_ref[rows, :] = t
                for r in range(CH_ATT // SUBLANES):
                    m_blk = jnp.maximum(m_blk, t[SUBLANES * r:SUBLANES * (r + 1), :])
            m_old = m_ref[hh:hh + 1, :]
            m_new = jnp.maximum(m_old, jnp.max(m_blk, axis=0, keepdims=True))
            m_ref[hh:hh + 1, :] = m_new
            for rows in chunks():
                p_ref[rows, :] = jnp.exp2(s_ref[rows, :] - m_new).astype(BF16)
            pv = _dot(vT_ref[VP * hh:VP * (hh + 1), :], p_ref[...])
            acc_ref[VP * hh:VP * (hh + 1), :] = jnp.exp2(m_old - m_new) * acc_ref[VP * hh:VP * (hh + 1), :] + pv

    def both(extra):
        @pl.when(bounded)
        def _():
            sweep_bounded(extra)

        @pl.when(jnp.logical_not(bounded))
        def _():
            sweep_running_max(extra)

    if with_mask:
        add_ref[...] = mask_ref[...].astype(F32)

        @pl.when(j < i - 1)
        def _():
            both(lambda hh, rows: add_ref[rows, :])

        @pl.when(j >= i - 1)
        def _():
            both(lambda hh, rows: add_ref[rows, :] + bias_ref[hh, rows, :].astype(F32))
    else:
        @pl.when(j < i)
        def _():
            both(lambda hh, rows: None)

        @pl.when(j == i)
        def _():
            both(lambda hh, rows: diag_ref[rows, :])

    @pl.when(j == i)
    def _():
        dv = VP - BF16_ROWS
        for hh in range(heads):
            inv = 1.0 / acc_ref[VP * hh + dv:VP * hh + dv + 1, :]
            s_ref[dv * hh:dv * (hh + 1), :] = acc_ref[VP * hh:VP * hh + dv, :] * inv
        o_ref[...] = s_ref[0:dv * heads, :].T.astype(o_ref.dtype)


def _pair_tables(S):
    n = S // TQ_ATT
    ti = np.array([i for i in range(n) for _ in range(i + 1)], np.int32)
    tj = np.array([j for i in range(n) for j in range(i + 1)], np.int32)
    return jnp.asarray(ti), jnp.asarray(tj)


def _attend(qT, k, vT, safe, extra_inputs, extra_specs, extra_scratch, with_mask, name):
    B, S, _ = k.shape
    tq, tk = TQ_ATT, TK_ATT
    assert tq == tk and tk >= B_V * A_HEADS
    ti, tj = _pair_tables(S)
    heads = A_HEADS
    kern = functools.partial(_attend_kernel, heads=heads, with_mask=with_mask)
    grid_spec = pltpu.PrefetchScalarGridSpec(
        num_scalar_prefetch=3,
        grid=(B, ti.shape[0]),
        in_specs=[
            pl.BlockSpec((None, heads * HP, tq), lambda b, s, ti, tj, sf: (b, 0, ti[s])),
            pl.BlockSpec((None, tk, heads * HP), lambda b, s, ti, tj, sf: (b, tj[s], 0)),
            pl.BlockSpec((None, heads * VP, tk), lambda b, s, ti, tj, sf: (b, 0, tj[s])),
        ] + extra_specs,
        out_specs=pl.BlockSpec((None, tq, heads * B_V), lambda b, s, ti, tj, sf: (b, ti[s], 0)),
        scratch_shapes=[pltpu.VMEM((heads * VP, tq), F32),
                        pltpu.VMEM((tk, tq), F32),
                        pltpu.VMEM((tk, tq), BF16),
                        pltpu.VMEM((heads, tq), F32)] + extra_scratch,
    )
    return pl.pallas_call(
        kern, grid_spec=grid_spec,
        out_shape=jax.ShapeDtypeStruct((B, S, heads * B_V), BF16),
        compiler_params=_cparams(("parallel", "arbitrary")),
        name=name,
    )(ti, tj, safe, qT, k, vT, *extra_inputs)


def _attend_a(qaT, ka, vaT, mask, bias_tiles, safe):
    tq, tk = TQ_ATT, TK_ATT
    specs = [
        pl.BlockSpec((None, tk, tq), lambda b, s, ti, tj, sf: (b, tj[s], ti[s])),
        pl.BlockSpec((A_HEADS, None, tk, tq),
                     lambda b, s, ti, tj, sf: (0, jnp.where(tj[s] == ti[s], 0, 1), 0, 0)),
    ]
    return _attend(qaT, ka, vaT, safe, (mask, bias_tiles), specs, [pltpu.VMEM((tk, tq), F32)],
                   True, "attend_a")


def _attend_b(qbT, kb, vbT, diag_tile, safe):
    tq, tk = TQ_ATT, TK_ATT
    specs = [pl.BlockSpec((tk, tq), lambda b, s, ti, tj, sf: (0, 0))]
    return _attend(qbT, kb, vbT, safe, (diag_tile,), specs, [], False, "attend_b")


def _merge_kernel(x_ref, ya_ref, yb_ref, g_ref, wga_ref, wgb_ref, bga_ref, bgb_ref,
                  wpa_ref, wpb_ref, wo_ref, o_ref):
    x = x_ref[...]
    h = (x * lax.rsqrt(jnp.mean(x * x, axis=-1, keepdims=True) + EPS) * g_ref[...]).astype(BF16)
    gate_a = jax.nn.sigmoid(_dot(h, wga_ref[...]) + bga_ref[...])
    gate_b = jax.nn.sigmoid(_dot(h, wgb_ref[...]) + bgb_ref[...])
    merged = gate_a * _dot(ya_ref[...], wpa_ref[...]) + gate_b * _dot(yb_ref[...], wpb_ref[...])
    o_ref[...] = x + _dot(merged.astype(BF16), wo_ref[...])


def _merge_out(x2, ya2, yb2, g, wga, wgb, bga, bgb, wpa, wpb, wo):
    R, D = x2.shape
    tm = TM_MERGE
    row = lambda w: pl.BlockSpec((tm, w), lambda i: (i, 0))
    full = lambda a: pl.BlockSpec(a.shape, lambda i: (0,) * a.ndim)
    consts = (g, wga, wgb, bga, bgb, wpa, wpb, wo)
    return pl.pallas_call(
        _merge_kernel,
        grid=(R // tm,),
        in_specs=[row(D), row(A_WIDTH), row(B_WIDTH)] + [full(a) for a in consts],
        out_specs=row(D),
        out_shape=jax.ShapeDtypeStruct((R, D), F32),
        compiler_params=_cparams(("parallel",)),
        name="merge_out",
    )(x2, ya2, yb2, *consts)


def _ffn_kernel(x_ref, xp_ref, g_ref, wuv_ref, wug_ref, cwv_ref, cwg_ref, cbv_ref, cbg_ref, wd_ref,
                o_ref, uv_ref, ug_ref, acc_ref, *, tiles_per_seq):
    i = pl.program_id(0)
    f = pl.program_id(1)
    tm = x_ref.shape[0]
    halo = SUBLANES
    g = g_ref[...]

    def normed(v):
        return (v * lax.rsqrt(jnp.mean(v * v, axis=-1, keepdims=True) + EPS) * g).astype(BF16)

    h = normed(x_ref[...])
    keep = jnp.where(i % tiles_per_seq == 0, 0.0, 1.0)
    hp = normed(xp_ref[...])

    def conv(u_ref, w_ref, cw_ref, cb_ref):
        u_ref[0:halo, :] = _dot(hp, w_ref[...]) * keep
        u_ref[halo:halo + tm, :] = _dot(h, w_ref[...])
        out = cb_ref[...]
        for t in range(CONV_W):
            lo = halo - (CONV_W - 1) + t
            out = out + cw_ref[t:t + 1, :] * u_ref[lo:lo + tm, :]
        return out

    val = conv(uv_ref, wuv_ref, cwv_ref, cbv_ref)
    gat = conv(ug_ref, wug_ref, cwg_ref, cbg_ref)
    act = (gat * jax.nn.sigmoid(gat) * val).astype(BF16)
    part = _dot(act, wd_ref[...])

    @pl.when(f == 0)
    def _():
        acc_ref[...] = part

    @pl.when(f > 0)
    def _():
        acc_ref[...] = acc_ref[...] + part

    @pl.when(f == pl.num_programs(1) - 1)
    def _():
        o_ref[...] = x_ref[...] + acc_ref[...]


def _conv_ffn(x2, S, g, wu, cw, cb, wd):
    R, D = x2.shape
    tm, tf = TM_FFN, TF_FFN
    nf = D_FF // tf
    halo_blocks = tm // SUBLANES
    kern = functools.partial(_ffn_kernel, tiles_per_seq=S // tm)
    return pl.pallas_call(
        kern,
        grid=(R // tm, nf),
        in_specs=[
            pl.BlockSpec((tm, D), lambda i, f: (i, 0)),
            pl.BlockSpec((SUBLANES, D), lambda i, f: (jnp.maximum(i * halo_blocks - 1, 0), 0)),
            pl.BlockSpec((1, D), lambda i, f: (0, 0)),
            pl.BlockSpec((D, tf), lambda i, f: (0, f)),
            pl.BlockSpec((D, tf), lambda i, f: (0, nf + f)),
            pl.BlockSpec((CONV_W, tf), lambda i, f: (0, f)),
            pl.BlockSpec((CONV_W, tf), lambda i, f: (0, nf + f)),
            pl.BlockSpec((1, tf), lambda i, f: (0, f)),
            pl.BlockSpec((1, tf), lambda i, f: (0, nf + f)),
            pl.BlockSpec((tf, D), lambda i, f: (f, 0)),
        ],
        out_specs=pl.BlockSpec((tm, D), lambda i, f: (i, 0)),
        out_shape=jax.ShapeDtypeStruct((R, D), F32),
        scratch_shapes=[pltpu.VMEM((tm + SUBLANES, tf), F32), pltpu.VMEM((tm + SUBLANES, tf), F32),
                        pltpu.VMEM((tm, D), F32)],
        compiler_params=_cparams(("parallel", "arbitrary")),
        name="conv_ffn",
    )(x2, x2, g, wu, wu, cw, cw, cb, cb, wd)


def _t5_bucket(rel):
    nb = REL_BUCKETS // 2
    max_exact = nb // 2
    side = jnp.where(rel > 0, nb, 0)
    n = jnp.abs(rel)
    nf = jnp.maximum(n, 1).astype(F32)
    large = max_exact + (jnp.log(nf / max_exact) / math.log(REL_MAX_DIST / max_exact)
                         * (nb - max_exact)).astype(I32)
    large = jnp.minimum(large, nb - 1)
    return side + jnp.where(n < max_exact, n, large)


def _bias_tiles(rel_bias):
    tq, tk = TQ_ATT, TK_ATT
    assert tq == tk and tk >= REL_MAX_DIST
    kk = jnp.arange(tk, dtype=I32)[:, None]
    qq = jnp.arange(tq, dtype=I32)[None, :]
    bucket = _t5_bucket(jnp.stack([kk - qq, kk - qq - tk])).reshape(1, 2 * tk * tq)
    onehot = (jnp.arange(REL_BUCKETS, dtype=I32)[:, None] == bucket).astype(F32)
    far = rel_bias[_t5_bucket(jnp.asarray(-REL_MAX_DIST, I32))]
    table = ((rel_bias - far[None, :]) * LOG2E).T
    tiles = jnp.dot(table, onehot, precision=lax.Precision.HIGHEST)
    return tiles.reshape(A_HEADS, 2, tk, tq).astype(BF16), jnp.max(jnp.abs(table), axis=1)


def _diag_tile():
    kk = np.arange(TK_ATT)[:, None] // CHUNK
    qq = np.arange(TQ_ATT)[None, :] // CHUNK
    return jnp.asarray(np.where(kk <= qq, 0.0, NEG).astype(np.float32))


def _rope_tables(S):
    half = B_ROPE // 2
    inv = ROPE_BASE ** (-jnp.arange(half, dtype=F32) / half)
    ang = jnp.arange(S, dtype=I32).astype(F32)[:, None] * inv[None, :]
    cos, sin = jnp.cos(ang), jnp.sin(ang)
    ones = jnp.ones((S, B_NOPE), F32)
    zeros_n = jnp.zeros((S, B_NOPE), F32)
    zeros_p = jnp.zeros((S, HP - B_QK), F32)
    cos_t = jnp.concatenate([ones, cos, cos, zeros_p], axis=1)
    sin_t = jnp.concatenate([zeros_n, -sin, sin, zeros_p], axis=1)
    return cos_t, sin_t


def _pad_cols(w, width):
    return jnp.pad(w, ((0, 0), (0, width - w.shape[1])))


def _pad_heads(w, heads, width):
    rows = w.shape[0]
    w = w.reshape(rows, heads, -1)
    return jnp.pad(w, ((0, 0), (0, 0), (0, width - w.shape[2]))).reshape(rows, heads * width)


def _swap_halves(w):
    half = w.shape[-1] // 2
    return jnp.concatenate([w[..., half:], w[..., :half]], axis=-1)


def _layer_weights(l, w_in, b_w_uq, b_w_ukv, b_q_norm, b_k_norm, a_q_norm, a_k_norm, bias_max):
    w = w_in[l]
    o = np.cumsum([0, A_WIDTH, A_WIDTH, A_WIDTH, IDX_HEADS * IDX_DIM, IDX_DIM, IDX_HEADS,
                   B_Q_LORA, B_KV_LORA, B_ROPE, D_MODEL, D_MODEL])
    seg = [w[:, o[t]:o[t + 1]] for t in range(11)]
    zn = jnp.zeros((D_MODEL, B_NOPE), F32)
    kr = seg[8]
    w1 = jnp.concatenate([
        _pad_heads(seg[0], A_HEADS, HP), _pad_heads(seg[1], A_HEADS, HP), seg[2], seg[3],
        _pad_cols(seg[4], LANES), _pad_cols(seg[5], LANES), seg[6], seg[7],
        _pad_cols(jnp.concatenate([zn, kr], axis=1), HP),
        _pad_cols(jnp.concatenate([zn, _swap_halves(kr)], axis=1), HP),
    ], axis=1).astype(BF16)
    assert w1.shape[1] == _C_END

    uq = b_w_uq[l].reshape(B_Q_LORA, B_HEADS, B_QK)
    zq = jnp.zeros((B_Q_LORA, B_HEADS, B_NOPE), F32)
    pq = jnp.zeros((B_Q_LORA, B_HEADS, HP - B_QK), F32)
    wqm = jnp.concatenate([uq, pq], axis=-1).reshape(B_Q_LORA, B_HEADS * HP).astype(BF16)
    wqs = jnp.concatenate([zq, _swap_halves(uq[..., B_NOPE:]), pq], axis=-1)
    wqs = wqs.reshape(B_Q_LORA, B_HEADS * HP).astype(BF16)

    ukv = b_w_ukv[l].reshape(B_KV_LORA, B_HEADS, B_NOPE + B_V)
    pk = jnp.zeros((B_KV_LORA, B_HEADS, HP - B_NOPE), F32)
    wkm = jnp.concatenate([ukv[..., :B_NOPE], pk], axis=-1).reshape(B_KV_LORA, B_HEADS * HP).astype(BF16)
    wv = ukv[..., B_NOPE:].reshape(B_KV_LORA, B_WIDTH).astype(BF16)

    gqa = _pad_cols(a_q_norm[l][None, :], HP)
    gka = _pad_cols(a_k_norm[l][None, :], HP)
    gqb = _pad_cols(b_q_norm[l][None, :], HP)
    gkb = _pad_cols(b_k_norm[l][None, :], HP)

    ka_bound = math.sqrt(A_HEAD_DIM) * jnp.max(jnp.abs(a_k_norm[l]))
    kb_bound = math.sqrt(B_QK) * jnp.max(jnp.abs(b_k_norm[l]))
    bnd = jnp.zeros((SUBLANES, LANES), F32)
    bnd = bnd.at[0, :].set(ka_bound).at[1, :].set(kb_bound).at[2, :A_HEADS].set(bias_max)
    qa_bound = math.sqrt(A_HEAD_DIM) * jnp.max(jnp.abs(a_q_norm[l])) * (A_HEAD_DIM ** -0.5 * LOG2E)
    qb_bound = math.sqrt(B_QK) * jnp.max(jnp.abs(b_q_norm[l])) * (B_QK ** -0.5 * LOG2E)
    range_a = (qa_bound * ka_bound + jnp.max(bias_max)) * BOUND_SLACK + jnp.max(bias_max)
    range_b = qb_bound * kb_bound * BOUND_SLACK
    safe_a = (range_a <= MAX_LOG2_RANGE).astype(I32).reshape(1)
    safe_b = (range_b <= MAX_LOG2_RANGE).astype(I32).reshape(1)
    return (w1, wqm, wqs, wkm, wv, gqa, gka, gqb, gkb, bnd, safe_a, safe_b,
            seg[9].astype(BF16), seg[10].astype(BF16))


def kernel(x, rel_bias, norm_mix, w_in, a_q_norm, a_k_norm, b_cq_norm, b_ckv_norm, b_w_uq, b_w_ukv,
           b_q_norm, b_k_norm, w_proj_a, w_proj_b, b_gate, w_out, norm_ffn, w_up, conv_w, conv_b, w_down):
    B, S, D = x.shape
    assert D == D_MODEL and S % TQ_ATT == 0 and S % TM_FFN == 0
    topk = min(TOPK_MAX, S // 4)
    assert TQ_SEL >= topk
    depth = w_in.shape[0]

    cos_t, sin_t = _rope_tables(S)
    bias_tiles, bias_max = _bias_tiles(rel_bias)
    diag_tile = _diag_tile()

    for l in range(depth):
        (w1, wqm, wqs, wkm, wv, gqa, gka, gqb, gkb, bnd, safe_a, safe_b, wga, wgb) = _layer_weights(
            l, w_in, b_w_uq, b_w_ukv, b_q_norm, b_k_norm, a_q_norm, a_k_norm, bias_max)
        qaT, ka, vaT, qiT, ki, wT, qbT, kb, vbT = _token_prep(
            x, norm_mix[l][None, :], w1, wqm, wqs, wkm, wv, gqa, gka,
            b_cq_norm[l][None, :], b_ckv_norm[l][None, :], gqb, gkb, bnd,
            cos_t, sin_t)
        mask = _select_topk(qiT, wT, ki, topk)
        y_a = _attend_a(qaT, ka, vaT, mask, bias_tiles, safe_a)
        y_b = _attend_b(qbT, kb, vbT, diag_tile, safe_b)
        x2 = _merge_out(
            x.reshape(B * S, D), y_a.reshape(B * S, A_WIDTH), y_b.reshape(B * S, B_WIDTH),
            norm_mix[l][None, :], wga, wgb, b_gate[l][None, :D_MODEL], b_gate[l][None, D_MODEL:],
            w_proj_a[l].astype(BF16), w_proj_b[l].astype(BF16), w_out[l].astype(BF16))
        x2 = _conv_ffn(x2, S, norm_ffn[l][None, :], w_up[l].astype(BF16), conv_w[l], conv_b[l][None, :],
                       w_down[l].astype(BF16))
        x = x2.reshape(B, S, D)
    return x
```

```python
import functools
import math

import numpy as np
import jax
import jax.numpy as jnp
from jax import lax
from jax.experimental import pallas as pl
from jax.experimental.pallas import tpu as pltpu

F32 = jnp.float32
BF16 = jnp.bfloat16
I32 = jnp.int32
I16 = jnp.int16

D_MODEL = 1024
CHUNK = 64
CHUNK_SHIFT = 6
assert 1 << CHUNK_SHIFT == CHUNK
A_HEADS = 8
A_HEAD_DIM = 64
A_WIDTH = A_HEADS * A_HEAD_DIM
IDX_HEADS = 8
IDX_DIM = 64
TOPK_MAX = 256
B_HEADS = 8
B_Q_LORA = 384
B_KV_LORA = 256
B_NOPE = 64
B_ROPE = 32
B_QK = B_NOPE + B_ROPE
B_V = 64
B_WIDTH = B_HEADS * B_V
ROPE_BASE = 10000.0
REL_BUCKETS = 32
REL_MAX_DIST = 128
D_FF = 2816
CONV_W = 3
EPS = 1e-6

LANES = 128
SUBLANES = 8
BF16_ROWS = 16
VMEM_LIMIT = 56 * 1024 * 1024

TM_PREP = 256
TQ_SEL = 256
KU_SEL = 256
CH_SEL = 64
TQ_ATT = 512
TK_ATT = 512
CH_ATT = 64
TM_MERGE = 256
TM_FFN = 512
TF_FFN = 1408

HP = 128
VP = B_V + BF16_ROWS
NEG = -1e30
KEY_NEG_INF = -2139095041
INT_MIN = -2147483648
LOG2E = 1.4426950408889634
BOUND_SLACK = 1.01
MAX_LOG2_RANGE = 60.0


def _cparams(sem):
    return pltpu.CompilerParams(dimension_semantics=sem, vmem_limit_bytes=VMEM_LIMIT)


def _dot(a, b):
    return jnp.dot(a, b, preferred_element_type=F32)


def _dot_exact(a, b):
    return jnp.dot(a, b, preferred_element_type=F32, precision=lax.Precision.HIGHEST)


_C_QA = 0
_C_KA = _C_QA + A_HEADS * HP
_C_VA = _C_KA + A_HEADS * HP
_C_QI = _C_VA + A_WIDTH
_C_KI = _C_QI + IDX_HEADS * IDX_DIM
_C_WI = _C_KI + LANES
_C_CQ = _C_WI + LANES
_C_CKV = _C_CQ + B_Q_LORA
_C_KR = _C_CKV + B_KV_LORA
_C_KRS = _C_KR + HP
_C_END = _C_KRS + HP


def _token_prep_kernel(x_ref, g_ref, w1_ref, wqm_ref, wqs_ref, wkm_ref, wv_ref,
                       gqa_ref, gka_ref, gcq_ref, gckv_ref, gqb_ref, gkb_ref, bnd_ref,
                       cos_ref, sin_ref,
                       qaT_ref, ka_ref, vaT_ref, qiT_ref, ki_ref, wT_ref,
                       qbT_ref, kb_ref, vbT_ref):
    x = x_ref[...]
    tm = x.shape[0]
    h = (x * lax.rsqrt(jnp.mean(x * x, axis=-1, keepdims=True) + EPS) * g_ref[...]).astype(BF16)

    def proj(lo, hi):
        return _dot(h, w1_ref[:, lo:hi])

    def head_norm(zh, gain):
        ss = jnp.sum(zh * zh, axis=-1, keepdims=True) * (1.0 / A_HEAD_DIM)
        return zh * lax.rsqrt(ss + EPS) * gain

    ones_rows = jnp.where(lax.broadcasted_iota(I32, (BF16_ROWS, tm), 0) == 0, 1.0, 0.0).astype(BF16)

    def store_vT(ref, v):
        vT = v.T
        for hh in range(A_HEADS):
            ref[VP * hh:VP * hh + B_V, :] = vT[B_V * hh:B_V * (hh + 1), :].astype(BF16)
            ref[VP * hh + B_V:VP * (hh + 1), :] = ones_rows

    spare_a = jnp.where(lax.broadcasted_iota(I32, (1, HP), 1) == A_HEAD_DIM, 1.0, 0.0)
    qa = proj(_C_QA, _C_KA)
    ka = proj(_C_KA, _C_VA)
    gqa = gqa_ref[...] * (A_HEAD_DIM ** -0.5 * LOG2E)
    gka = gka_ref[...]
    for hh in range(A_HEADS):
        blk = slice(HP * hh, HP * (hh + 1))
        qh = head_norm(qa[:, blk], gqa)
        qnorm = jnp.sqrt(jnp.sum(qh * qh, axis=-1, keepdims=True))
        m_a = (qnorm * bnd_ref[0:1, :] + bnd_ref[2:3, hh:hh + 1]) * BOUND_SLACK
        qaT_ref[blk, :] = (qh - m_a * spare_a).T.astype(BF16)
        ka_ref[:, blk] = (head_norm(ka[:, blk], gka) + spare_a).astype(BF16)
    store_vT(vaT_ref, proj(_C_VA, _C_QI))

    zero_half = jnp.zeros((IDX_DIM, tm), BF16)
    qiT = proj(_C_QI, _C_KI).T
    for hh in range(IDX_HEADS):
        qiT_ref[HP * hh:HP * hh + IDX_DIM, :] = qiT[IDX_DIM * hh:IDX_DIM * (hh + 1), :].astype(BF16)
        qiT_ref[HP * hh + IDX_DIM:HP * (hh + 1), :] = zero_half
    ki_ref[...] = proj(_C_KI, _C_WI).astype(BF16)
    wi = proj(_C_WI, _C_CQ) * ((IDX_HEADS ** -0.5) * (IDX_DIM ** -0.5))
    wT_ref[...] = wi.T[0:IDX_HEADS, :]

    cos = cos_ref[...]
    sin = sin_ref[...]
    spare = jnp.where(lax.broadcasted_iota(I32, (1, HP), 1) == B_QK, 1.0, 0.0)
    cq = proj(_C_CQ, _C_CKV)
    cqn = (cq * lax.rsqrt(jnp.mean(cq * cq, axis=-1, keepdims=True) + EPS) * gcq_ref[...]).astype(BF16)
    qm = _dot(cqn, wqm_ref[...])
    qs = _dot(cqn, wqs_ref[...])
    gqb = gqb_ref[...] * (B_QK ** -0.5 * LOG2E)
    kbound = bnd_ref[1:2, :]
    for hh in range(B_HEADS):
        blk = slice(HP * hh, HP * (hh + 1))
        qh = qm[:, blk] * cos + qs[:, blk] * sin
        ss = jnp.sum(qh * qh, axis=-1, keepdims=True) * (1.0 / B_QK)
        qh = qh * lax.rsqrt(ss + EPS) * gqb
        m_b = jnp.sqrt(jnp.sum(qh * qh, axis=-1, keepdims=True)) * kbound * BOUND_SLACK
        qbT_ref[blk, :] = (qh - m_b * spare).T.astype(BF16)

    ckv = proj(_C_CKV, _C_KR)
    ckvn = (ckv * lax.rsqrt(jnp.mean(ckv * ckv, axis=-1, keepdims=True) + EPS) * gckv_ref[...]).astype(BF16)
    km = _dot(ckvn, wkm_ref[...])
    store_vT(vbT_ref, _dot(ckvn, wv_ref[...]))
    krot = proj(_C_KR, _C_KRS) * cos + proj(_C_KRS, _C_END) * sin
    gkb = gkb_ref[...]
    for hh in range(B_HEADS):
        blk = slice(HP * hh, HP * (hh + 1))
        kh = km[:, blk] + krot
        ss = jnp.sum(kh * kh, axis=-1, keepdims=True) * (1.0 / B_QK)
        kb_ref[:, blk] = (kh * lax.rsqrt(ss + EPS) * gkb + spare).astype(BF16)


def _token_prep(x, g, w1, wqm, wqs, wkm, wv, gqa, gka, gcq, gckv, gqb, gkb, bnd,
                cos_t, sin_t):
    B, S, D = x.shape
    tm = TM_PREP
    nt = S // tm
    row3 = lambda w: pl.BlockSpec((None, tm, w), lambda b, i: (b, i, 0))
    colT = lambda r: pl.BlockSpec((None, r, tm), lambda b, i: (b, 0, i))
    full = lambda a: pl.BlockSpec(a.shape, lambda b, i: (0,) * a.ndim)
    tab = pl.BlockSpec((tm, LANES), lambda b, i: (i, 0))
    consts = (g, w1, wqm, wqs, wkm, wv, gqa, gka, gcq, gckv, gqb, gkb, bnd)
    out_shape = (
        jax.ShapeDtypeStruct((B, A_HEADS * HP, S), BF16),
        jax.ShapeDtypeStruct((B, S, A_HEADS * HP), BF16),
        jax.ShapeDtypeStruct((B, A_HEADS * VP, S), BF16),
        jax.ShapeDtypeStruct((B, IDX_HEADS * HP, S), BF16),
        jax.ShapeDtypeStruct((B, S, LANES), BF16),
        jax.ShapeDtypeStruct((B, IDX_HEADS, S), F32),
        jax.ShapeDtypeStruct((B, B_HEADS * HP, S), BF16),
        jax.ShapeDtypeStruct((B, S, B_HEADS * HP), BF16),
        jax.ShapeDtypeStruct((B, B_HEADS * VP, S), BF16),
    )
    out_specs = (colT(A_HEADS * HP), row3(A_HEADS * HP), colT(A_HEADS * VP), colT(IDX_HEADS * HP),
                 row3(LANES), colT(IDX_HEADS), colT(B_HEADS * HP), row3(B_HEADS * HP), colT(B_HEADS * VP))
    return pl.pallas_call(
        _token_prep_kernel,
        grid=(B, nt),
        in_specs=[row3(D)] + [full(a) for a in consts] + [tab, tab],
        out_specs=out_specs,
        out_shape=out_shape,
        compiler_params=_cparams(("parallel", "parallel")),
        name="token_prep",
    )(x, *consts, cos_t, sin_t)


def _tree_sum(parts):
    while len(parts) > 1:
        nxt = [parts[t] + parts[t + 1] for t in range(0, len(parts) - 1, 2)]
        if len(parts) % 2:
            nxt.append(parts[-1])
        parts = nxt
    return parts[0]


def _select_kernel(qiT_ref, wT_ref, ki_ref, mask_ref, hi_ref, lo_ref, acc_ref, *, topk):
    S = ki_ref.shape[0]
    tq = TQ_SEL
    i = pl.program_id(1)
    n_keys = (i + 1) * tq
    q_chunk = (i * tq + lax.broadcasted_iota(I32, (CH_SEL, tq), 1)) >> CHUNK_SHIFT
    k_off = lax.broadcasted_iota(I32, (CH_SEL, tq), 0)

    def score_unit(u, carry):
        k0 = pl.multiple_of(u * KU_SEL, KU_SEL)
        kblk = ki_ref[pl.ds(k0, KU_SEL), :]
        for hh in range(IDX_HEADS):
            d = _dot(kblk, qiT_ref[HP * hh:HP * (hh + 1), :])
            w = wT_ref[hh:hh + 1, :]
            for c in range(KU_SEL // CH_SEL):
                rows = slice(CH_SEL * c, CH_SEL * (c + 1))
                t = jnp.maximum(d[rows, :], 0.0) * w
                if hh == 0:
                    acc_ref[rows, :] = t
                elif hh < IDX_HEADS - 1:
                    acc_ref[rows, :] += t
                else:
                    acc = acc_ref[rows, :] + t
                    acc = jnp.where(acc == 0.0, 0.0, acc)
                    adm = ((k0 + CH_SEL * c + k_off) >> CHUNK_SHIFT) <= q_chunk
                    acc = jnp.where(adm, acc, -jnp.inf)
                    bits = pltpu.bitcast(acc, I32)
                    key = bits ^ ((bits >> 31) & 0x7FFFFFFF)
                    r0 = k0 + CH_SEL * c
                    hi_ref[pl.ds(r0, CH_SEL), :] = (key >> 16).astype(I16)
                    lo_ref[pl.ds(r0, CH_SEL), :] = (key ^ 0x8000).astype(I16)
        return carry

    lax.fori_loop(0, n_keys // KU_SEL, score_unit, 0)

    one16 = jnp.ones((tq, tq), I16)
    zero16 = jnp.zeros((tq, tq), I16)
    rows16 = tq // BF16_ROWS

    def count(pred):
        def body(r, acc):
            r0 = pl.multiple_of(r * tq, tq)
            ind = jnp.where(pred(lambda: hi_ref[pl.ds(r0, tq), :], lambda: lo_ref[pl.ds(r0, tq), :]),
                            one16, zero16)
            return acc + _tree_sum([ind[BF16_ROWS * s:BF16_ROWS * (s + 1), :] for s in range(rows16)])
        acc = lax.fori_loop(0, i + 1, body, jnp.zeros((BF16_ROWS, tq), I16))
        return jnp.sum(acc.astype(I32), axis=0, keepdims=True)

    def radix16(pred_of, need, count_at_zero):
        def step(t, carry):
            prefix, cnt_acc = carry
            cand = prefix | lax.shift_left(jnp.int32(1), 15 - t)
            cnt = count(pred_of((cand - 32768).astype(I16)))
            ok = cnt >= need
            return jnp.where(ok, cand, prefix), jnp.where(ok, cnt, cnt_acc)
        prefix, cnt = lax.fori_loop(0, 16, step, (jnp.zeros((1, tq), I32), count_at_zero))
        return prefix - 32768, cnt

    t_hi32, cnt_hi_ge = radix16(lambda c: (lambda hi, lo: hi() >= c), topk,
                                jnp.full((1, tq), n_keys, I32))
    t_hi = t_hi32.astype(I16)
    lowest16 = jnp.full((tq, tq), -32768, I16)

    def above_and_park(r, acc):
        r0 = pl.multiple_of(r * tq, tq)
        blk_hi = hi_ref[pl.ds(r0, tq), :]
        lo_ref[pl.ds(r0, tq), :] = jnp.where(blk_hi == t_hi, lo_ref[pl.ds(r0, tq), :], lowest16)
        ind = jnp.where(blk_hi > t_hi, one16, zero16)
        return acc + _tree_sum([ind[BF16_ROWS * s:BF16_ROWS * (s + 1), :] for s in range(rows16)])

    cnt_above = jnp.sum(lax.fori_loop(0, i + 1, above_and_park, jnp.zeros((BF16_ROWS, tq), I16)).astype(I32),
                        axis=0, keepdims=True)
    t_lo32, cnt_lo_ge = radix16(lambda c: (lambda hi, lo: lo() >= c),
                                topk - cnt_above, cnt_hi_ge - cnt_above)
    t_lo = t_lo32.astype(I16)
    thr = (t_hi32 << 16) + (t_lo32 + 32768)
    cnt_ge = cnt_above + cnt_lo_ge
    tied = jnp.logical_and(cnt_ge > topk, thr > KEY_NEG_INF)
    any_tie = jnp.max(tied.astype(I32)) > 0
    thr_adm = jnp.maximum(thr, KEY_NEG_INF + 1)
    a_hi = (thr_adm >> 16).astype(I16)
    a_lo = ((thr_adm & 0xFFFF) - 32768).astype(I16)
    pass16 = jnp.zeros((tq, tq), BF16)
    drop16 = jnp.full((tq, tq), NEG, BF16)

    @pl.when(jnp.logical_not(any_tie))
    def _():
        def body(r, carry):
            r0 = pl.multiple_of(r * tq, tq)
            hi = hi_ref[pl.ds(r0, tq), :]
            lo = lo_ref[pl.ds(r0, tq), :]
            sel = jnp.logical_or(hi > a_hi, jnp.logical_and(hi == a_hi, lo >= a_lo))
            mask_ref[pl.ds(r0, tq), :] = jnp.where(sel, pass16, drop16)
            return carry
        lax.fori_loop(0, i + 1, body, 0)

    @pl.when(any_tie)
    def _():
        cnt_gt = cnt_above + count(lambda hi, lo: lo() > t_lo)
        n_take = (topk - cnt_gt).astype(F32)
        ltri = (lax.broadcasted_iota(I32, (KU_SEL, KU_SEL), 0)
                >= lax.broadcasted_iota(I32, (KU_SEL, KU_SEL), 1)).astype(BF16)

        def body(u, seen):
            k0 = pl.multiple_of(u * KU_SEL, KU_SEL)
            blk = ((hi_ref[pl.ds(k0, KU_SEL), :].astype(I32) << 16)
                   + (lo_ref[pl.ds(k0, KU_SEL), :].astype(I32) + 32768))
            eq = blk == thr
            eqf = eq.astype(F32)
            rank = _dot(ltri, eqf.astype(BF16)) + seen
            sel = jnp.logical_or(blk > thr, jnp.logical_and(eq, rank <= n_take))
            sel = jnp.logical_and(sel, blk != KEY_NEG_INF)
            mask_ref[pl.ds(k0, KU_SEL), :] = jnp.where(sel, 0.0, NEG).astype(BF16)
            return seen + jnp.sum(eqf, axis=0, keepdims=True)
        lax.fori_loop(0, n_keys // KU_SEL, body, jnp.zeros((1, tq), F32))

    def fill(r, carry):
        r0 = pl.multiple_of(r * tq, tq)
        mask_ref[pl.ds(r0, tq), :] = drop16
        return carry
    lax.fori_loop(i + 1, S // tq, fill, 0)


def _select_topk(qiT, wT, ki, topk):
    B, S, _ = ki.shape
    tq = TQ_SEL
    return pl.pallas_call(
        functools.partial(_select_kernel, topk=topk),
        grid=(B, S // tq),
        in_specs=[
            pl.BlockSpec((None, IDX_HEADS * HP, tq), lambda b, i: (b, 0, i)),
            pl.BlockSpec((None, IDX_HEADS, tq), lambda b, i: (b, 0, i)),
            pl.BlockSpec((None, S, LANES), lambda b, i: (b, 0, 0)),
        ],
        out_specs=pl.BlockSpec((None, S, tq), lambda b, i: (b, 0, i)),
        out_shape=jax.ShapeDtypeStruct((B, S, S), BF16),
        scratch_shapes=[pltpu.VMEM((S, tq), I16), pltpu.VMEM((S, tq), I16),
                        pltpu.VMEM((KU_SEL, tq), F32)],
        compiler_params=_cparams(("parallel", "parallel")),
        name="select_topk",
    )(qiT, wT, ki)


def _attend_kernel(ti_ref, tj_ref, safe_ref, *refs, heads, with_mask):
    if with_mask:
        qT_ref, k_ref, vT_ref, mask_ref, bias_ref, o_ref, acc_ref, s_ref, p_ref, m_ref, add_ref = refs
    else:
        qT_ref, k_ref, vT_ref, diag_ref, o_ref, acc_ref, s_ref, p_ref, m_ref = refs
    s_idx = pl.program_id(1)
    i = ti_ref[s_idx]
    j = tj_ref[s_idx]
    bounded = safe_ref[0] == 1
    tk = k_ref.shape[0]
    n_ch = tk // CH_ATT

    @pl.when(j == 0)
    def _():
        acc_ref[...] = jnp.zeros(acc_ref.shape, F32)
        m_ref[...] = jnp.full(m_ref.shape, NEG, F32)

    def logits(hh):
        s_ref[hh % 2] = _dot(k_ref[:, HP * hh:HP * (hh + 1)], qT_ref[HP * hh:HP * (hh + 1), :])

    def chunks():
        return [slice(CH_ATT * c, CH_ATT * (c + 1)) for c in range(n_ch)]

    def sweep_bounded(extra):
        logits(0)
        for hh in range(heads):
            buf = hh % 2
            if hh + 1 < heads:
                logits(hh + 1)
            for rows in chunks():
                t = s_ref[buf, rows, :]
                e = extra(hh, rows)
                if e is not None:
                    t = t + e
                p_ref[buf, rows, :] = jnp.exp2(t).astype(BF16)
            acc_ref[VP * hh:VP * (hh + 1), :] += _dot(vT_ref[VP * hh:VP * (hh + 1), :], p_ref[buf])

    def sweep_running_max(extra):
        for hh in range(heads):
            buf = hh % 2
            logits(hh)
            m_blk = jnp.full((SUBLANES, s_ref.shape[2]), NEG, F32)
            for rows in chunks():
                t = s_ref[buf, rows, :]
                e = extra(hh, rows)
                if e is not None:
                    t = t + e
                    s_ref[buf, rows, :] = t
                for r in range(CH_ATT // SUBLANES):
                    m_blk = jnp.maximum(m_blk, t[SUBLANES * r:SUBLANES * (r + 1), :])
            m_old = m_ref[hh:hh + 1, :]
            m_new = jnp.maximum(m_old, jnp.max(m_blk, axis=0, keepdims=True))
            m_ref[hh:hh + 1, :] = m_new
            for rows in chunks():
                p_ref[buf, rows, :] = jnp.exp2(s_ref[buf, rows, :] - m_new).astype(BF16)
            pv = _dot(vT_ref[VP * hh:VP * (hh + 1), :], p_ref[buf])
            acc_ref[VP * hh:VP * (hh + 1), :] = jnp.exp2(m_old - m_new) * acc_ref[VP * hh:VP * (hh + 1), :] + pv

    def both(extra):
        @pl.when(bounded)
        def _():
            sweep_bounded(extra)

        @pl.when(jnp.logical_not(bounded))
        def _():
            sweep_running_max(extra)

    if with_mask:
        add_ref[...] = mask_ref[...].astype(F32)

        @pl.when(j < i - 1)
        def _():
            both(lambda hh, rows: add_ref[rows, :])

        @pl.when(j >= i - 1)
        def _():
            both(lambda hh, rows: add_ref[rows, :] + bias_ref[hh, rows, :].astype(F32))
    else:
        @pl.when(j < i)
        def _():
            both(lambda hh, rows: None)

        @pl.when(j == i)
        def _():
            both(lambda hh, rows: diag_ref[rows, :])

    @pl.when(j == i)
    def _():
        dv = VP - BF16_ROWS
        for hh in range(heads):
            inv = 1.0 / acc_ref[VP * hh + dv:VP * hh + dv + 1, :]
            s_ref[0, dv * hh:dv * (hh + 1), :] = acc_ref[VP * hh:VP * hh + dv, :] * inv
        o_ref[...] = s_ref[0, 0:dv * heads, :].T.astype(o_ref.dtype)


def _pair_tables(S):
    n = S // TQ_ATT
    ti = np.array([i for i in range(n) for _ in range(i + 1)], np.int32)
    tj = np.array([j for i in range(n) for j in range(i + 1)], np.int32)
    return jnp.asarray(ti), jnp.asarray(tj)


def _attend(qT, k, vT, safe, extra_inputs, extra_specs, extra_scratch, with_mask, name):
    B, S, _ = k.shape
    tq, tk = TQ_ATT, TK_ATT
    assert tq == tk and tk >= B_V * A_HEADS
    ti, tj = _pair_tables(S)
    heads = A_HEADS
    kern = functools.partial(_attend_kernel, heads=heads, with_mask=with_mask)
    grid_spec = pltpu.PrefetchScalarGridSpec(
        num_scalar_prefetch=3,
        grid=(B, ti.shape[0]),
        in_specs=[
            pl.BlockSpec((None, heads * HP, tq), lambda b, s, ti, tj, sf: (b, 0, ti[s])),
            pl.BlockSpec((None, tk, heads * HP), lambda b, s, ti, tj, sf: (b, tj[s], 0)),
            pl.BlockSpec((None, heads * VP, tk), lambda b, s, ti, tj, sf: (b, 0, tj[s])),
        ] + extra_specs,
        out_specs=pl.BlockSpec((None, tq, heads * B_V), lambda b, s, ti, tj, sf: (b, ti[s], 0)),
        scratch_shapes=[pltpu.VMEM((heads * VP, tq), F32),
                        pltpu.VMEM((2, tk, tq), F32),
                        pltpu.VMEM((2, tk, tq), BF16),
                        pltpu.VMEM((heads, tq), F32)] + extra_scratch,
    )
    return pl.pallas_call(
        kern, grid_spec=grid_spec,
        out_shape=jax.ShapeDtypeStruct((B, S, heads * B_V), BF16),
        compiler_params=_cparams(("parallel", "arbitrary")),
        name=name,
    )(ti, tj, safe, qT, k, vT, *extra_inputs)


def _attend_a(qaT, ka, vaT, mask, bias_tiles, safe):
    tq, tk = TQ_ATT, TK_ATT
    specs = [
        pl.BlockSpec((None, tk, tq), lambda b, s, ti, tj, sf: (b, tj[s], ti[s])),
        pl.BlockSpec((A_HEADS, None, tk, tq),
                     lambda b, s, ti, tj, sf: (0, jnp.where(tj[s] == ti[s], 0, 1), 0, 0)),
    ]
    return _attend(qaT, ka, vaT, safe, (mask, bias_tiles), specs, [pltpu.VMEM((tk, tq), F32)],
                   True, "attend_a")


def _attend_b(qbT, kb, vbT, diag_tile, safe):
    tq, tk = TQ_ATT, TK_ATT
    specs = [pl.BlockSpec((tk, tq), lambda b, s, ti, tj, sf: (0, 0))]
    return _attend(qbT, kb, vbT, safe, (diag_tile,), specs, [], False, "attend_b")


def _merge_kernel(x_ref, ya_ref, yb_ref, g_ref, wga_ref, wgb_ref, bga_ref, bgb_ref,
                  wpa_ref, wpb_ref, wo_ref, o_ref):
    x = x_ref[...]
    h = (x * lax.rsqrt(jnp.mean(x * x, axis=-1, keepdims=True) + EPS) * g_ref[...]).astype(BF16)
    gate_a = jax.nn.sigmoid(_dot(h, wga_ref[...]) + bga_ref[...])
    gate_b = jax.nn.sigmoid(_dot(h, wgb_ref[...]) + bgb_ref[...])
    merged = gate_a * _dot(ya_ref[...], wpa_ref[...]) + gate_b * _dot(yb_ref[...], wpb_ref[...])
    o_ref[...] = x + _dot(merged.astype(BF16), wo_ref[...])


def _merge_out(x2, ya2, yb2, g, wga, wgb, bga, bgb, wpa, wpb, wo):
    R, D = x2.shape
    tm = TM_MERGE
    row = lambda w: pl.BlockSpec((tm, w), lambda i: (i, 0))
    full = lambda a: pl.BlockSpec(a.shape, lambda i: (0,) * a.ndim)
    consts = (g, wga, wgb, bga, bgb, wpa, wpb, wo)
    return pl.pallas_call(
        _merge_kernel,
        grid=(R // tm,),
        in_specs=[row(D), row(A_WIDTH), row(B_WIDTH)] + [full(a) for a in consts],
        out_specs=row(D),
        out_shape=jax.ShapeDtypeStruct((R, D), F32),
        compiler_params=_cparams(("parallel",)),
        name="merge_out",
    )(x2, ya2, yb2, *consts)


def _ffn_kernel(x_ref, xp_ref, g_ref, wuv_ref, wug_ref, cwv_ref, cwg_ref, cbv_ref, cbg_ref, wd_ref,
                o_ref, uv_ref, ug_ref, acc_ref, *, tiles_per_seq):
    i = pl.program_id(0)
    f = pl.program_id(1)
    tm = x_ref.shape[0]
    halo = SUBLANES
    g = g_ref[...]

    def normed(v):
        return (v * lax.rsqrt(jnp.mean(v * v, axis=-1, keepdims=True) + EPS) * g).astype(BF16)

    h = normed(x_ref[...])
    keep = jnp.where(i % tiles_per_seq == 0, 0.0, 1.0)
    hp = normed(xp_ref[...])

    def conv(u_ref, w_ref, cw_ref, cb_ref):
        u_ref[0:halo, :] = _dot(hp, w_ref[...]) * keep
        u_ref[halo:halo + tm, :] = _dot(h, w_ref[...])
        out = cb_ref[...]
        for t in range(CONV_W):
            lo = halo - (CONV_W - 1) + t
            out = out + cw_ref[t:t + 1, :] * u_ref[lo:lo + tm, :]
        return out

    val = conv(uv_ref, wuv_ref, cwv_ref, cbv_ref)
    gat = conv(ug_ref, wug_ref, cwg_ref, cbg_ref)
    act = (gat * jax.nn.sigmoid(gat) * val).astype(BF16)
    part = _dot(act, wd_ref[...])

    @pl.when(f == 0)
    def _():
        acc_ref[...] = part

    @pl.when(f > 0)
    def _():
        acc_ref[...] = acc_ref[...] + part

    @pl.when(f == pl.num_programs(1) - 1)
    def _():
        o_ref[...] = x_ref[...] + acc_ref[...]


def _conv_ffn(x2, S, g, wu, cw, cb, wd):
    R, D = x2.shape
    tm, tf = TM_FFN, TF_FFN
    nf = D_FF // tf
    halo_blocks = tm // SUBLANES
    kern = functools.partial(_ffn_kernel, tiles_per_seq=S // tm)
    return pl.pallas_call(
        kern,
        grid=(R // tm, nf),
        in_specs=[
            pl.BlockSpec((tm, D), lambda i, f: (i, 0)),
            pl.BlockSpec((SUBLANES, D), lambda i, f: (jnp.maximum(i * halo_blocks - 1, 0), 0)),
            pl.BlockSpec((1, D), lambda i, f: (0, 0)),
            pl.BlockSpec((D, tf), lambda i, f: (0, f)),
            pl.BlockSpec((D, tf), lambda i, f: (0, nf + f)),
            pl.BlockSpec((CONV_W, tf), lambda i, f: (0, f)),
            pl.BlockSpec((CONV_W, tf), lambda i, f: (0, nf + f)),
            pl.BlockSpec((1, tf), lambda i, f: (0, f)),
            pl.BlockSpec((1, tf), lambda i, f: (0, nf + f)),
            pl.BlockSpec((tf, D), lambda i, f: (f, 0)),
        ],
        out_specs=pl.BlockSpec((tm, D), lambda i, f: (i, 0)),
        out_shape=jax.ShapeDtypeStruct((R, D), F32),
        scratch_shapes=[pltpu.VMEM((tm + SUBLANES, tf), F32), pltpu.VMEM((tm + SUBLANES, tf), F32),
                        pltpu.VMEM((tm, D), F32)],
        compiler_params=_cparams(("parallel", "arbitrary")),
        name="conv_ffn",
    )(x2, x2, g, wu, wu, cw, cw, cb, cb, wd)


def _t5_bucket(rel):
    nb = REL_BUCKETS // 2
    max_exact = nb // 2
    side = jnp.where(rel > 0, nb, 0)
    n = jnp.abs(rel)
    nf = jnp.maximum(n, 1).astype(F32)
    large = max_exact + (jnp.log(nf / max_exact) / math.log(REL_MAX_DIST / max_exact)
                         * (nb - max_exact)).astype(I32)
    large = jnp.minimum(large, nb - 1)
    return side + jnp.where(n < max_exact, n, large)


def _bias_tiles(rel_bias):
    tq, tk = TQ_ATT, TK_ATT
    assert tq == tk and tk >= REL_MAX_DIST
    kk = jnp.arange(tk, dtype=I32)[:, None]
    qq = jnp.arange(tq, dtype=I32)[None, :]
    bucket = _t5_bucket(jnp.stack([kk - qq, kk - qq - tk])).reshape(1, 2 * tk * tq)
    onehot = (jnp.arange(REL_BUCKETS, dtype=I32)[:, None] == bucket).astype(F32)
    far = rel_bias[_t5_bucket(jnp.asarray(-REL_MAX_DIST, I32))]
    table = ((rel_bias - far[None, :]) * LOG2E).T
    tiles = jnp.dot(table, onehot, precision=lax.Precision.HIGHEST)
    return tiles.reshape(A_HEADS, 2, tk, tq).astype(BF16), jnp.max(jnp.abs(table), axis=1)


def _diag_tile():
    kk = np.arange(TK_ATT)[:, None] // CHUNK
    qq = np.arange(TQ_ATT)[None, :] // CHUNK
    return jnp.asarray(np.where(kk <= qq, 0.0, NEG).astype(np.float32))


def _rope_tables(S):
    half = B_ROPE // 2
    inv = ROPE_BASE ** (-jnp.arange(half, dtype=F32) / half)
    ang = jnp.arange(S, dtype=I32).astype(F32)[:, None] * inv[None, :]
    cos, sin = jnp.cos(ang), jnp.sin(ang)
    ones = jnp.ones((S, B_NOPE), F32)
    zeros_n = jnp.zeros((S, B_NOPE), F32)
    zeros_p = jnp.zeros((S, HP - B_QK), F32)
    cos_t = jnp.concatenate([ones, cos, cos, zeros_p], axis=1)
    sin_t = jnp.concatenate([zeros_n, -sin, sin, zeros_p], axis=1)
    return cos_t, sin_t


def _pad_cols(w, width):
    return jnp.pad(w, ((0, 0), (0, width - w.shape[1])))


def _pad_heads(w, heads, width):
    rows = w.shape[0]
    w = w.reshape(rows, heads, -1)
    return jnp.pad(w, ((0, 0), (0, 0), (0, width - w.shape[2]))).reshape(rows, heads * width)


def _swap_halves(w):
    half = w.shape[-1] // 2
    return jnp.concatenate([w[..., half:], w[..., :half]], axis=-1)


def _layer_weights(l, w_in, b_w_uq, b_w_ukv, b_q_norm, b_k_norm, a_q_norm, a_k_norm, bias_max):
    w = w_in[l]
    o = np.cumsum([0, A_WIDTH, A_WIDTH, A_WIDTH, IDX_HEADS * IDX_DIM, IDX_DIM, IDX_HEADS,
                   B_Q_LORA, B_KV_LORA, B_ROPE, D_MODEL, D_MODEL])
    seg = [w[:, o[t]:o[t + 1]] for t in range(11)]
    zn = jnp.zeros((D_MODEL, B_NOPE), F32)
    kr = seg[8]
    w1 = jnp.concatenate([
        _pad_heads(seg[0], A_HEADS, HP), _pad_heads(seg[1], A_HEADS, HP), seg[2], seg[3],
        _pad_cols(seg[4], LANES), _pad_cols(seg[5], LANES), seg[6], seg[7],
        _pad_cols(jnp.concatenate([zn, kr], axis=1), HP),
        _pad_cols(jnp.concatenate([zn, _swap_halves(kr)], axis=1), HP),
    ], axis=1).astype(BF16)
    assert w1.shape[1] == _C_END

    uq = b_w_uq[l].reshape(B_Q_LORA, B_HEADS, B_QK)
    zq = jnp.zeros((B_Q_LORA, B_HEADS, B_NOPE), F32)
    pq = jnp.zeros((B_Q_LORA, B_HEADS, HP - B_QK), F32)
    wqm = jnp.concatenate([uq, pq], axis=-1).reshape(B_Q_LORA, B_HEADS * HP).astype(BF16)
    wqs = jnp.concatenate([zq, _swap_halves(uq[..., B_NOPE:]), pq], axis=-1)
    wqs = wqs.reshape(B_Q_LORA, B_HEADS * HP).astype(BF16)

    ukv = b_w_ukv[l].reshape(B_KV_LORA, B_HEADS, B_NOPE + B_V)
    pk = jnp.zeros((B_KV_LORA, B_HEADS, HP - B_NOPE), F32)
    wkm = jnp.concatenate([ukv[..., :B_NOPE], pk], axis=-1).reshape(B_KV_LORA, B_HEADS * HP).astype(BF16)
    wv = ukv[..., B_NOPE:].reshape(B_KV_LORA, B_WIDTH).astype(BF16)

    gqa = _pad_cols(a_q_norm[l][None, :], HP)
    gka = _pad_cols(a_k_norm[l][None, :], HP)
    gqb = _pad_cols(b_q_norm[l][None, :], HP)
    gkb = _pad_cols(b_k_norm[l][None, :], HP)

    ka_bound = math.sqrt(A_HEAD_DIM) * jnp.max(jnp.abs(a_k_norm[l]))
    kb_bound = math.sqrt(B_QK) * jnp.max(jnp.abs(b_k_norm[l]))
    bnd = jnp.zeros((SUBLANES, LANES), F32)
    bnd = bnd.at[0, :].set(ka_bound).at[1, :].set(kb_bound).at[2, :A_HEADS].set(bias_max)
    qa_bound = math.sqrt(A_HEAD_DIM) * jnp.max(jnp.abs(a_q_norm[l])) * (A_HEAD_DIM ** -0.5 * LOG2E)
    qb_bound = math.sqrt(B_QK) * jnp.max(jnp.abs(b_q_norm[l])) * (B_QK ** -0.5 * LOG2E)
    range_a = (qa_bound * ka_bound + jnp.max(bias_max)) * BOUND_SLACK + jnp.max(bias_max)
    range_b = qb_bound * kb_bound * BOUND_SLACK
    safe_a = (range_a <= MAX_LOG2_RANGE).astype(I32).reshape(1)
    safe_b = (range_b <= MAX_LOG2_RANGE).astype(I32).reshape(1)
    return (w1, wqm, wqs, wkm, wv, gqa, gka, gqb, gkb, bnd, safe_a, safe_b,
            seg[9].astype(BF16), seg[10].astype(BF16))


def kernel(x, rel_bias, norm_mix, w_in, a_q_norm, a_k_norm, b_cq_norm, b_ckv_norm, b_w_uq, b_w_ukv,
           b_q_norm, b_k_norm, w_proj_a, w_proj_b, b_gate, w_out, norm_ffn, w_up, conv_w, conv_b, w_down):
    B, S, D = x.shape
    assert D == D_MODEL and S % TQ_ATT == 0 and S % TM_FFN == 0
    topk = min(TOPK_MAX, S // 4)
    assert TQ_SEL >= topk
    depth = w_in.shape[0]

    cos_t, sin_t = _rope_tables(S)
    bias_tiles, bias_max = _bias_tiles(rel_bias)
    diag_tile = _diag_tile()

    for l in range(depth):
        (w1, wqm, wqs, wkm, wv, gqa, gka, gqb, gkb, bnd, safe_a, safe_b, wga, wgb) = _layer_weights(
            l, w_in, b_w_uq, b_w_ukv, b_q_norm, b_k_norm, a_q_norm, a_k_norm, bias_max)
        qaT, ka, vaT, qiT, ki, wT, qbT, kb, vbT = _token_prep(
            x, norm_mix[l][None, :], w1, wqm, wqs, wkm, wv, gqa, gka,
            b_cq_norm[l][None, :], b_ckv_norm[l][None, :], gqb, gkb, bnd,
            cos_t, sin_t)
        mask = _select_topk(qiT, wT, ki, topk)
        y_a = _attend_a(qaT, ka, vaT, mask, bias_tiles, safe_a)
        y_b = _attend_b(qbT, kb, vbT, diag_tile, safe_b)
        x2 = _merge_out(
            x.reshape(B * S, D), y_a.reshape(B * S, A_WIDTH), y_b.reshape(B * S, B_WIDTH),
            norm_mix[l][None, :], wga, wgb, b_gate[l][None, :D_MODEL], b_gate[l][None, D_MODEL:],
            w_proj_a[l].astype(BF16), w_proj_b[l].astype(BF16), w_out[l].astype(BF16))
        x2 = _conv_ffn(x2, S, norm_ffn[l][None, :], w_up[l].astype(BF16), conv_w[l], conv_b[l][None, :],
                       w_down[l].astype(BF16))
        x = x2.reshape(B, S, D)
    return x
```

```python
import functools
import math

import numpy as np
import jax
import jax.numpy as jnp
from jax import lax
from jax.experimental import pallas as pl
from jax.experimental.pallas import tpu as pltpu

F32 = jnp.float32
BF16 = jnp.bfloat16
I32 = jnp.int32
I16 = jnp.int16

D_MODEL = 1024
CHUNK = 64
CHUNK_SHIFT = 6
assert 1 << CHUNK_SHIFT == CHUNK
A_HEADS = 8
A_HEAD_DIM = 64
A_WIDTH = A_HEADS * A_HEAD_DIM
IDX_HEADS = 8
IDX_DIM = 64
TOPK_MAX = 256
B_HEADS = 8
B_Q_LORA = 384
B_KV_LORA = 256
B_NOPE = 64
B_ROPE = 32
B_QK = B_NOPE + B_ROPE
B_V = 64
B_WIDTH = B_HEADS * B_V
ROPE_BASE = 10000.0
REL_BUCKETS = 32
REL_MAX_DIST = 128
D_FF = 2816
CONV_W = 3
EPS = 1e-6

LANES = 128
SUBLANES = 8
BF16_ROWS = 16
VMEM_LIMIT = 56 * 1024 * 1024

TM_PREP = 256
TQ_SEL = 256
KU_SEL = 256
CH_SEL = 64
TQ_ATT = 512
TK_ATT = 512
CH_ATT = 64
TM_MERGE = 256
TM_FFN = 512
TF_FFN = 1408

HP = 128
VP = B_V + BF16_ROWS
NEG = -1e30
KEY_NEG_INF = -2139095041
INT_MIN = -2147483648
LOG2E = 1.4426950408889634
BOUND_SLACK = 1.01
MAX_LOG2_RANGE = 60.0


def _cparams(sem):
    return pltpu.CompilerParams(dimension_semantics=sem, vmem_limit_bytes=VMEM_LIMIT)


def _dot(a, b):
    return jnp.dot(a, b, preferred_element_type=F32)


def _dot_exact(a, b):
    return jnp.dot(a, b, preferred_element_type=F32, precision=lax.Precision.HIGHEST)


_C_QA = 0
_C_KA = _C_QA + A_HEADS * HP
_C_VA = _C_KA + A_HEADS * HP
_C_QI = _C_VA + A_WIDTH
_C_KI = _C_QI + IDX_HEADS * HP
_C_WI = _C_KI + LANES
_C_CQ = _C_WI + LANES
_C_CKV = _C_CQ + B_Q_LORA
_C_KR = _C_CKV + B_KV_LORA
_C_KRS = _C_KR + HP
_C_END = _C_KRS + HP


def _token_prep_kernel(x_ref, g_ref, w1_ref, wqm_ref, wqs_ref, wkm_ref, wv_ref,
                       gqa_ref, gka_ref, gcq_ref, gckv_ref, gqb_ref, gkb_ref, bnd_ref,
                       cos_ref, sin_ref,
                       qaT_ref, ka_ref, vaT_ref, qi_ref, kiT_ref, wi_ref,
                       qbT_ref, kb_ref, vbT_ref):
    x = x_ref[...]
    tm = x.shape[0]
    h = (x * lax.rsqrt(jnp.mean(x * x, axis=-1, keepdims=True) + EPS) * g_ref[...]).astype(BF16)

    def proj(lo, hi):
        return _dot(h, w1_ref[:, lo:hi])

    def head_norm(zh, gain):
        ss = jnp.sum(zh * zh, axis=-1, keepdims=True) * (1.0 / A_HEAD_DIM)
        return zh * lax.rsqrt(ss + EPS) * gain

    ones_rows = jnp.where(lax.broadcasted_iota(I32, (BF16_ROWS, tm), 0) == 0, 1.0, 0.0).astype(BF16)

    def store_vT(ref, v):
        vT = v.T
        for hh in range(A_HEADS):
            ref[VP * hh:VP * hh + B_V, :] = vT[B_V * hh:B_V * (hh + 1), :].astype(BF16)
            ref[VP * hh + B_V:VP * (hh + 1), :] = ones_rows

    spare_a = jnp.where(lax.broadcasted_iota(I32, (1, HP), 1) == A_HEAD_DIM, 1.0, 0.0)
    qa = proj(_C_QA, _C_KA)
    ka = proj(_C_KA, _C_VA)
    gqa = gqa_ref[...] * (A_HEAD_DIM ** -0.5 * LOG2E)
    gka = gka_ref[...]
    for hh in range(A_HEADS):
        blk = slice(HP * hh, HP * (hh + 1))
        qh = head_norm(qa[:, blk], gqa)
        qnorm = jnp.sqrt(jnp.sum(qh * qh, axis=-1, keepdims=True))
        m_a = (qnorm * bnd_ref[0:1, :] + bnd_ref[2:3, hh:hh + 1]) * BOUND_SLACK
        qaT_ref[blk, :] = (qh - m_a * spare_a).T.astype(BF16)
        ka_ref[:, blk] = (head_norm(ka[:, blk], gka) + spare_a).astype(BF16)
    store_vT(vaT_ref, proj(_C_VA, _C_QI))

    qi_ref[...] = proj(_C_QI, _C_KI).astype(BF16)
    kiT_ref[...] = proj(_C_KI, _C_WI).T.astype(BF16)
    wi_ref[...] = proj(_C_WI, _C_CQ) * ((IDX_HEADS ** -0.5) * (IDX_DIM ** -0.5))

    cos = cos_ref[...]
    sin = sin_ref[...]
    spare = jnp.where(lax.broadcasted_iota(I32, (1, HP), 1) == B_QK, 1.0, 0.0)
    cq = proj(_C_CQ, _C_CKV)
    cqn = (cq * lax.rsqrt(jnp.mean(cq * cq, axis=-1, keepdims=True) + EPS) * gcq_ref[...]).astype(BF16)
    qm = _dot(cqn, wqm_ref[...])
    qs = _dot(cqn, wqs_ref[...])
    gqb = gqb_ref[...] * (B_QK ** -0.5 * LOG2E)
    kbound = bnd_ref[1:2, :]
    for hh in range(B_HEADS):
        blk = slice(HP * hh, HP * (hh + 1))
        qh = qm[:, blk] * cos + qs[:, blk] * sin
        ss = jnp.sum(qh * qh, axis=-1, keepdims=True) * (1.0 / B_QK)
        qh = qh * lax.rsqrt(ss + EPS) * gqb
        m_b = jnp.sqrt(jnp.sum(qh * qh, axis=-1, keepdims=True)) * kbound * BOUND_SLACK
        qbT_ref[blk, :] = (qh - m_b * spare).T.astype(BF16)

    ckv = proj(_C_CKV, _C_KR)
    ckvn = (ckv * lax.rsqrt(jnp.mean(ckv * ckv, axis=-1, keepdims=True) + EPS) * gckv_ref[...]).astype(BF16)
    km = _dot(ckvn, wkm_ref[...])
    store_vT(vbT_ref, _dot(ckvn, wv_ref[...]))
    krot = proj(_C_KR, _C_KRS) * cos + proj(_C_KRS, _C_END) * sin
    gkb = gkb_ref[...]
    for hh in range(B_HEADS):
        blk = slice(HP * hh, HP * (hh + 1))
        kh = km[:, blk] + krot
        ss = jnp.sum(kh * kh, axis=-1, keepdims=True) * (1.0 / B_QK)
        kb_ref[:, blk] = (kh * lax.rsqrt(ss + EPS) * gkb + spare).astype(BF16)


def _token_prep(x, g, w1, wqm, wqs, wkm, wv, gqa, gka, gcq, gckv, gqb, gkb, bnd,
                cos_t, sin_t):
    B, S, D = x.shape
    tm = TM_PREP
    nt = S // tm
    row3 = lambda w: pl.BlockSpec((None, tm, w), lambda b, i: (b, i, 0))
    colT = lambda r: pl.BlockSpec((None, r, tm), lambda b, i: (b, 0, i))
    full = lambda a: pl.BlockSpec(a.shape, lambda b, i: (0,) * a.ndim)
    tab = pl.BlockSpec((tm, LANES), lambda b, i: (i, 0))
    consts = (g, w1, wqm, wqs, wkm, wv, gqa, gka, gcq, gckv, gqb, gkb, bnd)
    out_shape = (
        jax.ShapeDtypeStruct((B, A_HEADS * HP, S), BF16),
        jax.ShapeDtypeStruct((B, S, A_HEADS * HP), BF16),
        jax.ShapeDtypeStruct((B, A_HEADS * VP, S), BF16),
        jax.ShapeDtypeStruct((B, S, IDX_HEADS * HP), BF16),
        jax.ShapeDtypeStruct((B, nt, LANES, tm), BF16),
        jax.ShapeDtypeStruct((B, S, LANES), F32),
        jax.ShapeDtypeStruct((B, B_HEADS * HP, S), BF16),
        jax.ShapeDtypeStruct((B, S, B_HEADS * HP), BF16),
        jax.ShapeDtypeStruct((B, B_HEADS * VP, S), BF16),
    )
    kiT_spec = pl.BlockSpec((None, None, LANES, tm), lambda b, i: (b, i, 0, 0))
    out_specs = (colT(A_HEADS * HP), row3(A_HEADS * HP), colT(A_HEADS * VP), row3(IDX_HEADS * HP),
                 kiT_spec, row3(LANES), colT(B_HEADS * HP), row3(B_HEADS * HP), colT(B_HEADS * VP))
    return pl.pallas_call(
        _token_prep_kernel,
        grid=(B, nt),
        in_specs=[row3(D)] + [full(a) for a in consts] + [tab, tab],
        out_specs=out_specs,
        out_shape=out_shape,
        compiler_params=_cparams(("parallel", "parallel")),
        name="token_prep",
    )(x, *consts, cos_t, sin_t)


def _tree_sum(parts):
    while len(parts) > 1:
        nxt = [parts[t] + parts[t + 1] for t in range(0, len(parts) - 1, 2)]
        if len(parts) % 2:
            nxt.append(parts[-1])
        parts = nxt
    return parts[0]


def _select_kernel(qi_ref, wi_ref, kiT_ref, mask_ref, hi_ref, lo_ref, acca_ref, accb_ref, wb_ref, *, topk):
    S = mask_ref.shape[0]
    tq = TQ_SEL
    i = pl.program_id(1)
    n_keys = (i + 1) * tq
    q_chunk = (i * tq + lax.broadcasted_iota(I32, (CH_SEL, tq), 1)) >> CHUNK_SHIFT
    k_off = lax.broadcasted_iota(I32, (CH_SEL, tq), 0)

    for hh in range(IDX_HEADS):
        wb_ref[hh] = jnp.broadcast_to(wi_ref[:, hh:hh + 1], (tq, LANES))

    n_units = n_keys // KU_SEL
    accb_ref[...] = jnp.zeros(accb_ref.shape, F32)

    def heads(u, dst_ref):
        kT = kiT_ref[jnp.minimum(u, n_units - 1)]
        for hh in range(IDX_HEADS):
            d = _dot(qi_ref[:, HP * hh:HP * (hh + 1)], kT)
            for c in range(tq // CH_SEL):
                rows = slice(CH_SEL * c, CH_SEL * (c + 1))
                w = wb_ref[hh, rows, :]
                for lanes in (slice(0, LANES), slice(LANES, 2 * LANES)):
                    t = jnp.maximum(d[rows, lanes], 0.0) * w
                    if hh == 0:
                        dst_ref[rows, lanes] = t
                    else:
                        dst_ref[rows, lanes] += t

    def convert(src_ref, u):
        k0 = pl.multiple_of(jnp.maximum(u, 0) * KU_SEL, KU_SEL)
        accT = src_ref[...].T
        for c in range(KU_SEL // CH_SEL):
            acc = accT[CH_SEL * c:CH_SEL * (c + 1), :] + 0.0
            adm = ((k0 + CH_SEL * c + k_off) >> CHUNK_SHIFT) <= q_chunk
            acc = jnp.where(adm, acc, -jnp.inf)
            bits = pltpu.bitcast(acc, I32)
            key = bits ^ ((bits >> 31) & 0x7FFFFFFF)
            r0 = k0 + CH_SEL * c
            hi_ref[pl.ds(r0, CH_SEL), :] = (key >> 16).astype(I16)
            lo_ref[pl.ds(r0, CH_SEL), :] = (key ^ 0x8000).astype(I16)

    def unit_pair(t, carry):
        heads(2 * t, acca_ref)
        convert(accb_ref, 2 * t - 1)
        heads(2 * t + 1, accb_ref)
        convert(acca_ref, 2 * t)
        return carry

    lax.fori_loop(0, n_units // 2 + 1, unit_pair, 0)

    rb = 2 * tq
    n_rb = (i + 2) // 2
    lowest16 = jnp.full((rb, tq), -32768, I16)

    @pl.when((i + 1) % 2 == 1)
    def _():
        hi_ref[pl.ds(pl.multiple_of(n_keys, tq), tq), :] = lowest16[0:tq, :]
        lo_ref[pl.ds(pl.multiple_of(n_keys, tq), tq), :] = lowest16[0:tq, :]

    one16 = jnp.ones((rb, tq), I16)
    zero16 = jnp.zeros((rb, tq), I16)
    rows16 = rb // BF16_ROWS

    def count(pred):
        def body(r, acc):
            r0 = pl.multiple_of(r * rb, rb)
            ind = jnp.where(pred(lambda: hi_ref[pl.ds(r0, rb), :], lambda: lo_ref[pl.ds(r0, rb), :]),
                            one16, zero16)
            return acc + _tree_sum([ind[BF16_ROWS * s:BF16_ROWS * (s + 1), :] for s in range(rows16)])
        acc = lax.fori_loop(0, n_rb, body, jnp.zeros((BF16_ROWS, tq), I16))
        return jnp.sum(acc.astype(I32), axis=0, keepdims=True)

    def radix16(pred_of, need, count_at_zero):
        def step(t, carry):
            prefix, cnt_acc = carry
            cand = prefix | lax.shift_left(jnp.int32(1), 15 - t)
            cnt = count(pred_of((cand - 32768).astype(I16)))
            ok = cnt >= need
            return jnp.where(ok, cand, prefix), jnp.where(ok, cnt, cnt_acc)
        prefix, cnt = lax.fori_loop(0, 16, step, (jnp.zeros((1, tq), I32), count_at_zero))
        return prefix - 32768, cnt

    t_hi32, cnt_hi_ge = radix16(lambda c: (lambda hi, lo: hi() >= c), topk,
                                jnp.full((1, tq), n_keys, I32))
    t_hi = t_hi32.astype(I16)

    def above_and_park(r, acc):
        r0 = pl.multiple_of(r * rb, rb)
        blk_hi = hi_ref[pl.ds(r0, rb), :]
        lo_ref[pl.ds(r0, rb), :] = jnp.where(blk_hi == t_hi, lo_ref[pl.ds(r0, rb), :], lowest16)
        ind = jnp.where(blk_hi > t_hi, one16, zero16)
        return acc + _tree_sum([ind[BF16_ROWS * s:BF16_ROWS * (s + 1), :] for s in range(rows16)])

    cnt_above = jnp.sum(lax.fori_loop(0, n_rb, above_and_park, jnp.zeros((BF16_ROWS, tq), I16)).astype(I32),
                        axis=0, keepdims=True)
    t_lo32, cnt_lo_ge = radix16(lambda c: (lambda hi, lo: lo() >= c),
                                topk - cnt_above, cnt_hi_ge - cnt_above)
    t_lo = t_lo32.astype(I16)
    thr = (t_hi32 << 16) + (t_lo32 + 32768)
    cnt_ge = cnt_above + cnt_lo_ge
    tied = jnp.logical_and(cnt_ge > topk, thr > KEY_NEG_INF)
    any_tie = jnp.max(tied.astype(I32)) > 0
    thr_adm = jnp.maximum(thr, KEY_NEG_INF + 1)
    a_hi = (thr_adm >> 16).astype(I16)
    a_lo = ((thr_adm & 0xFFFF) - 32768).astype(I16)
    pass16 = jnp.zeros((rb, tq), BF16)
    drop16 = jnp.full((rb, tq), NEG, BF16)

    @pl.when(jnp.logical_not(any_tie))
    def _():
        def body(r, carry):
            r0 = pl.multiple_of(r * rb, rb)
            hi = hi_ref[pl.ds(r0, rb), :]
            lo = lo_ref[pl.ds(r0, rb), :]
            sel = jnp.logical_or(hi > a_hi, jnp.logical_and(hi == a_hi, lo >= a_lo))
            mask_ref[pl.ds(r0, rb), :] = jnp.where(sel, pass16, drop16)
            return carry
        lax.fori_loop(0, n_rb, body, 0)

    @pl.when(any_tie)
    def _():
        cnt_gt = cnt_above + count(lambda hi, lo: lo() > t_lo)
        n_take = (topk - cnt_gt).astype(F32)
        ltri = (lax.broadcasted_iota(I32, (KU_SEL, KU_SEL), 0)
                >= lax.broadcasted_iota(I32, (KU_SEL, KU_SEL), 1)).astype(BF16)

        def body(u, seen):
            k0 = pl.multiple_of(u * KU_SEL, KU_SEL)
            blk = ((hi_ref[pl.ds(k0, KU_SEL), :].astype(I32) << 16)
                   + (lo_ref[pl.ds(k0, KU_SEL), :].astype(I32) + 32768))
            eq = blk == thr
            eqf = eq.astype(F32)
            rank = _dot(ltri, eqf.astype(BF16)) + seen
            sel = jnp.logical_or(blk > thr, jnp.logical_and(eq, rank <= n_take))
            sel = jnp.logical_and(sel, blk != KEY_NEG_INF)
            mask_ref[pl.ds(k0, KU_SEL), :] = jnp.where(sel, 0.0, NEG).astype(BF16)
            return seen + jnp.sum(eqf, axis=0, keepdims=True)
        lax.fori_loop(0, n_keys // KU_SEL, body, jnp.zeros((1, tq), F32))

    def fill(r, carry):
        r0 = pl.multiple_of(r * tq, tq)
        mask_ref[pl.ds(r0, tq), :] = drop16[0:tq, :]
        return carry
    lax.fori_loop(i + 1, S // tq, fill, 0)


def _select_topk(qi, wi, kiT, topk):
    B, S, _ = qi.shape
    tq = TQ_SEL
    n_units, dims, unit = kiT.shape[1:]
    assert unit == KU_SEL == tq and n_units * unit == S
    return pl.pallas_call(
        functools.partial(_select_kernel, topk=topk),
        grid=(B, S // tq),
        in_specs=[
            pl.BlockSpec((None, tq, IDX_HEADS * HP), lambda b, i: (b, i, 0)),
            pl.BlockSpec((None, tq, LANES), lambda b, i: (b, i, 0)),
            pl.BlockSpec((None, n_units, dims, unit), lambda b, i: (b, 0, 0, 0)),
        ],
        out_specs=pl.BlockSpec((None, S, tq), lambda b, i: (b, 0, i)),
        out_shape=jax.ShapeDtypeStruct((B, S, S), BF16),
        scratch_shapes=[pltpu.VMEM((S + tq, tq), I16), pltpu.VMEM((S + tq, tq), I16),
                        pltpu.VMEM((tq, KU_SEL), F32), pltpu.VMEM((tq, KU_SEL), F32),
                        pltpu.VMEM((IDX_HEADS, tq, LANES), F32)],
        compiler_params=_cparams(("parallel", "parallel")),
        name="select_topk",
    )(qi, wi, kiT)


def _attend_kernel(ti_ref, tj_ref, safe_ref, *refs, heads, with_mask):
    if with_mask:
        qT_ref, k_ref, vT_ref, mask_ref, bias_ref, o_ref, acc_ref, s_ref, p_ref, m_ref, add_ref = refs
    else:
        qT_ref, k_ref, vT_ref, diag_ref, o_ref, acc_ref, s_ref, p_ref, m_ref = refs
    s_idx = pl.program_id(1)
    i = ti_ref[s_idx]
    j = tj_ref[s_idx]
    bounded = safe_ref[0] == 1
    tk = k_ref.shape[0]
    n_ch = tk // CH_ATT

    @pl.when(j == 0)
    def _():
        acc_ref[...] = jnp.zeros(acc_ref.shape, F32)
        m_ref[...] = jnp.full(m_ref.shape, NEG, F32)

    def logits(hh):
        s_ref[hh % 2] = _dot(k_ref[:, HP * hh:HP * (hh + 1)], qT_ref[HP * hh:HP * (hh + 1), :])

    def chunks():
        return [slice(CH_ATT * c, CH_ATT * (c + 1)) for c in range(n_ch)]

    def sweep_bounded(extra):
        logits(0)
        for hh in range(heads):
            buf = hh % 2
            if hh + 1 < heads:
                logits(hh + 1)
            for rows in chunks():
                t = s_ref[buf, rows, :]
                e = extra(hh, rows)
                if e is not None:
                    t = t + e
                p_ref[buf, rows, :] = jnp.exp2(t).astype(BF16)
            acc_ref[VP * hh:VP * (hh + 1), :] += _dot(vT_ref[VP * hh:VP * (hh + 1), :], p_ref[buf])

    def sweep_running_max(extra):
        for hh in range(heads):
            buf = hh % 2
            logits(hh)
            m_blk = jnp.full((SUBLANES, s_ref.shape[2]), NEG, F32)
            for rows in chunks():
                t = s_ref[buf, rows, :]
                e = extra(hh, rows)
                if e is not None:
                    t = t + e
                    s_ref[buf, rows, :] = t
                for r in range(CH_ATT // SUBLANES):
                    m_blk = jnp.maximum(m_blk, t[SUBLANES * r:SUBLANES * (r + 1), :])
            m_old = m_ref[hh:hh + 1, :]
            m_new = jnp.maximum(m_old, jnp.max(m_blk, axis=0, keepdims=True))
            m_ref[hh:hh + 1, :] = m_new
            for rows in chunks():
                p_ref[buf, rows, :] = jnp.exp2(s_ref[buf, rows, :] - m_new).astype(BF16)
            pv = _dot(vT_ref[VP * hh:VP * (hh + 1), :], p_ref[buf])
            acc_ref[VP * hh:VP * (hh + 1), :] = jnp.exp2(m_old - m_new) * acc_ref[VP * hh:VP * (hh + 1), :] + pv

    def both(extra):
        @pl.when(bounded)
        def _():
            sweep_bounded(extra)

        @pl.when(jnp.logical_not(bounded))
        def _():
            sweep_running_max(extra)

    if with_mask:
        add_ref[...] = mask_ref[...].astype(F32)

        @pl.when(j < i - 1)
        def _():
            both(lambda hh, rows: add_ref[rows, :])

        @pl.when(j >= i - 1)
        def _():
            both(lambda hh, rows: add_ref[rows, :] + bias_ref[hh, rows, :].astype(F32))
    else:
        @pl.when(j < i)
        def _():
            both(lambda hh, rows: None)

        @pl.when(j == i)
        def _():
            both(lambda hh, rows: diag_ref[rows, :])

    @pl.when(j == i)
    def _():
        dv = VP - BF16_ROWS
        for hh in range(heads):
            inv = 1.0 / acc_ref[VP * hh + dv:VP * hh + dv + 1, :]
            s_ref[0, dv * hh:dv * (hh + 1), :] = acc_ref[VP * hh:VP * hh + dv, :] * inv
        o_ref[...] = s_ref[0, 0:dv * heads, :].T.astype(o_ref.dtype)


def _pair_tables(S):
    n = S // TQ_ATT
    ti = np.array([i for i in range(n) for _ in range(i + 1)], np.int32)
    tj = np.array([j for i in range(n) for j in range(i + 1)], np.int32)
    return jnp.asarray(ti), jnp.asarray(tj)


def _attend(qT, k, vT, safe, extra_inputs, extra_specs, extra_scratch, with_mask, name):
    B, S, _ = k.shape
    tq, tk = TQ_ATT, TK_ATT
    assert tq == tk and tk >= B_V * A_HEADS
    ti, tj = _pair_tables(S)
    heads = A_HEADS
    kern = functools.partial(_attend_kernel, heads=heads, with_mask=with_mask)
    grid_spec = pltpu.PrefetchScalarGridSpec(
        num_scalar_prefetch=3,
        grid=(B, ti.shape[0]),
        in_specs=[
            pl.BlockSpec((None, heads * HP, tq), lambda b, s, ti, tj, sf: (b, 0, ti[s])),
            pl.BlockSpec((None, tk, heads * HP), lambda b, s, ti, tj, sf: (b, tj[s], 0)),
            pl.BlockSpec((None, heads * VP, tk), lambda b, s, ti, tj, sf: (b, 0, tj[s])),
        ] + extra_specs,
        out_specs=pl.BlockSpec((None, tq, heads * B_V), lambda b, s, ti, tj, sf: (b, ti[s], 0)),
        scratch_shapes=[pltpu.VMEM((heads * VP, tq), F32),
                        pltpu.VMEM((2, tk, tq), F32),
                        pltpu.VMEM((2, tk, tq), BF16),
                        pltpu.VMEM((heads, tq), F32)] + extra_scratch,
    )
    return pl.pallas_call(
        kern, grid_spec=grid_spec,
        out_shape=jax.ShapeDtypeStruct((B, S, heads * B_V), BF16),
        compiler_params=_cparams(("parallel", "arbitrary")),
        name=name,
    )(ti, tj, safe, qT, k, vT, *extra_inputs)


def _attend_a(qaT, ka, vaT, mask, bias_tiles, safe):
    tq, tk = TQ_ATT, TK_ATT
    specs = [
        pl.BlockSpec((None, tk, tq), lambda b, s, ti, tj, sf: (b, tj[s], ti[s])),
        pl.BlockSpec((A_HEADS, None, tk, tq),
                     lambda b, s, ti, tj, sf: (0, jnp.where(tj[s] == ti[s], 0, 1), 0, 0)),
    ]
    return _attend(qaT, ka, vaT, safe, (mask, bias_tiles), specs, [pltpu.VMEM((tk, tq), F32)],
                   True, "attend_a")


def _attend_b(qbT, kb, vbT, diag_tile, safe):
    tq, tk = TQ_ATT, TK_ATT
    specs = [pl.BlockSpec((tk, tq), lambda b, s, ti, tj, sf: (0, 0))]
    return _attend(qbT, kb, vbT, safe, (diag_tile,), specs, [], False, "attend_b")


def _merge_kernel(x_ref, ya_ref, yb_ref, g_ref, wga_ref, wgb_ref, bga_ref, bgb_ref,
                  wpa_ref, wpb_ref, wo_ref, o_ref):
    x = x_ref[...]
    h = (x * lax.rsqrt(jnp.mean(x * x, axis=-1, keepdims=True) + EPS) * g_ref[...]).astype(BF16)
    gate_a = jax.nn.sigmoid(_dot(h, wga_ref[...]) + bga_ref[...])
    gate_b = jax.nn.sigmoid(_dot(h, wgb_ref[...]) + bgb_ref[...])
    merged = gate_a * _dot(ya_ref[...], wpa_ref[...]) + gate_b * _dot(yb_ref[...], wpb_ref[...])
    o_ref[...] = x + _dot(merged.astype(BF16), wo_ref[...])


def _merge_out(x2, ya2, yb2, g, wga, wgb, bga, bgb, wpa, wpb, wo):
    R, D = x2.shape
    tm = TM_MERGE
    row = lambda w: pl.BlockSpec((tm, w), lambda i: (i, 0))
    full = lambda a: pl.BlockSpec(a.shape, lambda i: (0,) * a.ndim)
    consts = (g, wga, wgb, bga, bgb, wpa, wpb, wo)
    return pl.pallas_call(
        _merge_kernel,
        grid=(R // tm,),
        in_specs=[row(D), row(A_WIDTH), row(B_WIDTH)] + [full(a) for a in consts],
        out_specs=row(D),
        out_shape=jax.ShapeDtypeStruct((R, D), F32),
        compiler_params=_cparams(("parallel",)),
        name="merge_out",
    )(x2, ya2, yb2, *consts)


def _ffn_kernel(x_ref, xp_ref, g_ref, wuv_ref, wug_ref, cwv_ref, cwg_ref, cbv_ref, cbg_ref, wd_ref,
                o_ref, uv_ref, ug_ref, acc_ref, *, tiles_per_seq):
    i = pl.program_id(0)
    f = pl.program_id(1)
    tm = x_ref.shape[0]
    halo = SUBLANES
    g = g_ref[...]

    def normed(v):
        return (v * lax.rsqrt(jnp.mean(v * v, axis=-1, keepdims=True) + EPS) * g).astype(BF16)

    h = normed(x_ref[...])
    keep = jnp.where(i % tiles_per_seq == 0, 0.0, 1.0)
    hp = normed(xp_ref[...])

    def conv(u_ref, w_ref, cw_ref, cb_ref):
        u_ref[0:halo, :] = _dot(hp, w_ref[...]) * keep
        u_ref[halo:halo + tm, :] = _dot(h, w_ref[...])
        out = cb_ref[...]
        for t in range(CONV_W):
            lo = halo - (CONV_W - 1) + t
            out = out + cw_ref[t:t + 1, :] * u_ref[lo:lo + tm, :]
        return out

    val = conv(uv_ref, wuv_ref, cwv_ref, cbv_ref)
    gat = conv(ug_ref, wug_ref, cwg_ref, cbg_ref)
    act = (gat * jax.nn.sigmoid(gat) * val).astype(BF16)
    part = _dot(act, wd_ref[...])

    @pl.when(f == 0)
    def _():
        acc_ref[...] = part

    @pl.when(f > 0)
    def _():
        acc_ref[...] = acc_ref[...] + part

    @pl.when(f == pl.num_programs(1) - 1)
    def _():
        o_ref[...] = x_ref[...] + acc_ref[...]


def _conv_ffn(x2, S, g, wu, cw, cb, wd):
    R, D = x2.shape
    tm, tf = TM_FFN, TF_FFN
    nf = D_FF // tf
    halo_blocks = tm // SUBLANES
    kern = functools.partial(_ffn_kernel, tiles_per_seq=S // tm)
    return pl.pallas_call(
        kern,
        grid=(R // tm, nf),
        in_specs=[
            pl.BlockSpec((tm, D), lambda i, f: (i, 0)),
            pl.BlockSpec((SUBLANES, D), lambda i, f: (jnp.maximum(i * halo_blocks - 1, 0), 0)),
            pl.BlockSpec((1, D), lambda i, f: (0, 0)),
            pl.BlockSpec((D, tf), lambda i, f: (0, f)),
            pl.BlockSpec((D, tf), lambda i, f: (0, nf + f)),
            pl.BlockSpec((CONV_W, tf), lambda i, f: (0, f)),
            pl.BlockSpec((CONV_W, tf), lambda i, f: (0, nf + f)),
            pl.BlockSpec((1, tf), lambda i, f: (0, f)),
            pl.BlockSpec((1, tf), lambda i, f: (0, nf + f)),
            pl.BlockSpec((tf, D), lambda i, f: (f, 0)),
        ],
        out_specs=pl.BlockSpec((tm, D), lambda i, f: (i, 0)),
        out_shape=jax.ShapeDtypeStruct((R, D), F32),
        scratch_shapes=[pltpu.VMEM((tm + SUBLANES, tf), F32), pltpu.VMEM((tm + SUBLANES, tf), F32),
                        pltpu.VMEM((tm, D), F32)],
        compiler_params=_cparams(("parallel", "arbitrary")),
        name="conv_ffn",
    )(x2, x2, g, wu, wu, cw, cw, cb, cb, wd)


def _t5_bucket(rel):
    nb = REL_BUCKETS // 2
    max_exact = nb // 2
    side = jnp.where(rel > 0, nb, 0)
    n = jnp.abs(rel)
    nf = jnp.maximum(n, 1).astype(F32)
    large = max_exact + (jnp.log(nf / max_exact) / math.log(REL_MAX_DIST / max_exact)
                         * (nb - max_exact)).astype(I32)
    large = jnp.minimum(large, nb - 1)
    return side + jnp.where(n < max_exact, n, large)


def _bias_tiles(rel_bias):
    tq, tk = TQ_ATT, TK_ATT
    assert tq == tk and tk >= REL_MAX_DIST
    kk = jnp.arange(tk, dtype=I32)[:, None]
    qq = jnp.arange(tq, dtype=I32)[None, :]
    bucket = _t5_bucket(jnp.stack([kk - qq, kk - qq - tk])).reshape(1, 2 * tk * tq)
    onehot = (jnp.arange(REL_BUCKETS, dtype=I32)[:, None] == bucket).astype(F32)
    far = rel_bias[_t5_bucket(jnp.asarray(-REL_MAX_DIST, I32))]
    table = ((rel_bias - far[None, :]) * LOG2E).T
    tiles = jnp.dot(table, onehot, precision=lax.Precision.HIGHEST)
    return tiles.reshape(A_HEADS, 2, tk, tq).astype(BF16), jnp.max(jnp.abs(table), axis=1)


def _diag_tile():
    kk = np.arange(TK_ATT)[:, None] // CHUNK
    qq = np.arange(TQ_ATT)[None, :] // CHUNK
    return jnp.asarray(np.where(kk <= qq, 0.0, NEG).astype(np.float32))


def _rope_tables(S):
    half = B_ROPE // 2
    inv = ROPE_BASE ** (-jnp.arange(half, dtype=F32) / half)
    ang = jnp.arange(S, dtype=I32).astype(F32)[:, None] * inv[None, :]
    cos, sin = jnp.cos(ang), jnp.sin(ang)
    ones = jnp.ones((S, B_NOPE), F32)
    zeros_n = jnp.zeros((S, B_NOPE), F32)
    zeros_p = jnp.zeros((S, HP - B_QK), F32)
    cos_t = jnp.concatenate([ones, cos, cos, zeros_p], axis=1)
    sin_t = jnp.concatenate([zeros_n, -sin, sin, zeros_p], axis=1)
    return cos_t, sin_t


def _pad_cols(w, width):
    return jnp.pad(w, ((0, 0), (0, width - w.shape[1])))


def _pad_heads(w, heads, width):
    rows = w.shape[0]
    w = w.reshape(rows, heads, -1)
    return jnp.pad(w, ((0, 0), (0, 0), (0, width - w.shape[2]))).reshape(rows, heads * width)


def _swap_halves(w):
    half = w.shape[-1] // 2
    return jnp.concatenate([w[..., half:], w[..., :half]], axis=-1)


def _layer_weights(l, w_in, b_w_uq, b_w_ukv, b_q_norm, b_k_norm, a_q_norm, a_k_norm, bias_max):
    w = w_in[l]
    o = np.cumsum([0, A_WIDTH, A_WIDTH, A_WIDTH, IDX_HEADS * IDX_DIM, IDX_DIM, IDX_HEADS,
                   B_Q_LORA, B_KV_LORA, B_ROPE, D_MODEL, D_MODEL])
    seg = [w[:, o[t]:o[t + 1]] for t in range(11)]
    zn = jnp.zeros((D_MODEL, B_NOPE), F32)
    kr = seg[8]
    w1 = jnp.concatenate([
        _pad_heads(seg[0], A_HEADS, HP), _pad_heads(seg[1], A_HEADS, HP), seg[2],
        _pad_heads(seg[3], IDX_HEADS, HP),
        _pad_cols(seg[4], LANES), _pad_cols(seg[5], LANES), seg[6], seg[7],
        _pad_cols(jnp.concatenate([zn, kr], axis=1), HP),
        _pad_cols(jnp.concatenate([zn, _swap_halves(kr)], axis=1), HP),
    ], axis=1).astype(BF16)
    assert w1.shape[1] == _C_END

    uq = b_w_uq[l].reshape(B_Q_LORA, B_HEADS, B_QK)
    zq = jnp.zeros((B_Q_LORA, B_HEADS, B_NOPE), F32)
    pq = jnp.zeros((B_Q_LORA, B_HEADS, HP - B_QK), F32)
    wqm = jnp.concatenate([uq, pq], axis=-1).reshape(B_Q_LORA, B_HEADS * HP).astype(BF16)
    wqs = jnp.concatenate([zq, _swap_halves(uq[..., B_NOPE:]), pq], axis=-1)
    wqs = wqs.reshape(B_Q_LORA, B_HEADS * HP).astype(BF16)

    ukv = b_w_ukv[l].reshape(B_KV_LORA, B_HEADS, B_NOPE + B_V)
    pk = jnp.zeros((B_KV_LORA, B_HEADS, HP - B_NOPE), F32)
    wkm = jnp.concatenate([ukv[..., :B_NOPE], pk], axis=-1).reshape(B_KV_LORA, B_HEADS * HP).astype(BF16)
    wv = ukv[..., B_NOPE:].reshape(B_KV_LORA, B_WIDTH).astype(BF16)

    gqa = _pad_cols(a_q_norm[l][None, :], HP)
    gka = _pad_cols(a_k_norm[l][None, :], HP)
    gqb = _pad_cols(b_q_norm[l][None, :], HP)
    gkb = _pad_cols(b_k_norm[l][None, :], HP)

    ka_bound = math.sqrt(A_HEAD_DIM) * jnp.max(jnp.abs(a_k_norm[l]))
    kb_bound = math.sqrt(B_QK) * jnp.max(jnp.abs(b_k_norm[l]))
    bnd = jnp.zeros((SUBLANES, LANES), F32)
    bnd = bnd.at[0, :].set(ka_bound).at[1, :].set(kb_bound).at[2, :A_HEADS].set(bias_max)
    qa_bound = math.sqrt(A_HEAD_DIM) * jnp.max(jnp.abs(a_q_norm[l])) * (A_HEAD_DIM ** -0.5 * LOG2E)
    qb_bound = math.sqrt(B_QK) * jnp.max(jnp.abs(b_q_norm[l])) * (B_QK ** -0.5 * LOG2E)
    range_a = (qa_bound * ka_bound + jnp.max(bias_max)) * BOUND_SLACK + jnp.max(bias_max)
    range_b = qb_bound * kb_bound * BOUND_SLACK
    safe_a = (range_a <= MAX_LOG2_RANGE).astype(I32).reshape(1)
    safe_b = (range_b <= MAX_LOG2_RANGE).astype(I32).reshape(1)
    return (w1, wqm, wqs, wkm, wv, gqa, gka, gqb, gkb, bnd, safe_a, safe_b,
            seg[9].astype(BF16), seg[10].astype(BF16))


def kernel(x, rel_bias, norm_mix, w_in, a_q_norm, a_k_norm, b_cq_norm, b_ckv_norm, b_w_uq, b_w_ukv,
           b_q_norm, b_k_norm, w_proj_a, w_proj_b, b_gate, w_out, norm_ffn, w_up, conv_w, conv_b, w_down):
    B, S, D = x.shape
    assert D == D_MODEL and S % TQ_ATT == 0 and S % TM_FFN == 0
    topk = min(TOPK_MAX, S // 4)
    assert TQ_SEL >= topk
    depth = w_in.shape[0]

    cos_t, sin_t = _rope_tables(S)
    bias_tiles, bias_max = _bias_tiles(rel_bias)
    diag_tile = _diag_tile()

    for l in range(depth):
        (w1, wqm, wqs, wkm, wv, gqa, gka, gqb, gkb, bnd, safe_a, safe_b, wga, wgb) = _layer_weights(
            l, w_in, b_w_uq, b_w_ukv, b_q_norm, b_k_norm, a_q_norm, a_k_norm, bias_max)
        qaT, ka, vaT, qi, kiT, wi, qbT, kb, vbT = _token_prep(
            x, norm_mix[l][None, :], w1, wqm, wqs, wkm, wv, gqa, gka,
            b_cq_norm[l][None, :], b_ckv_norm[l][None, :], gqb, gkb, bnd,
            cos_t, sin_t)
        mask = _select_topk(qi, wi, kiT, topk)
        y_a = _attend_a(qaT, ka, vaT, mask, bias_tiles, safe_a)
        y_b = _attend_b(qbT, kb, vbT, diag_tile, safe_b)
        x2 = _merge_out(
            x.reshape(B * S, D), y_a.reshape(B * S, A_WIDTH), y_b.reshape(B * S, B_WIDTH),
            norm_mix[l][None, :], wga, wgb, b_gate[l][None, :D_MODEL], b_gate[l][None, D_MODEL:],
            w_proj_a[l].astype(BF16), w_proj_b[l].astype(BF16), w_out[l].astype(BF16))
        x2 = _conv_ffn(x2, S, norm_ffn[l][None, :], w_up[l].astype(BF16), conv_w[l], conv_b[l][None, :],
                       w_down[l].astype(BF16))
        x = x2.reshape(B, S, D)
    return x
```

```python
import functools
import math

import numpy as np
import jax
import jax.numpy as jnp
from jax import lax
from jax.experimental import pallas as pl
from jax.experimental.pallas import tpu as pltpu

F32 = jnp.float32
BF16 = jnp.bfloat16
I32 = jnp.int32
I16 = jnp.int16

D_MODEL = 1024
CHUNK = 64
CHUNK_SHIFT = 6
assert 1 << CHUNK_SHIFT == CHUNK
A_HEADS = 8
A_HEAD_DIM = 64
A_WIDTH = A_HEADS * A_HEAD_DIM
IDX_HEADS = 8
IDX_DIM = 64
TOPK_MAX = 256
B_HEADS = 8
B_Q_LORA = 384
B_KV_LORA = 256
B_NOPE = 64
B_ROPE = 32
B_QK = B_NOPE + B_ROPE
B_V = 64
B_WIDTH = B_HEADS * B_V
ROPE_BASE = 10000.0
REL_BUCKETS = 32
REL_MAX_DIST = 128
D_FF = 2816
CONV_W = 3
EPS = 1e-6

LANES = 128
SUBLANES = 8
BF16_ROWS = 16
VMEM_LIMIT = 56 * 1024 * 1024

TM_PREP = 256
TQ_SEL = 256
KU_SEL = 256
CH_SEL = 64
TQ_ATT = 512
TK_ATT = 512
NSUB_ATT = 2
CH_ATT = 64
TM_MERGE = 256
TM_FFN = 512
TF_FFN = 1408

HP = 128
VP = B_V + BF16_ROWS
NEG = -1e30
KEY_NEG_INF = -2139095041
INT_MIN = -2147483648
LOG2E = 1.4426950408889634
BOUND_SLACK = 1.01
MAX_LOG2_RANGE = 60.0


def _cparams(sem):
    return pltpu.CompilerParams(dimension_semantics=sem, vmem_limit_bytes=VMEM_LIMIT)


def _dot(a, b):
    return jnp.dot(a, b, preferred_element_type=F32)


def _dot_exact(a, b):
    return jnp.dot(a, b, preferred_element_type=F32, precision=lax.Precision.HIGHEST)


_C_QA = 0
_C_KA = _C_QA + A_HEADS * HP
_C_VA = _C_KA + A_HEADS * HP
_C_QI = _C_VA + A_WIDTH
_C_KI = _C_QI + IDX_HEADS * HP
_C_WI = _C_KI + LANES
_C_CQ = _C_WI + LANES
_C_CKV = _C_CQ + B_Q_LORA
_C_KR = _C_CKV + B_KV_LORA
_C_KRS = _C_KR + HP
_C_END = _C_KRS + HP


def _token_prep_kernel(x_ref, g_ref, w1_ref, wqm_ref, wqs_ref, wkm_ref, wv_ref,
                       gqa_ref, gka_ref, gcq_ref, gckv_ref, gqb_ref, gkb_ref, bnd_ref,
                       cos_ref, sin_ref,
                       qaT_ref, ka_ref, vaT_ref, qi_ref, kiT_ref, wi_ref,
                       qbT_ref, kb_ref, vbT_ref):
    x = x_ref[...]
    tm = x.shape[0]
    h = (x * lax.rsqrt(jnp.mean(x * x, axis=-1, keepdims=True) + EPS) * g_ref[...]).astype(BF16)

    def proj(lo, hi):
        return _dot(h, w1_ref[:, lo:hi])

    def head_norm(zh, gain):
        ss = jnp.sum(zh * zh, axis=-1, keepdims=True) * (1.0 / A_HEAD_DIM)
        return zh * lax.rsqrt(ss + EPS) * gain

    ones_rows = jnp.where(lax.broadcasted_iota(I32, (BF16_ROWS, tm), 0) == 0, 1.0, 0.0).astype(BF16)

    def store_vT(ref, v):
        vT = v.T
        for hh in range(A_HEADS):
            ref[VP * hh:VP * hh + B_V, :] = vT[B_V * hh:B_V * (hh + 1), :].astype(BF16)
            ref[VP * hh + B_V:VP * (hh + 1), :] = ones_rows

    spare_a = jnp.where(lax.broadcasted_iota(I32, (1, HP), 1) == A_HEAD_DIM, 1.0, 0.0)
    qa = proj(_C_QA, _C_KA)
    ka = proj(_C_KA, _C_VA)
    gqa = gqa_ref[...] * (A_HEAD_DIM ** -0.5 * LOG2E)
    gka = gka_ref[...]
    for hh in range(A_HEADS):
        blk = slice(HP * hh, HP * (hh + 1))
        qh = head_norm(qa[:, blk], gqa)
        qnorm = jnp.sqrt(jnp.sum(qh * qh, axis=-1, keepdims=True))
        m_a = (qnorm * bnd_ref[0:1, :] + bnd_ref[2:3, hh:hh + 1]) * BOUND_SLACK
        qaT_ref[blk, :] = (qh - m_a * spare_a).T.astype(BF16)
        ka_ref[:, blk] = (head_norm(ka[:, blk], gka) + spare_a).astype(BF16)
    store_vT(vaT_ref, proj(_C_VA, _C_QI))

    qi_ref[...] = proj(_C_QI, _C_KI).astype(BF16)
    kiT_ref[...] = proj(_C_KI, _C_WI).T.astype(BF16)
    wi_ref[...] = proj(_C_WI, _C_CQ) * ((IDX_HEADS ** -0.5) * (IDX_DIM ** -0.5))

    cos = cos_ref[...]
    sin = sin_ref[...]
    spare = jnp.where(lax.broadcasted_iota(I32, (1, HP), 1) == B_QK, 1.0, 0.0)
    cq = proj(_C_CQ, _C_CKV)
    cqn = (cq * lax.rsqrt(jnp.mean(cq * cq, axis=-1, keepdims=True) + EPS) * gcq_ref[...]).astype(BF16)
    qm = _dot(cqn, wqm_ref[...])
    qs = _dot(cqn, wqs_ref[...])
    gqb = gqb_ref[...] * (B_QK ** -0.5 * LOG2E)
    kbound = bnd_ref[1:2, :]
    for hh in range(B_HEADS):
        blk = slice(HP * hh, HP * (hh + 1))
        qh = qm[:, blk] * cos + qs[:, blk] * sin
        ss = jnp.sum(qh * qh, axis=-1, keepdims=True) * (1.0 / B_QK)
        qh = qh * lax.rsqrt(ss + EPS) * gqb
        m_b = jnp.sqrt(jnp.sum(qh * qh, axis=-1, keepdims=True)) * kbound * BOUND_SLACK
        qbT_ref[blk, :] = (qh - m_b * spare).T.astype(BF16)

    ckv = proj(_C_CKV, _C_KR)
    ckvn = (ckv * lax.rsqrt(jnp.mean(ckv * ckv, axis=-1, keepdims=True) + EPS) * gckv_ref[...]).astype(BF16)
    km = _dot(ckvn, wkm_ref[...])
    store_vT(vbT_ref, _dot(ckvn, wv_ref[...]))
    krot = proj(_C_KR, _C_KRS) * cos + proj(_C_KRS, _C_END) * sin
    gkb = gkb_ref[...]
    for hh in range(B_HEADS):
        blk = slice(HP * hh, HP * (hh + 1))
        kh = km[:, blk] + krot
        ss = jnp.sum(kh * kh, axis=-1, keepdims=True) * (1.0 / B_QK)
        kb_ref[:, blk] = (kh * lax.rsqrt(ss + EPS) * gkb + spare).astype(BF16)


def _token_prep(x, g, w1, wqm, wqs, wkm, wv, gqa, gka, gcq, gckv, gqb, gkb, bnd,
                cos_t, sin_t):
    B, S, D = x.shape
    tm = TM_PREP
    nt = S // tm
    row3 = lambda w: pl.BlockSpec((None, tm, w), lambda b, i: (b, i, 0))
    colT = lambda r: pl.BlockSpec((None, r, tm), lambda b, i: (b, 0, i))
    full = lambda a: pl.BlockSpec(a.shape, lambda b, i: (0,) * a.ndim)
    tab = pl.BlockSpec((tm, LANES), lambda b, i: (i, 0))
    consts = (g, w1, wqm, wqs, wkm, wv, gqa, gka, gcq, gckv, gqb, gkb, bnd)
    out_shape = (
        jax.ShapeDtypeStruct((B, A_HEADS * HP, S), BF16),
        jax.ShapeDtypeStruct((B, S, A_HEADS * HP), BF16),
        jax.ShapeDtypeStruct((B, A_HEADS * VP, S), BF16),
        jax.ShapeDtypeStruct((B, S, IDX_HEADS * HP), BF16),
        jax.ShapeDtypeStruct((B, nt, LANES, tm), BF16),
        jax.ShapeDtypeStruct((B, S, LANES), F32),
        jax.ShapeDtypeStruct((B, B_HEADS * HP, S), BF16),
        jax.ShapeDtypeStruct((B, S, B_HEADS * HP), BF16),
        jax.ShapeDtypeStruct((B, B_HEADS * VP, S), BF16),
    )
    kiT_spec = pl.BlockSpec((None, None, LANES, tm), lambda b, i: (b, i, 0, 0))
    out_specs = (colT(A_HEADS * HP), row3(A_HEADS * HP), colT(A_HEADS * VP), row3(IDX_HEADS * HP),
                 kiT_spec, row3(LANES), colT(B_HEADS * HP), row3(B_HEADS * HP), colT(B_HEADS * VP))
    return pl.pallas_call(
        _token_prep_kernel,
        grid=(B, nt),
        in_specs=[row3(D)] + [full(a) for a in consts] + [tab, tab],
        out_specs=out_specs,
        out_shape=out_shape,
        compiler_params=_cparams(("parallel", "parallel")),
        name="token_prep",
    )(x, *consts, cos_t, sin_t)


def _tree_sum(parts):
    while len(parts) > 1:
        nxt = [parts[t] + parts[t + 1] for t in range(0, len(parts) - 1, 2)]
        if len(parts) % 2:
            nxt.append(parts[-1])
        parts = nxt
    return parts[0]


def _select_kernel(qi_ref, wi_ref, kiT_ref, mask_ref, hi_ref, lo_ref, acca_ref, accb_ref, wb_ref, *, topk):
    S = mask_ref.shape[0]
    tq = TQ_SEL
    i = pl.program_id(1)
    n_keys = (i + 1) * tq
    q_chunk = (i * tq + lax.broadcasted_iota(I32, (CH_SEL, tq), 1)) >> CHUNK_SHIFT
    k_off = lax.broadcasted_iota(I32, (CH_SEL, tq), 0)

    for hh in range(IDX_HEADS):
        wb_ref[hh] = jnp.broadcast_to(wi_ref[:, hh:hh + 1], (tq, LANES))

    n_units = n_keys // KU_SEL
    accb_ref[...] = jnp.zeros(accb_ref.shape, F32)

    def heads(u, dst_ref):
        kT = kiT_ref[jnp.minimum(u, n_units - 1)]
        for hh in range(IDX_HEADS):
            d = _dot(qi_ref[:, HP * hh:HP * (hh + 1)], kT)
            for c in range(tq // CH_SEL):
                rows = slice(CH_SEL * c, CH_SEL * (c + 1))
                w = wb_ref[hh, rows, :]
                for lanes in (slice(0, LANES), slice(LANES, 2 * LANES)):
                    t = jnp.maximum(d[rows, lanes], 0.0) * w
                    if hh == 0:
                        dst_ref[rows, lanes] = t
                    else:
                        dst_ref[rows, lanes] += t

    def convert(src_ref, u):
        k0 = pl.multiple_of(jnp.maximum(u, 0) * KU_SEL, KU_SEL)
        accT = src_ref[...].T
        for c in range(KU_SEL // CH_SEL):
            acc = accT[CH_SEL * c:CH_SEL * (c + 1), :] + 0.0
            adm = ((k0 + CH_SEL * c + k_off) >> CHUNK_SHIFT) <= q_chunk
            acc = jnp.where(adm, acc, -jnp.inf)
            bits = pltpu.bitcast(acc, I32)
            key = bits ^ ((bits >> 31) & 0x7FFFFFFF)
            r0 = k0 + CH_SEL * c
            hi_ref[pl.ds(r0, CH_SEL), :] = (key >> 16).astype(I16)
            lo_ref[pl.ds(r0, CH_SEL), :] = (key ^ 0x8000).astype(I16)

    def unit_pair(t, carry):
        heads(2 * t, acca_ref)
        convert(accb_ref, 2 * t - 1)
        heads(2 * t + 1, accb_ref)
        convert(acca_ref, 2 * t)
        return carry

    lax.fori_loop(0, n_units // 2 + 1, unit_pair, 0)

    rb = 2 * tq
    n_rb = (i + 2) // 2
    lowest16 = jnp.full((rb, tq), -32768, I16)

    @pl.when((i + 1) % 2 == 1)
    def _():
        hi_ref[pl.ds(pl.multiple_of(n_keys, tq), tq), :] = lowest16[0:tq, :]
        lo_ref[pl.ds(pl.multiple_of(n_keys, tq), tq), :] = lowest16[0:tq, :]

    one16 = jnp.ones((rb, tq), I16)
    zero16 = jnp.zeros((rb, tq), I16)
    rows16 = rb // BF16_ROWS

    def count(pred):
        def body(r, acc):
            r0 = pl.multiple_of(r * rb, rb)
            ind = jnp.where(pred(lambda: hi_ref[pl.ds(r0, rb), :], lambda: lo_ref[pl.ds(r0, rb), :]),
                            one16, zero16)
            return acc + _tree_sum([ind[BF16_ROWS * s:BF16_ROWS * (s + 1), :] for s in range(rows16)])
        acc = lax.fori_loop(0, n_rb, body, jnp.zeros((BF16_ROWS, tq), I16))
        return jnp.sum(acc.astype(I32), axis=0, keepdims=True)

    def radix16(pred_of, need, count_at_zero):
        def step(t, carry):
            prefix, cnt_acc = carry
            cand = prefix | lax.shift_left(jnp.int32(1), 15 - t)
            cnt = count(pred_of((cand - 32768).astype(I16)))
            ok = cnt >= need
            return jnp.where(ok, cand, prefix), jnp.where(ok, cnt, cnt_acc)
        prefix, cnt = lax.fori_loop(0, 16, step, (jnp.zeros((1, tq), I32), count_at_zero))
        return prefix - 32768, cnt

    t_hi32, cnt_hi_ge = radix16(lambda c: (lambda hi, lo: hi() >= c), topk,
                                jnp.full((1, tq), n_keys, I32))
    t_hi = t_hi32.astype(I16)

    def above_and_park(r, acc):
        r0 = pl.multiple_of(r * rb, rb)
        blk_hi = hi_ref[pl.ds(r0, rb), :]
        lo_ref[pl.ds(r0, rb), :] = jnp.where(blk_hi == t_hi, lo_ref[pl.ds(r0, rb), :], lowest16)
        ind = jnp.where(blk_hi > t_hi, one16, zero16)
        return acc + _tree_sum([ind[BF16_ROWS * s:BF16_ROWS * (s + 1), :] for s in range(rows16)])

    cnt_above = jnp.sum(lax.fori_loop(0, n_rb, above_and_park, jnp.zeros((BF16_ROWS, tq), I16)).astype(I32),
                        axis=0, keepdims=True)
    t_lo32, cnt_lo_ge = radix16(lambda c: (lambda hi, lo: lo() >= c),
                                topk - cnt_above, cnt_hi_ge - cnt_above)
    t_lo = t_lo32.astype(I16)
    thr = (t_hi32 << 16) + (t_lo32 + 32768)
    cnt_ge = cnt_above + cnt_lo_ge
    tied = jnp.logical_and(cnt_ge > topk, thr > KEY_NEG_INF)
    any_tie = jnp.max(tied.astype(I32)) > 0
    thr_adm = jnp.maximum(thr, KEY_NEG_INF + 1)
    a_hi = (thr_adm >> 16).astype(I16)
    a_lo = ((thr_adm & 0xFFFF) - 32768).astype(I16)
    pass16 = jnp.zeros((rb, tq), BF16)
    drop16 = jnp.full((rb, tq), NEG, BF16)

    @pl.when(jnp.logical_not(any_tie))
    def _():
        def body(r, carry):
            r0 = pl.multiple_of(r * rb, rb)
            hi = hi_ref[pl.ds(r0, rb), :]
            lo = lo_ref[pl.ds(r0, rb), :]
            sel = jnp.logical_or(hi > a_hi, jnp.logical_and(hi == a_hi, lo >= a_lo))
            mask_ref[pl.ds(r0, rb), :] = jnp.where(sel, pass16, drop16)
            return carry
        lax.fori_loop(0, n_rb, body, 0)

    @pl.when(any_tie)
    def _():
        cnt_gt = cnt_above + count(lambda hi, lo: lo() > t_lo)
        n_take = (topk - cnt_gt).astype(F32)
        ltri = (lax.broadcasted_iota(I32, (KU_SEL, KU_SEL), 0)
                >= lax.broadcasted_iota(I32, (KU_SEL, KU_SEL), 1)).astype(BF16)

        def body(u, seen):
            k0 = pl.multiple_of(u * KU_SEL, KU_SEL)
            blk = ((hi_ref[pl.ds(k0, KU_SEL), :].astype(I32) << 16)
                   + (lo_ref[pl.ds(k0, KU_SEL), :].astype(I32) + 32768))
            eq = blk == thr
            eqf = eq.astype(F32)
            rank = _dot(ltri, eqf.astype(BF16)) + seen
            sel = jnp.logical_or(blk > thr, jnp.logical_and(eq, rank <= n_take))
            sel = jnp.logical_and(sel, blk != KEY_NEG_INF)
            mask_ref[pl.ds(k0, KU_SEL), :] = jnp.where(sel, 0.0, NEG).astype(BF16)
            return seen + jnp.sum(eqf, axis=0, keepdims=True)
        lax.fori_loop(0, n_keys // KU_SEL, body, jnp.zeros((1, tq), F32))

    def fill(r, carry):
        r0 = pl.multiple_of(r * tq, tq)
        mask_ref[pl.ds(r0, tq), :] = drop16[0:tq, :]
        return carry
    lax.fori_loop(i + 1, S // tq, fill, 0)


def _select_topk(qi, wi, kiT, topk):
    B, S, _ = qi.shape
    tq = TQ_SEL
    n_units, dims, unit = kiT.shape[1:]
    assert unit == KU_SEL == tq and n_units * unit == S
    return pl.pallas_call(
        functools.partial(_select_kernel, topk=topk),
        grid=(B, S // tq),
        in_specs=[
            pl.BlockSpec((None, tq, IDX_HEADS * HP), lambda b, i: (b, i, 0)),
            pl.BlockSpec((None, tq, LANES), lambda b, i: (b, i, 0)),
            pl.BlockSpec((None, n_units, dims, unit), lambda b, i: (b, 0, 0, 0)),
        ],
        out_specs=pl.BlockSpec((None, S, tq), lambda b, i: (b, 0, i)),
        out_shape=jax.ShapeDtypeStruct((B, S, S), BF16),
        scratch_shapes=[pltpu.VMEM((S + tq, tq), I16), pltpu.VMEM((S + tq, tq), I16),
                        pltpu.VMEM((tq, KU_SEL), F32), pltpu.VMEM((tq, KU_SEL), F32),
                        pltpu.VMEM((IDX_HEADS, tq, LANES), F32)],
        compiler_params=_cparams(("parallel", "parallel")),
        name="select_topk",
    )(qi, wi, kiT)


def _attend_kernel(ti_ref, tj_ref, safe_ref, *refs, heads, with_mask):
    if with_mask:
        qT_ref, k_ref, vT_ref, mask_ref, bias_ref, o_ref, acc_ref, s_ref, p_ref, m_ref, add_ref = refs
    else:
        qT_ref, k_ref, vT_ref, diag_ref, o_ref, acc_ref, s_ref, p_ref, m_ref = refs
    s_idx = pl.program_id(1)
    i = ti_ref[s_idx]
    jj = tj_ref[s_idx]
    bounded = safe_ref[0] == 1
    tk = TK_ATT
    n_ch = tk // CH_ATT

    @pl.when(jj == 0)
    def _():
        acc_ref[...] = jnp.zeros(acc_ref.shape, F32)
        m_ref[...] = jnp.full(m_ref.shape, NEG, F32)

    def chunks():
        return [slice(CH_ATT * c, CH_ATT * (c + 1)) for c in range(n_ch)]

    def key_block(sb, carry):
        j = jj * NSUB_ATT + sb
        k0 = pl.multiple_of(sb * tk, tk)

        def logits(hh):
            s_ref[hh % 2] = _dot(k_ref[pl.ds(k0, tk), HP * hh:HP * (hh + 1)],
                                 qT_ref[HP * hh:HP * (hh + 1), :])

        def pv(hh, buf):
            return _dot(vT_ref[VP * hh:VP * (hh + 1), pl.ds(k0, tk)], p_ref[buf])

        def sweep_bounded(extra):
            logits(0)
            for hh in range(heads):
                buf = hh % 2
                if hh + 1 < heads:
                    logits(hh + 1)
                for rows in chunks():
                    t = s_ref[buf, rows, :]
                    e = extra(hh, rows)
                    if e is not None:
                        t = t + e
                    p_ref[buf, rows, :] = jnp.exp2(t).astype(BF16)
                acc_ref[VP * hh:VP * (hh + 1), :] += pv(hh, buf)

        def sweep_running_max(extra):
            for hh in range(heads):
                buf = hh % 2
                logits(hh)
                m_blk = jnp.full((SUBLANES, s_ref.shape[2]), NEG, F32)
                for rows in chunks():
                    t = s_ref[buf, rows, :]
                    e = extra(hh, rows)
                    if e is not None:
                        t = t + e
                        s_ref[buf, rows, :] = t
                    for r in range(CH_ATT // SUBLANES):
                        m_blk = jnp.maximum(m_blk, t[SUBLANES * r:SUBLANES * (r + 1), :])
                m_old = m_ref[hh:hh + 1, :]
                m_new = jnp.maximum(m_old, jnp.max(m_blk, axis=0, keepdims=True))
                m_ref[hh:hh + 1, :] = m_new
                for rows in chunks():
                    p_ref[buf, rows, :] = jnp.exp2(s_ref[buf, rows, :] - m_new).astype(BF16)
                acc_ref[VP * hh:VP * (hh + 1), :] = (
                    jnp.exp2(m_old - m_new) * acc_ref[VP * hh:VP * (hh + 1), :] + pv(hh, buf))

        def both(extra):
            @pl.when(bounded)
            def _():
                sweep_bounded(extra)

            @pl.when(jnp.logical_not(bounded))
            def _():
                sweep_running_max(extra)

        if with_mask:
            @pl.when(j <= i)
            def _():
                add_ref[...] = mask_ref[pl.ds(k0, tk), :].astype(F32)

            @pl.when(j < i - 1)
            def _():
                both(lambda hh, rows: add_ref[rows, :])

            @pl.when(jnp.logical_and(j >= i - 1, j <= i))
            def _():
                near = i - j
                both(lambda hh, rows: add_ref[rows, :] + bias_ref[hh, near, rows, :].astype(F32))
        else:
            @pl.when(j < i)
            def _():
                both(lambda hh, rows: None)

            @pl.when(j == i)
            def _():
                both(lambda hh, rows: diag_ref[rows, :])
        return carry

    lax.fori_loop(0, NSUB_ATT, key_block, 0)

    @pl.when(jj == i // NSUB_ATT)
    def _():
        dv = VP - BF16_ROWS
        for hh in range(heads):
            inv = 1.0 / acc_ref[VP * hh + dv:VP * hh + dv + 1, :]
            s_ref[0, dv * hh:dv * (hh + 1), :] = acc_ref[VP * hh:VP * hh + dv, :] * inv
        o_ref[...] = s_ref[0, 0:dv * heads, :].T.astype(o_ref.dtype)


def _pair_tables(S):
    n = S // TQ_ATT
    ti = np.array([i for i in range(n) for _ in range(i // NSUB_ATT + 1)], np.int32)
    tj = np.array([j for i in range(n) for j in range(i // NSUB_ATT + 1)], np.int32)
    return jnp.asarray(ti), jnp.asarray(tj)


def _attend(qT, k, vT, safe, extra_inputs, extra_specs, extra_scratch, with_mask, name):
    B, S, _ = k.shape
    tq, tk = TQ_ATT, TK_ATT
    tg = tk * NSUB_ATT
    assert tq == tk and tk >= B_V * A_HEADS and S % tg == 0
    ti, tj = _pair_tables(S)
    heads = A_HEADS
    kern = functools.partial(_attend_kernel, heads=heads, with_mask=with_mask)
    grid_spec = pltpu.PrefetchScalarGridSpec(
        num_scalar_prefetch=3,
        grid=(B, ti.shape[0]),
        in_specs=[
            pl.BlockSpec((None, heads * HP, tq), lambda b, s, ti, tj, sf: (b, 0, ti[s])),
            pl.BlockSpec((None, tg, heads * HP), lambda b, s, ti, tj, sf: (b, tj[s], 0)),
            pl.BlockSpec((None, heads * VP, tg), lambda b, s, ti, tj, sf: (b, 0, tj[s])),
        ] + extra_specs,
        out_specs=pl.BlockSpec((None, tq, heads * B_V), lambda b, s, ti, tj, sf: (b, ti[s], 0)),
        scratch_shapes=[pltpu.VMEM((heads * VP, tq), F32),
                        pltpu.VMEM((2, tk, tq), F32),
                        pltpu.VMEM((2, tk, tq), BF16),
                        pltpu.VMEM((heads, tq), F32)] + extra_scratch,
    )
    return pl.pallas_call(
        kern, grid_spec=grid_spec,
        out_shape=jax.ShapeDtypeStruct((B, S, heads * B_V), BF16),
        compiler_params=_cparams(("parallel", "arbitrary")),
        name=name,
    )(ti, tj, safe, qT, k, vT, *extra_inputs)


def _attend_a(qaT, ka, vaT, mask, bias_tiles, safe):
    tq, tk = TQ_ATT, TK_ATT
    specs = [
        pl.BlockSpec((None, tk * NSUB_ATT, tq), lambda b, s, ti, tj, sf: (b, tj[s], ti[s])),
        pl.BlockSpec(bias_tiles.shape, lambda b, s, ti, tj, sf: (0, 0, 0, 0)),
    ]
    return _attend(qaT, ka, vaT, safe, (mask, bias_tiles), specs, [pltpu.VMEM((tk, tq), F32)],
                   True, "attend_a")


def _attend_b(qbT, kb, vbT, diag_tile, safe):
    tq, tk = TQ_ATT, TK_ATT
    specs = [pl.BlockSpec((tk, tq), lambda b, s, ti, tj, sf: (0, 0))]
    return _attend(qbT, kb, vbT, safe, (diag_tile,), specs, [], False, "attend_b")


def _merge_kernel(x_ref, ya_ref, yb_ref, g_ref, wga_ref, wgb_ref, bga_ref, bgb_ref,
                  wpa_ref, wpb_ref, wo_ref, o_ref):
    x = x_ref[...]
    h = (x * lax.rsqrt(jnp.mean(x * x, axis=-1, keepdims=True) + EPS) * g_ref[...]).astype(BF16)
    gate_a = jax.nn.sigmoid(_dot(h, wga_ref[...]) + bga_ref[...])
    gate_b = jax.nn.sigmoid(_dot(h, wgb_ref[...]) + bgb_ref[...])
    merged = gate_a * _dot(ya_ref[...], wpa_ref[...]) + gate_b * _dot(yb_ref[...], wpb_ref[...])
    o_ref[...] = x + _dot(merged.astype(BF16), wo_ref[...])


def _merge_out(x2, ya2, yb2, g, wga, wgb, bga, bgb, wpa, wpb, wo):
    R, D = x2.shape
    tm = TM_MERGE
    row = lambda w: pl.BlockSpec((tm, w), lambda i: (i, 0))
    full = lambda a: pl.BlockSpec(a.shape, lambda i: (0,) * a.ndim)
    consts = (g, wga, wgb, bga, bgb, wpa, wpb, wo)
    return pl.pallas_call(
        _merge_kernel,
        grid=(R // tm,),
        in_specs=[row(D), row(A_WIDTH), row(B_WIDTH)] + [full(a) for a in consts],
        out_specs=row(D),
        out_shape=jax.ShapeDtypeStruct((R, D), F32),
        compiler_params=_cparams(("parallel",)),
        name="merge_out",
    )(x2, ya2, yb2, *consts)


def _ffn_kernel(x_ref, xp_ref, g_ref, wuv_ref, wug_ref, cwv_ref, cwg_ref, cbv_ref, cbg_ref, wd_ref,
                o_ref, uv_ref, ug_ref, acc_ref, *, tiles_per_seq):
    i = pl.program_id(0)
    f = pl.program_id(1)
    tm = x_ref.shape[0]
    halo = SUBLANES
    g = g_ref[...]

    def normed(v):
        return (v * lax.rsqrt(jnp.mean(v * v, axis=-1, keepdims=True) + EPS) * g).astype(BF16)

    h = normed(x_ref[...])
    keep = jnp.where(i % tiles_per_seq == 0, 0.0, 1.0)
    hp = normed(xp_ref[...])

    def conv(u_ref, w_ref, cw_ref, cb_ref):
        u_ref[0:halo, :] = _dot(hp, w_ref[...]) * keep
        u_ref[halo:halo + tm, :] = _dot(h, w_ref[...])
        out = cb_ref[...]
        for t in range(CONV_W):
            lo = halo - (CONV_W - 1) + t
            out = out + cw_ref[t:t + 1, :] * u_ref[lo:lo + tm, :]
        return out

    val = conv(uv_ref, wuv_ref, cwv_ref, cbv_ref)
    gat = conv(ug_ref, wug_ref, cwg_ref, cbg_ref)
    act = (gat * jax.nn.sigmoid(gat) * val).astype(BF16)
    part = _dot(act, wd_ref[...])

    @pl.when(f == 0)
    def _():
        acc_ref[...] = part

    @pl.when(f > 0)
    def _():
        acc_ref[...] = acc_ref[...] + part

    @pl.when(f == pl.num_programs(1) - 1)
    def _():
        o_ref[...] = x_ref[...] + acc_ref[...]


def _conv_ffn(x2, S, g, wu, cw, cb, wd):
    R, D = x2.shape
    tm, tf = TM_FFN, TF_FFN
    nf = D_FF // tf
    halo_blocks = tm // SUBLANES
    kern = functools.partial(_ffn_kernel, tiles_per_seq=S // tm)
    return pl.pallas_call(
        kern,
        grid=(R // tm, nf),
        in_specs=[
            pl.BlockSpec((tm, D), lambda i, f: (i, 0)),
            pl.BlockSpec((SUBLANES, D), lambda i, f: (jnp.maximum(i * halo_blocks - 1, 0), 0)),
            pl.BlockSpec((1, D), lambda i, f: (0, 0)),
            pl.BlockSpec((D, tf), lambda i, f: (0, f)),
            pl.BlockSpec((D, tf), lambda i, f: (0, nf + f)),
            pl.BlockSpec((CONV_W, tf), lambda i, f: (0, f)),
            pl.BlockSpec((CONV_W, tf), lambda i, f: (0, nf + f)),
            pl.BlockSpec((1, tf), lambda i, f: (0, f)),
            pl.BlockSpec((1, tf), lambda i, f: (0, nf + f)),
            pl.BlockSpec((tf, D), lambda i, f: (f, 0)),
        ],
        out_specs=pl.BlockSpec((tm, D), lambda i, f: (i, 0)),
        out_shape=jax.ShapeDtypeStruct((R, D), F32),
        scratch_shapes=[pltpu.VMEM((tm + SUBLANES, tf), F32), pltpu.VMEM((tm + SUBLANES, tf), F32),
                        pltpu.VMEM((tm, D), F32)],
        compiler_params=_cparams(("parallel", "arbitrary")),
        name="conv_ffn",
    )(x2, x2, g, wu, wu, cw, cw, cb, cb, wd)


def _t5_bucket(rel):
    nb = REL_BUCKETS // 2
    max_exact = nb // 2
    side = jnp.where(rel > 0, nb, 0)
    n = jnp.abs(rel)
    nf = jnp.maximum(n, 1).astype(F32)
    large = max_exact + (jnp.log(nf / max_exact) / math.log(REL_MAX_DIST / max_exact)
                         * (nb - max_exact)).astype(I32)
    large = jnp.minimum(large, nb - 1)
    return side + jnp.where(n < max_exact, n, large)


def _bias_tiles(rel_bias):
    tq, tk = TQ_ATT, TK_ATT
    assert tq == tk and tk >= REL_MAX_DIST
    kk = jnp.arange(tk, dtype=I32)[:, None]
    qq = jnp.arange(tq, dtype=I32)[None, :]
    bucket = _t5_bucket(jnp.stack([kk - qq, kk - qq - tk])).reshape(1, 2 * tk * tq)
    onehot = (jnp.arange(REL_BUCKETS, dtype=I32)[:, None] == bucket).astype(F32)
    far = rel_bias[_t5_bucket(jnp.asarray(-REL_MAX_DIST, I32))]
    table = ((rel_bias - far[None, :]) * LOG2E).T
    tiles = jnp.dot(table, onehot, precision=lax.Precision.HIGHEST)
    return tiles.reshape(A_HEADS, 2, tk, tq).astype(BF16), jnp.max(jnp.abs(table), axis=1)


def _diag_tile():
    kk = np.arange(TK_ATT)[:, None] // CHUNK
    qq = np.arange(TQ_ATT)[None, :] // CHUNK
    return jnp.asarray(np.where(kk <= qq, 0.0, NEG).astype(np.float32))


def _rope_tables(S):
    half = B_ROPE // 2
    inv = ROPE_BASE ** (-jnp.arange(half, dtype=F32) / half)
    ang = jnp.arange(S, dtype=I32).astype(F32)[:, None] * inv[None, :]
    cos, sin = jnp.cos(ang), jnp.sin(ang)
    ones = jnp.ones((S, B_NOPE), F32)
    zeros_n = jnp.zeros((S, B_NOPE), F32)
    zeros_p = jnp.zeros((S, HP - B_QK), F32)
    cos_t = jnp.concatenate([ones, cos, cos, zeros_p], axis=1)
    sin_t = jnp.concatenate([zeros_n, -sin, sin, zeros_p], axis=1)
    return cos_t, sin_t


def _pad_cols(w, width):
    return jnp.pad(w, ((0, 0), (0, width - w.shape[1])))


def _pad_heads(w, heads, width):
    rows = w.shape[0]
    w = w.reshape(rows, heads, -1)
    return jnp.pad(w, ((0, 0), (0, 0), (0, width - w.shape[2]))).reshape(rows, heads * width)


def _swap_halves(w):
    half = w.shape[-1] // 2
    return jnp.concatenate([w[..., half:], w[..., :half]], axis=-1)


def _layer_weights(l, w_in, b_w_uq, b_w_ukv, b_q_norm, b_k_norm, a_q_norm, a_k_norm, bias_max):
    w = w_in[l]
    o = np.cumsum([0, A_WIDTH, A_WIDTH, A_WIDTH, IDX_HEADS * IDX_DIM, IDX_DIM, IDX_HEADS,
                   B_Q_LORA, B_KV_LORA, B_ROPE, D_MODEL, D_MODEL])
    seg = [w[:, o[t]:o[t + 1]] for t in range(11)]
    zn = jnp.zeros((D_MODEL, B_NOPE), F32)
    kr = seg[8]
    w1 = jnp.concatenate([
        _pad_heads(seg[0], A_HEADS, HP), _pad_heads(seg[1], A_HEADS, HP), seg[2],
        _pad_heads(seg[3], IDX_HEADS, HP),
        _pad_cols(seg[4], LANES), _pad_cols(seg[5], LANES), seg[6], seg[7],
        _pad_cols(jnp.concatenate([zn, kr], axis=1), HP),
        _pad_cols(jnp.concatenate([zn, _swap_halves(kr)], axis=1), HP),
    ], axis=1).astype(BF16)
    assert w1.shape[1] == _C_END

    uq = b_w_uq[l].reshape(B_Q_LORA, B_HEADS, B_QK)
    zq = jnp.zeros((B_Q_LORA, B_HEADS, B_NOPE), F32)
    pq = jnp.zeros((B_Q_LORA, B_HEADS, HP - B_QK), F32)
    wqm = jnp.concatenate([uq, pq], axis=-1).reshape(B_Q_LORA, B_HEADS * HP).astype(BF16)
    wqs = jnp.concatenate([zq, _swap_halves(uq[..., B_NOPE:]), pq], axis=-1)
    wqs = wqs.reshape(B_Q_LORA, B_HEADS * HP).astype(BF16)

    ukv = b_w_ukv[l].reshape(B_KV_LORA, B_HEADS, B_NOPE + B_V)
    pk = jnp.zeros((B_KV_LORA, B_HEADS, HP - B_NOPE), F32)
    wkm = jnp.concatenate([ukv[..., :B_NOPE], pk], axis=-1).reshape(B_KV_LORA, B_HEADS * HP).astype(BF16)
    wv = ukv[..., B_NOPE:].reshape(B_KV_LORA, B_WIDTH).astype(BF16)

    gqa = _pad_cols(a_q_norm[l][None, :], HP)
    gka = _pad_cols(a_k_norm[l][None, :], HP)
    gqb = _pad_cols(b_q_norm[l][None, :], HP)
    gkb = _pad_cols(b_k_norm[l][None, :], HP)

    ka_bound = math.sqrt(A_HEAD_DIM) * jnp.max(jnp.abs(a_k_norm[l]))
    kb_bound = math.sqrt(B_QK) * jnp.max(jnp.abs(b_k_norm[l]))
    bnd = jnp.zeros((SUBLANES, LANES), F32)
    bnd = bnd.at[0, :].set(ka_bound).at[1, :].set(kb_bound).at[2, :A_HEADS].set(bias_max)
    qa_bound = math.sqrt(A_HEAD_DIM) * jnp.max(jnp.abs(a_q_norm[l])) * (A_HEAD_DIM ** -0.5 * LOG2E)
    qb_bound = math.sqrt(B_QK) * jnp.max(jnp.abs(b_q_norm[l])) * (B_QK ** -0.5 * LOG2E)
    range_a = (qa_bound * ka_bound + jnp.max(bias_max)) * BOUND_SLACK + jnp.max(bias_max)
    range_b = qb_bound * kb_bound * BOUND_SLACK
    safe_a = (range_a <= MAX_LOG2_RANGE).astype(I32).reshape(1)
    safe_b = (range_b <= MAX_LOG2_RANGE).astype(I32).reshape(1)
    return (w1, wqm, wqs, wkm, wv, gqa, gka, gqb, gkb, bnd, safe_a, safe_b,
            seg[9].astype(BF16), seg[10].astype(BF16))


def kernel(x, rel_bias, norm_mix, w_in, a_q_norm, a_k_norm, b_cq_norm, b_ckv_norm, b_w_uq, b_w_ukv,
           b_q_norm, b_k_norm, w_proj_a, w_proj_b, b_gate, w_out, norm_ffn, w_up, conv_w, conv_b, w_down):
    B, S, D = x.shape
    assert D == D_MODEL and S % TQ_ATT == 0 and S % TM_FFN == 0
    topk = min(TOPK_MAX, S // 4)
    assert TQ_SEL >= topk
    depth = w_in.shape[0]

    cos_t, sin_t = _rope_tables(S)
    bias_tiles, bias_max = _bias_tiles(rel_bias)
    diag_tile = _diag_tile()

    for l in range(depth):
        (w1, wqm, wqs, wkm, wv, gqa, gka, gqb, gkb, bnd, safe_a, safe_b, wga, wgb) = _layer_weights(
            l, w_in, b_w_uq, b_w_ukv, b_q_norm, b_k_norm, a_q_norm, a_k_norm, bias_max)
        qaT, ka, vaT, qi, kiT, wi, qbT, kb, vbT = _token_prep(
            x, norm_mix[l][None, :], w1, wqm, wqs, wkm, wv, gqa, gka,
            b_cq_norm[l][None, :], b_ckv_norm[l][None, :], gqb, gkb, bnd,
            cos_t, sin_t)
        mask = _select_topk(qi, wi, kiT, topk)
        y_a = _attend_a(qaT, ka, vaT, mask, bias_tiles, safe_a)
        y_b = _attend_b(qbT, kb, vbT, diag_tile, safe_b)
        x2 = _merge_out(
            x.reshape(B * S, D), y_a.reshape(B * S, A_WIDTH), y_b.reshape(B * S, B_WIDTH),
            norm_mix[l][None, :], wga, wgb, b_gate[l][None, :D_MODEL], b_gate[l][None, D_MODEL:],
            w_proj_a[l].astype(BF16), w_proj_b[l].astype(BF16), w_out[l].astype(BF16))
        x2 = _conv_ffn(x2, S, norm_ffn[l][None, :], w_up[l].astype(BF16), conv_w[l], conv_b[l][None, :],
                       w_down[l].astype(BF16))
        x = x2.reshape(B, S, D)
    return x
```

```python
import functools
import math

import numpy as np
import jax
import jax.numpy as jnp
from jax import lax
from jax.experimental import pallas as pl
from jax.experimental.pallas import tpu as pltpu

F32 = jnp.float32
BF16 = jnp.bfloat16
I32 = jnp.int32
I16 = jnp.int16

D_MODEL = 1024
CHUNK = 64
CHUNK_SHIFT = 6
assert 1 << CHUNK_SHIFT == CHUNK
A_HEADS = 8
A_HEAD_DIM = 64
A_WIDTH = A_HEADS * A_HEAD_DIM
IDX_HEADS = 8
IDX_DIM = 64
TOPK_MAX = 256
B_HEADS = 8
B_Q_LORA = 384
B_KV_LORA = 256
B_NOPE = 64
B_ROPE = 32
B_QK = B_NOPE + B_ROPE
B_V = 64
B_WIDTH = B_HEADS * B_V
ROPE_BASE = 10000.0
REL_BUCKETS = 32
REL_MAX_DIST = 128
D_FF = 2816
CONV_W = 3
EPS = 1e-6

LANES = 128
SUBLANES = 8
BF16_ROWS = 16
VMEM_LIMIT = 56 * 1024 * 1024

TM_PREP = 256
TQ_SEL = 256
KU_SEL = 256
CH_SEL = 64
TQ_ATT = 512
TK_ATT = 512
NSUB_ATT = 2
CH_ATT = 64
TM_MERGE = 256
TM_FFN = 512
TF_FFN = 1408

HP = 128
VP = B_V + BF16_ROWS
NEG = -1e30
KEY_NEG_INF = -2139095041
LOG2E = 1.4426950408889634
BOUND_SLACK = 1.01
MAX_LOG2_RANGE = 60.0


def _cparams(sem):
    return pltpu.CompilerParams(dimension_semantics=sem, vmem_limit_bytes=VMEM_LIMIT)


def _dot(a, b):
    return jnp.dot(a, b, preferred_element_type=F32)


_C_QA = 0
_C_KA = _C_QA + A_HEADS * HP
_C_VA = _C_KA + A_HEADS * HP
_C_QI = _C_VA + A_WIDTH
_C_KI = _C_QI + IDX_HEADS * HP
_C_WI = _C_KI + LANES
_C_CQ = _C_WI + LANES
_C_CKV = _C_CQ + B_Q_LORA
_C_KR = _C_CKV + B_KV_LORA
_C_KRS = _C_KR + HP
_C_END = _C_KRS + HP


def _token_prep_kernel(x_ref, g_ref, w1_ref, wqm_ref, wqs_ref, wkm_ref, wv_ref,
                       gqa_ref, gka_ref, gcq_ref, gckv_ref, gqb_ref, gkb_ref, bnd_ref,
                       cos_ref, sin_ref,
                       qaT_ref, ka_ref, vaT_ref, qi_ref, kiT_ref, wi_ref,
                       qbT_ref, kb_ref, vbT_ref):
    x = x_ref[...]
    tm = x.shape[0]
    h = (x * lax.rsqrt(jnp.mean(x * x, axis=-1, keepdims=True) + EPS) * g_ref[...]).astype(BF16)

    def proj(lo, hi):
        return _dot(h, w1_ref[:, lo:hi])

    def head_norm(zh, gain):
        ss = jnp.sum(zh * zh, axis=-1, keepdims=True) * (1.0 / A_HEAD_DIM)
        return zh * lax.rsqrt(ss + EPS) * gain

    ones_rows = jnp.where(lax.broadcasted_iota(I32, (BF16_ROWS, tm), 0) == 0, 1.0, 0.0).astype(BF16)

    def store_vT(ref, v):
        vT = v.T
        for hh in range(A_HEADS):
            ref[VP * hh:VP * hh + B_V, :] = vT[B_V * hh:B_V * (hh + 1), :].astype(BF16)
            ref[VP * hh + B_V:VP * (hh + 1), :] = ones_rows

    spare_a = jnp.where(lax.broadcasted_iota(I32, (1, HP), 1) == A_HEAD_DIM, 1.0, 0.0)
    qa = proj(_C_QA, _C_KA)
    ka = proj(_C_KA, _C_VA)
    gqa = gqa_ref[...] * (A_HEAD_DIM ** -0.5 * LOG2E)
    gka = gka_ref[...]
    for hh in range(A_HEADS):
        blk = slice(HP * hh, HP * (hh + 1))
        qh = head_norm(qa[:, blk], gqa)
        qnorm = jnp.sqrt(jnp.sum(qh * qh, axis=-1, keepdims=True))
        m_a = (qnorm * bnd_ref[0:1, :] + bnd_ref[2:3, hh:hh + 1]) * BOUND_SLACK
        qaT_ref[blk, :] = (qh - m_a * spare_a).T.astype(BF16)
        ka_ref[:, blk] = (head_norm(ka[:, blk], gka) + spare_a).astype(BF16)
    store_vT(vaT_ref, proj(_C_VA, _C_QI))

    qi_ref[...] = proj(_C_QI, _C_KI).astype(BF16)
    kiT_ref[...] = proj(_C_KI, _C_WI).T.astype(BF16)
    wi_ref[...] = proj(_C_WI, _C_CQ) * ((IDX_HEADS ** -0.5) * (IDX_DIM ** -0.5))

    cos = cos_ref[...]
    sin = sin_ref[...]
    spare = jnp.where(lax.broadcasted_iota(I32, (1, HP), 1) == B_QK, 1.0, 0.0)
    cq = proj(_C_CQ, _C_CKV)
    cqn = (cq * lax.rsqrt(jnp.mean(cq * cq, axis=-1, keepdims=True) + EPS) * gcq_ref[...]).astype(BF16)
    qm = _dot(cqn, wqm_ref[...])
    qs = _dot(cqn, wqs_ref[...])
    gqb = gqb_ref[...] * (B_QK ** -0.5 * LOG2E)
    kbound = bnd_ref[1:2, :]
    for hh in range(B_HEADS):
        blk = slice(HP * hh, HP * (hh + 1))
        qh = qm[:, blk] * cos + qs[:, blk] * sin
        ss = jnp.sum(qh * qh, axis=-1, keepdims=True) * (1.0 / B_QK)
        qh = qh * lax.rsqrt(ss + EPS) * gqb
        m_b = jnp.sqrt(jnp.sum(qh * qh, axis=-1, keepdims=True)) * kbound * BOUND_SLACK
        qbT_ref[blk, :] = (qh - m_b * spare).T.astype(BF16)

    ckv = proj(_C_CKV, _C_KR)
    ckvn = (ckv * lax.rsqrt(jnp.mean(ckv * ckv, axis=-1, keepdims=True) + EPS) * gckv_ref[...]).astype(BF16)
    km = _dot(ckvn, wkm_ref[...])
    store_vT(vbT_ref, _dot(ckvn, wv_ref[...]))
    krot = proj(_C_KR, _C_KRS) * cos + proj(_C_KRS, _C_END) * sin
    gkb = gkb_ref[...]
    for hh in range(B_HEADS):
        blk = slice(HP * hh, HP * (hh + 1))
        kh = km[:, blk] + krot
        ss = jnp.sum(kh * kh, axis=-1, keepdims=True) * (1.0 / B_QK)
        kb_ref[:, blk] = (kh * lax.rsqrt(ss + EPS) * gkb + spare).astype(BF16)


def _token_prep(x, g, w1, wqm, wqs, wkm, wv, gqa, gka, gcq, gckv, gqb, gkb, bnd,
                cos_t, sin_t):
    B, S, D = x.shape
    tm = TM_PREP
    nt = S // tm
    row3 = lambda w: pl.BlockSpec((None, tm, w), lambda b, i: (b, i, 0))
    colT = lambda r: pl.BlockSpec((None, r, tm), lambda b, i: (b, 0, i))
    full = lambda a: pl.BlockSpec(a.shape, lambda b, i: (0,) * a.ndim)
    tab = pl.BlockSpec((tm, LANES), lambda b, i: (i, 0))
    consts = (g, w1, wqm, wqs, wkm, wv, gqa, gka, gcq, gckv, gqb, gkb, bnd)
    out_shape = (
        jax.ShapeDtypeStruct((B, A_HEADS * HP, S), BF16),
        jax.ShapeDtypeStruct((B, S, A_HEADS * HP), BF16),
        jax.ShapeDtypeStruct((B, A_HEADS * VP, S), BF16),
        jax.ShapeDtypeStruct((B, S, IDX_HEADS * HP), BF16),
        jax.ShapeDtypeStruct((B, nt, LANES, tm), BF16),
        jax.ShapeDtypeStruct((B, S, LANES), F32),
        jax.ShapeDtypeStruct((B, B_HEADS * HP, S), BF16),
        jax.ShapeDtypeStruct((B, S, B_HEADS * HP), BF16),
        jax.ShapeDtypeStruct((B, B_HEADS * VP, S), BF16),
    )
    kiT_spec = pl.BlockSpec((None, None, LANES, tm), lambda b, i: (b, i, 0, 0))
    out_specs = (colT(A_HEADS * HP), row3(A_HEADS * HP), colT(A_HEADS * VP), row3(IDX_HEADS * HP),
                 kiT_spec, row3(LANES), colT(B_HEADS * HP), row3(B_HEADS * HP), colT(B_HEADS * VP))
    return pl.pallas_call(
        _token_prep_kernel,
        grid=(B, nt),
        in_specs=[row3(D)] + [full(a) for a in consts] + [tab, tab],
        out_specs=out_specs,
        out_shape=out_shape,
        compiler_params=_cparams(("parallel", "parallel")),
        name="token_prep",
    )(x, *consts, cos_t, sin_t)


def _tree_sum(parts):
    while len(parts) > 1:
        nxt = [parts[t] + parts[t + 1] for t in range(0, len(parts) - 1, 2)]
        if len(parts) % 2:
            nxt.append(parts[-1])
        parts = nxt
    return parts[0]


def _select_kernel(qi_ref, wi_ref, kiT_ref, mask_ref, hi_ref, lo_ref, acca_ref, accb_ref, wb_ref, *, topk):
    S = mask_ref.shape[0]
    tq = TQ_SEL
    i = pl.program_id(1)
    n_keys = (i + 1) * tq
    q_chunk = (i * tq + lax.broadcasted_iota(I32, (CH_SEL, tq), 1)) >> CHUNK_SHIFT
    k_off = lax.broadcasted_iota(I32, (CH_SEL, tq), 0)

    for hh in range(IDX_HEADS):
        wb_ref[hh] = jnp.broadcast_to(wi_ref[:, hh:hh + 1], (tq, LANES))

    n_units = n_keys // KU_SEL
    accb_ref[...] = jnp.zeros(accb_ref.shape, F32)

    def heads(u, dst_ref):
        kT = kiT_ref[jnp.minimum(u, n_units - 1)]
        for hh in range(IDX_HEADS):
            d = _dot(qi_ref[:, HP * hh:HP * (hh + 1)], kT)
            for c in range(tq // CH_SEL):
                rows = slice(CH_SEL * c, CH_SEL * (c + 1))
                w = wb_ref[hh, rows, :]
                for lanes in (slice(0, LANES), slice(LANES, 2 * LANES)):
                    t = jnp.maximum(d[rows, lanes], 0.0) * w
                    if hh == 0:
                        dst_ref[rows, lanes] = t
                    else:
                        dst_ref[rows, lanes] += t

    def convert(src_ref, u):
        k0 = pl.multiple_of(jnp.maximum(u, 0) * KU_SEL, KU_SEL)
        accT = src_ref[...].T
        for c in range(KU_SEL // CH_SEL):
            acc = accT[CH_SEL * c:CH_SEL * (c + 1), :] + 0.0
            bits = pltpu.bitcast(acc, I32)
            key = bits ^ ((bits >> 31) & 0x7FFFFFFF)
            r0 = k0 + CH_SEL * c
            hi_ref[pl.ds(r0, CH_SEL), :] = (key >> 16).astype(I16)
            lo_ref[pl.ds(r0, CH_SEL), :] = (key ^ 0x8000).astype(I16)

    def unit_pair(t, carry):
        heads(2 * t, acca_ref)
        convert(accb_ref, 2 * t - 1)
        heads(2 * t + 1, accb_ref)
        convert(acca_ref, 2 * t)
        return carry

    lax.fori_loop(0, n_units // 2 + 1, unit_pair, 0)

    for c in range(KU_SEL // CH_SEL):
        r0 = pl.multiple_of(i * tq + CH_SEL * c, CH_SEL)
        adm = ((i * tq + CH_SEL * c + k_off) >> CHUNK_SHIFT) <= q_chunk
        hi_ref[pl.ds(r0, CH_SEL), :] = jnp.where(
            adm, hi_ref[pl.ds(r0, CH_SEL), :].astype(I32), KEY_NEG_INF >> 16).astype(I16)
        lo_ref[pl.ds(r0, CH_SEL), :] = jnp.where(
            adm, lo_ref[pl.ds(r0, CH_SEL), :].astype(I32), (KEY_NEG_INF ^ 0x8000) & 0xFFFF).astype(I16)

    rb = 2 * tq
    n_rb = (i + 2) // 2
    lowest16 = jnp.full((rb, tq), -32768, I16)

    @pl.when((i + 1) % 2 == 1)
    def _():
        hi_ref[pl.ds(pl.multiple_of(n_keys, tq), tq), :] = lowest16[0:tq, :]
        lo_ref[pl.ds(pl.multiple_of(n_keys, tq), tq), :] = lowest16[0:tq, :]

    one16 = jnp.ones((rb, tq), I16)
    zero16 = jnp.zeros((rb, tq), I16)
    rows16 = rb // BF16_ROWS

    def count(pred):
        def body(r, acc):
            r0 = pl.multiple_of(r * rb, rb)
            ind = jnp.where(pred(lambda: hi_ref[pl.ds(r0, rb), :], lambda: lo_ref[pl.ds(r0, rb), :]),
                            one16, zero16)
            return acc + _tree_sum([ind[BF16_ROWS * s:BF16_ROWS * (s + 1), :] for s in range(rows16)])
        acc = lax.fori_loop(0, n_rb, body, jnp.zeros((BF16_ROWS, tq), I16))
        return jnp.sum(acc.astype(I32), axis=0, keepdims=True)

    def radix16(pred_of, need, count_at_zero):
        def step(t, carry):
            prefix, cnt_acc = carry
            cand = prefix | lax.shift_left(jnp.int32(1), 15 - t)
            cnt = count(pred_of((cand - 32768).astype(I16)))
            ok = cnt >= need
            return jnp.where(ok, cand, prefix), jnp.where(ok, cnt, cnt_acc)
        prefix, cnt = lax.fori_loop(0, 16, step, (jnp.zeros((1, tq), I32), count_at_zero))
        return prefix - 32768, cnt

    t_hi32, cnt_hi_ge = radix16(lambda c: (lambda hi, lo: hi() >= c), topk,
                                jnp.full((1, tq), n_keys, I32))
    t_hi = t_hi32.astype(I16)

    def above_and_park(r, acc):
        r0 = pl.multiple_of(r * rb, rb)
        blk_hi = hi_ref[pl.ds(r0, rb), :]
        lo_ref[pl.ds(r0, rb), :] = jnp.where(blk_hi == t_hi, lo_ref[pl.ds(r0, rb), :], lowest16)
        ind = jnp.where(blk_hi > t_hi, one16, zero16)
        return acc + _tree_sum([ind[BF16_ROWS * s:BF16_ROWS * (s + 1), :] for s in range(rows16)])

    cnt_above = jnp.sum(lax.fori_loop(0, n_rb, above_and_park, jnp.zeros((BF16_ROWS, tq), I16)).astype(I32),
                        axis=0, keepdims=True)
    t_lo32, cnt_lo_ge = radix16(lambda c: (lambda hi, lo: lo() >= c),
                                topk - cnt_above, cnt_hi_ge - cnt_above)
    t_lo = t_lo32.astype(I16)
    thr = (t_hi32 << 16) + (t_lo32 + 32768)
    cnt_ge = cnt_above + cnt_lo_ge
    tied = jnp.logical_and(cnt_ge > topk, thr > KEY_NEG_INF)
    any_tie = jnp.max(tied.astype(I32)) > 0
    thr_adm = jnp.maximum(thr, KEY_NEG_INF + 1)
    a_hi = (thr_adm >> 16).astype(I16)
    a_lo = ((thr_adm & 0xFFFF) - 32768).astype(I16)
    pass16 = jnp.zeros((rb, tq), BF16)
    drop16 = jnp.full((rb, tq), NEG, BF16)

    @pl.when(jnp.logical_not(any_tie))
    def _():
        def body(r, carry):
            r0 = pl.multiple_of(r * rb, rb)
            hi = hi_ref[pl.ds(r0, rb), :]
            lo = lo_ref[pl.ds(r0, rb), :]
            sel = jnp.logical_or(hi > a_hi, jnp.logical_and(hi == a_hi, lo >= a_lo))
            mask_ref[pl.ds(r0, rb), :] = jnp.where(sel, pass16, drop16)
            return carry
        lax.fori_loop(0, n_rb, body, 0)

    @pl.when(any_tie)
    def _():
        cnt_gt = cnt_above + count(lambda hi, lo: lo() > t_lo)
        n_take = (topk - cnt_gt).astype(F32)
        ltri = (lax.broadcasted_iota(I32, (KU_SEL, KU_SEL), 0)
                >= lax.broadcasted_iota(I32, (KU_SEL, KU_SEL), 1)).astype(BF16)

        def body(u, seen):
            k0 = pl.multiple_of(u * KU_SEL, KU_SEL)
            blk = ((hi_ref[pl.ds(k0, KU_SEL), :].astype(I32) << 16)
                   + (lo_ref[pl.ds(k0, KU_SEL), :].astype(I32) + 32768))
            eq = blk == thr
            eqf = eq.astype(F32)
            rank = _dot(ltri, eqf.astype(BF16)) + seen
            sel = jnp.logical_or(blk > thr, jnp.logical_and(eq, rank <= n_take))
            sel = jnp.logical_and(sel, blk != KEY_NEG_INF)
            mask_ref[pl.ds(k0, KU_SEL), :] = jnp.where(sel, 0.0, NEG).astype(BF16)
            return seen + jnp.sum(eqf, axis=0, keepdims=True)
        lax.fori_loop(0, n_keys // KU_SEL, body, jnp.zeros((1, tq), F32))

    def fill(r, carry):
        r0 = pl.multiple_of(r * tq, tq)
        mask_ref[pl.ds(r0, tq), :] = drop16[0:tq, :]
        return carry
    lax.fori_loop(i + 1, S // tq, fill, 0)


def _select_topk(qi, wi, kiT, topk):
    B, S, _ = qi.shape
    tq = TQ_SEL
    n_units, dims, unit = kiT.shape[1:]
    assert unit == KU_SEL == tq and n_units * unit == S
    return pl.pallas_call(
        functools.partial(_select_kernel, topk=topk),
        grid=(B, S // tq),
        in_specs=[
            pl.BlockSpec((None, tq, IDX_HEADS * HP), lambda b, i: (b, i, 0)),
            pl.BlockSpec((None, tq, LANES), lambda b, i: (b, i, 0)),
            pl.BlockSpec((None, n_units, dims, unit), lambda b, i: (b, 0, 0, 0)),
        ],
        out_specs=pl.BlockSpec((None, S, tq), lambda b, i: (b, 0, i)),
        out_shape=jax.ShapeDtypeStruct((B, S, S), BF16),
        scratch_shapes=[pltpu.VMEM((S + tq, tq), I16), pltpu.VMEM((S + tq, tq), I16),
                        pltpu.VMEM((tq, KU_SEL), F32), pltpu.VMEM((tq, KU_SEL), F32),
                        pltpu.VMEM((IDX_HEADS, tq, LANES), F32)],
        compiler_params=_cparams(("parallel", "parallel")),
        name="select_topk",
    )(qi, wi, kiT)


def _attend_kernel(ti_ref, tj_ref, safe_ref, *refs, heads, with_mask):
    if with_mask:
        qT_ref, k_ref, vT_ref, mask_ref, bias_ref, o_ref, acc_ref, s_ref, p_ref, m_ref, add_ref = refs
    else:
        qT_ref, k_ref, vT_ref, diag_ref, o_ref, acc_ref, s_ref, p_ref, m_ref = refs
    s_idx = pl.program_id(1)
    i = ti_ref[s_idx]
    jj = tj_ref[s_idx]
    bounded = safe_ref[0] == 1
    tk = TK_ATT
    n_ch = tk // CH_ATT

    @pl.when(jj == 0)
    def _():
        acc_ref[...] = jnp.zeros(acc_ref.shape, F32)
        m_ref[...] = jnp.full(m_ref.shape, NEG, F32)

    def chunks():
        return [slice(CH_ATT * c, CH_ATT * (c + 1)) for c in range(n_ch)]

    def key_block(sb, carry):
        j = jj * NSUB_ATT + sb
        k0 = pl.multiple_of(sb * tk, tk)

        def logits(hh):
            s_ref[hh % 2] = _dot(k_ref[pl.ds(k0, tk), HP * hh:HP * (hh + 1)],
                                 qT_ref[HP * hh:HP * (hh + 1), :])

        def pv(hh, buf):
            return _dot(vT_ref[VP * hh:VP * (hh + 1), pl.ds(k0, tk)], p_ref[buf])

        def sweep_bounded(extra):
            logits(0)
            for hh in range(heads):
                buf = hh % 2
                if hh + 1 < heads:
                    logits(hh + 1)
                for rows in chunks():
                    t = s_ref[buf, rows, :]
                    e = extra(hh, rows)
                    if e is not None:
                        t = t + e
                    p_ref[buf, rows, :] = jnp.exp2(t).astype(BF16)
                acc_ref[VP * hh:VP * (hh + 1), :] += pv(hh, buf)

        def sweep_running_max(extra):
            for hh in range(heads):
                buf = hh % 2
                logits(hh)
                m_blk = jnp.full((SUBLANES, s_ref.shape[2]), NEG, F32)
                for rows in chunks():
                    t = s_ref[buf, rows, :]
                    e = extra(hh, rows)
                    if e is not None:
                        t = t + e
                        s_ref[buf, rows, :] = t
                    for r in range(CH_ATT // SUBLANES):
                        m_blk = jnp.maximum(m_blk, t[SUBLANES * r:SUBLANES * (r + 1), :])
                m_old = m_ref[hh:hh + 1, :]
                m_new = jnp.maximum(m_old, jnp.max(m_blk, axis=0, keepdims=True))
                m_ref[hh:hh + 1, :] = m_new
                for rows in chunks():
                    p_ref[buf, rows, :] = jnp.exp2(s_ref[buf, rows, :] - m_new).astype(BF16)
                acc_ref[VP * hh:VP * (hh + 1), :] = (
                    jnp.exp2(m_old - m_new) * acc_ref[VP * hh:VP * (hh + 1), :] + pv(hh, buf))

        def both(extra):
            @pl.when(bounded)
            def _():
                sweep_bounded(extra)

            @pl.when(jnp.logical_not(bounded))
            def _():
                sweep_running_max(extra)

        if with_mask:
            @pl.when(j <= i)
            def _():
                add_ref[...] = mask_ref[pl.ds(k0, tk), :].astype(F32)

            @pl.when(j < i - 1)
            def _():
                both(lambda hh, rows: add_ref[rows, :])

            @pl.when(jnp.logical_and(j >= i - 1, j <= i))
            def _():
                near = i - j
                both(lambda hh, rows: add_ref[rows, :] + bias_ref[hh, near, rows, :].astype(F32))
        else:
            @pl.when(j < i)
            def _():
                both(lambda hh, rows: None)

            @pl.when(j == i)
            def _():
                both(lambda hh, rows: diag_ref[rows, :])
        return carry

    lax.fori_loop(0, NSUB_ATT, key_block, 0)

    @pl.when(jj == i // NSUB_ATT)
    def _():
        dv = VP - BF16_ROWS
        for hh in range(heads):
            inv = 1.0 / acc_ref[VP * hh + dv:VP * hh + dv + 1, :]
            s_ref[0, dv * hh:dv * (hh + 1), :] = acc_ref[VP * hh:VP * hh + dv, :] * inv
        o_ref[...] = s_ref[0, 0:dv * heads, :].T.astype(o_ref.dtype)


def _pair_tables(S):
    n = S // TQ_ATT
    ti = np.array([i for i in range(n) for _ in range(i // NSUB_ATT + 1)], np.int32)
    tj = np.array([j for i in range(n) for j in range(i // NSUB_ATT + 1)], np.int32)
    return jnp.asarray(ti), jnp.asarray(tj)


def _attend(qT, k, vT, safe, extra_inputs, extra_specs, extra_scratch, with_mask, name):
    B, S, _ = k.shape
    tq, tk = TQ_ATT, TK_ATT
    tg = tk * NSUB_ATT
    assert tq == tk and tk >= B_V * A_HEADS and S % tg == 0
    ti, tj = _pair_tables(S)
    heads = A_HEADS
    kern = functools.partial(_attend_kernel, heads=heads, with_mask=with_mask)
    grid_spec = pltpu.PrefetchScalarGridSpec(
        num_scalar_prefetch=3,
        grid=(B, ti.shape[0]),
        in_specs=[
            pl.BlockSpec((None, heads * HP, tq), lambda b, s, ti, tj, sf: (b, 0, ti[s])),
            pl.BlockSpec((None, tg, heads * HP), lambda b, s, ti, tj, sf: (b, tj[s], 0)),
            pl.BlockSpec((None, heads * VP, tg), lambda b, s, ti, tj, sf: (b, 0, tj[s])),
        ] + extra_specs,
        out_specs=pl.BlockSpec((None, tq, heads * B_V), lambda b, s, ti, tj, sf: (b, ti[s], 0)),
        scratch_shapes=[pltpu.VMEM((heads * VP, tq), F32),
                        pltpu.VMEM((2, tk, tq), F32),
                        pltpu.VMEM((2, tk, tq), BF16),
                        pltpu.VMEM((heads, tq), F32)] + extra_scratch,
    )
    return pl.pallas_call(
        kern, grid_spec=grid_spec,
        out_shape=jax.ShapeDtypeStruct((B, S, heads * B_V), BF16),
        compiler_params=_cparams(("parallel", "arbitrary")),
        name=name,
    )(ti, tj, safe, qT, k, vT, *extra_inputs)


def _attend_a(qaT, ka, vaT, mask, bias_tiles, safe):
    tq, tk = TQ_ATT, TK_ATT
    specs = [
        pl.BlockSpec((None, tk * NSUB_ATT, tq), lambda b, s, ti, tj, sf: (b, tj[s], ti[s])),
        pl.BlockSpec(bias_tiles.shape, lambda b, s, ti, tj, sf: (0, 0, 0, 0)),
    ]
    return _attend(qaT, ka, vaT, safe, (mask, bias_tiles), specs, [pltpu.VMEM((tk, tq), F32)],
                   True, "attend_a")


def _attend_b(qbT, kb, vbT, diag_tile, safe):
    tq, tk = TQ_ATT, TK_ATT
    specs = [pl.BlockSpec((tk, tq), lambda b, s, ti, tj, sf: (0, 0))]
    return _attend(qbT, kb, vbT, safe, (diag_tile,), specs, [], False, "attend_b")


def _merge_kernel(x_ref, ya_ref, yb_ref, g_ref, wga_ref, wgb_ref, bga_ref, bgb_ref,
                  wpa_ref, wpb_ref, wo_ref, o_ref):
    x = x_ref[...]
    h = (x * lax.rsqrt(jnp.mean(x * x, axis=-1, keepdims=True) + EPS) * g_ref[...]).astype(BF16)
    gate_a = jax.nn.sigmoid(_dot(h, wga_ref[...]) + bga_ref[...])
    gate_b = jax.nn.sigmoid(_dot(h, wgb_ref[...]) + bgb_ref[...])
    merged = gate_a * _dot(ya_ref[...], wpa_ref[...]) + gate_b * _dot(yb_ref[...], wpb_ref[...])
    o_ref[...] = x + _dot(merged.astype(BF16), wo_ref[...])


def _merge_out(x2, ya2, yb2, g, wga, wgb, bga, bgb, wpa, wpb, wo):
    R, D = x2.shape
    tm = TM_MERGE
    row = lambda w: pl.BlockSpec((tm, w), lambda i: (i, 0))
    full = lambda a: pl.BlockSpec(a.shape, lambda i: (0,) * a.ndim)
    consts = (g, wga, wgb, bga, bgb, wpa, wpb, wo)
    return pl.pallas_call(
        _merge_kernel,
        grid=(R // tm,),
        in_specs=[row(D), row(A_WIDTH), row(B_WIDTH)] + [full(a) for a in consts],
        out_specs=row(D),
        out_shape=jax.ShapeDtypeStruct((R, D), F32),
        compiler_params=_cparams(("parallel",)),
        name="merge_out",
    )(x2, ya2, yb2, *consts)


def _ffn_kernel(x_ref, xp_ref, g_ref, wuv_ref, wug_ref, cwv_ref, cwg_ref, cbv_ref, cbg_ref, wd_ref,
                o_ref, uv_ref, ug_ref, acc_ref, *, tiles_per_seq):
    i = pl.program_id(0)
    f = pl.program_id(1)
    tm = x_ref.shape[0]
    halo = SUBLANES
    g = g_ref[...]

    def normed(v):
        return (v * lax.rsqrt(jnp.mean(v * v, axis=-1, keepdims=True) + EPS) * g).astype(BF16)

    h = normed(x_ref[...])
    keep = jnp.where(i % tiles_per_seq == 0, 0.0, 1.0)
    hp = normed(xp_ref[...])

    def conv(u_ref, w_ref, cw_ref, cb_ref):
        u_ref[0:halo, :] = _dot(hp, w_ref[...]) * keep
        u_ref[halo:halo + tm, :] = _dot(h, w_ref[...])
        out = cb_ref[...]
        for t in range(CONV_W):
            lo = halo - (CONV_W - 1) + t
            out = out + cw_ref[t:t + 1, :] * u_ref[lo:lo + tm, :]
        return out

    val = conv(uv_ref, wuv_ref, cwv_ref, cbv_ref)
    gat = conv(ug_ref, wug_ref, cwg_ref, cbg_ref)
    act = (gat * jax.nn.sigmoid(gat) * val).astype(BF16)
    part = _dot(act, wd_ref[...])

    @pl.when(f == 0)
    def _():
        acc_ref[...] = part

    @pl.when(f > 0)
    def _():
        acc_ref[...] = acc_ref[...] + part

    @pl.when(f == pl.num_programs(1) - 1)
    def _():
        o_ref[...] = x_ref[...] + acc_ref[...]


def _conv_ffn(x2, S, g, wu, cw, cb, wd):
    R, D = x2.shape
    tm, tf = TM_FFN, TF_FFN
    nf = D_FF // tf
    halo_blocks = tm // SUBLANES
    kern = functools.partial(_ffn_kernel, tiles_per_seq=S // tm)
    return pl.pallas_call(
        kern,
        grid=(R // tm, nf),
        in_specs=[
            pl.BlockSpec((tm, D), lambda i, f: (i, 0)),
            pl.BlockSpec((SUBLANES, D), lambda i, f: (jnp.maximum(i * halo_blocks - 1, 0), 0)),
            pl.BlockSpec((1, D), lambda i, f: (0, 0)),
            pl.BlockSpec((D, tf), lambda i, f: (0, f)),
            pl.BlockSpec((D, tf), lambda i, f: (0, nf + f)),
            pl.BlockSpec((CONV_W, tf), lambda i, f: (0, f)),
            pl.BlockSpec((CONV_W, tf), lambda i, f: (0, nf + f)),
            pl.BlockSpec((1, tf), lambda i, f: (0, f)),
            pl.BlockSpec((1, tf), lambda i, f: (0, nf + f)),
            pl.BlockSpec((tf, D), lambda i, f: (f, 0)),
        ],
        out_specs=pl.BlockSpec((tm, D), lambda i, f: (i, 0)),
        out_shape=jax.ShapeDtypeStruct((R, D), F32),
        scratch_shapes=[pltpu.VMEM((tm + SUBLANES, tf), F32), pltpu.VMEM((tm + SUBLANES, tf), F32),
                        pltpu.VMEM((tm, D), F32)],
        compiler_params=_cparams(("parallel", "arbitrary")),
        name="conv_ffn",
    )(x2, x2, g, wu, wu, cw, cw, cb, cb, wd)


def _t5_bucket(rel):
    nb = REL_BUCKETS // 2
    max_exact = nb // 2
    side = jnp.where(rel > 0, nb, 0)
    n = jnp.abs(rel)
    nf = jnp.maximum(n, 1).astype(F32)
    large = max_exact + (jnp.log(nf / max_exact) / math.log(REL_MAX_DIST / max_exact)
                         * (nb - max_exact)).astype(I32)
    large = jnp.minimum(large, nb - 1)
    return side + jnp.where(n < max_exact, n, large)


def _bias_tiles(rel_bias):
    tq, tk = TQ_ATT, TK_ATT
    assert tq == tk and tk >= REL_MAX_DIST
    kk = jnp.arange(tk, dtype=I32)[:, None]
    qq = jnp.arange(tq, dtype=I32)[None, :]
    bucket = _t5_bucket(jnp.stack([kk - qq, kk - qq - tk])).reshape(1, 2 * tk * tq)
    onehot = (jnp.arange(REL_BUCKETS, dtype=I32)[:, None] == bucket).astype(F32)
    far = rel_bias[_t5_bucket(jnp.asarray(-REL_MAX_DIST, I32))]
    table = ((rel_bias - far[None, :]) * LOG2E).T
    tiles = jnp.dot(table, onehot, precision=lax.Precision.HIGHEST)
    return tiles.reshape(A_HEADS, 2, tk, tq).astype(BF16), jnp.max(jnp.abs(table), axis=1)


def _diag_tile():
    kk = np.arange(TK_ATT)[:, None] // CHUNK
    qq = np.arange(TQ_ATT)[None, :] // CHUNK
    return jnp.asarray(np.where(kk <= qq, 0.0, NEG).astype(np.float32))


def _rope_tables(S):
    half = B_ROPE // 2
    inv = ROPE_BASE ** (-jnp.arange(half, dtype=F32) / half)
    ang = jnp.arange(S, dtype=I32).astype(F32)[:, None] * inv[None, :]
    cos, sin = jnp.cos(ang), jnp.sin(ang)
    ones = jnp.ones((S, B_NOPE), F32)
    zeros_n = jnp.zeros((S, B_NOPE), F32)
    zeros_p = jnp.zeros((S, HP - B_QK), F32)
    cos_t = jnp.concatenate([ones, cos, cos, zeros_p], axis=1)
    sin_t = jnp.concatenate([zeros_n, -sin, sin, zeros_p], axis=1)
    return cos_t, sin_t


def _pad_cols(w, width):
    return jnp.pad(w, ((0, 0), (0, width - w.shape[1])))


def _pad_heads(w, heads, width):
    rows = w.shape[0]
    w = w.reshape(rows, heads, -1)
    return jnp.pad(w, ((0, 0), (0, 0), (0, width - w.shape[2]))).reshape(rows, heads * width)


def _swap_halves(w):
    half = w.shape[-1] // 2
    return jnp.concatenate([w[..., half:], w[..., :half]], axis=-1)


def _layer_weights(l, w_in, b_w_uq, b_w_ukv, b_q_norm, b_k_norm, a_q_norm, a_k_norm, bias_max):
    w = w_in[l]
    o = np.cumsum([0, A_WIDTH, A_WIDTH, A_WIDTH, IDX_HEADS * IDX_DIM, IDX_DIM, IDX_HEADS,
                   B_Q_LORA, B_KV_LORA, B_ROPE, D_MODEL, D_MODEL])
    seg = [w[:, o[t]:o[t + 1]] for t in range(11)]
    zn = jnp.zeros((D_MODEL, B_NOPE), F32)
    kr = seg[8]
    w1 = jnp.concatenate([
        _pad_heads(seg[0], A_HEADS, HP), _pad_heads(seg[1], A_HEADS, HP), seg[2],
        _pad_heads(seg[3], IDX_HEADS, HP),
        _pad_cols(seg[4], LANES), _pad_cols(seg[5], LANES), seg[6], seg[7],
        _pad_cols(jnp.concatenate([zn, kr], axis=1), HP),
        _pad_cols(jnp.concatenate([zn, _swap_halves(kr)], axis=1), HP),
    ], axis=1).astype(BF16)
    assert w1.shape[1] == _C_END

    uq = b_w_uq[l].reshape(B_Q_LORA, B_HEADS, B_QK)
    zq = jnp.zeros((B_Q_LORA, B_HEADS, B_NOPE), F32)
    pq = jnp.zeros((B_Q_LORA, B_HEADS, HP - B_QK), F32)
    wqm = jnp.concatenate([uq, pq], axis=-1).reshape(B_Q_LORA, B_HEADS * HP).astype(BF16)
    wqs = jnp.concatenate([zq, _swap_halves(uq[..., B_NOPE:]), pq], axis=-1)
    wqs = wqs.reshape(B_Q_LORA, B_HEADS * HP).astype(BF16)

    ukv = b_w_ukv[l].reshape(B_KV_LORA, B_HEADS, B_NOPE + B_V)
    pk = jnp.zeros((B_KV_LORA, B_HEADS, HP - B_NOPE), F32)
    wkm = jnp.concatenate([ukv[..., :B_NOPE], pk], axis=-1).reshape(B_KV_LORA, B_HEADS * HP).astype(BF16)
    wv = ukv[..., B_NOPE:].reshape(B_KV_LORA, B_WIDTH).astype(BF16)

    gqa = _pad_cols(a_q_norm[l][None, :], HP)
    gka = _pad_cols(a_k_norm[l][None, :], HP)
    gqb = _pad_cols(b_q_norm[l][None, :], HP)
    gkb = _pad_cols(b_k_norm[l][None, :], HP)

    ka_bound = math.sqrt(A_HEAD_DIM) * jnp.max(jnp.abs(a_k_norm[l]))
    kb_bound = math.sqrt(B_QK) * jnp.max(jnp.abs(b_k_norm[l]))
    bnd = jnp.zeros((SUBLANES, LANES), F32)
    bnd = bnd.at[0, :].set(ka_bound).at[1, :].set(kb_bound).at[2, :A_HEADS].set(bias_max)
    qa_bound = math.sqrt(A_HEAD_DIM) * jnp.max(jnp.abs(a_q_norm[l])) * (A_HEAD_DIM ** -0.5 * LOG2E)
    qb_bound = math.sqrt(B_QK) * jnp.max(jnp.abs(b_q_norm[l])) * (B_QK ** -0.5 * LOG2E)
    range_a = (qa_bound * ka_bound + jnp.max(bias_max)) * BOUND_SLACK + jnp.max(bias_max)
    range_b = qb_bound * kb_bound * BOUND_SLACK
    safe_a = (range_a <= MAX_LOG2_RANGE).astype(I32).reshape(1)
    safe_b = (range_b <= MAX_LOG2_RANGE).astype(I32).reshape(1)
    return (w1, wqm, wqs, wkm, wv, gqa, gka, gqb, gkb, bnd, safe_a, safe_b,
            seg[9].astype(BF16), seg[10].astype(BF16))


def kernel(x, rel_bias, norm_mix, w_in, a_q_norm, a_k_norm, b_cq_norm, b_ckv_norm, b_w_uq, b_w_ukv,
           b_q_norm, b_k_norm, w_proj_a, w_proj_b, b_gate, w_out, norm_ffn, w_up, conv_w, conv_b, w_down):
    B, S, D = x.shape
    assert D == D_MODEL and S % TQ_ATT == 0 and S % TM_FFN == 0
    topk = min(TOPK_MAX, S // 4)
    assert TQ_SEL >= topk
    depth = w_in.shape[0]

    cos_t, sin_t = _rope_tables(S)
    bias_tiles, bias_max = _bias_tiles(rel_bias)
    diag_tile = _diag_tile()

    for l in range(depth):
        (w1, wqm, wqs, wkm, wv, gqa, gka, gqb, gkb, bnd, safe_a, safe_b, wga, wgb) = _layer_weights(
            l, w_in, b_w_uq, b_w_ukv, b_q_norm, b_k_norm, a_q_norm, a_k_norm, bias_max)
        qaT, ka, vaT, qi, kiT, wi, qbT, kb, vbT = _token_prep(
            x, norm_mix[l][None, :], w1, wqm, wqs, wkm, wv, gqa, gka,
            b_cq_norm[l][None, :], b_ckv_norm[l][None, :], gqb, gkb, bnd,
            cos_t, sin_t)
        mask = _select_topk(qi, wi, kiT, topk)
        y_a = _attend_a(qaT, ka, vaT, mask, bias_tiles, safe_a)
        y_b = _attend_b(qbT, kb, vbT, diag_tile, safe_b)
        x2 = _merge_out(
            x.reshape(B * S, D), y_a.reshape(B * S, A_WIDTH), y_b.reshape(B * S, B_WIDTH),
            norm_mix[l][None, :], wga, wgb, b_gate[l][None, :D_MODEL], b_gate[l][None, D_MODEL:],
            w_proj_a[l].astype(BF16), w_proj_b[l].astype(BF16), w_out[l].astype(BF16))
        x2 = _conv_ffn(x2, S, norm_ffn[l][None, :], w_up[l].astype(BF16), conv_w[l], conv_b[l][None, :],
                       w_down[l].astype(BF16))
        x = x2.reshape(B, S, D)
    return x
```

```python
import functools
import math

import numpy as np
import jax
import jax.numpy as jnp
from jax import lax
from jax.experimental import pallas as pl
from jax.experimental.pallas import tpu as pltpu

F32 = jnp.float32
BF16 = jnp.bfloat16
I32 = jnp.int32
I16 = jnp.int16

D_MODEL = 1024
CHUNK = 64
CHUNK_SHIFT = 6
assert 1 << CHUNK_SHIFT == CHUNK
A_HEADS = 8
A_HEAD_DIM = 64
A_WIDTH = A_HEADS * A_HEAD_DIM
IDX_HEADS = 8
IDX_DIM = 64
TOPK_MAX = 256
B_HEADS = 8
B_Q_LORA = 384
B_KV_LORA = 256
B_NOPE = 64
B_ROPE = 32
B_QK = B_NOPE + B_ROPE
B_V = 64
B_WIDTH = B_HEADS * B_V
ROPE_BASE = 10000.0
REL_BUCKETS = 32
REL_MAX_DIST = 128
D_FF = 2816
CONV_W = 3
EPS = 1e-6

LANES = 128
SUBLANES = 8
BF16_ROWS = 16
VMEM_LIMIT = 56 * 1024 * 1024

TM_PREP = 256
TQ_SEL = 256
KU_SEL = 256
CH_SEL = 64
SEL_BITS = 32
GU_SEL = 8
TQ_ATT = 512
TK_ATT = 512
NSUB_ATT = 2
CH_ATT = 64
TM_MERGE = 256
TM_FFN = 512
TF_FFN = 1408

HP = 128
VP = B_V + BF16_ROWS
NEG = -1e30
SIGN_BIT = -2147483648
LOG2E = 1.4426950408889634
BOUND_SLACK = 1.01
MAX_LOG2_RANGE = 60.0


def _cparams(sem):
    return pltpu.CompilerParams(dimension_semantics=sem, vmem_limit_bytes=VMEM_LIMIT)


def _dot(a, b):
    return jnp.dot(a, b, preferred_element_type=F32)


_C_QA = 0
_C_KA = _C_QA + A_HEADS * HP
_C_VA = _C_KA + A_HEADS * HP
_C_QI = _C_VA + A_WIDTH
_C_KI = _C_QI + IDX_HEADS * HP
_C_WI = _C_KI + LANES
_C_CQ = _C_WI + LANES
_C_CKV = _C_CQ + B_Q_LORA
_C_KR = _C_CKV + B_KV_LORA
_C_KRS = _C_KR + HP
_C_END = _C_KRS + HP


def _token_prep_kernel(x_ref, g_ref, w1_ref, wqm_ref, wqs_ref, wkm_ref, wv_ref,
                       gqa_ref, gka_ref, gcq_ref, gckv_ref, gqb_ref, gkb_ref, bnd_ref,
                       cos_ref, sin_ref,
                       qaT_ref, ka_ref, vaT_ref, qi_ref, kiT_ref, wi_ref,
                       qbT_ref, kb_ref, vbT_ref):
    x = x_ref[...]
    tm = x.shape[0]
    h = (x * lax.rsqrt(jnp.mean(x * x, axis=-1, keepdims=True) + EPS) * g_ref[...]).astype(BF16)

    def proj(lo, hi):
        return _dot(h, w1_ref[:, lo:hi])

    def head_norm(zh, gain):
        ss = jnp.sum(zh * zh, axis=-1, keepdims=True) * (1.0 / A_HEAD_DIM)
        return zh * lax.rsqrt(ss + EPS) * gain

    ones_rows = jnp.where(lax.broadcasted_iota(I32, (BF16_ROWS, tm), 0) == 0, 1.0, 0.0).astype(BF16)

    def store_vT(ref, v):
        vT = v.T
        for hh in range(A_HEADS):
            ref[VP * hh:VP * hh + B_V, :] = vT[B_V * hh:B_V * (hh + 1), :].astype(BF16)
            ref[VP * hh + B_V:VP * (hh + 1), :] = ones_rows

    spare_a = jnp.where(lax.broadcasted_iota(I32, (1, HP), 1) == A_HEAD_DIM, 1.0, 0.0)
    qa = proj(_C_QA, _C_KA)
    ka = proj(_C_KA, _C_VA)
    gqa = gqa_ref[...] * (A_HEAD_DIM ** -0.5 * LOG2E)
    gka = gka_ref[...]
    for hh in range(A_HEADS):
        blk = slice(HP * hh, HP * (hh + 1))
        qh = head_norm(qa[:, blk], gqa)
        qnorm = jnp.sqrt(jnp.sum(qh * qh, axis=-1, keepdims=True))
        m_a = (qnorm * bnd_ref[0:1, :] + bnd_ref[2:3, hh:hh + 1]) * BOUND_SLACK
        qaT_ref[blk, :] = (qh - m_a * spare_a).T.astype(BF16)
        ka_ref[:, blk] = (head_norm(ka[:, blk], gka) + spare_a).astype(BF16)
    store_vT(vaT_ref, proj(_C_VA, _C_QI))

    qi_ref[...] = proj(_C_QI, _C_KI).astype(BF16)
    kiT_ref[...] = proj(_C_KI, _C_WI).T.astype(BF16)
    wi_ref[...] = proj(_C_WI, _C_CQ) * ((IDX_HEADS ** -0.5) * (IDX_DIM ** -0.5))

    cos = cos_ref[...]
    sin = sin_ref[...]
    spare = jnp.where(lax.broadcasted_iota(I32, (1, HP), 1) == B_QK, 1.0, 0.0)
    cq = proj(_C_CQ, _C_CKV)
    cqn = (cq * lax.rsqrt(jnp.mean(cq * cq, axis=-1, keepdims=True) + EPS) * gcq_ref[...]).astype(BF16)
    qm = _dot(cqn, wqm_ref[...])
    qs = _dot(cqn, wqs_ref[...])
    gqb = gqb_ref[...] * (B_QK ** -0.5 * LOG2E)
    kbound = bnd_ref[1:2, :]
    for hh in range(B_HEADS):
        blk = slice(HP * hh, HP * (hh + 1))
        qh = qm[:, blk] * cos + qs[:, blk] * sin
        ss = jnp.sum(qh * qh, axis=-1, keepdims=True) * (1.0 / B_QK)
        qh = qh * lax.rsqrt(ss + EPS) * gqb
        m_b = jnp.sqrt(jnp.sum(qh * qh, axis=-1, keepdims=True)) * kbound * BOUND_SLACK
        qbT_ref[blk, :] = (qh - m_b * spare).T.astype(BF16)

    ckv = proj(_C_CKV, _C_KR)
    ckvn = (ckv * lax.rsqrt(jnp.mean(ckv * ckv, axis=-1, keepdims=True) + EPS) * gckv_ref[...]).astype(BF16)
    km = _dot(ckvn, wkm_ref[...])
    store_vT(vbT_ref, _dot(ckvn, wv_ref[...]))
    krot = proj(_C_KR, _C_KRS) * cos + proj(_C_KRS, _C_END) * sin
    gkb = gkb_ref[...]
    for hh in range(B_HEADS):
        blk = slice(HP * hh, HP * (hh + 1))
        kh = km[:, blk] + krot
        ss = jnp.sum(kh * kh, axis=-1, keepdims=True) * (1.0 / B_QK)
        kb_ref[:, blk] = (kh * lax.rsqrt(ss + EPS) * gkb + spare).astype(BF16)


def _token_prep(x, g, w1, wqm, wqs, wkm, wv, gqa, gka, gcq, gckv, gqb, gkb, bnd,
                cos_t, sin_t):
    B, S, D = x.shape
    tm = TM_PREP
    nt = S // tm
    row3 = lambda w: pl.BlockSpec((None, tm, w), lambda b, i: (b, i, 0))
    colT = lambda r: pl.BlockSpec((None, r, tm), lambda b, i: (b, 0, i))
    full = lambda a: pl.BlockSpec(a.shape, lambda b, i: (0,) * a.ndim)
    tab = pl.BlockSpec((tm, LANES), lambda b, i: (i, 0))
    consts = (g, w1, wqm, wqs, wkm, wv, gqa, gka, gcq, gckv, gqb, gkb, bnd)
    out_shape = (
        jax.ShapeDtypeStruct((B, A_HEADS * HP, S), BF16),
        jax.ShapeDtypeStruct((B, S, A_HEADS * HP), BF16),
        jax.ShapeDtypeStruct((B, A_HEADS * VP, S), BF16),
        jax.ShapeDtypeStruct((B, S, IDX_HEADS * HP), BF16),
        jax.ShapeDtypeStruct((B, nt, LANES, tm), BF16),
        jax.ShapeDtypeStruct((B, S, LANES), F32),
        jax.ShapeDtypeStruct((B, B_HEADS * HP, S), BF16),
        jax.ShapeDtypeStruct((B, S, B_HEADS * HP), BF16),
        jax.ShapeDtypeStruct((B, B_HEADS * VP, S), BF16),
    )
    kiT_spec = pl.BlockSpec((None, None, LANES, tm), lambda b, i: (b, i, 0, 0))
    out_specs = (colT(A_HEADS * HP), row3(A_HEADS * HP), colT(A_HEADS * VP), row3(IDX_HEADS * HP),
                 kiT_spec, row3(LANES), colT(B_HEADS * HP), row3(B_HEADS * HP), colT(B_HEADS * VP))
    return pl.pallas_call(
        _token_prep_kernel,
        grid=(B, nt),
        in_specs=[row3(D)] + [full(a) for a in consts] + [tab, tab],
        out_specs=out_specs,
        out_shape=out_shape,
        compiler_params=_cparams(("parallel", "parallel")),
        name="token_prep",
    )(x, *consts, cos_t, sin_t)


def _tree_sum(parts):
    while len(parts) > 1:
        nxt = [parts[t] + parts[t + 1] for t in range(0, len(parts) - 1, 2)]
        if len(parts) % 2:
            nxt.append(parts[-1])
        parts = nxt
    return parts[0]


def _bit_transpose32(words):
    a = list(words)
    j, m = 16, 0x0000FFFF
    while j:
        k = 0
        while k < 32:
            t = (a[k] ^ lax.shift_right_logical(a[k + j], jnp.int32(j))) & m
            a[k] = a[k] ^ t
            a[k + j] = a[k + j] ^ (t << j)
            k = (k + j + 1) & ~j
        j >>= 1
        m = m ^ (m << j)
    return a


def _expand_bits(word):
    return jnp.concatenate(
        [lax.shift_right_logical(word, jnp.int32(SEL_BITS - 1 - j)) & 1 for j in range(SEL_BITS)], axis=0)


def _select_kernel(qi_ref, wi_ref, kiT_ref, mask_ref, planes_ref, alive_ref, gt_ref,
                   acca_ref, accb_ref, wb_ref, *, topk):
    S = mask_ref.shape[0]
    tq = TQ_SEL
    i = pl.program_id(1)
    n_keys = (i + 1) * tq
    n_groups = (i + GU_SEL) // GU_SEL

    for hh in range(IDX_HEADS):
        wb_ref[hh] = jnp.broadcast_to(wi_ref[:, hh:hh + 1], (tq, LANES))

    zeros_group = jnp.zeros((GU_SEL, SUBLANES, tq), I32)
    for r in range(SEL_BITS):
        planes_ref[r, pl.ds((n_groups - 1) * GU_SEL, GU_SEL)] = zeros_group

    n_units = n_keys // KU_SEL
    accb_ref[...] = jnp.zeros(accb_ref.shape, F32)

    def heads(u, dst_ref):
        kT = kiT_ref[jnp.minimum(u, n_units - 1)]
        for hh in range(IDX_HEADS):
            d = _dot(qi_ref[:, HP * hh:HP * (hh + 1)], kT)
            for c in range(tq // CH_SEL):
                rows = slice(CH_SEL * c, CH_SEL * (c + 1))
                w = wb_ref[hh, rows, :]
                for lanes in (slice(0, LANES), slice(LANES, 2 * LANES)):
                    t = jnp.maximum(d[rows, lanes], 0.0) * w
                    if hh == 0:
                        dst_ref[rows, lanes] = t
                    else:
                        dst_ref[rows, lanes] += t

    def convert(src_ref, u):
        accT = src_ref[...].T
        words = []
        for j in range(SEL_BITS):
            acc = accT[SUBLANES * j:SUBLANES * (j + 1), :] + 0.0
            bits = pltpu.bitcast(acc, I32)
            words.append(bits ^ ((bits >> 31) | SIGN_BIT))
        unit = jnp.maximum(u, 0)
        for r, plane in enumerate(_bit_transpose32(words)):
            planes_ref[r, unit] = plane

    def unit_pair(t, carry):
        heads(2 * t, acca_ref)
        convert(accb_ref, 2 * t - 1)
        heads(2 * t + 1, accb_ref)
        convert(acca_ref, 2 * t)
        return carry

    lax.fori_loop(0, n_units // 2 + 1, unit_pair, 0)

    full_group = jnp.full((GU_SEL, SUBLANES, tq), -1, I32)

    def init(g, carry):
        unit = g * GU_SEL + lax.broadcasted_iota(I32, (GU_SEL, SUBLANES, tq), 0)
        alive_ref[pl.ds(g * GU_SEL, GU_SEL)] = jnp.where(unit < i, full_group, zeros_group)
        gt_ref[pl.ds(g * GU_SEL, GU_SEL)] = zeros_group
        return carry
    lax.fori_loop(0, n_groups, init, 0)

    q_chunk = (i * tq + lax.broadcasted_iota(I32, (SUBLANES, tq), 1)) >> CHUNK_SHIFT
    k_row = i * tq + lax.broadcasted_iota(I32, (SUBLANES, tq), 0)
    adm_bits = jnp.zeros((SUBLANES, tq), I32)
    for j in range(SEL_BITS):
        adm = ((k_row + SUBLANES * j) >> CHUNK_SHIFT) <= q_chunk
        adm_bits = adm_bits | jnp.where(adm, jnp.int32(1) << (SEL_BITS - 1 - j), 0)
    alive_ref[i] = adm_bits

    def popcount_rows(x):
        return jnp.sum(lax.population_count(x), axis=0)

    def count_alive(plane_of):
        def body(g, acc):
            return acc + popcount_rows(plane_of(g) & alive_ref[pl.ds(g * GU_SEL, GU_SEL)])
        acc = lax.fori_loop(0, n_groups, body, jnp.zeros((SUBLANES, tq), I32))
        return jnp.sum(acc, axis=0, keepdims=True)

    def bit_pass(r, need):
        ones_here = count_alive(lambda g: planes_ref[r, pl.ds(g * GU_SEL, GU_SEL)])
        take = ones_here >= need

        def update(g, carry):
            grp = pl.ds(g * GU_SEL, GU_SEL)
            alive = alive_ref[grp]
            with_bit = alive & planes_ref[r, grp]
            alive_ref[grp] = jnp.where(take, with_bit, alive ^ with_bit)
            gt_ref[grp] = jnp.where(take, gt_ref[grp], gt_ref[grp] | with_bit)
            return carry
        lax.fori_loop(0, n_groups, update, 0)
        return jnp.where(take, need, need - ones_here)

    need = lax.fori_loop(0, SEL_BITS, bit_pass, jnp.full((1, tq), topk, I32))
    n_equal = count_alive(lambda g: full_group)
    any_tie = jnp.max(jnp.where(n_equal > need, 1, 0)) > 0
    drop_unit = jnp.full((tq, tq), NEG, BF16)

    @pl.when(jnp.logical_not(any_tie))
    def _():
        def body(u, carry):
            sel = _expand_bits(gt_ref[u] | alive_ref[u])
            mask_ref[pl.ds(pl.multiple_of(u * KU_SEL, KU_SEL), KU_SEL), :] = (
                jnp.where(sel != 0, 0.0, NEG).astype(BF16))
            return carry
        lax.fori_loop(0, n_units, body, 0)

    @pl.when(any_tie)
    def _():
        n_take = need.astype(F32)
        ltri = (lax.broadcasted_iota(I32, (KU_SEL, KU_SEL), 0)
                >= lax.broadcasted_iota(I32, (KU_SEL, KU_SEL), 1)).astype(BF16)

        def body(u, seen):
            eqf = _expand_bits(alive_ref[u]).astype(F32)
            rank = _dot(ltri, eqf.astype(BF16)) + seen
            sel = jnp.logical_or(_expand_bits(gt_ref[u]) != 0,
                                 jnp.logical_and(eqf != 0.0, rank <= n_take))
            mask_ref[pl.ds(pl.multiple_of(u * KU_SEL, KU_SEL), KU_SEL), :] = (
                jnp.where(sel, 0.0, NEG).astype(BF16))
            return seen + jnp.sum(eqf, axis=0, keepdims=True)
        lax.fori_loop(0, n_units, body, jnp.zeros((1, tq), F32))

    def fill(r, carry):
        r0 = pl.multiple_of(r * tq, tq)
        mask_ref[pl.ds(r0, tq), :] = drop_unit
        return carry
    lax.fori_loop(i + 1, S // tq, fill, 0)


def _select_topk(qi, wi, kiT, topk):
    B, S, _ = qi.shape
    tq = TQ_SEL
    n_units, dims, unit = kiT.shape[1:]
    assert unit == KU_SEL == tq == SEL_BITS * SUBLANES and n_units * unit == S
    n_alloc = -(-n_units // GU_SEL) * GU_SEL
    return pl.pallas_call(
        functools.partial(_select_kernel, topk=topk),
        grid=(B, S // tq),
        in_specs=[
            pl.BlockSpec((None, tq, IDX_HEADS * HP), lambda b, i: (b, i, 0)),
            pl.BlockSpec((None, tq, LANES), lambda b, i: (b, i, 0)),
            pl.BlockSpec((None, n_units, dims, unit), lambda b, i: (b, 0, 0, 0)),
        ],
        out_specs=pl.BlockSpec((None, S, tq), lambda b, i: (b, 0, i)),
        out_shape=jax.ShapeDtypeStruct((B, S, S), BF16),
        scratch_shapes=[pltpu.VMEM((SEL_BITS, n_alloc + 1, SUBLANES, tq), I32),
                        pltpu.VMEM((n_alloc, SUBLANES, tq), I32),
                        pltpu.VMEM((n_alloc, SUBLANES, tq), I32),
                        pltpu.VMEM((tq, KU_SEL), F32), pltpu.VMEM((tq, KU_SEL), F32),
                        pltpu.VMEM((IDX_HEADS, tq, LANES), F32)],
        compiler_params=_cparams(("parallel", "parallel")),
        name="select_topk",
    )(qi, wi, kiT)


def _attend_kernel(ti_ref, tj_ref, safe_ref, *refs, heads, with_mask):
    if with_mask:
        qT_ref, k_ref, vT_ref, mask_ref, bias_ref, o_ref, acc_ref, s_ref, p_ref, m_ref, add_ref = refs
    else:
        qT_ref, k_ref, vT_ref, diag_ref, o_ref, acc_ref, s_ref, p_ref, m_ref = refs
    s_idx = pl.program_id(1)
    i = ti_ref[s_idx]
    jj = tj_ref[s_idx]
    bounded = safe_ref[0] == 1
    tk = TK_ATT
    n_ch = tk // CH_ATT

    @pl.when(jj == 0)
    def _():
        acc_ref[...] = jnp.zeros(acc_ref.shape, F32)
        m_ref[...] = jnp.full(m_ref.shape, NEG, F32)

    def chunks():
        return [slice(CH_ATT * c, CH_ATT * (c + 1)) for c in range(n_ch)]

    def key_block(sb, carry):
        j = jj * NSUB_ATT + sb
        k0 = pl.multiple_of(sb * tk, tk)

        def logits(hh):
            s_ref[hh % 2] = _dot(k_ref[pl.ds(k0, tk), HP * hh:HP * (hh + 1)],
                                 qT_ref[HP * hh:HP * (hh + 1), :])

        def pv(hh, buf):
            return _dot(vT_ref[VP * hh:VP * (hh + 1), pl.ds(k0, tk)], p_ref[buf])

        def sweep_bounded(extra):
            logits(0)
            for hh in range(heads):
                buf = hh % 2
                if hh + 1 < heads:
                    logits(hh + 1)
                for rows in chunks():
                    t = s_ref[buf, rows, :]
                    e = extra(hh, rows)
                    if e is not None:
                        t = t + e
                    p_ref[buf, rows, :] = jnp.exp2(t).astype(BF16)
                acc_ref[VP * hh:VP * (hh + 1), :] += pv(hh, buf)

        def sweep_running_max(extra):
            for hh in range(heads):
                buf = hh % 2
                logits(hh)
                m_blk = jnp.full((SUBLANES, s_ref.shape[2]), NEG, F32)
                for rows in chunks():
                    t = s_ref[buf, rows, :]
                    e = extra(hh, rows)
                    if e is not None:
                        t = t + e
                        s_ref[buf, rows, :] = t
                    for r in range(CH_ATT // SUBLANES):
                        m_blk = jnp.maximum(m_blk, t[SUBLANES * r:SUBLANES * (r + 1), :])
                m_old = m_ref[hh:hh + 1, :]
                m_new = jnp.maximum(m_old, jnp.max(m_blk, axis=0, keepdims=True))
                m_ref[hh:hh + 1, :] = m_new
                for rows in chunks():
                    p_ref[buf, rows, :] = jnp.exp2(s_ref[buf, rows, :] - m_new).astype(BF16)
                acc_ref[VP * hh:VP * (hh + 1), :] = (
                    jnp.exp2(m_old - m_new) * acc_ref[VP * hh:VP * (hh + 1), :] + pv(hh, buf))

        def both(extra):
            @pl.when(bounded)
            def _():
                sweep_bounded(extra)

            @pl.when(jnp.logical_not(bounded))
            def _():
                sweep_running_max(extra)

        if with_mask:
            @pl.when(j <= i)
            def _():
                add_ref[...] = mask_ref[pl.ds(k0, tk), :].astype(F32)

            @pl.when(j < i - 1)
            def _():
                both(lambda hh, rows: add_ref[rows, :])

            @pl.when(jnp.logical_and(j >= i - 1, j <= i))
            def _():
                near = i - j
                both(lambda hh, rows: add_ref[rows, :] + bias_ref[hh, near, rows, :].astype(F32))
        else:
            @pl.when(j < i)
            def _():
                both(lambda hh, rows: None)

            @pl.when(j == i)
            def _():
                both(lambda hh, rows: diag_ref[rows, :])
        return carry

    lax.fori_loop(0, NSUB_ATT, key_block, 0)

    @pl.when(jj == i // NSUB_ATT)
    def _():
        dv = VP - BF16_ROWS
        for hh in range(heads):
            inv = 1.0 / acc_ref[VP * hh + dv:VP * hh + dv + 1, :]
            s_ref[0, dv * hh:dv * (hh + 1), :] = acc_ref[VP * hh:VP * hh + dv, :] * inv
        o_ref[...] = s_ref[0, 0:dv * heads, :].T.astype(o_ref.dtype)


def _pair_tables(S):
    n = S // TQ_ATT
    ti = np.array([i for i in range(n) for _ in range(i // NSUB_ATT + 1)], np.int32)
    tj = np.array([j for i in range(n) for j in range(i // NSUB_ATT + 1)], np.int32)
    return jnp.asarray(ti), jnp.asarray(tj)


def _attend(qT, k, vT, safe, extra_inputs, extra_specs, extra_scratch, with_mask, name):
    B, S, _ = k.shape
    tq, tk = TQ_ATT, TK_ATT
    tg = tk * NSUB_ATT
    assert tq == tk and tk >= B_V * A_HEADS and S % tg == 0
    ti, tj = _pair_tables(S)
    heads = A_HEADS
    kern = functools.partial(_attend_kernel, heads=heads, with_mask=with_mask)
    grid_spec = pltpu.PrefetchScalarGridSpec(
        num_scalar_prefetch=3,
        grid=(B, ti.shape[0]),
        in_specs=[
            pl.BlockSpec((None, heads * HP, tq), lambda b, s, ti, tj, sf: (b, 0, ti[s])),
            pl.BlockSpec((None, tg, heads * HP), lambda b, s, ti, tj, sf: (b, tj[s], 0)),
            pl.BlockSpec((None, heads * VP, tg), lambda b, s, ti, tj, sf: (b, 0, tj[s])),
        ] + extra_specs,
        out_specs=pl.BlockSpec((None, tq, heads * B_V), lambda b, s, ti, tj, sf: (b, ti[s], 0)),
        scratch_shapes=[pltpu.VMEM((heads * VP, tq), F32),
                        pltpu.VMEM((2, tk, tq), F32),
                        pltpu.VMEM((2, tk, tq), BF16),
                        pltpu.VMEM((heads, tq), F32)] + extra_scratch,
    )
    return pl.pallas_call(
        kern, grid_spec=grid_spec,
        out_shape=jax.ShapeDtypeStruct((B, S, heads * B_V), BF16),
        compiler_params=_cparams(("parallel", "arbitrary")),
        name=name,
    )(ti, tj, safe, qT, k, vT, *extra_inputs)


def _attend_a(qaT, ka, vaT, mask, bias_tiles, safe):
    tq, tk = TQ_ATT, TK_ATT
    specs = [
        pl.BlockSpec((None, tk * NSUB_ATT, tq), lambda b, s, ti, tj, sf: (b, tj[s], ti[s])),
        pl.BlockSpec(bias_tiles.shape, lambda b, s, ti, tj, sf: (0, 0, 0, 0)),
    ]
    return _attend(qaT, ka, vaT, safe, (mask, bias_tiles), specs, [pltpu.VMEM((tk, tq), F32)],
                   True, "attend_a")


def _attend_b(qbT, kb, vbT, diag_tile, safe):
    tq, tk = TQ_ATT, TK_ATT
    specs = [pl.BlockSpec((tk, tq), lambda b, s, ti, tj, sf: (0, 0))]
    return _attend(qbT, kb, vbT, safe, (diag_tile,), specs, [], False, "attend_b")


def _merge_kernel(x_ref, ya_ref, yb_ref, g_ref, wga_ref, wgb_ref, bga_ref, bgb_ref,
                  wpa_ref, wpb_ref, wo_ref, o_ref):
    x = x_ref[...]
    h = (x * lax.rsqrt(jnp.mean(x * x, axis=-1, keepdims=True) + EPS) * g_ref[...]).astype(BF16)
    gate_a = jax.nn.sigmoid(_dot(h, wga_ref[...]) + bga_ref[...])
    gate_b = jax.nn.sigmoid(_dot(h, wgb_ref[...]) + bgb_ref[...])
    merged = gate_a * _dot(ya_ref[...], wpa_ref[...]) + gate_b * _dot(yb_ref[...], wpb_ref[...])
    o_ref[...] = x + _dot(merged.astype(BF16), wo_ref[...])


def _merge_out(x2, ya2, yb2, g, wga, wgb, bga, bgb, wpa, wpb, wo):
    R, D = x2.shape
    tm = TM_MERGE
    row = lambda w: pl.BlockSpec((tm, w), lambda i: (i, 0))
    full = lambda a: pl.BlockSpec(a.shape, lambda i: (0,) * a.ndim)
    consts = (g, wga, wgb, bga, bgb, wpa, wpb, wo)
    return pl.pallas_call(
        _merge_kernel,
        grid=(R // tm,),
        in_specs=[row(D), row(A_WIDTH), row(B_WIDTH)] + [full(a) for a in consts],
        out_specs=row(D),
        out_shape=jax.ShapeDtypeStruct((R, D), F32),
        compiler_params=_cparams(("parallel",)),
        name="merge_out",
    )(x2, ya2, yb2, *consts)


def _ffn_kernel(x_ref, xp_ref, g_ref, wuv_ref, wug_ref, cwv_ref, cwg_ref, cbv_ref, cbg_ref, wd_ref,
                o_ref, uv_ref, ug_ref, acc_ref, *, tiles_per_seq):
    i = pl.program_id(0)
    f = pl.program_id(1)
    tm = x_ref.shape[0]
    halo = SUBLANES
    g = g_ref[...]

    def normed(v):
        return (v * lax.rsqrt(jnp.mean(v * v, axis=-1, keepdims=True) + EPS) * g).astype(BF16)

    h = normed(x_ref[...])
    keep = jnp.where(i % tiles_per_seq == 0, 0.0, 1.0)
    hp = normed(xp_ref[...])

    def conv(u_ref, w_ref, cw_ref, cb_ref):
        u_ref[0:halo, :] = _dot(hp, w_ref[...]) * keep
        u_ref[halo:halo + tm, :] = _dot(h, w_ref[...])
        out = cb_ref[...]
        for t in range(CONV_W):
            lo = halo - (CONV_W - 1) + t
            out = out + cw_ref[t:t + 1, :] * u_ref[lo:lo + tm, :]
        return out

    val = conv(uv_ref, wuv_ref, cwv_ref, cbv_ref)
    gat = conv(ug_ref, wug_ref, cwg_ref, cbg_ref)
    act = (gat * jax.nn.sigmoid(gat) * val).astype(BF16)
    part = _dot(act, wd_ref[...])

    @pl.when(f == 0)
    def _():
        acc_ref[...] = part

    @pl.when(f > 0)
    def _():
        acc_ref[...] = acc_ref[...] + part

    @pl.when(f == pl.num_programs(1) - 1)
    def _():
        o_ref[...] = x_ref[...] + acc_ref[...]


def _conv_ffn(x2, S, g, wu, cw, cb, wd):
    R, D = x2.shape
    tm, tf = TM_FFN, TF_FFN
    nf = D_FF // tf
    halo_blocks = tm // SUBLANES
    kern = functools.partial(_ffn_kernel, tiles_per_seq=S // tm)
    return pl.pallas_call(
        kern,
        grid=(R // tm, nf),
        in_specs=[
            pl.BlockSpec((tm, D), lambda i, f: (i, 0)),
            pl.BlockSpec((SUBLANES, D), lambda i, f: (jnp.maximum(i * halo_blocks - 1, 0), 0)),
            pl.BlockSpec((1, D), lambda i, f: (0, 0)),
            pl.BlockSpec((D, tf), lambda i, f: (0, f)),
            pl.BlockSpec((D, tf), lambda i, f: (0, nf + f)),
            pl.BlockSpec((CONV_W, tf), lambda i, f: (0, f)),
            pl.BlockSpec((CONV_W, tf), lambda i, f: (0, nf + f)),
            pl.BlockSpec((1, tf), lambda i, f: (0, f)),
            pl.BlockSpec((1, tf), lambda i, f: (0, nf + f)),
            pl.BlockSpec((tf, D), lambda i, f: (f, 0)),
        ],
        out_specs=pl.BlockSpec((tm, D), lambda i, f: (i, 0)),
        out_shape=jax.ShapeDtypeStruct((R, D), F32),
        scratch_shapes=[pltpu.VMEM((tm + SUBLANES, tf), F32), pltpu.VMEM((tm + SUBLANES, tf), F32),
                        pltpu.VMEM((tm, D), F32)],
        compiler_params=_cparams(("parallel", "arbitrary")),
        name="conv_ffn",
    )(x2, x2, g, wu, wu, cw, cw, cb, cb, wd)


def _t5_bucket(rel):
    nb = REL_BUCKETS // 2
    max_exact = nb // 2
    side = jnp.where(rel > 0, nb, 0)
    n = jnp.abs(rel)
    nf = jnp.maximum(n, 1).astype(F32)
    large = max_exact + (jnp.log(nf / max_exact) / math.log(REL_MAX_DIST / max_exact)
                         * (nb - max_exact)).astype(I32)
    large = jnp.minimum(large, nb - 1)
    return side + jnp.where(n < max_exact, n, large)


def _bias_tiles(rel_bias):
    tq, tk = TQ_ATT, TK_ATT
    assert tq == tk and tk >= REL_MAX_DIST
    kk = jnp.arange(tk, dtype=I32)[:, None]
    qq = jnp.arange(tq, dtype=I32)[None, :]
    bucket = _t5_bucket(jnp.stack([kk - qq, kk - qq - tk])).reshape(1, 2 * tk * tq)
    onehot = (jnp.arange(REL_BUCKETS, dtype=I32)[:, None] == bucket).astype(F32)
    far = rel_bias[_t5_bucket(jnp.asarray(-REL_MAX_DIST, I32))]
    table = ((rel_bias - far[None, :]) * LOG2E).T
    tiles = jnp.dot(table, onehot, precision=lax.Precision.HIGHEST)
    return tiles.reshape(A_HEADS, 2, tk, tq).astype(BF16), jnp.max(jnp.abs(table), axis=1)


def _diag_tile():
    kk = np.arange(TK_ATT)[:, None] // CHUNK
    qq = np.arange(TQ_ATT)[None, :] // CHUNK
    return jnp.asarray(np.where(kk <= qq, 0.0, NEG).astype(np.float32))


def _rope_tables(S):
    half = B_ROPE // 2
    inv = ROPE_BASE ** (-jnp.arange(half, dtype=F32) / half)
    ang = jnp.arange(S, dtype=I32).astype(F32)[:, None] * inv[None, :]
    cos, sin = jnp.cos(ang), jnp.sin(ang)
    ones = jnp.ones((S, B_NOPE), F32)
    zeros_n = jnp.zeros((S, B_NOPE), F32)
    zeros_p = jnp.zeros((S, HP - B_QK), F32)
    cos_t = jnp.concatenate([ones, cos, cos, zeros_p], axis=1)
    sin_t = jnp.concatenate([zeros_n, -sin, sin, zeros_p], axis=1)
    return cos_t, sin_t


def _pad_cols(w, width):
    return jnp.pad(w, ((0, 0), (0, width - w.shape[1])))


def _pad_heads(w, heads, width):
    rows = w.shape[0]
    w = w.reshape(rows, heads, -1)
    return jnp.pad(w, ((0, 0), (0, 0), (0, width - w.shape[2]))).reshape(rows, heads * width)


def _swap_halves(w):
    half = w.shape[-1] // 2
    return jnp.concatenate([w[..., half:], w[..., :half]], axis=-1)


def _layer_weights(l, w_in, b_w_uq, b_w_ukv, b_q_norm, b_k_norm, a_q_norm, a_k_norm, bias_max):
    w = w_in[l]
    o = np.cumsum([0, A_WIDTH, A_WIDTH, A_WIDTH, IDX_HEADS * IDX_DIM, IDX_DIM, IDX_HEADS,
                   B_Q_LORA, B_KV_LORA, B_ROPE, D_MODEL, D_MODEL])
    seg = [w[:, o[t]:o[t + 1]] for t in range(11)]
    zn = jnp.zeros((D_MODEL, B_NOPE), F32)
    kr = seg[8]
    w1 = jnp.concatenate([
        _pad_heads(seg[0], A_HEADS, HP), _pad_heads(seg[1], A_HEADS, HP), seg[2],
        _pad_heads(seg[3], IDX_HEADS, HP),
        _pad_cols(seg[4], LANES), _pad_cols(seg[5], LANES), seg[6], seg[7],
        _pad_cols(jnp.concatenate([zn, kr], axis=1), HP),
        _pad_cols(jnp.concatenate([zn, _swap_halves(kr)], axis=1), HP),
    ], axis=1).astype(BF16)
    assert w1.shape[1] == _C_END

    uq = b_w_uq[l].reshape(B_Q_LORA, B_HEADS, B_QK)
    zq = jnp.zeros((B_Q_LORA, B_HEADS, B_NOPE), F32)
    pq = jnp.zeros((B_Q_LORA, B_HEADS, HP - B_QK), F32)
    wqm = jnp.concatenate([uq, pq], axis=-1).reshape(B_Q_LORA, B_HEADS * HP).astype(BF16)
    wqs = jnp.concatenate([zq, _swap_halves(uq[..., B_NOPE:]), pq], axis=-1)
    wqs = wqs.reshape(B_Q_LORA, B_HEADS * HP).astype(BF16)

    ukv = b_w_ukv[l].reshape(B_KV_LORA, B_HEADS, B_NOPE + B_V)
    pk = jnp.zeros((B_KV_LORA, B_HEADS, HP - B_NOPE), F32)
    wkm = jnp.concatenate([ukv[..., :B_NOPE], pk], axis=-1).reshape(B_KV_LORA, B_HEADS * HP).astype(BF16)
    wv = ukv[..., B_NOPE:].reshape(B_KV_LORA, B_WIDTH).astype(BF16)

    gqa = _pad_cols(a_q_norm[l][None, :], HP)
    gka = _pad_cols(a_k_norm[l][None, :], HP)
    gqb = _pad_cols(b_q_norm[l][None, :], HP)
    gkb = _pad_cols(b_k_norm[l][None, :], HP)

    ka_bound = math.sqrt(A_HEAD_DIM) * jnp.max(jnp.abs(a_k_norm[l]))
    kb_bound = math.sqrt(B_QK) * jnp.max(jnp.abs(b_k_norm[l]))
    bnd = jnp.zeros((SUBLANES, LANES), F32)
    bnd = bnd.at[0, :].set(ka_bound).at[1, :].set(kb_bound).at[2, :A_HEADS].set(bias_max)
    qa_bound = math.sqrt(A_HEAD_DIM) * jnp.max(jnp.abs(a_q_norm[l])) * (A_HEAD_DIM ** -0.5 * LOG2E)
    qb_bound = math.sqrt(B_QK) * jnp.max(jnp.abs(b_q_norm[l])) * (B_QK ** -0.5 * LOG2E)
    range_a = (qa_bound * ka_bound + jnp.max(bias_max)) * BOUND_SLACK + jnp.max(bias_max)
    range_b = qb_bound * kb_bound * BOUND_SLACK
    safe_a = (range_a <= MAX_LOG2_RANGE).astype(I32).reshape(1)
    safe_b = (range_b <= MAX_LOG2_RANGE).astype(I32).reshape(1)
    return (w1, wqm, wqs, wkm, wv, gqa, gka, gqb, gkb, bnd, safe_a, safe_b,
            seg[9].astype(BF16), seg[10].astype(BF16))


def kernel(x, rel_bias, norm_mix, w_in, a_q_norm, a_k_norm, b_cq_norm, b_ckv_norm, b_w_uq, b_w_ukv,
           b_q_norm, b_k_norm, w_proj_a, w_proj_b, b_gate, w_out, norm_ffn, w_up, conv_w, conv_b, w_down):
    B, S, D = x.shape
    assert D == D_MODEL and S % TQ_ATT == 0 and S % TM_FFN == 0
    topk = min(TOPK_MAX, S // 4)
    assert TQ_SEL >= topk
    depth = w_in.shape[0]

    cos_t, sin_t = _rope_tables(S)
    bias_tiles, bias_max = _bias_tiles(rel_bias)
    diag_tile = _diag_tile()

    for l in range(depth):
        (w1, wqm, wqs, wkm, wv, gqa, gka, gqb, gkb, bnd, safe_a, safe_b, wga, wgb) = _layer_weights(
            l, w_in, b_w_uq, b_w_ukv, b_q_norm, b_k_norm, a_q_norm, a_k_norm, bias_max)
        qaT, ka, vaT, qi, kiT, wi, qbT, kb, vbT = _token_prep(
            x, norm_mix[l][None, :], w1, wqm, wqs, wkm, wv, gqa, gka,
            b_cq_norm[l][None, :], b_ckv_norm[l][None, :], gqb, gkb, bnd,
            cos_t, sin_t)
        mask = _select_topk(qi, wi, kiT, topk)
        y_a = _attend_a(qaT, ka, vaT, mask, bias_tiles, safe_a)
        y_b = _attend_b(qbT, kb, vbT, diag_tile, safe_b)
        x2 = _merge_out(
            x.reshape(B * S, D), y_a.reshape(B * S, A_WIDTH), y_b.reshape(B * S, B_WIDTH),
            norm_mix[l][None, :], wga, wgb, b_gate[l][None, :D_MODEL], b_gate[l][None, D_MODEL:],
            w_proj_a[l].astype(BF16), w_proj_b[l].astype(BF16), w_out[l].astype(BF16))
        x2 = _conv_ffn(x2, S, norm_ffn[l][None, :], w_up[l].astype(BF16), conv_w[l], conv_b[l][None, :],
                       w_down[l].astype(BF16))
        x = x2.reshape(B, S, D)
    return x
```

```python
import functools
import math

import numpy as np
import jax
import jax.numpy as jnp
from jax import lax
from jax.experimental import pallas as pl
from jax.experimental.pallas import tpu as pltpu

F32 = jnp.float32
BF16 = jnp.bfloat16
I32 = jnp.int32
I16 = jnp.int16

D_MODEL = 1024
CHUNK = 64
CHUNK_SHIFT = 6
assert 1 << CHUNK_SHIFT == CHUNK
A_HEADS = 8
A_HEAD_DIM = 64
A_WIDTH = A_HEADS * A_HEAD_DIM
IDX_HEADS = 8
IDX_DIM = 64
TOPK_MAX = 256
B_HEADS = 8
B_Q_LORA = 384
B_KV_LORA = 256
B_NOPE = 64
B_ROPE = 32
B_QK = B_NOPE + B_ROPE
B_V = 64
B_WIDTH = B_HEADS * B_V
ROPE_BASE = 10000.0
REL_BUCKETS = 32
REL_MAX_DIST = 128
D_FF = 2816
CONV_W = 3
EPS = 1e-6

LANES = 128
SUBLANES = 8
BF16_ROWS = 16
VMEM_LIMIT = 56 * 1024 * 1024

TM_PREP = 256
TQ_SEL = 256
KU_SEL = 256
CH_SEL = 64
SEL_BITS = 32
GU_SEL = 16
TQ_ATT = 512
TK_ATT = 512
NSUB_ATT = 2
CH_ATT = 64
TM_MERGE = 256
TM_FFN = 512
TF_FFN = 1408

HP = 128
VP = B_V + BF16_ROWS
NEG = -1e30
SIGN_BIT = -2147483648
LOG2E = 1.4426950408889634
BOUND_SLACK = 1.01
MAX_LOG2_RANGE = 60.0


def _cparams(sem):
    return pltpu.CompilerParams(dimension_semantics=sem, vmem_limit_bytes=VMEM_LIMIT)


def _dot(a, b):
    return jnp.dot(a, b, preferred_element_type=F32)


_C_QA = 0
_C_KA = _C_QA + A_HEADS * HP
_C_VA = _C_KA + A_HEADS * HP
_C_QI = _C_VA + A_WIDTH
_C_KI = _C_QI + IDX_HEADS * HP
_C_WI = _C_KI + LANES
_C_CQ = _C_WI + LANES
_C_CKV = _C_CQ + B_Q_LORA
_C_KR = _C_CKV + B_KV_LORA
_C_KRS = _C_KR + HP
_C_END = _C_KRS + HP


def _token_prep_kernel(x_ref, g_ref, w1_ref, wqm_ref, wqs_ref, wkm_ref, wv_ref,
                       gqa_ref, gka_ref, gcq_ref, gckv_ref, gqb_ref, gkb_ref, bnd_ref,
                       cos_ref, sin_ref,
                       qaT_ref, ka_ref, vaT_ref, qi_ref, kiT_ref, wi_ref,
                       qbT_ref, kb_ref, vbT_ref):
    x = x_ref[...]
    tm = x.shape[0]
    h = (x * lax.rsqrt(jnp.mean(x * x, axis=-1, keepdims=True) + EPS) * g_ref[...]).astype(BF16)

    def proj(lo, hi):
        return _dot(h, w1_ref[:, lo:hi])

    def head_norm(zh, gain):
        ss = jnp.sum(zh * zh, axis=-1, keepdims=True) * (1.0 / A_HEAD_DIM)
        return zh * lax.rsqrt(ss + EPS) * gain

    ones_rows = jnp.where(lax.broadcasted_iota(I32, (BF16_ROWS, tm), 0) == 0, 1.0, 0.0).astype(BF16)

    def store_vT(ref, v):
        vT = v.T
        for hh in range(A_HEADS):
            ref[VP * hh:VP * hh + B_V, :] = vT[B_V * hh:B_V * (hh + 1), :].astype(BF16)
            ref[VP * hh + B_V:VP * (hh + 1), :] = ones_rows

    spare_a = jnp.where(lax.broadcasted_iota(I32, (1, HP), 1) == A_HEAD_DIM, 1.0, 0.0)
    qa = proj(_C_QA, _C_KA)
    ka = proj(_C_KA, _C_VA)
    gqa = gqa_ref[...] * (A_HEAD_DIM ** -0.5 * LOG2E)
    gka = gka_ref[...]
    for hh in range(A_HEADS):
        blk = slice(HP * hh, HP * (hh + 1))
        qh = head_norm(qa[:, blk], gqa)
        qnorm = jnp.sqrt(jnp.sum(qh * qh, axis=-1, keepdims=True))
        m_a = (qnorm * bnd_ref[0:1, :] + bnd_ref[2:3, hh:hh + 1]) * BOUND_SLACK
        qaT_ref[blk, :] = (qh - m_a * spare_a).T.astype(BF16)
        ka_ref[:, blk] = (head_norm(ka[:, blk], gka) + spare_a).astype(BF16)
    store_vT(vaT_ref, proj(_C_VA, _C_QI))

    qi_ref[...] = proj(_C_QI, _C_KI).astype(BF16)
    kiT_ref[...] = proj(_C_KI, _C_WI).T.astype(BF16)
    wi_ref[...] = proj(_C_WI, _C_CQ) * ((IDX_HEADS ** -0.5) * (IDX_DIM ** -0.5))

    cos = cos_ref[...]
    sin = sin_ref[...]
    spare = jnp.where(lax.broadcasted_iota(I32, (1, HP), 1) == B_QK, 1.0, 0.0)
    cq = proj(_C_CQ, _C_CKV)
    cqn = (cq * lax.rsqrt(jnp.mean(cq * cq, axis=-1, keepdims=True) + EPS) * gcq_ref[...]).astype(BF16)
    qm = _dot(cqn, wqm_ref[...])
    qs = _dot(cqn, wqs_ref[...])
    gqb = gqb_ref[...] * (B_QK ** -0.5 * LOG2E)
    kbound = bnd_ref[1:2, :]
    for hh in range(B_HEADS):
        blk = slice(HP * hh, HP * (hh + 1))
        qh = qm[:, blk] * cos + qs[:, blk] * sin
        ss = jnp.sum(qh * qh, axis=-1, keepdims=True) * (1.0 / B_QK)
        qh = qh * lax.rsqrt(ss + EPS) * gqb
        m_b = jnp.sqrt(jnp.sum(qh * qh, axis=-1, keepdims=True)) * kbound * BOUND_SLACK
        qbT_ref[blk, :] = (qh - m_b * spare).T.astype(BF16)

    ckv = proj(_C_CKV, _C_KR)
    ckvn = (ckv * lax.rsqrt(jnp.mean(ckv * ckv, axis=-1, keepdims=True) + EPS) * gckv_ref[...]).astype(BF16)
    km = _dot(ckvn, wkm_ref[...])
    store_vT(vbT_ref, _dot(ckvn, wv_ref[...]))
    krot = proj(_C_KR, _C_KRS) * cos + proj(_C_KRS, _C_END) * sin
    gkb = gkb_ref[...]
    for hh in range(B_HEADS):
        blk = slice(HP * hh, HP * (hh + 1))
        kh = km[:, blk] + krot
        ss = jnp.sum(kh * kh, axis=-1, keepdims=True) * (1.0 / B_QK)
        kb_ref[:, blk] = (kh * lax.rsqrt(ss + EPS) * gkb + spare).astype(BF16)


def _token_prep(x, g, w1, wqm, wqs, wkm, wv, gqa, gka, gcq, gckv, gqb, gkb, bnd,
                cos_t, sin_t):
    B, S, D = x.shape
    tm = TM_PREP
    nt = S // tm
    row3 = lambda w: pl.BlockSpec((None, tm, w), lambda b, i: (b, i, 0))
    colT = lambda r: pl.BlockSpec((None, r, tm), lambda b, i: (b, 0, i))
    full = lambda a: pl.BlockSpec(a.shape, lambda b, i: (0,) * a.ndim)
    tab = pl.BlockSpec((tm, LANES), lambda b, i: (i, 0))
    consts = (g, w1, wqm, wqs, wkm, wv, gqa, gka, gcq, gckv, gqb, gkb, bnd)
    out_shape = (
        jax.ShapeDtypeStruct((B, A_HEADS * HP, S), BF16),
        jax.ShapeDtypeStruct((B, S, A_HEADS * HP), BF16),
        jax.ShapeDtypeStruct((B, A_HEADS * VP, S), BF16),
        jax.ShapeDtypeStruct((B, S, IDX_HEADS * HP), BF16),
        jax.ShapeDtypeStruct((B, nt, LANES, tm), BF16),
        jax.ShapeDtypeStruct((B, S, LANES), F32),
        jax.ShapeDtypeStruct((B, B_HEADS * HP, S), BF16),
        jax.ShapeDtypeStruct((B, S, B_HEADS * HP), BF16),
        jax.ShapeDtypeStruct((B, B_HEADS * VP, S), BF16),
    )
    kiT_spec = pl.BlockSpec((None, None, LANES, tm), lambda b, i: (b, i, 0, 0))
    out_specs = (colT(A_HEADS * HP), row3(A_HEADS * HP), colT(A_HEADS * VP), row3(IDX_HEADS * HP),
                 kiT_spec, row3(LANES), colT(B_HEADS * HP), row3(B_HEADS * HP), colT(B_HEADS * VP))
    return pl.pallas_call(
        _token_prep_kernel,
        grid=(B, nt),
        in_specs=[row3(D)] + [full(a) for a in consts] + [tab, tab],
        out_specs=out_specs,
        out_shape=out_shape,
        compiler_params=_cparams(("parallel", "parallel")),
        name="token_prep",
    )(x, *consts, cos_t, sin_t)


def _tree_sum(parts):
    while len(parts) > 1:
        nxt = [parts[t] + parts[t + 1] for t in range(0, len(parts) - 1, 2)]
        if len(parts) % 2:
            nxt.append(parts[-1])
        parts = nxt
    return parts[0]


def _bit_transpose32(words):
    a = list(words)
    j, m = 16, 0x0000FFFF
    while j:
        k = 0
        while k < 32:
            t = (a[k] ^ lax.shift_right_logical(a[k + j], jnp.int32(j))) & m
            a[k] = a[k] ^ t
            a[k + j] = a[k + j] ^ (t << j)
            k = (k + j + 1) & ~j
        j >>= 1
        m = m ^ (m << j)
    return a


def _expand_bits(word):
    return jnp.concatenate(
        [lax.shift_right_logical(word, jnp.int32(SEL_BITS - 1 - j)) & 1 for j in range(SEL_BITS)], axis=0)


def _select_kernel(qi_ref, wi_ref, kiT_ref, mask_ref, planes_ref, alive_ref, gt_ref,
                   acca_ref, accb_ref, wb_ref, *, topk):
    S = mask_ref.shape[0]
    tq = TQ_SEL
    i = pl.program_id(1)
    n_keys = (i + 1) * tq
    n_groups = (i + GU_SEL) // GU_SEL

    for hh in range(IDX_HEADS):
        wb_ref[hh] = jnp.broadcast_to(wi_ref[:, hh:hh + 1], (tq, LANES))

    zeros_group = jnp.zeros((GU_SEL, SUBLANES, tq), I32)
    for r in range(SEL_BITS):
        planes_ref[r, pl.ds((n_groups - 1) * GU_SEL, GU_SEL)] = zeros_group

    n_units = n_keys // KU_SEL
    accb_ref[...] = jnp.zeros(accb_ref.shape, F32)

    def heads(u, dst_ref):
        kT = kiT_ref[jnp.minimum(u, n_units - 1)]
        for hh in range(IDX_HEADS):
            d = _dot(qi_ref[:, HP * hh:HP * (hh + 1)], kT)
            for c in range(tq // CH_SEL):
                rows = slice(CH_SEL * c, CH_SEL * (c + 1))
                w = wb_ref[hh, rows, :]
                for lanes in (slice(0, LANES), slice(LANES, 2 * LANES)):
                    t = jnp.maximum(d[rows, lanes], 0.0) * w
                    if hh == 0:
                        dst_ref[rows, lanes] = t
                    else:
                        dst_ref[rows, lanes] += t

    def convert(src_ref, u):
        accT = src_ref[...].T
        words = []
        for j in range(SEL_BITS):
            acc = accT[SUBLANES * j:SUBLANES * (j + 1), :] + 0.0
            bits = pltpu.bitcast(acc, I32)
            words.append(bits ^ ((bits >> 31) | SIGN_BIT))
        unit = jnp.maximum(u, 0)
        for r, plane in enumerate(_bit_transpose32(words)):
            planes_ref[r, unit] = plane

    def unit_pair(t, carry):
        heads(2 * t, acca_ref)
        convert(accb_ref, 2 * t - 1)
        heads(2 * t + 1, accb_ref)
        convert(acca_ref, 2 * t)
        return carry

    lax.fori_loop(0, n_units // 2 + 1, unit_pair, 0)

    full_group = jnp.full((GU_SEL, SUBLANES, tq), -1, I32)

    def init(g, carry):
        unit = g * GU_SEL + lax.broadcasted_iota(I32, (GU_SEL, SUBLANES, tq), 0)
        alive_ref[pl.ds(g * GU_SEL, GU_SEL)] = jnp.where(unit < i, full_group, zeros_group)
        gt_ref[pl.ds(g * GU_SEL, GU_SEL)] = zeros_group
        return carry
    lax.fori_loop(0, n_groups, init, 0)

    q_chunk = (i * tq + lax.broadcasted_iota(I32, (SUBLANES, tq), 1)) >> CHUNK_SHIFT
    k_row = i * tq + lax.broadcasted_iota(I32, (SUBLANES, tq), 0)
    adm_bits = jnp.zeros((SUBLANES, tq), I32)
    for j in range(SEL_BITS):
        adm = ((k_row + SUBLANES * j) >> CHUNK_SHIFT) <= q_chunk
        adm_bits = adm_bits | jnp.where(adm, jnp.int32(1) << (SEL_BITS - 1 - j), 0)
    alive_ref[i] = adm_bits

    def popcount_rows(x):
        return jnp.sum(lax.population_count(x), axis=0)

    def count_alive(plane_of):
        def body(g, acc):
            return acc + popcount_rows(plane_of(g) & alive_ref[pl.ds(g * GU_SEL, GU_SEL)])
        acc = lax.fori_loop(0, n_groups, body, jnp.zeros((SUBLANES, tq), I32))
        return jnp.sum(acc, axis=0, keepdims=True)

    def bit_pass(r, need):
        ones_here = count_alive(lambda g: planes_ref[r, pl.ds(g * GU_SEL, GU_SEL)])
        take = ones_here >= need

        def update(g, carry):
            grp = pl.ds(g * GU_SEL, GU_SEL)
            alive = alive_ref[grp]
            with_bit = alive & planes_ref[r, grp]
            alive_ref[grp] = jnp.where(take, with_bit, alive ^ with_bit)
            gt_ref[grp] = jnp.where(take, gt_ref[grp], gt_ref[grp] | with_bit)
            return carry
        lax.fori_loop(0, n_groups, update, 0)
        return jnp.where(take, need, need - ones_here)

    need = lax.fori_loop(0, SEL_BITS, bit_pass, jnp.full((1, tq), topk, I32))
    n_equal = count_alive(lambda g: full_group)
    any_tie = jnp.max(jnp.where(n_equal > need, 1, 0)) > 0
    drop_unit = jnp.full((tq, tq), NEG, BF16)

    @pl.when(jnp.logical_not(any_tie))
    def _():
        def body(u, carry):
            sel = _expand_bits(gt_ref[u] | alive_ref[u])
            mask_ref[pl.ds(pl.multiple_of(u * KU_SEL, KU_SEL), KU_SEL), :] = (
                jnp.where(sel != 0, 0.0, NEG).astype(BF16))
            return carry
        lax.fori_loop(0, n_units, body, 0)

    @pl.when(any_tie)
    def _():
        n_take = need.astype(F32)
        ltri = (lax.broadcasted_iota(I32, (KU_SEL, KU_SEL), 0)
                >= lax.broadcasted_iota(I32, (KU_SEL, KU_SEL), 1)).astype(BF16)

        def body(u, seen):
            eqf = _expand_bits(alive_ref[u]).astype(F32)
            rank = _dot(ltri, eqf.astype(BF16)) + seen
            sel = jnp.logical_or(_expand_bits(gt_ref[u]) != 0,
                                 jnp.logical_and(eqf != 0.0, rank <= n_take))
            mask_ref[pl.ds(pl.multiple_of(u * KU_SEL, KU_SEL), KU_SEL), :] = (
                jnp.where(sel, 0.0, NEG).astype(BF16))
            return seen + jnp.sum(eqf, axis=0, keepdims=True)
        lax.fori_loop(0, n_units, body, jnp.zeros((1, tq), F32))

    def fill(r, carry):
        r0 = pl.multiple_of(r * tq, tq)
        mask_ref[pl.ds(r0, tq), :] = drop_unit
        return carry
    lax.fori_loop(i + 1, S // tq, fill, 0)


def _select_topk(qi, wi, kiT, topk):
    B, S, _ = qi.shape
    tq = TQ_SEL
    n_units, dims, unit = kiT.shape[1:]
    assert unit == KU_SEL == tq == SEL_BITS * SUBLANES and n_units * unit == S
    n_alloc = -(-n_units // GU_SEL) * GU_SEL
    return pl.pallas_call(
        functools.partial(_select_kernel, topk=topk),
        grid=(B, S // tq),
        in_specs=[
            pl.BlockSpec((None, tq, IDX_HEADS * HP), lambda b, i: (b, i, 0)),
            pl.BlockSpec((None, tq, LANES), lambda b, i: (b, i, 0)),
            pl.BlockSpec((None, n_units, dims, unit), lambda b, i: (b, 0, 0, 0)),
        ],
        out_specs=pl.BlockSpec((None, S, tq), lambda b, i: (b, 0, i)),
        out_shape=jax.ShapeDtypeStruct((B, S, S), BF16),
        scratch_shapes=[pltpu.VMEM((SEL_BITS, n_alloc + 1, SUBLANES, tq), I32),
                        pltpu.VMEM((n_alloc, SUBLANES, tq), I32),
                        pltpu.VMEM((n_alloc, SUBLANES, tq), I32),
                        pltpu.VMEM((tq, KU_SEL), F32), pltpu.VMEM((tq, KU_SEL), F32),
                        pltpu.VMEM((IDX_HEADS, tq, LANES), F32)],
        compiler_params=_cparams(("parallel", "parallel")),
        name="select_topk",
    )(qi, wi, kiT)


def _attend_kernel(ti_ref, tj_ref, safe_ref, *refs, heads, with_mask):
    if with_mask:
        qT_ref, k_ref, vT_ref, mask_ref, bias_ref, o_ref, acc_ref, s_ref, p_ref, m_ref, add_ref = refs
    else:
        qT_ref, k_ref, vT_ref, diag_ref, o_ref, acc_ref, s_ref, p_ref, m_ref = refs
    s_idx = pl.program_id(1)
    i = ti_ref[s_idx]
    jj = tj_ref[s_idx]
    bounded = safe_ref[0] == 1
    tk = TK_ATT
    n_ch = tk // CH_ATT

    @pl.when(jj == 0)
    def _():
        acc_ref[...] = jnp.zeros(acc_ref.shape, F32)
        m_ref[...] = jnp.full(m_ref.shape, NEG, F32)

    def chunks():
        return [slice(CH_ATT * c, CH_ATT * (c + 1)) for c in range(n_ch)]

    def key_block(sb, carry):
        j = jj * NSUB_ATT + sb
        k0 = pl.multiple_of(sb * tk, tk)

        def logits(hh):
            s_ref[hh % 2] = _dot(k_ref[pl.ds(k0, tk), HP * hh:HP * (hh + 1)],
                                 qT_ref[HP * hh:HP * (hh + 1), :])

        def pv(hh, buf):
            return _dot(vT_ref[VP * hh:VP * (hh + 1), pl.ds(k0, tk)], p_ref[buf])

        def sweep_bounded(extra):
            logits(0)
            for hh in range(heads):
                buf = hh % 2
                if hh + 1 < heads:
                    logits(hh + 1)
                for rows in chunks():
                    t = s_ref[buf, rows, :]
                    e = extra(hh, rows)
                    if e is not None:
                        t = t + e
                    p_ref[buf, rows, :] = jnp.exp2(t).astype(BF16)
                acc_ref[VP * hh:VP * (hh + 1), :] += pv(hh, buf)

        def sweep_running_max(extra):
            for hh in range(heads):
                buf = hh % 2
                logits(hh)
                m_blk = jnp.full((SUBLANES, s_ref.shape[2]), NEG, F32)
                for rows in chunks():
                    t = s_ref[buf, rows, :]
                    e = extra(hh, rows)
                    if e is not None:
                        t = t + e
                        s_ref[buf, rows, :] = t
                    for r in range(CH_ATT // SUBLANES):
                        m_blk = jnp.maximum(m_blk, t[SUBLANES * r:SUBLANES * (r + 1), :])
                m_old = m_ref[hh:hh + 1, :]
                m_new = jnp.maximum(m_old, jnp.max(m_blk, axis=0, keepdims=True))
                m_ref[hh:hh + 1, :] = m_new
                for rows in chunks():
                    p_ref[buf, rows, :] = jnp.exp2(s_ref[buf, rows, :] - m_new).astype(BF16)
                acc_ref[VP * hh:VP * (hh + 1), :] = (
                    jnp.exp2(m_old - m_new) * acc_ref[VP * hh:VP * (hh + 1), :] + pv(hh, buf))

        def both(extra):
            @pl.when(bounded)
            def _():
                sweep_bounded(extra)

            @pl.when(jnp.logical_not(bounded))
            def _():
                sweep_running_max(extra)

        if with_mask:
            @pl.when(j <= i)
            def _():
                add_ref[...] = mask_ref[pl.ds(k0, tk), :].astype(F32)

            @pl.when(j < i - 1)
            def _():
                both(lambda hh, rows: add_ref[rows, :])

            @pl.when(jnp.logical_and(j >= i - 1, j <= i))
            def _():
                near = i - j
                both(lambda hh, rows: add_ref[rows, :] + bias_ref[hh, near, rows, :].astype(F32))
        else:
            @pl.when(j < i)
            def _():
                both(lambda hh, rows: None)

            @pl.when(j == i)
            def _():
                both(lambda hh, rows: diag_ref[rows, :])
        return carry

    lax.fori_loop(0, NSUB_ATT, key_block, 0)

    @pl.when(jj == i // NSUB_ATT)
    def _():
        dv = VP - BF16_ROWS
        for hh in range(heads):
            inv = 1.0 / acc_ref[VP * hh + dv:VP * hh + dv + 1, :]
            s_ref[0, dv * hh:dv * (hh + 1), :] = acc_ref[VP * hh:VP * hh + dv, :] * inv
        o_ref[...] = s_ref[0, 0:dv * heads, :].T.astype(o_ref.dtype)


def _pair_tables(S):
    n = S // TQ_ATT
    ti = np.array([i for i in range(n) for _ in range(i // NSUB_ATT + 1)], np.int32)
    tj = np.array([j for i in range(n) for j in range(i // NSUB_ATT + 1)], np.int32)
    return jnp.asarray(ti), jnp.asarray(tj)


def _attend(qT, k, vT, safe, extra_inputs, extra_specs, extra_scratch, with_mask, name):
    B, S, _ = k.shape
    tq, tk = TQ_ATT, TK_ATT
    tg = tk * NSUB_ATT
    assert tq == tk and tk >= B_V * A_HEADS and S % tg == 0
    ti, tj = _pair_tables(S)
    heads = A_HEADS
    kern = functools.partial(_attend_kernel, heads=heads, with_mask=with_mask)
    grid_spec = pltpu.PrefetchScalarGridSpec(
        num_scalar_prefetch=3,
        grid=(B, ti.shape[0]),
        in_specs=[
            pl.BlockSpec((None, heads * HP, tq), lambda b, s, ti, tj, sf: (b, 0, ti[s])),
            pl.BlockSpec((None, tg, heads * HP), lambda b, s, ti, tj, sf: (b, tj[s], 0)),
            pl.BlockSpec((None, heads * VP, tg), lambda b, s, ti, tj, sf: (b, 0, tj[s])),
        ] + extra_specs,
        out_specs=pl.BlockSpec((None, tq, heads * B_V), lambda b, s, ti, tj, sf: (b, ti[s], 0)),
        scratch_shapes=[pltpu.VMEM((heads * VP, tq), F32),
                        pltpu.VMEM((2, tk, tq), F32),
                        pltpu.VMEM((2, tk, tq), BF16),
                        pltpu.VMEM((heads, tq), F32)] + extra_scratch,
    )
    return pl.pallas_call(
        kern, grid_spec=grid_spec,
        out_shape=jax.ShapeDtypeStruct((B, S, heads * B_V), BF16),
        compiler_params=_cparams(("parallel", "arbitrary")),
        name=name,
    )(ti, tj, safe, qT, k, vT, *extra_inputs)


def _attend_a(qaT, ka, vaT, mask, bias_tiles, safe):
    tq, tk = TQ_ATT, TK_ATT
    specs = [
        pl.BlockSpec((None, tk * NSUB_ATT, tq), lambda b, s, ti, tj, sf: (b, tj[s], ti[s])),
        pl.BlockSpec(bias_tiles.shape, lambda b, s, ti, tj, sf: (0, 0, 0, 0)),
    ]
    return _attend(qaT, ka, vaT, safe, (mask, bias_tiles), specs, [pltpu.VMEM((tk, tq), F32)],
                   True, "attend_a")


def _attend_b(qbT, kb, vbT, diag_tile, safe):
    tq, tk = TQ_ATT, TK_ATT
    specs = [pl.BlockSpec((tk, tq), lambda b, s, ti, tj, sf: (0, 0))]
    return _attend(qbT, kb, vbT, safe, (diag_tile,), specs, [], False, "attend_b")


def _merge_kernel(x_ref, ya_ref, yb_ref, g_ref, wga_ref, wgb_ref, bga_ref, bgb_ref,
                  wpa_ref, wpb_ref, wo_ref, o_ref):
    x = x_ref[...]
    h = (x * lax.rsqrt(jnp.mean(x * x, axis=-1, keepdims=True) + EPS) * g_ref[...]).astype(BF16)
    gate_a = jax.nn.sigmoid(_dot(h, wga_ref[...]) + bga_ref[...])
    gate_b = jax.nn.sigmoid(_dot(h, wgb_ref[...]) + bgb_ref[...])
    merged = gate_a * _dot(ya_ref[...], wpa_ref[...]) + gate_b * _dot(yb_ref[...], wpb_ref[...])
    o_ref[...] = x + _dot(merged.astype(BF16), wo_ref[...])


def _merge_out(x2, ya2, yb2, g, wga, wgb, bga, bgb, wpa, wpb, wo):
    R, D = x2.shape
    tm = TM_MERGE
    row = lambda w: pl.BlockSpec((tm, w), lambda i: (i, 0))
    full = lambda a: pl.BlockSpec(a.shape, lambda i: (0,) * a.ndim)
    consts = (g, wga, wgb, bga, bgb, wpa, wpb, wo)
    return pl.pallas_call(
        _merge_kernel,
        grid=(R // tm,),
        in_specs=[row(D), row(A_WIDTH), row(B_WIDTH)] + [full(a) for a in consts],
        out_specs=row(D),
        out_shape=jax.ShapeDtypeStruct((R, D), F32),
        compiler_params=_cparams(("parallel",)),
        name="merge_out",
    )(x2, ya2, yb2, *consts)


def _ffn_kernel(x_ref, xp_ref, g_ref, wuv_ref, wug_ref, cwv_ref, cwg_ref, cbv_ref, cbg_ref, wd_ref,
                o_ref, uv_ref, ug_ref, acc_ref, *, tiles_per_seq):
    i = pl.program_id(0)
    f = pl.program_id(1)
    tm = x_ref.shape[0]
    halo = SUBLANES
    g = g_ref[...]

    def normed(v):
        return (v * lax.rsqrt(jnp.mean(v * v, axis=-1, keepdims=True) + EPS) * g).astype(BF16)

    h = normed(x_ref[...])
    keep = jnp.where(i % tiles_per_seq == 0, 0.0, 1.0)
    hp = normed(xp_ref[...])

    def conv(u_ref, w_ref, cw_ref, cb_ref):
        u_ref[0:halo, :] = _dot(hp, w_ref[...]) * keep
        u_ref[halo:halo + tm, :] = _dot(h, w_ref[...])
        out = cb_ref[...]
        for t in range(CONV_W):
            lo = halo - (CONV_W - 1) + t
            out = out + cw_ref[t:t + 1, :] * u_ref[lo:lo + tm, :]
        return out

    val = conv(uv_ref, wuv_ref, cwv_ref, cbv_ref)
    gat = conv(ug_ref, wug_ref, cwg_ref, cbg_ref)
    act = (gat * jax.nn.sigmoid(gat) * val).astype(BF16)
    part = _dot(act, wd_ref[...])

    @pl.when(f == 0)
    def _():
        acc_ref[...] = part

    @pl.when(f > 0)
    def _():
        acc_ref[...] = acc_ref[...] + part

    @pl.when(f == pl.num_programs(1) - 1)
    def _():
        o_ref[...] = x_ref[...] + acc_ref[...]


def _conv_ffn(x2, S, g, wu, cw, cb, wd):
    R, D = x2.shape
    tm, tf = TM_FFN, TF_FFN
    nf = D_FF // tf
    halo_blocks = tm // SUBLANES
    kern = functools.partial(_ffn_kernel, tiles_per_seq=S // tm)
    return pl.pallas_call(
        kern,
        grid=(R // tm, nf),
        in_specs=[
            pl.BlockSpec((tm, D), lambda i, f: (i, 0)),
            pl.BlockSpec((SUBLANES, D), lambda i, f: (jnp.maximum(i * halo_blocks - 1, 0), 0)),
            pl.BlockSpec((1, D), lambda i, f: (0, 0)),
            pl.BlockSpec((D, tf), lambda i, f: (0, f)),
            pl.BlockSpec((D, tf), lambda i, f: (0, nf + f)),
            pl.BlockSpec((CONV_W, tf), lambda i, f: (0, f)),
            pl.BlockSpec((CONV_W, tf), lambda i, f: (0, nf + f)),
            pl.BlockSpec((1, tf), lambda i, f: (0, f)),
            pl.BlockSpec((1, tf), lambda i, f: (0, nf + f)),
            pl.BlockSpec((tf, D), lambda i, f: (f, 0)),
        ],
        out_specs=pl.BlockSpec((tm, D), lambda i, f: (i, 0)),
        out_shape=jax.ShapeDtypeStruct((R, D), F32),
        scratch_shapes=[pltpu.VMEM((tm + SUBLANES, tf), F32), pltpu.VMEM((tm + SUBLANES, tf), F32),
                        pltpu.VMEM((tm, D), F32)],
        compiler_params=_cparams(("parallel", "arbitrary")),
        name="conv_ffn",
    )(x2, x2, g, wu, wu, cw, cw, cb, cb, wd)


def _t5_bucket(rel):
    nb = REL_BUCKETS // 2
    max_exact = nb // 2
    side = jnp.where(rel > 0, nb, 0)
    n = jnp.abs(rel)
    nf = jnp.maximum(n, 1).astype(F32)
    large = max_exact + (jnp.log(nf / max_exact) / math.log(REL_MAX_DIST / max_exact)
                         * (nb - max_exact)).astype(I32)
    large = jnp.minimum(large, nb - 1)
    return side + jnp.where(n < max_exact, n, large)


def _bias_tiles(rel_bias):
    tq, tk = TQ_ATT, TK_ATT
    assert tq == tk and tk >= REL_MAX_DIST
    kk = jnp.arange(tk, dtype=I32)[:, None]
    qq = jnp.arange(tq, dtype=I32)[None, :]
    bucket = _t5_bucket(jnp.stack([kk - qq, kk - qq - tk])).reshape(1, 2 * tk * tq)
    onehot = (jnp.arange(REL_BUCKETS, dtype=I32)[:, None] == bucket).astype(F32)
    far = rel_bias[_t5_bucket(jnp.asarray(-REL_MAX_DIST, I32))]
    table = ((rel_bias - far[None, :]) * LOG2E).T
    tiles = jnp.dot(table, onehot, precision=lax.Precision.HIGHEST)
    return tiles.reshape(A_HEADS, 2, tk, tq).astype(BF16), jnp.max(jnp.abs(table), axis=1)


def _diag_tile():
    kk = np.arange(TK_ATT)[:, None] // CHUNK
    qq = np.arange(TQ_ATT)[None, :] // CHUNK
    return jnp.asarray(np.where(kk <= qq, 0.0, NEG).astype(np.float32))


def _rope_tables(S):
    half = B_ROPE // 2
    inv = ROPE_BASE ** (-jnp.arange(half, dtype=F32) / half)
    ang = jnp.arange(S, dtype=I32).astype(F32)[:, None] * inv[None, :]
    cos, sin = jnp.cos(ang), jnp.sin(ang)
    ones = jnp.ones((S, B_NOPE), F32)
    zeros_n = jnp.zeros((S, B_NOPE), F32)
    zeros_p = jnp.zeros((S, HP - B_QK), F32)
    cos_t = jnp.concatenate([ones, cos, cos, zeros_p], axis=1)
    sin_t = jnp.concatenate([zeros_n, -sin, sin, zeros_p], axis=1)
    return cos_t, sin_t


def _pad_cols(w, width):
    return jnp.pad(w, ((0, 0), (0, width - w.shape[1])))


def _pad_heads(w, heads, width):
    rows = w.shape[0]
    w = w.reshape(rows, heads, -1)
    return jnp.pad(w, ((0, 0), (0, 0), (0, width - w.shape[2]))).reshape(rows, heads * width)


def _swap_halves(w):
    half = w.shape[-1] // 2
    return jnp.concatenate([w[..., half:], w[..., :half]], axis=-1)


def _layer_weights(l, w_in, b_w_uq, b_w_ukv, b_q_norm, b_k_norm, a_q_norm, a_k_norm, bias_max):
    w = w_in[l]
    o = np.cumsum([0, A_WIDTH, A_WIDTH, A_WIDTH, IDX_HEADS * IDX_DIM, IDX_DIM, IDX_HEADS,
                   B_Q_LORA, B_KV_LORA, B_ROPE, D_MODEL, D_MODEL])
    seg = [w[:, o[t]:o[t + 1]] for t in range(11)]
    zn = jnp.zeros((D_MODEL, B_NOPE), F32)
    kr = seg[8]
    w1 = jnp.concatenate([
        _pad_heads(seg[0], A_HEADS, HP), _pad_heads(seg[1], A_HEADS, HP), seg[2],
        _pad_heads(seg[3], IDX_HEADS, HP),
        _pad_cols(seg[4], LANES), _pad_cols(seg[5], LANES), seg[6], seg[7],
        _pad_cols(jnp.concatenate([zn, kr], axis=1), HP),
        _pad_cols(jnp.concatenate([zn, _swap_halves(kr)], axis=1), HP),
    ], axis=1).astype(BF16)
    assert w1.shape[1] == _C_END

    uq = b_w_uq[l].reshape(B_Q_LORA, B_HEADS, B_QK)
    zq = jnp.zeros((B_Q_LORA, B_HEADS, B_NOPE), F32)
    pq = jnp.zeros((B_Q_LORA, B_HEADS, HP - B_QK), F32)
    wqm = jnp.concatenate([uq, pq], axis=-1).reshape(B_Q_LORA, B_HEADS * HP).astype(BF16)
    wqs = jnp.concatenate([zq, _swap_halves(uq[..., B_NOPE:]), pq], axis=-1)
    wqs = wqs.reshape(B_Q_LORA, B_HEADS * HP).astype(BF16)

    ukv = b_w_ukv[l].reshape(B_KV_LORA, B_HEADS, B_NOPE + B_V)
    pk = jnp.zeros((B_KV_LORA, B_HEADS, HP - B_NOPE), F32)
    wkm = jnp.concatenate([ukv[..., :B_NOPE], pk], axis=-1).reshape(B_KV_LORA, B_HEADS * HP).astype(BF16)
    wv = ukv[..., B_NOPE:].reshape(B_KV_LORA, B_WIDTH).astype(BF16)

    gqa = _pad_cols(a_q_norm[l][None, :], HP)
    gka = _pad_cols(a_k_norm[l][None, :], HP)
    gqb = _pad_cols(b_q_norm[l][None, :], HP)
    gkb = _pad_cols(b_k_norm[l][None, :], HP)

    ka_bound = math.sqrt(A_HEAD_DIM) * jnp.max(jnp.abs(a_k_norm[l]))
    kb_bound = math.sqrt(B_QK) * jnp.max(jnp.abs(b_k_norm[l]))
    bnd = jnp.zeros((SUBLANES, LANES), F32)
    bnd = bnd.at[0, :].set(ka_bound).at[1, :].set(kb_bound).at[2, :A_HEADS].set(bias_max)
    qa_bound = math.sqrt(A_HEAD_DIM) * jnp.max(jnp.abs(a_q_norm[l])) * (A_HEAD_DIM ** -0.5 * LOG2E)
    qb_bound = math.sqrt(B_QK) * jnp.max(jnp.abs(b_q_norm[l])) * (B_QK ** -0.5 * LOG2E)
    range_a = (qa_bound * ka_bound + jnp.max(bias_max)) * BOUND_SLACK + jnp.max(bias_max)
    range_b = qb_bound * kb_bound * BOUND_SLACK
    safe_a = (range_a <= MAX_LOG2_RANGE).astype(I32).reshape(1)
    safe_b = (range_b <= MAX_LOG2_RANGE).astype(I32).reshape(1)
    return (w1, wqm, wqs, wkm, wv, gqa, gka, gqb, gkb, bnd, safe_a, safe_b,
            seg[9].astype(BF16), seg[10].astype(BF16))


def kernel(x, rel_bias, norm_mix, w_in, a_q_norm, a_k_norm, b_cq_norm, b_ckv_norm, b_w_uq, b_w_ukv,
           b_q_norm, b_k_norm, w_proj_a, w_proj_b, b_gate, w_out, norm_ffn, w_up, conv_w, conv_b, w_down):
    B, S, D = x.shape
    assert D == D_MODEL and S % TQ_ATT == 0 and S % TM_FFN == 0
    topk = min(TOPK_MAX, S // 4)
    assert TQ_SEL >= topk
    depth = w_in.shape[0]

    cos_t, sin_t = _rope_tables(S)
    bias_tiles, bias_max = _bias_tiles(rel_bias)
    diag_tile = _diag_tile()

    for l in range(depth):
        (w1, wqm, wqs, wkm, wv, gqa, gka, gqb, gkb, bnd, safe_a, safe_b, wga, wgb) = _layer_weights(
            l, w_in, b_w_uq, b_w_ukv, b_q_norm, b_k_norm, a_q_norm, a_k_norm, bias_max)
        qaT, ka, vaT, qi, kiT, wi, qbT, kb, vbT = _token_prep(
            x, norm_mix[l][None, :], w1, wqm, wqs, wkm, wv, gqa, gka,
            b_cq_norm[l][None, :], b_ckv_norm[l][None, :], gqb, gkb, bnd,
            cos_t, sin_t)
        mask = _select_topk(qi, wi, kiT, topk)
        y_a = _attend_a(qaT, ka, vaT, mask, bias_tiles, safe_a)
        y_b = _attend_b(qbT, kb, vbT, diag_tile, safe_b)
        x2 = _merge_out(
            x.reshape(B * S, D), y_a.reshape(B * S, A_WIDTH), y_b.reshape(B * S, B_WIDTH),
            norm_mix[l][None, :], wga, wgb, b_gate[l][None, :D_MODEL], b_gate[l][None, D_MODEL:],
            w_proj_a[l].astype(BF16), w_proj_b[l].astype(BF16), w_out[l].astype(BF16))
        x2 = _conv_ffn(x2, S, norm_ffn[l][None, :], w_up[l].astype(BF16), conv_w[l], conv_b[l][None, :],
                       w_down[l].astype(BF16))
        x = x2.reshape(B, S, D)
    return x
```

```python
import functools
import math

import numpy as np
import jax
import jax.numpy as jnp
from jax import lax
from jax.experimental import pallas as pl
from jax.experimental.pallas import tpu as pltpu

F32 = jnp.float32
BF16 = jnp.bfloat16
I32 = jnp.int32
I16 = jnp.int16

D_MODEL = 1024
CHUNK = 64
CHUNK_SHIFT = 6
assert 1 << CHUNK_SHIFT == CHUNK
A_HEADS = 8
A_HEAD_DIM = 64
A_WIDTH = A_HEADS * A_HEAD_DIM
IDX_HEADS = 8
IDX_DIM = 64
TOPK_MAX = 256
B_HEADS = 8
B_Q_LORA = 384
B_KV_LORA = 256
B_NOPE = 64
B_ROPE = 32
B_QK = B_NOPE + B_ROPE
B_V = 64
B_WIDTH = B_HEADS * B_V
ROPE_BASE = 10000.0
REL_BUCKETS = 32
REL_MAX_DIST = 128
D_FF = 2816
CONV_W = 3
EPS = 1e-6

LANES = 128
SUBLANES = 8
BF16_ROWS = 16
VMEM_LIMIT = 56 * 1024 * 1024

TM_PREP = 512
TQ_SEL = 256
KU_SEL = 256
CH_SEL = 64
SEL_BITS = 32
GU_SEL = 16
TQ_ATT = 512
TK_ATT = 512
NSUB_ATT = 2
CH_ATT = 64
TM_MERGE = 512
TM_FFN = 512
TF_FFN = 1408

HP = 128
VP = B_V + BF16_ROWS
NEG = -1e30
SIGN_BIT = -2147483648
LOG2E = 1.4426950408889634
BOUND_SLACK = 1.01
MAX_LOG2_RANGE = 60.0


def _cparams(sem):
    return pltpu.CompilerParams(dimension_semantics=sem, vmem_limit_bytes=VMEM_LIMIT)


def _dot(a, b):
    return jnp.dot(a, b, preferred_element_type=F32)


_C_QA = 0
_C_KA = _C_QA + A_HEADS * HP
_C_VA = _C_KA + A_HEADS * HP
_C_QI = _C_VA + A_WIDTH
_C_KI = _C_QI + IDX_HEADS * HP
_C_WI = _C_KI + LANES
_C_CQ = _C_WI + LANES
_C_CKV = _C_CQ + B_Q_LORA
_C_KR = _C_CKV + B_KV_LORA
_C_KRS = _C_KR + HP
_C_END = _C_KRS + HP


def _token_prep_kernel(x_ref, g_ref, w1_ref, wqm_ref, wqs_ref, wkm_ref, wv_ref,
                       gqa_ref, gka_ref, gcq_ref, gckv_ref, gqb_ref, gkb_ref, bnd_ref,
                       cos_ref, sin_ref,
                       qaT_ref, ka_ref, vaT_ref, qi_ref, kiT_ref, wi_ref,
                       qbT_ref, kb_ref, vbT_ref):
    x = x_ref[...]
    tm = x.shape[0]
    h = (x * lax.rsqrt(jnp.mean(x * x, axis=-1, keepdims=True) + EPS) * g_ref[...]).astype(BF16)

    def proj(lo, hi):
        return _dot(h, w1_ref[:, lo:hi])

    def head_norm(zh, gain):
        ss = jnp.sum(zh * zh, axis=-1, keepdims=True) * (1.0 / A_HEAD_DIM)
        return zh * lax.rsqrt(ss + EPS) * gain

    ones_rows = jnp.where(lax.broadcasted_iota(I32, (BF16_ROWS, tm), 0) == 0, 1.0, 0.0).astype(BF16)

    def store_vT(ref, v):
        vT = v.T
        for hh in range(A_HEADS):
            ref[VP * hh:VP * hh + B_V, :] = vT[B_V * hh:B_V * (hh + 1), :].astype(BF16)
            ref[VP * hh + B_V:VP * (hh + 1), :] = ones_rows

    spare_a = jnp.where(lax.broadcasted_iota(I32, (1, HP), 1) == A_HEAD_DIM, 1.0, 0.0)
    qa = proj(_C_QA, _C_KA)
    ka = proj(_C_KA, _C_VA)
    gqa = gqa_ref[...] * (A_HEAD_DIM ** -0.5 * LOG2E)
    gka = gka_ref[...]
    for hh in range(A_HEADS):
        blk = slice(HP * hh, HP * (hh + 1))
        qh = head_norm(qa[:, blk], gqa)
        qnorm = jnp.sqrt(jnp.sum(qh * qh, axis=-1, keepdims=True))
        m_a = (qnorm * bnd_ref[0:1, :] + bnd_ref[2:3, hh:hh + 1]) * BOUND_SLACK
        qaT_ref[blk, :] = (qh - m_a * spare_a).T.astype(BF16)
        ka_ref[:, blk] = (head_norm(ka[:, blk], gka) + spare_a).astype(BF16)
    store_vT(vaT_ref, proj(_C_VA, _C_QI))

    qi_ref[...] = proj(_C_QI, _C_KI).astype(BF16)
    kiT = proj(_C_KI, _C_WI).T.astype(BF16)
    for c in range(tm // KU_SEL):
        kiT_ref[c] = kiT[:, KU_SEL * c:KU_SEL * (c + 1)]
    wi_ref[...] = proj(_C_WI, _C_CQ) * ((IDX_HEADS ** -0.5) * (IDX_DIM ** -0.5))

    cos = cos_ref[...]
    sin = sin_ref[...]
    spare = jnp.where(lax.broadcasted_iota(I32, (1, HP), 1) == B_QK, 1.0, 0.0)
    cq = proj(_C_CQ, _C_CKV)
    cqn = (cq * lax.rsqrt(jnp.mean(cq * cq, axis=-1, keepdims=True) + EPS) * gcq_ref[...]).astype(BF16)
    qm = _dot(cqn, wqm_ref[...])
    qs = _dot(cqn, wqs_ref[...])
    gqb = gqb_ref[...] * (B_QK ** -0.5 * LOG2E)
    kbound = bnd_ref[1:2, :]
    for hh in range(B_HEADS):
        blk = slice(HP * hh, HP * (hh + 1))
        qh = qm[:, blk] * cos + qs[:, blk] * sin
        ss = jnp.sum(qh * qh, axis=-1, keepdims=True) * (1.0 / B_QK)
        qh = qh * lax.rsqrt(ss + EPS) * gqb
        m_b = jnp.sqrt(jnp.sum(qh * qh, axis=-1, keepdims=True)) * kbound * BOUND_SLACK
        qbT_ref[blk, :] = (qh - m_b * spare).T.astype(BF16)

    ckv = proj(_C_CKV, _C_KR)
    ckvn = (ckv * lax.rsqrt(jnp.mean(ckv * ckv, axis=-1, keepdims=True) + EPS) * gckv_ref[...]).astype(BF16)
    km = _dot(ckvn, wkm_ref[...])
    store_vT(vbT_ref, _dot(ckvn, wv_ref[...]))
    krot = proj(_C_KR, _C_KRS) * cos + proj(_C_KRS, _C_END) * sin
    gkb = gkb_ref[...]
    for hh in range(B_HEADS):
        blk = slice(HP * hh, HP * (hh + 1))
        kh = km[:, blk] + krot
        ss = jnp.sum(kh * kh, axis=-1, keepdims=True) * (1.0 / B_QK)
        kb_ref[:, blk] = (kh * lax.rsqrt(ss + EPS) * gkb + spare).astype(BF16)


def _token_prep(x, g, w1, wqm, wqs, wkm, wv, gqa, gka, gcq, gckv, gqb, gkb, bnd,
                cos_t, sin_t):
    B, S, D = x.shape
    tm = TM_PREP
    nt = S // tm
    row3 = lambda w: pl.BlockSpec((None, tm, w), lambda b, i: (b, i, 0))
    colT = lambda r: pl.BlockSpec((None, r, tm), lambda b, i: (b, 0, i))
    full = lambda a: pl.BlockSpec(a.shape, lambda b, i: (0,) * a.ndim)
    tab = pl.BlockSpec((tm, LANES), lambda b, i: (i, 0))
    consts = (g, w1, wqm, wqs, wkm, wv, gqa, gka, gcq, gckv, gqb, gkb, bnd)
    out_shape = (
        jax.ShapeDtypeStruct((B, A_HEADS * HP, S), BF16),
        jax.ShapeDtypeStruct((B, S, A_HEADS * HP), BF16),
        jax.ShapeDtypeStruct((B, A_HEADS * VP, S), BF16),
        jax.ShapeDtypeStruct((B, S, IDX_HEADS * HP), BF16),
        jax.ShapeDtypeStruct((B, S // KU_SEL, LANES, KU_SEL), BF16),
        jax.ShapeDtypeStruct((B, S, LANES), F32),
        jax.ShapeDtypeStruct((B, B_HEADS * HP, S), BF16),
        jax.ShapeDtypeStruct((B, S, B_HEADS * HP), BF16),
        jax.ShapeDtypeStruct((B, B_HEADS * VP, S), BF16),
    )
    kiT_spec = pl.BlockSpec((None, tm // KU_SEL, LANES, KU_SEL), lambda b, i: (b, i, 0, 0))
    out_specs = (colT(A_HEADS * HP), row3(A_HEADS * HP), colT(A_HEADS * VP), row3(IDX_HEADS * HP),
                 kiT_spec, row3(LANES), colT(B_HEADS * HP), row3(B_HEADS * HP), colT(B_HEADS * VP))
    return pl.pallas_call(
        _token_prep_kernel,
        grid=(B, nt),
        in_specs=[row3(D)] + [full(a) for a in consts] + [tab, tab],
        out_specs=out_specs,
        out_shape=out_shape,
        compiler_params=_cparams(("parallel", "parallel")),
        name="token_prep",
    )(x, *consts, cos_t, sin_t)


def _tree_sum(parts):
    while len(parts) > 1:
        nxt = [parts[t] + parts[t + 1] for t in range(0, len(parts) - 1, 2)]
        if len(parts) % 2:
            nxt.append(parts[-1])
        parts = nxt
    return parts[0]


def _bit_transpose32(words):
    a = list(words)
    j, m = 16, 0x0000FFFF
    while j:
        k = 0
        while k < 32:
            t = (a[k] ^ lax.shift_right_logical(a[k + j], jnp.int32(j))) & m
            a[k] = a[k] ^ t
            a[k + j] = a[k + j] ^ (t << j)
            k = (k + j + 1) & ~j
        j >>= 1
        m = m ^ (m << j)
    return a


def _expand_bits(word):
    return jnp.concatenate(
        [lax.shift_right_logical(word, jnp.int32(SEL_BITS - 1 - j)) & 1 for j in range(SEL_BITS)], axis=0)


def _select_kernel(qi_ref, wi_ref, kiT_ref, sel_ref, planes_ref, alive_ref, gt_ref,
                   acca_ref, accb_ref, wb_ref, *, topk):
    tq = TQ_SEL
    i = pl.program_id(1)
    n_keys = (i + 1) * tq
    n_groups = (i + GU_SEL) // GU_SEL

    for hh in range(IDX_HEADS):
        wb_ref[hh] = jnp.broadcast_to(wi_ref[:, hh:hh + 1], (tq, LANES))

    zeros_group = jnp.zeros((GU_SEL, SUBLANES, tq), I32)
    for r in range(SEL_BITS):
        planes_ref[r, pl.ds((n_groups - 1) * GU_SEL, GU_SEL)] = zeros_group

    n_units = n_keys // KU_SEL
    accb_ref[...] = jnp.zeros(accb_ref.shape, F32)

    def heads(u, dst_ref):
        kT = kiT_ref[jnp.minimum(u, n_units - 1)]
        for hh in range(IDX_HEADS):
            d = _dot(qi_ref[:, HP * hh:HP * (hh + 1)], kT)
            for c in range(tq // CH_SEL):
                rows = slice(CH_SEL * c, CH_SEL * (c + 1))
                w = wb_ref[hh, rows, :]
                for lanes in (slice(0, LANES), slice(LANES, 2 * LANES)):
                    t = jnp.maximum(d[rows, lanes], 0.0) * w
                    if hh == 0:
                        dst_ref[rows, lanes] = t
                    else:
                        dst_ref[rows, lanes] += t

    def convert(src_ref, u):
        accT = src_ref[...].T
        words = []
        for j in range(SEL_BITS):
            acc = accT[SUBLANES * j:SUBLANES * (j + 1), :] + 0.0
            bits = pltpu.bitcast(acc, I32)
            words.append(bits ^ ((bits >> 31) | SIGN_BIT))
        unit = jnp.maximum(u, 0)
        for r, plane in enumerate(_bit_transpose32(words)):
            planes_ref[r, unit] = plane

    def unit_pair(t, carry):
        heads(2 * t, acca_ref)
        convert(accb_ref, 2 * t - 1)
        heads(2 * t + 1, accb_ref)
        convert(acca_ref, 2 * t)
        return carry

    lax.fori_loop(0, n_units // 2 + 1, unit_pair, 0)

    full_group = jnp.full((GU_SEL, SUBLANES, tq), -1, I32)

    def init(g, carry):
        unit = g * GU_SEL + lax.broadcasted_iota(I32, (GU_SEL, SUBLANES, tq), 0)
        alive_ref[pl.ds(g * GU_SEL, GU_SEL)] = jnp.where(unit < i, full_group, zeros_group)
        gt_ref[pl.ds(g * GU_SEL, GU_SEL)] = zeros_group
        return carry
    lax.fori_loop(0, n_groups, init, 0)

    q_chunk = (i * tq + lax.broadcasted_iota(I32, (SUBLANES, tq), 1)) >> CHUNK_SHIFT
    k_row = i * tq + lax.broadcasted_iota(I32, (SUBLANES, tq), 0)
    adm_bits = jnp.zeros((SUBLANES, tq), I32)
    for j in range(SEL_BITS):
        adm = ((k_row + SUBLANES * j) >> CHUNK_SHIFT) <= q_chunk
        adm_bits = adm_bits | jnp.where(adm, jnp.int32(1) << (SEL_BITS - 1 - j), 0)
    alive_ref[i] = adm_bits

    def popcount_rows(x):
        return jnp.sum(lax.population_count(x), axis=0)

    def count_alive(plane_of):
        def body(g, acc):
            return acc + popcount_rows(plane_of(g) & alive_ref[pl.ds(g * GU_SEL, GU_SEL)])
        acc = lax.fori_loop(0, n_groups, body, jnp.zeros((SUBLANES, tq), I32))
        return jnp.sum(acc, axis=0, keepdims=True)

    def bit_pass(r, need):
        ones_here = count_alive(lambda g: planes_ref[r, pl.ds(g * GU_SEL, GU_SEL)])
        take = ones_here >= need

        def update(g, carry):
            grp = pl.ds(g * GU_SEL, GU_SEL)
            alive = alive_ref[grp]
            with_bit = alive & planes_ref[r, grp]
            alive_ref[grp] = jnp.where(take, with_bit, alive ^ with_bit)
            gt_ref[grp] = jnp.where(take, gt_ref[grp], gt_ref[grp] | with_bit)
            return carry
        lax.fori_loop(0, n_groups, update, 0)
        return jnp.where(take, need, need - ones_here)

    need = lax.fori_loop(0, SEL_BITS, bit_pass, jnp.full((1, tq), topk, I32))
    n_equal = count_alive(lambda g: full_group)
    any_tie = jnp.max(jnp.where(n_equal > need, 1, 0)) > 0

    @pl.when(jnp.logical_not(any_tie))
    def _():
        def body(g, carry):
            grp = pl.ds(g * GU_SEL, GU_SEL)
            sel_ref[grp] = gt_ref[grp] | alive_ref[grp]
            return carry
        lax.fori_loop(0, n_groups, body, 0)

    @pl.when(any_tie)
    def _():
        n_take = need.astype(F32)
        ltri = (lax.broadcasted_iota(I32, (KU_SEL, KU_SEL), 0)
                >= lax.broadcasted_iota(I32, (KU_SEL, KU_SEL), 1)).astype(BF16)

        def body(u, seen):
            eqf = _expand_bits(alive_ref[u]).astype(F32)
            rank = _dot(ltri, eqf.astype(BF16)) + seen
            first = jnp.logical_and(eqf != 0.0, rank <= n_take)
            words = gt_ref[u]
            for j in range(SEL_BITS):
                words = words | jnp.where(first[SUBLANES * j:SUBLANES * (j + 1), :],
                                          jnp.int32(1) << (SEL_BITS - 1 - j), 0)
            sel_ref[u] = words
            return seen + jnp.sum(eqf, axis=0, keepdims=True)
        lax.fori_loop(0, n_units, body, jnp.zeros((1, tq), F32))

        def clear(u, carry):
            sel_ref[u] = jnp.zeros((SUBLANES, tq), I32)
            return carry
        lax.fori_loop(n_units, n_groups * GU_SEL, clear, 0)

    def fill(g, carry):
        sel_ref[pl.ds(g * GU_SEL, GU_SEL)] = zeros_group
        return carry
    lax.fori_loop(n_groups, sel_ref.shape[0] // GU_SEL, fill, 0)


def _select_topk(qi, wi, kiT, topk):
    B, S, _ = qi.shape
    tq = TQ_SEL
    n_units, dims, unit = kiT.shape[1:]
    assert unit == KU_SEL == tq == SEL_BITS * SUBLANES and n_units * unit == S
    n_alloc = -(-n_units // GU_SEL) * GU_SEL
    return pl.pallas_call(
        functools.partial(_select_kernel, topk=topk),
        grid=(B, S // tq),
        in_specs=[
            pl.BlockSpec((None, tq, IDX_HEADS * HP), lambda b, i: (b, i, 0)),
            pl.BlockSpec((None, tq, LANES), lambda b, i: (b, i, 0)),
            pl.BlockSpec((None, n_units, dims, unit), lambda b, i: (b, 0, 0, 0)),
        ],
        out_specs=pl.BlockSpec((None, n_alloc, SUBLANES, tq), lambda b, i: (b, 0, 0, i)),
        out_shape=jax.ShapeDtypeStruct((B, n_alloc, SUBLANES, S), I32),
        scratch_shapes=[pltpu.VMEM((SEL_BITS, n_alloc + 1, SUBLANES, tq), I32),
                        pltpu.VMEM((n_alloc, SUBLANES, tq), I32),
                        pltpu.VMEM((n_alloc, SUBLANES, tq), I32),
                        pltpu.VMEM((tq, KU_SEL), F32), pltpu.VMEM((tq, KU_SEL), F32),
                        pltpu.VMEM((IDX_HEADS, tq, LANES), F32)],
        compiler_params=_cparams(("parallel", "parallel")),
        name="select_topk",
    )(qi, wi, kiT)


def _attend_kernel(ti_ref, tj_ref, safe_ref, *refs, heads, with_mask):
    if with_mask:
        qT_ref, k_ref, vT_ref, mask_ref, bias_ref, o_ref, acc_ref, s_ref, p_ref, m_ref, add_ref = refs
    else:
        qT_ref, k_ref, vT_ref, diag_ref, o_ref, acc_ref, s_ref, p_ref, m_ref = refs
    s_idx = pl.program_id(1)
    i = ti_ref[s_idx]
    jj = tj_ref[s_idx]
    bounded = safe_ref[0] == 1
    tk = TK_ATT
    n_ch = tk // CH_ATT

    @pl.when(jj == 0)
    def _():
        acc_ref[...] = jnp.zeros(acc_ref.shape, F32)
        m_ref[...] = jnp.full(m_ref.shape, NEG, F32)

    def chunks():
        return [slice(CH_ATT * c, CH_ATT * (c + 1)) for c in range(n_ch)]

    def key_block(sb, carry):
        j = jj * NSUB_ATT + sb
        k0 = pl.multiple_of(sb * tk, tk)

        def logits(hh):
            s_ref[hh % 2] = _dot(k_ref[pl.ds(k0, tk), HP * hh:HP * (hh + 1)],
                                 qT_ref[HP * hh:HP * (hh + 1), :])

        def pv(hh, buf):
            return _dot(vT_ref[VP * hh:VP * (hh + 1), pl.ds(k0, tk)], p_ref[buf])

        def sweep_bounded(extra):
            logits(0)
            for hh in range(heads):
                buf = hh % 2
                if hh + 1 < heads:
                    logits(hh + 1)
                for rows in chunks():
                    t = s_ref[buf, rows, :]
                    e = extra(hh, rows)
                    if e is not None:
                        t = t + e
                    p_ref[buf, rows, :] = jnp.exp2(t).astype(BF16)
                acc_ref[VP * hh:VP * (hh + 1), :] += pv(hh, buf)

        def sweep_running_max(extra):
            for hh in range(heads):
                buf = hh % 2
                logits(hh)
                m_blk = jnp.full((SUBLANES, s_ref.shape[2]), NEG, F32)
                for rows in chunks():
                    t = s_ref[buf, rows, :]
                    e = extra(hh, rows)
                    if e is not None:
                        t = t + e
                        s_ref[buf, rows, :] = t
                    for r in range(CH_ATT // SUBLANES):
                        m_blk = jnp.maximum(m_blk, t[SUBLANES * r:SUBLANES * (r + 1), :])
                m_old = m_ref[hh:hh + 1, :]
                m_new = jnp.maximum(m_old, jnp.max(m_blk, axis=0, keepdims=True))
                m_ref[hh:hh + 1, :] = m_new
                for rows in chunks():
                    p_ref[buf, rows, :] = jnp.exp2(s_ref[buf, rows, :] - m_new).astype(BF16)
                acc_ref[VP * hh:VP * (hh + 1), :] = (
                    jnp.exp2(m_old - m_new) * acc_ref[VP * hh:VP * (hh + 1), :] + pv(hh, buf))

        def both(extra):
            @pl.when(bounded)
            def _():
                sweep_bounded(extra)

            @pl.when(jnp.logical_not(bounded))
            def _():
                sweep_running_max(extra)

        if with_mask:
            @pl.when(j <= i)
            def _():
                for c in range(tk // KU_SEL):
                    bits = _expand_bits(mask_ref[sb * (tk // KU_SEL) + c])
                    add_ref[KU_SEL * c:KU_SEL * (c + 1), :] = jnp.where(bits != 0, 0.0, NEG)

            @pl.when(j < i - 1)
            def _():
                both(lambda hh, rows: add_ref[rows, :])

            @pl.when(jnp.logical_and(j >= i - 1, j <= i))
            def _():
                near = i - j
                both(lambda hh, rows: add_ref[rows, :] + bias_ref[hh, near, rows, :].astype(F32))
        else:
            @pl.when(j < i)
            def _():
                both(lambda hh, rows: None)

            @pl.when(j == i)
            def _():
                both(lambda hh, rows: diag_ref[rows, :])
        return carry

    lax.fori_loop(0, NSUB_ATT, key_block, 0)

    @pl.when(jj == i // NSUB_ATT)
    def _():
        dv = VP - BF16_ROWS
        for hh in range(heads):
            inv = 1.0 / acc_ref[VP * hh + dv:VP * hh + dv + 1, :]
            s_ref[0, dv * hh:dv * (hh + 1), :] = acc_ref[VP * hh:VP * hh + dv, :] * inv
        o_ref[...] = s_ref[0, 0:dv * heads, :].T.astype(o_ref.dtype)


def _pair_tables(S):
    n = S // TQ_ATT
    ti = np.array([i for i in range(n) for _ in range(i // NSUB_ATT + 1)], np.int32)
    tj = np.array([j for i in range(n) for j in range(i // NSUB_ATT + 1)], np.int32)
    return jnp.asarray(ti), jnp.asarray(tj)


def _attend(qT, k, vT, safe, extra_inputs, extra_specs, extra_scratch, with_mask, name):
    B, S, _ = k.shape
    tq, tk = TQ_ATT, TK_ATT
    tg = tk * NSUB_ATT
    assert tq == tk and tk >= B_V * A_HEADS and S % tg == 0
    ti, tj = _pair_tables(S)
    heads = A_HEADS
    kern = functools.partial(_attend_kernel, heads=heads, with_mask=with_mask)
    grid_spec = pltpu.PrefetchScalarGridSpec(
        num_scalar_prefetch=3,
        grid=(B, ti.shape[0]),
        in_specs=[
            pl.BlockSpec((None, heads * HP, tq), lambda b, s, ti, tj, sf: (b, 0, ti[s])),
            pl.BlockSpec((None, tg, heads * HP), lambda b, s, ti, tj, sf: (b, tj[s], 0)),
            pl.BlockSpec((None, heads * VP, tg), lambda b, s, ti, tj, sf: (b, 0, tj[s])),
        ] + extra_specs,
        out_specs=pl.BlockSpec((None, tq, heads * B_V), lambda b, s, ti, tj, sf: (b, ti[s], 0)),
        scratch_shapes=[pltpu.VMEM((heads * VP, tq), F32),
                        pltpu.VMEM((2, tk, tq), F32),
                        pltpu.VMEM((2, tk, tq), BF16),
                        pltpu.VMEM((heads, tq), F32)] + extra_scratch,
    )
    return pl.pallas_call(
        kern, grid_spec=grid_spec,
        out_shape=jax.ShapeDtypeStruct((B, S, heads * B_V), BF16),
        compiler_params=_cparams(("parallel", "arbitrary")),
        name=name,
    )(ti, tj, safe, qT, k, vT, *extra_inputs)


def _attend_a(qaT, ka, vaT, mask, bias_tiles, safe):
    tq, tk = TQ_ATT, TK_ATT
    specs = [
        pl.BlockSpec((None, tk * NSUB_ATT // KU_SEL, SUBLANES, tq),
                     lambda b, s, ti, tj, sf: (b, tj[s], 0, ti[s])),
        pl.BlockSpec(bias_tiles.shape, lambda b, s, ti, tj, sf: (0, 0, 0, 0)),
    ]
    return _attend(qaT, ka, vaT, safe, (mask, bias_tiles), specs, [pltpu.VMEM((tk, tq), F32)],
                   True, "attend_a")


def _attend_b(qbT, kb, vbT, diag_tile, safe):
    tq, tk = TQ_ATT, TK_ATT
    specs = [pl.BlockSpec((tk, tq), lambda b, s, ti, tj, sf: (0, 0))]
    return _attend(qbT, kb, vbT, safe, (diag_tile,), specs, [], False, "attend_b")


def _merge_kernel(x_ref, ya_ref, yb_ref, g_ref, wga_ref, wgb_ref, bga_ref, bgb_ref,
                  wpa_ref, wpb_ref, wo_ref, o_ref):
    x = x_ref[...]
    h = (x * lax.rsqrt(jnp.mean(x * x, axis=-1, keepdims=True) + EPS) * g_ref[...]).astype(BF16)
    gate_a = jax.nn.sigmoid(_dot(h, wga_ref[...]) + bga_ref[...])
    gate_b = jax.nn.sigmoid(_dot(h, wgb_ref[...]) + bgb_ref[...])
    merged = gate_a * _dot(ya_ref[...], wpa_ref[...]) + gate_b * _dot(yb_ref[...], wpb_ref[...])
    o_ref[...] = x + _dot(merged.astype(BF16), wo_ref[...])


def _merge_out(x2, ya2, yb2, g, wga, wgb, bga, bgb, wpa, wpb, wo):
    R, D = x2.shape
    tm = TM_MERGE
    row = lambda w: pl.BlockSpec((tm, w), lambda i: (i, 0))
    full = lambda a: pl.BlockSpec(a.shape, lambda i: (0,) * a.ndim)
    consts = (g, wga, wgb, bga, bgb, wpa, wpb, wo)
    return pl.pallas_call(
        _merge_kernel,
        grid=(R // tm,),
        in_specs=[row(D), row(A_WIDTH), row(B_WIDTH)] + [full(a) for a in consts],
        out_specs=row(D),
        out_shape=jax.ShapeDtypeStruct((R, D), F32),
        compiler_params=_cparams(("parallel",)),
        name="merge_out",
    )(x2, ya2, yb2, *consts)


def _ffn_kernel(x_ref, xp_ref, g_ref, wuv_ref, wug_ref, cwv_ref, cwg_ref, cbv_ref, cbg_ref, wd_ref,
                o_ref, uv_ref, ug_ref, acc_ref, *, tiles_per_seq):
    i = pl.program_id(0)
    f = pl.program_id(1)
    tm = x_ref.shape[0]
    halo = SUBLANES
    g = g_ref[...]

    def normed(v):
        return (v * lax.rsqrt(jnp.mean(v * v, axis=-1, keepdims=True) + EPS) * g).astype(BF16)

    h = normed(x_ref[...])
    keep = jnp.where(i % tiles_per_seq == 0, 0.0, 1.0)
    hp = normed(xp_ref[...])

    def conv(u_ref, w_ref, cw_ref, cb_ref):
        u_ref[0:halo, :] = _dot(hp, w_ref[...]) * keep
        u_ref[halo:halo + tm, :] = _dot(h, w_ref[...])
        out = cb_ref[...]
        for t in range(CONV_W):
            lo = halo - (CONV_W - 1) + t
            out = out + cw_ref[t:t + 1, :] * u_ref[lo:lo + tm, :]
        return out

    val = conv(uv_ref, wuv_ref, cwv_ref, cbv_ref)
    gat = conv(ug_ref, wug_ref, cwg_ref, cbg_ref)
    act = (gat * jax.nn.sigmoid(gat) * val).astype(BF16)
    part = _dot(act, wd_ref[...])

    @pl.when(f == 0)
    def _():
        acc_ref[...] = part

    @pl.when(f > 0)
    def _():
        acc_ref[...] = acc_ref[...] + part

    @pl.when(f == pl.num_programs(1) - 1)
    def _():
        o_ref[...] = x_ref[...] + acc_ref[...]


def _conv_ffn(x2, S, g, wu, cw, cb, wd):
    R, D = x2.shape
    tm, tf = TM_FFN, TF_FFN
    nf = D_FF // tf
    halo_blocks = tm // SUBLANES
    kern = functools.partial(_ffn_kernel, tiles_per_seq=S // tm)
    return pl.pallas_call(
        kern,
        grid=(R // tm, nf),
        in_specs=[
            pl.BlockSpec((tm, D), lambda i, f: (i, 0)),
            pl.BlockSpec((SUBLANES, D), lambda i, f: (jnp.maximum(i * halo_blocks - 1, 0), 0)),
            pl.BlockSpec((1, D), lambda i, f: (0, 0)),
            pl.BlockSpec((D, tf), lambda i, f: (0, f)),
            pl.BlockSpec((D, tf), lambda i, f: (0, nf + f)),
            pl.BlockSpec((CONV_W, tf), lambda i, f: (0, f)),
            pl.BlockSpec((CONV_W, tf), lambda i, f: (0, nf + f)),
            pl.BlockSpec((1, tf), lambda i, f: (0, f)),
            pl.BlockSpec((1, tf), lambda i, f: (0, nf + f)),
            pl.BlockSpec((tf, D), lambda i, f: (f, 0)),
        ],
        out_specs=pl.BlockSpec((tm, D), lambda i, f: (i, 0)),
        out_shape=jax.ShapeDtypeStruct((R, D), F32),
        scratch_shapes=[pltpu.VMEM((tm + SUBLANES, tf), F32), pltpu.VMEM((tm + SUBLANES, tf), F32),
                        pltpu.VMEM((tm, D), F32)],
        compiler_params=_cparams(("parallel", "arbitrary")),
        name="conv_ffn",
    )(x2, x2, g, wu, wu, cw, cw, cb, cb, wd)


def _t5_bucket(rel):
    nb = REL_BUCKETS // 2
    max_exact = nb // 2
    side = jnp.where(rel > 0, nb, 0)
    n = jnp.abs(rel)
    nf = jnp.maximum(n, 1).astype(F32)
    large = max_exact + (jnp.log(nf / max_exact) / math.log(REL_MAX_DIST / max_exact)
                         * (nb - max_exact)).astype(I32)
    large = jnp.minimum(large, nb - 1)
    return side + jnp.where(n < max_exact, n, large)


def _bias_tiles(rel_bias):
    tq, tk = TQ_ATT, TK_ATT
    assert tq == tk and tk >= REL_MAX_DIST
    kk = jnp.arange(tk, dtype=I32)[:, None]
    qq = jnp.arange(tq, dtype=I32)[None, :]
    bucket = _t5_bucket(jnp.stack([kk - qq, kk - qq - tk])).reshape(1, 2 * tk * tq)
    onehot = (jnp.arange(REL_BUCKETS, dtype=I32)[:, None] == bucket).astype(F32)
    far = rel_bias[_t5_bucket(jnp.asarray(-REL_MAX_DIST, I32))]
    table = ((rel_bias - far[None, :]) * LOG2E).T
    tiles = jnp.dot(table, onehot, precision=lax.Precision.HIGHEST)
    return tiles.reshape(A_HEADS, 2, tk, tq).astype(BF16), jnp.max(jnp.abs(table), axis=1)


def _diag_tile():
    kk = np.arange(TK_ATT)[:, None] // CHUNK
    qq = np.arange(TQ_ATT)[None, :] // CHUNK
    return jnp.asarray(np.where(kk <= qq, 0.0, NEG).astype(np.float32))


def _rope_tables(S):
    half = B_ROPE // 2
    inv = ROPE_BASE ** (-jnp.arange(half, dtype=F32) / half)
    ang = jnp.arange(S, dtype=I32).astype(F32)[:, None] * inv[None, :]
    cos, sin = jnp.cos(ang), jnp.sin(ang)
    ones = jnp.ones((S, B_NOPE), F32)
    zeros_n = jnp.zeros((S, B_NOPE), F32)
    zeros_p = jnp.zeros((S, HP - B_QK), F32)
    cos_t = jnp.concatenate([ones, cos, cos, zeros_p], axis=1)
    sin_t = jnp.concatenate([zeros_n, -sin, sin, zeros_p], axis=1)
    return cos_t, sin_t


def _pad_cols(w, width):
    return jnp.pad(w, ((0, 0), (0, width - w.shape[1])))


def _pad_heads(w, heads, width):
    rows = w.shape[0]
    w = w.reshape(rows, heads, -1)
    return jnp.pad(w, ((0, 0), (0, 0), (0, width - w.shape[2]))).reshape(rows, heads * width)


def _swap_halves(w):
    half = w.shape[-1] // 2
    return jnp.concatenate([w[..., half:], w[..., :half]], axis=-1)


def _layer_weights(l, w_in, b_w_uq, b_w_ukv, b_q_norm, b_k_norm, a_q_norm, a_k_norm, bias_max):
    w = w_in[l]
    o = np.cumsum([0, A_WIDTH, A_WIDTH, A_WIDTH, IDX_HEADS * IDX_DIM, IDX_DIM, IDX_HEADS,
                   B_Q_LORA, B_KV_LORA, B_ROPE, D_MODEL, D_MODEL])
    seg = [w[:, o[t]:o[t + 1]] for t in range(11)]
    zn = jnp.zeros((D_MODEL, B_NOPE), F32)
    kr = seg[8]
    w1 = jnp.concatenate([
        _pad_heads(seg[0], A_HEADS, HP), _pad_heads(seg[1], A_HEADS, HP), seg[2],
        _pad_heads(seg[3], IDX_HEADS, HP),
        _pad_cols(seg[4], LANES), _pad_cols(seg[5], LANES), seg[6], seg[7],
        _pad_cols(jnp.concatenate([zn, kr], axis=1), HP),
        _pad_cols(jnp.concatenate([zn, _swap_halves(kr)], axis=1), HP),
    ], axis=1).astype(BF16)
    assert w1.shape[1] == _C_END

    uq = b_w_uq[l].reshape(B_Q_LORA, B_HEADS, B_QK)
    zq = jnp.zeros((B_Q_LORA, B_HEADS, B_NOPE), F32)
    pq = jnp.zeros((B_Q_LORA, B_HEADS, HP - B_QK), F32)
    wqm = jnp.concatenate([uq, pq], axis=-1).reshape(B_Q_LORA, B_HEADS * HP).astype(BF16)
    wqs = jnp.concatenate([zq, _swap_halves(uq[..., B_NOPE:]), pq], axis=-1)
    wqs = wqs.reshape(B_Q_LORA, B_HEADS * HP).astype(BF16)

    ukv = b_w_ukv[l].reshape(B_KV_LORA, B_HEADS, B_NOPE + B_V)
    pk = jnp.zeros((B_KV_LORA, B_HEADS, HP - B_NOPE), F32)
    wkm = jnp.concatenate([ukv[..., :B_NOPE], pk], axis=-1).reshape(B_KV_LORA, B_HEADS * HP).astype(BF16)
    wv = ukv[..., B_NOPE:].reshape(B_KV_LORA, B_WIDTH).astype(BF16)

    gqa = _pad_cols(a_q_norm[l][None, :], HP)
    gka = _pad_cols(a_k_norm[l][None, :], HP)
    gqb = _pad_cols(b_q_norm[l][None, :], HP)
    gkb = _pad_cols(b_k_norm[l][None, :], HP)

    ka_bound = math.sqrt(A_HEAD_DIM) * jnp.max(jnp.abs(a_k_norm[l]))
    kb_bound = math.sqrt(B_QK) * jnp.max(jnp.abs(b_k_norm[l]))
    bnd = jnp.zeros((SUBLANES, LANES), F32)
    bnd = bnd.at[0, :].set(ka_bound).at[1, :].set(kb_bound).at[2, :A_HEADS].set(bias_max)
    qa_bound = math.sqrt(A_HEAD_DIM) * jnp.max(jnp.abs(a_q_norm[l])) * (A_HEAD_DIM ** -0.5 * LOG2E)
    qb_bound = math.sqrt(B_QK) * jnp.max(jnp.abs(b_q_norm[l])) * (B_QK ** -0.5 * LOG2E)
    range_a = (qa_bound * ka_bound + jnp.max(bias_max)) * BOUND_SLACK + jnp.max(bias_max)
    range_b = qb_bound * kb_bound * BOUND_SLACK
    safe_a = (range_a <= MAX_LOG2_RANGE).astype(I32).reshape(1)
    safe_b = (range_b <= MAX_LOG2_RANGE).astype(I32).reshape(1)
    return (w1, wqm, wqs, wkm, wv, gqa, gka, gqb, gkb, bnd, safe_a, safe_b,
            seg[9].astype(BF16), seg[10].astype(BF16))


def kernel(x, rel_bias, norm_mix, w_in, a_q_norm, a_k_norm, b_cq_norm, b_ckv_norm, b_w_uq, b_w_ukv,
           b_q_norm, b_k_norm, w_proj_a, w_proj_b, b_gate, w_out, norm_ffn, w_up, conv_w, conv_b, w_down):
    B, S, D = x.shape
    assert D == D_MODEL and S % TQ_ATT == 0 and S % TM_FFN == 0
    topk = min(TOPK_MAX, S // 4)
    assert TQ_SEL >= topk
    depth = w_in.shape[0]

    cos_t, sin_t = _rope_tables(S)
    bias_tiles, bias_max = _bias_tiles(rel_bias)
    diag_tile = _diag_tile()

    for l in range(depth):
        (w1, wqm, wqs, wkm, wv, gqa, gka, gqb, gkb, bnd, safe_a, safe_b, wga, wgb) = _layer_weights(
            l, w_in, b_w_uq, b_w_ukv, b_q_norm, b_k_norm, a_q_norm, a_k_norm, bias_max)
        qaT, ka, vaT, qi, kiT, wi, qbT, kb, vbT = _token_prep(
            x, norm_mix[l][None, :], w1, wqm, wqs, wkm, wv, gqa, gka,
            b_cq_norm[l][None, :], b_ckv_norm[l][None, :], gqb, gkb, bnd,
            cos_t, sin_t)
        mask = _select_topk(qi, wi, kiT, topk)
        y_a = _attend_a(qaT, ka, vaT, mask, bias_tiles, safe_a)
        y_b = _attend_b(qbT, kb, vbT, diag_tile, safe_b)
        x2 = _merge_out(
            x.reshape(B * S, D), y_a.reshape(B * S, A_WIDTH), y_b.reshape(B * S, B_WIDTH),
            norm_mix[l][None, :], wga, wgb, b_gate[l][None, :D_MODEL], b_gate[l][None, D_MODEL:],
            w_proj_a[l].astype(BF16), w_proj_b[l].astype(BF16), w_out[l].astype(BF16))
        x2 = _conv_ffn(x2, S, norm_ffn[l][None, :], w_up[l].astype(BF16), conv_w[l], conv_b[l][None, :],
                       w_down[l].astype(BF16))
        x = x2.reshape(B, S, D)
    return x
```

```python
import functools
import math

import numpy as np
import jax
import jax.numpy as jnp
from jax import lax
from jax.experimental import pallas as pl
from jax.experimental.pallas import tpu as pltpu

F32 = jnp.float32
BF16 = jnp.bfloat16
I32 = jnp.int32

D_MODEL = 1024
CHUNK = 64
CHUNK_SHIFT = 6
assert 1 << CHUNK_SHIFT == CHUNK
A_HEADS = 8
A_HEAD_DIM = 64
A_WIDTH = A_HEADS * A_HEAD_DIM
IDX_HEADS = 8
IDX_DIM = 64
TOPK_MAX = 256
B_HEADS = 8
B_Q_LORA = 384
B_KV_LORA = 256
B_NOPE = 64
B_ROPE = 32
B_QK = B_NOPE + B_ROPE
B_V = 64
B_WIDTH = B_HEADS * B_V
ROPE_BASE = 10000.0
REL_BUCKETS = 32
REL_MAX_DIST = 128
D_FF = 2816
CONV_W = 3
EPS = 1e-6

LANES = 128
SUBLANES = 8
BF16_ROWS = 16
VMEM_LIMIT = 56 * 1024 * 1024

TM_PREP = 512
TQ_SEL = 256
KU_SEL = 256
CH_SEL = 64
SEL_BITS = 32
GU_SEL = 16
TQ_ATT = 512
TK_ATT = 512
NSUB_ATT = 2
CH_ATT = 64
TM_MERGE = 512
TM_FFN = 512
TF_FFN = 1408

HP = 128
VP = B_V + BF16_ROWS
NEG = -1e30
SIGN_BIT = -2147483648
LOG2E = 1.4426950408889634
BOUND_SLACK = 1.01
MAX_LOG2_RANGE = 60.0


def _cparams(sem):
    return pltpu.CompilerParams(dimension_semantics=sem, vmem_limit_bytes=VMEM_LIMIT)


def _dot(a, b):
    return jnp.dot(a, b, preferred_element_type=F32)


_C_QA = 0
_C_KA = _C_QA + A_HEADS * HP
_C_VA = _C_KA + A_HEADS * HP
_C_QI = _C_VA + A_WIDTH
_C_KI = _C_QI + IDX_HEADS * HP
_C_WI = _C_KI + LANES
_C_CQ = _C_WI + LANES
_C_CKV = _C_CQ + B_Q_LORA
_C_KR = _C_CKV + B_KV_LORA
_C_KRS = _C_KR + HP
_C_END = _C_KRS + HP


def _token_prep_kernel(x_ref, g_ref, w1_ref, wqm_ref, wqs_ref, wkm_ref, wv_ref,
                       gqa_ref, gka_ref, gcq_ref, gckv_ref, gqb_ref, gkb_ref, bnd_ref,
                       cos_ref, sin_ref,
                       qaT_ref, ka_ref, vaT_ref, qi_ref, kiT_ref, wi_ref,
                       qbT_ref, kb_ref, vbT_ref):
    x = x_ref[...]
    tm = x.shape[0]
    h = (x * lax.rsqrt(jnp.mean(x * x, axis=-1, keepdims=True) + EPS) * g_ref[...]).astype(BF16)

    def proj(lo, hi):
        return _dot(h, w1_ref[:, lo:hi])

    def head_norm(zh, gain):
        ss = jnp.sum(zh * zh, axis=-1, keepdims=True) * (1.0 / A_HEAD_DIM)
        return zh * lax.rsqrt(ss + EPS) * gain

    ones_rows = jnp.where(lax.broadcasted_iota(I32, (BF16_ROWS, tm), 0) == 0, 1.0, 0.0).astype(BF16)

    def store_vT(ref, v):
        vT = v.T
        for hh in range(A_HEADS):
            ref[VP * hh:VP * hh + B_V, :] = vT[B_V * hh:B_V * (hh + 1), :].astype(BF16)
            ref[VP * hh + B_V:VP * (hh + 1), :] = ones_rows

    spare_a = jnp.where(lax.broadcasted_iota(I32, (1, HP), 1) == A_HEAD_DIM, 1.0, 0.0)
    qa = proj(_C_QA, _C_KA)
    ka = proj(_C_KA, _C_VA)
    gqa = gqa_ref[...] * (A_HEAD_DIM ** -0.5 * LOG2E)
    gka = gka_ref[...]
    for hh in range(A_HEADS):
        blk = slice(HP * hh, HP * (hh + 1))
        qh = head_norm(qa[:, blk], gqa)
        qnorm = jnp.sqrt(jnp.sum(qh * qh, axis=-1, keepdims=True))
        m_a = (qnorm * bnd_ref[0:1, :] + bnd_ref[2:3, hh:hh + 1]) * BOUND_SLACK
        qaT_ref[blk, :] = (qh - m_a * spare_a).T.astype(BF16)
        ka_ref[:, blk] = (head_norm(ka[:, blk], gka) + spare_a).astype(BF16)
    store_vT(vaT_ref, proj(_C_VA, _C_QI))

    qi_ref[...] = proj(_C_QI, _C_KI).astype(BF16)
    kiT = proj(_C_KI, _C_WI).T.astype(BF16)
    for c in range(tm // KU_SEL):
        kiT_ref[c] = kiT[:, KU_SEL * c:KU_SEL * (c + 1)]
    wi_ref[...] = proj(_C_WI, _C_CQ) * ((IDX_HEADS ** -0.5) * (IDX_DIM ** -0.5))

    cos = cos_ref[...]
    sin = sin_ref[...]
    spare = jnp.where(lax.broadcasted_iota(I32, (1, HP), 1) == B_QK, 1.0, 0.0)
    cq = proj(_C_CQ, _C_CKV)
    cqn = (cq * lax.rsqrt(jnp.mean(cq * cq, axis=-1, keepdims=True) + EPS) * gcq_ref[...]).astype(BF16)
    qm = _dot(cqn, wqm_ref[...])
    qs = _dot(cqn, wqs_ref[...])
    gqb = gqb_ref[...] * (B_QK ** -0.5 * LOG2E)
    kbound = bnd_ref[1:2, :]
    for hh in range(B_HEADS):
        blk = slice(HP * hh, HP * (hh + 1))
        qh = qm[:, blk] * cos + qs[:, blk] * sin
        ss = jnp.sum(qh * qh, axis=-1, keepdims=True) * (1.0 / B_QK)
        qh = qh * lax.rsqrt(ss + EPS) * gqb
        m_b = jnp.sqrt(jnp.sum(qh * qh, axis=-1, keepdims=True)) * kbound * BOUND_SLACK
        qbT_ref[blk, :] = (qh - m_b * spare).T.astype(BF16)

    ckv = proj(_C_CKV, _C_KR)
    ckvn = (ckv * lax.rsqrt(jnp.mean(ckv * ckv, axis=-1, keepdims=True) + EPS) * gckv_ref[...]).astype(BF16)
    km = _dot(ckvn, wkm_ref[...])
    store_vT(vbT_ref, _dot(ckvn, wv_ref[...]))
    krot = proj(_C_KR, _C_KRS) * cos + proj(_C_KRS, _C_END) * sin
    gkb = gkb_ref[...]
    for hh in range(B_HEADS):
        blk = slice(HP * hh, HP * (hh + 1))
        kh = km[:, blk] + krot
        ss = jnp.sum(kh * kh, axis=-1, keepdims=True) * (1.0 / B_QK)
        kb_ref[:, blk] = (kh * lax.rsqrt(ss + EPS) * gkb + spare).astype(BF16)


def _token_prep(x, g, w1, wqm, wqs, wkm, wv, gqa, gka, gcq, gckv, gqb, gkb, bnd,
                cos_t, sin_t):
    B, S, D = x.shape
    tm = TM_PREP
    nt = S // tm
    row3 = lambda w: pl.BlockSpec((None, tm, w), lambda b, i: (b, i, 0))
    colT = lambda r: pl.BlockSpec((None, r, tm), lambda b, i: (b, 0, i))
    full = lambda a: pl.BlockSpec(a.shape, lambda b, i: (0,) * a.ndim)
    tab = pl.BlockSpec((tm, LANES), lambda b, i: (i, 0))
    consts = (g, w1, wqm, wqs, wkm, wv, gqa, gka, gcq, gckv, gqb, gkb, bnd)
    out_shape = (
        jax.ShapeDtypeStruct((B, A_HEADS * HP, S), BF16),
        jax.ShapeDtypeStruct((B, S, A_HEADS * HP), BF16),
        jax.ShapeDtypeStruct((B, A_HEADS * VP, S), BF16),
        jax.ShapeDtypeStruct((B, S, IDX_HEADS * HP), BF16),
        jax.ShapeDtypeStruct((B, S // KU_SEL, LANES, KU_SEL), BF16),
        jax.ShapeDtypeStruct((B, S, LANES), F32),
        jax.ShapeDtypeStruct((B, B_HEADS * HP, S), BF16),
        jax.ShapeDtypeStruct((B, S, B_HEADS * HP), BF16),
        jax.ShapeDtypeStruct((B, B_HEADS * VP, S), BF16),
    )
    kiT_spec = pl.BlockSpec((None, tm // KU_SEL, LANES, KU_SEL), lambda b, i: (b, i, 0, 0))
    out_specs = (colT(A_HEADS * HP), row3(A_HEADS * HP), colT(A_HEADS * VP), row3(IDX_HEADS * HP),
                 kiT_spec, row3(LANES), colT(B_HEADS * HP), row3(B_HEADS * HP), colT(B_HEADS * VP))
    return pl.pallas_call(
        _token_prep_kernel,
        grid=(B, nt),
        in_specs=[row3(D)] + [full(a) for a in consts] + [tab, tab],
        out_specs=out_specs,
        out_shape=out_shape,
        compiler_params=_cparams(("parallel", "parallel")),
        name="token_prep",
    )(x, *consts, cos_t, sin_t)


def _bit_transpose32(words):
    a = list(words)
    j, m = 16, 0x0000FFFF
    while j:
        k = 0
        while k < 32:
            t = (a[k] ^ lax.shift_right_logical(a[k + j], jnp.int32(j))) & m
            a[k] = a[k] ^ t
            a[k + j] = a[k + j] ^ (t << j)
            k = (k + j + 1) & ~j
        j >>= 1
        m = m ^ (m << j)
    return a


def _expand_bits(word):
    return jnp.concatenate(
        [lax.shift_right_logical(word, jnp.int32(SEL_BITS - 1 - j)) & 1 for j in range(SEL_BITS)], axis=0)


def _select_kernel(qi_ref, wi_ref, kiT_ref, sel_ref, planes_ref, alive_ref, gt_ref,
                   acca_ref, accb_ref, wb_ref, *, topk):
    tq = TQ_SEL
    i = pl.program_id(1)
    n_keys = (i + 1) * tq
    n_groups = (i + GU_SEL) // GU_SEL

    for hh in range(IDX_HEADS):
        wb_ref[hh] = jnp.broadcast_to(wi_ref[:, hh:hh + 1], (tq, LANES))

    zeros_group = jnp.zeros((GU_SEL, SUBLANES, tq), I32)
    for r in range(SEL_BITS):
        planes_ref[r, pl.ds((n_groups - 1) * GU_SEL, GU_SEL)] = zeros_group

    n_units = n_keys // KU_SEL
    accb_ref[...] = jnp.zeros(accb_ref.shape, F32)

    def heads(u, dst_ref):
        kT = kiT_ref[jnp.minimum(u, n_units - 1)]
        for hh in range(IDX_HEADS):
            d = _dot(qi_ref[:, HP * hh:HP * (hh + 1)], kT)
            for c in range(tq // CH_SEL):
                rows = slice(CH_SEL * c, CH_SEL * (c + 1))
                w = wb_ref[hh, rows, :]
                for lanes in (slice(0, LANES), slice(LANES, 2 * LANES)):
                    t = jnp.maximum(d[rows, lanes], 0.0) * w
                    if hh == 0:
                        dst_ref[rows, lanes] = t
                    else:
                        dst_ref[rows, lanes] += t

    def convert(src_ref, u):
        accT = src_ref[...].T
        words = []
        for j in range(SEL_BITS):
            acc = accT[SUBLANES * j:SUBLANES * (j + 1), :] + 0.0
            bits = pltpu.bitcast(acc, I32)
            words.append(bits ^ ((bits >> 31) | SIGN_BIT))
        unit = jnp.maximum(u, 0)
        for r, plane in enumerate(_bit_transpose32(words)):
            planes_ref[r, unit] = plane

    def unit_pair(t, carry):
        heads(2 * t, acca_ref)
        convert(accb_ref, 2 * t - 1)
        heads(2 * t + 1, accb_ref)
        convert(acca_ref, 2 * t)
        return carry

    lax.fori_loop(0, n_units // 2 + 1, unit_pair, 0)

    full_group = jnp.full((GU_SEL, SUBLANES, tq), -1, I32)

    def init(g, carry):
        unit = g * GU_SEL + lax.broadcasted_iota(I32, (GU_SEL, SUBLANES, tq), 0)
        alive_ref[pl.ds(g * GU_SEL, GU_SEL)] = jnp.where(unit < i, full_group, zeros_group)
        gt_ref[pl.ds(g * GU_SEL, GU_SEL)] = zeros_group
        return carry
    lax.fori_loop(0, n_groups, init, 0)

    q_chunk = (i * tq + lax.broadcasted_iota(I32, (SUBLANES, tq), 1)) >> CHUNK_SHIFT
    k_row = i * tq + lax.broadcasted_iota(I32, (SUBLANES, tq), 0)
    adm_bits = jnp.zeros((SUBLANES, tq), I32)
    for j in range(SEL_BITS):
        adm = ((k_row + SUBLANES * j) >> CHUNK_SHIFT) <= q_chunk
        adm_bits = adm_bits | jnp.where(adm, jnp.int32(1) << (SEL_BITS - 1 - j), 0)
    alive_ref[i] = adm_bits

    def popcount_rows(x):
        return jnp.sum(lax.population_count(x), axis=0)

    def count_alive(plane_of):
        def body(g, acc):
            return acc + popcount_rows(plane_of(g) & alive_ref[pl.ds(g * GU_SEL, GU_SEL)])
        acc = lax.fori_loop(0, n_groups, body, jnp.zeros((SUBLANES, tq), I32))
        return jnp.sum(acc, axis=0, keepdims=True)

    def bit_pass(r, need):
        ones_here = count_alive(lambda g: planes_ref[r, pl.ds(g * GU_SEL, GU_SEL)])
        take = ones_here >= need

        def update(g, carry):
            grp = pl.ds(g * GU_SEL, GU_SEL)
            alive = alive_ref[grp]
            with_bit = alive & planes_ref[r, grp]
            alive_ref[grp] = jnp.where(take, with_bit, alive ^ with_bit)
            gt_ref[grp] = jnp.where(take, gt_ref[grp], gt_ref[grp] | with_bit)
            return carry
        lax.fori_loop(0, n_groups, update, 0)
        return jnp.where(take, need, need - ones_here)

    need = lax.fori_loop(0, SEL_BITS, bit_pass, jnp.full((1, tq), topk, I32))
    n_equal = count_alive(lambda g: full_group)
    any_tie = jnp.max(jnp.where(n_equal > need, 1, 0)) > 0

    @pl.when(jnp.logical_not(any_tie))
    def _():
        def body(g, carry):
            grp = pl.ds(g * GU_SEL, GU_SEL)
            sel_ref[grp] = gt_ref[grp] | alive_ref[grp]
            return carry
        lax.fori_loop(0, n_groups, body, 0)

    @pl.when(any_tie)
    def _():
        n_take = need.astype(F32)
        ltri = (lax.broadcasted_iota(I32, (KU_SEL, KU_SEL), 0)
                >= lax.broadcasted_iota(I32, (KU_SEL, KU_SEL), 1)).astype(BF16)

        def body(u, seen):
            eqf = _expand_bits(alive_ref[u]).astype(F32)
            rank = _dot(ltri, eqf.astype(BF16)) + seen
            first = jnp.logical_and(eqf != 0.0, rank <= n_take)
            words = gt_ref[u]
            for j in range(SEL_BITS):
                words = words | jnp.where(first[SUBLANES * j:SUBLANES * (j + 1), :],
                                          jnp.int32(1) << (SEL_BITS - 1 - j), 0)
            sel_ref[u] = words
            return seen + jnp.sum(eqf, axis=0, keepdims=True)
        lax.fori_loop(0, n_units, body, jnp.zeros((1, tq), F32))

        def clear(u, carry):
            sel_ref[u] = jnp.zeros((SUBLANES, tq), I32)
            return carry
        lax.fori_loop(n_units, n_groups * GU_SEL, clear, 0)

    def fill(g, carry):
        sel_ref[pl.ds(g * GU_SEL, GU_SEL)] = zeros_group
        return carry
    lax.fori_loop(n_groups, sel_ref.shape[0] // GU_SEL, fill, 0)


def _select_topk(qi, wi, kiT, topk):
    B, S, _ = qi.shape
    tq = TQ_SEL
    n_units, dims, unit = kiT.shape[1:]
    assert unit == KU_SEL == tq == SEL_BITS * SUBLANES and n_units * unit == S
    n_alloc = -(-n_units // GU_SEL) * GU_SEL
    return pl.pallas_call(
        functools.partial(_select_kernel, topk=topk),
        grid=(B, S // tq),
        in_specs=[
            pl.BlockSpec((None, tq, IDX_HEADS * HP), lambda b, i: (b, i, 0)),
            pl.BlockSpec((None, tq, LANES), lambda b, i: (b, i, 0)),
            pl.BlockSpec((None, n_units, dims, unit), lambda b, i: (b, 0, 0, 0)),
        ],
        out_specs=pl.BlockSpec((None, n_alloc, SUBLANES, tq), lambda b, i: (b, 0, 0, i)),
        out_shape=jax.ShapeDtypeStruct((B, n_alloc, SUBLANES, S), I32),
        scratch_shapes=[pltpu.VMEM((SEL_BITS, n_alloc + 1, SUBLANES, tq), I32),
                        pltpu.VMEM((n_alloc, SUBLANES, tq), I32),
                        pltpu.VMEM((n_alloc, SUBLANES, tq), I32),
                        pltpu.VMEM((tq, KU_SEL), F32), pltpu.VMEM((tq, KU_SEL), F32),
                        pltpu.VMEM((IDX_HEADS, tq, LANES), F32)],
        compiler_params=_cparams(("parallel", "parallel")),
        name="select_topk",
    )(qi, wi, kiT)


def _attend_kernel(ti_ref, tj_ref, safe_ref, *refs, heads, with_mask):
    if with_mask:
        qT_ref, k_ref, vT_ref, selbits_ref, bias_ref, o_ref, acc_ref, s_ref, p_ref, m_ref, add_ref = refs
    else:
        qT_ref, k_ref, vT_ref, diag_ref, o_ref, acc_ref, s_ref, p_ref, m_ref = refs
    s_idx = pl.program_id(1)
    i = ti_ref[s_idx]
    jj = tj_ref[s_idx]
    bounded = safe_ref[0] == 1
    tk = TK_ATT
    n_ch = tk // CH_ATT

    @pl.when(jj == 0)
    def _():
        acc_ref[...] = jnp.zeros(acc_ref.shape, F32)
        m_ref[...] = jnp.full(m_ref.shape, NEG, F32)

    def chunks():
        return [slice(CH_ATT * c, CH_ATT * (c + 1)) for c in range(n_ch)]

    def key_block(sb, carry):
        j = jj * NSUB_ATT + sb
        k0 = pl.multiple_of(sb * tk, tk)

        def logits(hh):
            s_ref[hh % 2] = _dot(k_ref[pl.ds(k0, tk), HP * hh:HP * (hh + 1)],
                                 qT_ref[HP * hh:HP * (hh + 1), :])

        def pv(hh, buf):
            return _dot(vT_ref[VP * hh:VP * (hh + 1), pl.ds(k0, tk)], p_ref[buf])

        def sweep_bounded(extra):
            logits(0)
            for hh in range(heads):
                buf = hh % 2
                if hh + 1 < heads:
                    logits(hh + 1)
                for rows in chunks():
                    t = s_ref[buf, rows, :]
                    e = extra(hh, rows)
                    if e is not None:
                        t = t + e
                    p_ref[buf, rows, :] = jnp.exp2(t).astype(BF16)
                acc_ref[VP * hh:VP * (hh + 1), :] += pv(hh, buf)

        def sweep_running_max(extra):
            for hh in range(heads):
                buf = hh % 2
                logits(hh)
                m_blk = jnp.full((SUBLANES, s_ref.shape[2]), NEG, F32)
                for rows in chunks():
                    t = s_ref[buf, rows, :]
                    e = extra(hh, rows)
                    if e is not None:
                        t = t + e
                        s_ref[buf, rows, :] = t
                    for r in range(CH_ATT // SUBLANES):
                        m_blk = jnp.maximum(m_blk, t[SUBLANES * r:SUBLANES * (r + 1), :])
                m_old = m_ref[hh:hh + 1, :]
                m_new = jnp.maximum(m_old, jnp.max(m_blk, axis=0, keepdims=True))
                m_ref[hh:hh + 1, :] = m_new
                for rows in chunks():
                    p_ref[buf, rows, :] = jnp.exp2(s_ref[buf, rows, :] - m_new).astype(BF16)
                acc_ref[VP * hh:VP * (hh + 1), :] = (
                    jnp.exp2(m_old - m_new) * acc_ref[VP * hh:VP * (hh + 1), :] + pv(hh, buf))

        def both(extra):
            @pl.when(bounded)
            def _():
                sweep_bounded(extra)

            @pl.when(jnp.logical_not(bounded))
            def _():
                sweep_running_max(extra)

        if with_mask:
            @pl.when(j <= i)
            def _():
                for c in range(tk // KU_SEL):
                    bits = _expand_bits(selbits_ref[sb * (tk // KU_SEL) + c])
                    add_ref[KU_SEL * c:KU_SEL * (c + 1), :] = jnp.where(bits != 0, 0.0, NEG)

            @pl.when(j < i - 1)
            def _():
                both(lambda hh, rows: add_ref[rows, :])

            @pl.when(jnp.logical_and(j >= i - 1, j <= i))
            def _():
                near = i - j
                both(lambda hh, rows: add_ref[rows, :] + bias_ref[hh, near, rows, :].astype(F32))
        else:
            @pl.when(j < i)
            def _():
                both(lambda hh, rows: None)

            @pl.when(j == i)
            def _():
                both(lambda hh, rows: diag_ref[rows, :])
        return carry

    lax.fori_loop(0, NSUB_ATT, key_block, 0)

    @pl.when(jj == i // NSUB_ATT)
    def _():
        dv = VP - BF16_ROWS
        for hh in range(heads):
            inv = 1.0 / acc_ref[VP * hh + dv:VP * hh + dv + 1, :]
            s_ref[0, dv * hh:dv * (hh + 1), :] = acc_ref[VP * hh:VP * hh + dv, :] * inv
        o_ref[...] = s_ref[0, 0:dv * heads, :].T.astype(o_ref.dtype)


def _pair_tables(S):
    n = S // TQ_ATT
    ti = np.array([i for i in range(n) for _ in range(i // NSUB_ATT + 1)], np.int32)
    tj = np.array([j for i in range(n) for j in range(i // NSUB_ATT + 1)], np.int32)
    return jnp.asarray(ti), jnp.asarray(tj)


def _attend(qT, k, vT, safe, extra_inputs, extra_specs, extra_scratch, with_mask, name):
    B, S, _ = k.shape
    tq, tk = TQ_ATT, TK_ATT
    tg = tk * NSUB_ATT
    assert tq == tk and tk >= B_V * A_HEADS and S % tg == 0
    ti, tj = _pair_tables(S)
    heads = A_HEADS
    kern = functools.partial(_attend_kernel, heads=heads, with_mask=with_mask)
    grid_spec = pltpu.PrefetchScalarGridSpec(
        num_scalar_prefetch=3,
        grid=(B, ti.shape[0]),
        in_specs=[
            pl.BlockSpec((None, heads * HP, tq), lambda b, s, ti, tj, sf: (b, 0, ti[s])),
            pl.BlockSpec((None, tg, heads * HP), lambda b, s, ti, tj, sf: (b, tj[s], 0)),
            pl.BlockSpec((None, heads * VP, tg), lambda b, s, ti, tj, sf: (b, 0, tj[s])),
        ] + extra_specs,
        out_specs=pl.BlockSpec((None, tq, heads * B_V), lambda b, s, ti, tj, sf: (b, ti[s], 0)),
        scratch_shapes=[pltpu.VMEM((heads * VP, tq), F32),
                        pltpu.VMEM((2, tk, tq), F32),
                        pltpu.VMEM((2, tk, tq), BF16),
                        pltpu.VMEM((heads, tq), F32)] + extra_scratch,
    )
    return pl.pallas_call(
        kern, grid_spec=grid_spec,
        out_shape=jax.ShapeDtypeStruct((B, S, heads * B_V), BF16),
        compiler_params=_cparams(("parallel", "arbitrary")),
        name=name,
    )(ti, tj, safe, qT, k, vT, *extra_inputs)


def _attend_a(qaT, ka, vaT, selbits, bias_tiles, safe):
    tq, tk = TQ_ATT, TK_ATT
    specs = [
        pl.BlockSpec((None, tk * NSUB_ATT // KU_SEL, SUBLANES, tq),
                     lambda b, s, ti, tj, sf: (b, tj[s], 0, ti[s])),
        pl.BlockSpec(bias_tiles.shape, lambda b, s, ti, tj, sf: (0, 0, 0, 0)),
    ]
    return _attend(qaT, ka, vaT, safe, (selbits, bias_tiles), specs, [pltpu.VMEM((tk, tq), F32)],
                   True, "attend_a")


def _attend_b(qbT, kb, vbT, diag_tile, safe):
    tq, tk = TQ_ATT, TK_ATT
    specs = [pl.BlockSpec((tk, tq), lambda b, s, ti, tj, sf: (0, 0))]
    return _attend(qbT, kb, vbT, safe, (diag_tile,), specs, [], False, "attend_b")


def _merge_kernel(x_ref, ya_ref, yb_ref, g_ref, wga_ref, wgb_ref, bga_ref, bgb_ref,
                  wpa_ref, wpb_ref, wo_ref, o_ref):
    x = x_ref[...]
    h = (x * lax.rsqrt(jnp.mean(x * x, axis=-1, keepdims=True) + EPS) * g_ref[...]).astype(BF16)
    gate_a = jax.nn.sigmoid(_dot(h, wga_ref[...]) + bga_ref[...])
    gate_b = jax.nn.sigmoid(_dot(h, wgb_ref[...]) + bgb_ref[...])
    merged = gate_a * _dot(ya_ref[...], wpa_ref[...]) + gate_b * _dot(yb_ref[...], wpb_ref[...])
    o_ref[...] = x + _dot(merged.astype(BF16), wo_ref[...])


def _merge_out(x2, ya2, yb2, g, wga, wgb, bga, bgb, wpa, wpb, wo):
    R, D = x2.shape
    tm = TM_MERGE
    row = lambda w: pl.BlockSpec((tm, w), lambda i: (i, 0))
    full = lambda a: pl.BlockSpec(a.shape, lambda i: (0,) * a.ndim)
    consts = (g, wga, wgb, bga, bgb, wpa, wpb, wo)
    return pl.pallas_call(
        _merge_kernel,
        grid=(R // tm,),
        in_specs=[row(D), row(A_WIDTH), row(B_WIDTH)] + [full(a) for a in consts],
        out_specs=row(D),
        out_shape=jax.ShapeDtypeStruct((R, D), F32),
        compiler_params=_cparams(("parallel",)),
        name="merge_out",
    )(x2, ya2, yb2, *consts)


def _ffn_kernel(x_ref, xp_ref, g_ref, wuv_ref, wug_ref, cwv_ref, cwg_ref, cbv_ref, cbg_ref, wd_ref,
                o_ref, uv_ref, ug_ref, acc_ref, *, tiles_per_seq):
    i = pl.program_id(0)
    f = pl.program_id(1)
    tm = x_ref.shape[0]
    halo = SUBLANES
    g = g_ref[...]

    def normed(v):
        return (v * lax.rsqrt(jnp.mean(v * v, axis=-1, keepdims=True) + EPS) * g).astype(BF16)

    h = normed(x_ref[...])
    keep = jnp.where(i % tiles_per_seq == 0, 0.0, 1.0)
    hp = normed(xp_ref[...])

    def conv(u_ref, w_ref, cw_ref, cb_ref):
        u_ref[0:halo, :] = _dot(hp, w_ref[...]) * keep
        u_ref[halo:halo + tm, :] = _dot(h, w_ref[...])
        out = cb_ref[...]
        for t in range(CONV_W):
            lo = halo - (CONV_W - 1) + t
            out = out + cw_ref[t:t + 1, :] * u_ref[lo:lo + tm, :]
        return out

    val = conv(uv_ref, wuv_ref, cwv_ref, cbv_ref)
    gat = conv(ug_ref, wug_ref, cwg_ref, cbg_ref)
    act = (gat * jax.nn.sigmoid(gat) * val).astype(BF16)
    part = _dot(act, wd_ref[...])

    @pl.when(f == 0)
    def _():
        acc_ref[...] = part

    @pl.when(f > 0)
    def _():
        acc_ref[...] = acc_ref[...] + part

    @pl.when(f == pl.num_programs(1) - 1)
    def _():
        o_ref[...] = x_ref[...] + acc_ref[...]


def _conv_ffn(x2, S, g, wu, cw, cb, wd):
    R, D = x2.shape
    tm, tf = TM_FFN, TF_FFN
    nf = D_FF // tf
    halo_blocks = tm // SUBLANES
    kern = functools.partial(_ffn_kernel, tiles_per_seq=S // tm)
    return pl.pallas_call(
        kern,
        grid=(R // tm, nf),
        in_specs=[
            pl.BlockSpec((tm, D), lambda i, f: (i, 0)),
            pl.BlockSpec((SUBLANES, D), lambda i, f: (jnp.maximum(i * halo_blocks - 1, 0), 0)),
            pl.BlockSpec((1, D), lambda i, f: (0, 0)),
            pl.BlockSpec((D, tf), lambda i, f: (0, f)),
            pl.BlockSpec((D, tf), lambda i, f: (0, nf + f)),
            pl.BlockSpec((CONV_W, tf), lambda i, f: (0, f)),
            pl.BlockSpec((CONV_W, tf), lambda i, f: (0, nf + f)),
            pl.BlockSpec((1, tf), lambda i, f: (0, f)),
            pl.BlockSpec((1, tf), lambda i, f: (0, nf + f)),
            pl.BlockSpec((tf, D), lambda i, f: (f, 0)),
        ],
        out_specs=pl.BlockSpec((tm, D), lambda i, f: (i, 0)),
        out_shape=jax.ShapeDtypeStruct((R, D), F32),
        scratch_shapes=[pltpu.VMEM((tm + SUBLANES, tf), F32), pltpu.VMEM((tm + SUBLANES, tf), F32),
                        pltpu.VMEM((tm, D), F32)],
        compiler_params=_cparams(("parallel", "arbitrary")),
        name="conv_ffn",
    )(x2, x2, g, wu, wu, cw, cw, cb, cb, wd)


def _t5_bucket(rel):
    nb = REL_BUCKETS // 2
    max_exact = nb // 2
    side = jnp.where(rel > 0, nb, 0)
    n = jnp.abs(rel)
    nf = jnp.maximum(n, 1).astype(F32)
    large = max_exact + (jnp.log(nf / max_exact) / math.log(REL_MAX_DIST / max_exact)
                         * (nb - max_exact)).astype(I32)
    large = jnp.minimum(large, nb - 1)
    return side + jnp.where(n < max_exact, n, large)


def _bias_tiles(rel_bias):
    tq, tk = TQ_ATT, TK_ATT
    assert tq == tk and tk >= REL_MAX_DIST
    kk = jnp.arange(tk, dtype=I32)[:, None]
    qq = jnp.arange(tq, dtype=I32)[None, :]
    bucket = _t5_bucket(jnp.stack([kk - qq, kk - qq - tk])).reshape(1, 2 * tk * tq)
    onehot = (jnp.arange(REL_BUCKETS, dtype=I32)[:, None] == bucket).astype(F32)
    far = rel_bias[_t5_bucket(jnp.asarray(-REL_MAX_DIST, I32))]
    table = ((rel_bias - far[None, :]) * LOG2E).T
    tiles = jnp.dot(table, onehot, precision=lax.Precision.HIGHEST)
    return tiles.reshape(A_HEADS, 2, tk, tq).astype(BF16), jnp.max(jnp.abs(table), axis=1)


def _diag_tile():
    kk = np.arange(TK_ATT)[:, None] // CHUNK
    qq = np.arange(TQ_ATT)[None, :] // CHUNK
    return jnp.asarray(np.where(kk <= qq, 0.0, NEG).astype(np.float32))


def _rope_tables(S):
    half = B_ROPE // 2
    inv = ROPE_BASE ** (-jnp.arange(half, dtype=F32) / half)
    ang = jnp.arange(S, dtype=I32).astype(F32)[:, None] * inv[None, :]
    cos, sin = jnp.cos(ang), jnp.sin(ang)
    ones = jnp.ones((S, B_NOPE), F32)
    zeros_n = jnp.zeros((S, B_NOPE), F32)
    zeros_p = jnp.zeros((S, HP - B_QK), F32)
    cos_t = jnp.concatenate([ones, cos, cos, zeros_p], axis=1)
    sin_t = jnp.concatenate([zeros_n, -sin, sin, zeros_p], axis=1)
    return cos_t, sin_t


def _pad_cols(w, width):
    return jnp.pad(w, ((0, 0), (0, width - w.shape[1])))


def _pad_heads(w, heads, width):
    rows = w.shape[0]
    w = w.reshape(rows, heads, -1)
    return jnp.pad(w, ((0, 0), (0, 0), (0, width - w.shape[2]))).reshape(rows, heads * width)


def _swap_halves(w):
    half = w.shape[-1] // 2
    return jnp.concatenate([w[..., half:], w[..., :half]], axis=-1)


def _layer_weights(l, w_in, b_w_uq, b_w_ukv, b_q_norm, b_k_norm, a_q_norm, a_k_norm, bias_max):
    w = w_in[l]
    o = np.cumsum([0, A_WIDTH, A_WIDTH, A_WIDTH, IDX_HEADS * IDX_DIM, IDX_DIM, IDX_HEADS,
                   B_Q_LORA, B_KV_LORA, B_ROPE, D_MODEL, D_MODEL])
    seg = [w[:, o[t]:o[t + 1]] for t in range(11)]
    zn = jnp.zeros((D_MODEL, B_NOPE), F32)
    kr = seg[8]
    w1 = jnp.concatenate([
        _pad_heads(seg[0], A_HEADS, HP), _pad_heads(seg[1], A_HEADS, HP), seg[2],
        _pad_heads(seg[3], IDX_HEADS, HP),
        _pad_cols(seg[4], LANES), _pad_cols(seg[5], LANES), seg[6], seg[7],
        _pad_cols(jnp.concatenate([zn, kr], axis=1), HP),
        _pad_cols(jnp.concatenate([zn, _swap_halves(kr)], axis=1), HP),
    ], axis=1).astype(BF16)
    assert w1.shape[1] == _C_END

    uq = b_w_uq[l].reshape(B_Q_LORA, B_HEADS, B_QK)
    zq = jnp.zeros((B_Q_LORA, B_HEADS, B_NOPE), F32)
    pq = jnp.zeros((B_Q_LORA, B_HEADS, HP - B_QK), F32)
    wqm = jnp.concatenate([uq, pq], axis=-1).reshape(B_Q_LORA, B_HEADS * HP).astype(BF16)
    wqs = jnp.concatenate([zq, _swap_halves(uq[..., B_NOPE:]), pq], axis=-1)
    wqs = wqs.reshape(B_Q_LORA, B_HEADS * HP).astype(BF16)

    ukv = b_w_ukv[l].reshape(B_KV_LORA, B_HEADS, B_NOPE + B_V)
    pk = jnp.zeros((B_KV_LORA, B_HEADS, HP - B_NOPE), F32)
    wkm = jnp.concatenate([ukv[..., :B_NOPE], pk], axis=-1).reshape(B_KV_LORA, B_HEADS * HP).astype(BF16)
    wv = ukv[..., B_NOPE:].reshape(B_KV_LORA, B_WIDTH).astype(BF16)

    gqa = _pad_cols(a_q_norm[l][None, :], HP)
    gka = _pad_cols(a_k_norm[l][None, :], HP)
    gqb = _pad_cols(b_q_norm[l][None, :], HP)
    gkb = _pad_cols(b_k_norm[l][None, :], HP)

    ka_bound = math.sqrt(A_HEAD_DIM) * jnp.max(jnp.abs(a_k_norm[l]))
    kb_bound = math.sqrt(B_QK) * jnp.max(jnp.abs(b_k_norm[l]))
    bnd = jnp.zeros((SUBLANES, LANES), F32)
    bnd = bnd.at[0, :].set(ka_bound).at[1, :].set(kb_bound).at[2, :A_HEADS].set(bias_max)
    qa_bound = math.sqrt(A_HEAD_DIM) * jnp.max(jnp.abs(a_q_norm[l])) * (A_HEAD_DIM ** -0.5 * LOG2E)
    qb_bound = math.sqrt(B_QK) * jnp.max(jnp.abs(b_q_norm[l])) * (B_QK ** -0.5 * LOG2E)
    range_a = (qa_bound * ka_bound + jnp.max(bias_max)) * BOUND_SLACK + jnp.max(bias_max)
    range_b = qb_bound * kb_bound * BOUND_SLACK
    safe_a = (range_a <= MAX_LOG2_RANGE).astype(I32).reshape(1)
    safe_b = (range_b <= MAX_LOG2_RANGE).astype(I32).reshape(1)
    return (w1, wqm, wqs, wkm, wv, gqa, gka, gqb, gkb, bnd, safe_a, safe_b,
            seg[9].astype(BF16), seg[10].astype(BF16))


def kernel(x, rel_bias, norm_mix, w_in, a_q_norm, a_k_norm, b_cq_norm, b_ckv_norm, b_w_uq, b_w_ukv,
           b_q_norm, b_k_norm, w_proj_a, w_proj_b, b_gate, w_out, norm_ffn, w_up, conv_w, conv_b, w_down):
    B, S, D = x.shape
    assert D == D_MODEL and S % TQ_ATT == 0 and S % TM_FFN == 0
    topk = min(TOPK_MAX, S // 4)
    assert TQ_SEL >= topk
    depth = w_in.shape[0]

    cos_t, sin_t = _rope_tables(S)
    bias_tiles, bias_max = _bias_tiles(rel_bias)
    diag_tile = _diag_tile()

    for l in range(depth):
        (w1, wqm, wqs, wkm, wv, gqa, gka, gqb, gkb, bnd, safe_a, safe_b, wga, wgb) = _layer_weights(
            l, w_in, b_w_uq, b_w_ukv, b_q_norm, b_k_norm, a_q_norm, a_k_norm, bias_max)
        qaT, ka, vaT, qi, kiT, wi, qbT, kb, vbT = _token_prep(
            x, norm_mix[l][None, :], w1, wqm, wqs, wkm, wv, gqa, gka,
            b_cq_norm[l][None, :], b_ckv_norm[l][None, :], gqb, gkb, bnd,
            cos_t, sin_t)
        selbits = _select_topk(qi, wi, kiT, topk)
        y_a = _attend_a(qaT, ka, vaT, selbits, bias_tiles, safe_a)
        y_b = _attend_b(qbT, kb, vbT, diag_tile, safe_b)
        x2 = _merge_out(
            x.reshape(B * S, D), y_a.reshape(B * S, A_WIDTH), y_b.reshape(B * S, B_WIDTH),
            norm_mix[l][None, :], wga, wgb, b_gate[l][None, :D_MODEL], b_gate[l][None, D_MODEL:],
            w_proj_a[l].astype(BF16), w_proj_b[l].astype(BF16), w_out[l].astype(BF16))
        x2 = _conv_ffn(x2, S, norm_ffn[l][None, :], w_up[l].astype(BF16), conv_w[l], conv_b[l][None, :],
                       w_down[l].astype(BF16))
        x = x2.reshape(B, S, D)
    return x
```

```python
import functools
import math

import numpy as np
import jax
import jax.numpy as jnp
from jax import lax
from jax.experimental import pallas as pl
from jax.experimental.pallas import tpu as pltpu

F32 = jnp.float32
BF16 = jnp.bfloat16
I32 = jnp.int32

D_MODEL = 1024
CHUNK = 64
CHUNK_SHIFT = 6
assert 1 << CHUNK_SHIFT == CHUNK
A_HEADS = 8
A_HEAD_DIM = 64
A_WIDTH = A_HEADS * A_HEAD_DIM
IDX_HEADS = 8
IDX_DIM = 64
TOPK_MAX = 256
B_HEADS = 8
B_Q_LORA = 384
B_KV_LORA = 256
B_NOPE = 64
B_ROPE = 32
B_QK = B_NOPE + B_ROPE
B_V = 64
B_WIDTH = B_HEADS * B_V
ROPE_BASE = 10000.0
REL_BUCKETS = 32
REL_MAX_DIST = 128
D_FF = 2816
CONV_W = 3
EPS = 1e-6

LANES = 128
SUBLANES = 8
BF16_ROWS = 16
VMEM_LIMIT = 56 * 1024 * 1024

TM_PREP = 512
TQ_SEL = 256
KU_SEL = 256
CH_SEL = 64
SEL_BITS = 32
GU_SEL = 16
TQ_ATT = 512
TK_ATT = 512
NSUB_ATT = 4
CH_ATT = 64
TM_MERGE = 512
TM_FFN = 512
TF_FFN = 1408

HP = 128
VP = B_V + BF16_ROWS
NEG = -1e30
SIGN_BIT = -2147483648
LOG2E = 1.4426950408889634
BOUND_SLACK = 1.01
MAX_LOG2_RANGE = 60.0


def _cparams(sem):
    return pltpu.CompilerParams(dimension_semantics=sem, vmem_limit_bytes=VMEM_LIMIT)


def _dot(a, b):
    return jnp.dot(a, b, preferred_element_type=F32)


_C_QA = 0
_C_KA = _C_QA + A_HEADS * HP
_C_VA = _C_KA + A_HEADS * HP
_C_QI = _C_VA + A_WIDTH
_C_KI = _C_QI + IDX_HEADS * HP
_C_WI = _C_KI + LANES
_C_CQ = _C_WI + LANES
_C_CKV = _C_CQ + B_Q_LORA
_C_KR = _C_CKV + B_KV_LORA
_C_KRS = _C_KR + HP
_C_END = _C_KRS + HP


def _token_prep_kernel(x_ref, g_ref, w1_ref, wqm_ref, wqs_ref, wkm_ref, wv_ref,
                       gqa_ref, gka_ref, gcq_ref, gckv_ref, gqb_ref, gkb_ref, bnd_ref,
                       cos_ref, sin_ref,
                       qaT_ref, ka_ref, vaT_ref, qi_ref, kiT_ref, wi_ref,
                       qbT_ref, kb_ref, vbT_ref):
    x = x_ref[...]
    tm = x.shape[0]
    h = (x * lax.rsqrt(jnp.mean(x * x, axis=-1, keepdims=True) + EPS) * g_ref[...]).astype(BF16)

    def proj(lo, hi):
        return _dot(h, w1_ref[:, lo:hi])

    def head_norm(zh, gain):
        ss = jnp.sum(zh * zh, axis=-1, keepdims=True) * (1.0 / A_HEAD_DIM)
        return zh * lax.rsqrt(ss + EPS) * gain

    ones_rows = jnp.where(lax.broadcasted_iota(I32, (BF16_ROWS, tm), 0) == 0, 1.0, 0.0).astype(BF16)

    def store_vT(ref, v):
        vT = v.T
        for hh in range(A_HEADS):
            ref[VP * hh:VP * hh + B_V, :] = vT[B_V * hh:B_V * (hh + 1), :].astype(BF16)
            ref[VP * hh + B_V:VP * (hh + 1), :] = ones_rows

    spare_a = jnp.where(lax.broadcasted_iota(I32, (1, HP), 1) == A_HEAD_DIM, 1.0, 0.0)
    qa = proj(_C_QA, _C_KA)
    ka = proj(_C_KA, _C_VA)
    gqa = gqa_ref[...] * (A_HEAD_DIM ** -0.5 * LOG2E)
    gka = gka_ref[...]
    for hh in range(A_HEADS):
        blk = slice(HP * hh, HP * (hh + 1))
        qh = head_norm(qa[:, blk], gqa)
        qnorm = jnp.sqrt(jnp.sum(qh * qh, axis=-1, keepdims=True))
        m_a = (qnorm * bnd_ref[0:1, :] + bnd_ref[2:3, hh:hh + 1]) * BOUND_SLACK
        qaT_ref[blk, :] = (qh - m_a * spare_a).T.astype(BF16)
        ka_ref[:, blk] = (head_norm(ka[:, blk], gka) + spare_a).astype(BF16)
    store_vT(vaT_ref, proj(_C_VA, _C_QI))

    qi_ref[...] = proj(_C_QI, _C_KI).astype(BF16)
    kiT = proj(_C_KI, _C_WI).T.astype(BF16)
    for c in range(tm // KU_SEL):
        kiT_ref[c] = kiT[:, KU_SEL * c:KU_SEL * (c + 1)]
    wi_ref[...] = proj(_C_WI, _C_CQ) * ((IDX_HEADS ** -0.5) * (IDX_DIM ** -0.5))

    cos = cos_ref[...]
    sin = sin_ref[...]
    spare = jnp.where(lax.broadcasted_iota(I32, (1, HP), 1) == B_QK, 1.0, 0.0)
    cq = proj(_C_CQ, _C_CKV)
    cqn = (cq * lax.rsqrt(jnp.mean(cq * cq, axis=-1, keepdims=True) + EPS) * gcq_ref[...]).astype(BF16)
    qm = _dot(cqn, wqm_ref[...])
    qs = _dot(cqn, wqs_ref[...])
    gqb = gqb_ref[...] * (B_QK ** -0.5 * LOG2E)
    kbound = bnd_ref[1:2, :]
    for hh in range(B_HEADS):
        blk = slice(HP * hh, HP * (hh + 1))
        qh = qm[:, blk] * cos + qs[:, blk] * sin
        ss = jnp.sum(qh * qh, axis=-1, keepdims=True) * (1.0 / B_QK)
        qh = qh * lax.rsqrt(ss + EPS) * gqb
        m_b = jnp.sqrt(jnp.sum(qh * qh, axis=-1, keepdims=True)) * kbound * BOUND_SLACK
        qbT_ref[blk, :] = (qh - m_b * spare).T.astype(BF16)

    ckv = proj(_C_CKV, _C_KR)
    ckvn = (ckv * lax.rsqrt(jnp.mean(ckv * ckv, axis=-1, keepdims=True) + EPS) * gckv_ref[...]).astype(BF16)
    km = _dot(ckvn, wkm_ref[...])
    store_vT(vbT_ref, _dot(ckvn, wv_ref[...]))
    krot = proj(_C_KR, _C_KRS) * cos + proj(_C_KRS, _C_END) * sin
    gkb = gkb_ref[...]
    for hh in range(B_HEADS):
        blk = slice(HP * hh, HP * (hh + 1))
        kh = km[:, blk] + krot
        ss = jnp.sum(kh * kh, axis=-1, keepdims=True) * (1.0 / B_QK)
        kb_ref[:, blk] = (kh * lax.rsqrt(ss + EPS) * gkb + spare).astype(BF16)


def _token_prep(x, g, w1, wqm, wqs, wkm, wv, gqa, gka, gcq, gckv, gqb, gkb, bnd,
                cos_t, sin_t):
    B, S, D = x.shape
    tm = TM_PREP
    nt = S // tm
    row3 = lambda w: pl.BlockSpec((None, tm, w), lambda b, i: (b, i, 0))
    colT = lambda r: pl.BlockSpec((None, r, tm), lambda b, i: (b, 0, i))
    full = lambda a: pl.BlockSpec(a.shape, lambda b, i: (0,) * a.ndim)
    tab = pl.BlockSpec((tm, LANES), lambda b, i: (i, 0))
    consts = (g, w1, wqm, wqs, wkm, wv, gqa, gka, gcq, gckv, gqb, gkb, bnd)
    out_shape = (
        jax.ShapeDtypeStruct((B, A_HEADS * HP, S), BF16),
        jax.ShapeDtypeStruct((B, S, A_HEADS * HP), BF16),
        jax.ShapeDtypeStruct((B, A_HEADS * VP, S), BF16),
        jax.ShapeDtypeStruct((B, S, IDX_HEADS * HP), BF16),
        jax.ShapeDtypeStruct((B, S // KU_SEL, LANES, KU_SEL), BF16),
        jax.ShapeDtypeStruct((B, S, LANES), F32),
        jax.ShapeDtypeStruct((B, B_HEADS * HP, S), BF16),
        jax.ShapeDtypeStruct((B, S, B_HEADS * HP), BF16),
        jax.ShapeDtypeStruct((B, B_HEADS * VP, S), BF16),
    )
    kiT_spec = pl.BlockSpec((None, tm // KU_SEL, LANES, KU_SEL), lambda b, i: (b, i, 0, 0))
    out_specs = (colT(A_HEADS * HP), row3(A_HEADS * HP), colT(A_HEADS * VP), row3(IDX_HEADS * HP),
                 kiT_spec, row3(LANES), colT(B_HEADS * HP), row3(B_HEADS * HP), colT(B_HEADS * VP))
    return pl.pallas_call(
        _token_prep_kernel,
        grid=(B, nt),
        in_specs=[row3(D)] + [full(a) for a in consts] + [tab, tab],
        out_specs=out_specs,
        out_shape=out_shape,
        compiler_params=_cparams(("parallel", "parallel")),
        name="token_prep",
    )(x, *consts, cos_t, sin_t)


def _bit_transpose32(words):
    a = list(words)
    j, m = 16, 0x0000FFFF
    while j:
        k = 0
        while k < 32:
            t = (a[k] ^ lax.shift_right_logical(a[k + j], jnp.int32(j))) & m
            a[k] = a[k] ^ t
            a[k + j] = a[k + j] ^ (t << j)
            k = (k + j + 1) & ~j
        j >>= 1
        m = m ^ (m << j)
    return a


def _expand_bits(word):
    return jnp.concatenate(
        [lax.shift_right_logical(word, jnp.int32(SEL_BITS - 1 - j)) & 1 for j in range(SEL_BITS)], axis=0)


def _select_kernel(qi_ref, wi_ref, kiT_ref, sel_ref, planes_ref, alive_ref, gt_ref,
                   acca_ref, accb_ref, wb_ref, *, topk):
    tq = TQ_SEL
    i = pl.program_id(1)
    n_keys = (i + 1) * tq
    n_groups = (i + GU_SEL) // GU_SEL

    for hh in range(IDX_HEADS):
        wb_ref[hh] = jnp.broadcast_to(wi_ref[:, hh:hh + 1], (tq, LANES))

    zeros_group = jnp.zeros((GU_SEL, SUBLANES, tq), I32)
    for r in range(SEL_BITS):
        planes_ref[r, pl.ds((n_groups - 1) * GU_SEL, GU_SEL)] = zeros_group

    n_units = n_keys // KU_SEL
    accb_ref[...] = jnp.zeros(accb_ref.shape, F32)

    def heads(u, dst_ref):
        kT = kiT_ref[jnp.minimum(u, n_units - 1)]
        for hh in range(IDX_HEADS):
            d = _dot(qi_ref[:, HP * hh:HP * (hh + 1)], kT)
            for c in range(tq // CH_SEL):
                rows = slice(CH_SEL * c, CH_SEL * (c + 1))
                w = wb_ref[hh, rows, :]
                for lanes in (slice(0, LANES), slice(LANES, 2 * LANES)):
                    t = jnp.maximum(d[rows, lanes], 0.0) * w
                    if hh == 0:
                        dst_ref[rows, lanes] = t
                    else:
                        dst_ref[rows, lanes] += t

    def convert(src_ref, u):
        accT = src_ref[...].T
        words = []
        for j in range(SEL_BITS):
            acc = accT[SUBLANES * j:SUBLANES * (j + 1), :] + 0.0
            bits = pltpu.bitcast(acc, I32)
            words.append(bits ^ ((bits >> 31) | SIGN_BIT))
        unit = jnp.maximum(u, 0)
        for r, plane in enumerate(_bit_transpose32(words)):
            planes_ref[r, unit] = plane

    def unit_pair(t, carry):
        heads(2 * t, acca_ref)
        convert(accb_ref, 2 * t - 1)
        heads(2 * t + 1, accb_ref)
        convert(acca_ref, 2 * t)
        return carry

    lax.fori_loop(0, n_units // 2 + 1, unit_pair, 0)

    full_group = jnp.full((GU_SEL, SUBLANES, tq), -1, I32)

    def init(g, carry):
        unit = g * GU_SEL + lax.broadcasted_iota(I32, (GU_SEL, SUBLANES, tq), 0)
        alive_ref[pl.ds(g * GU_SEL, GU_SEL)] = jnp.where(unit < i, full_group, zeros_group)
        gt_ref[pl.ds(g * GU_SEL, GU_SEL)] = zeros_group
        return carry
    lax.fori_loop(0, n_groups, init, 0)

    q_chunk = (i * tq + lax.broadcasted_iota(I32, (SUBLANES, tq), 1)) >> CHUNK_SHIFT
    k_row = i * tq + lax.broadcasted_iota(I32, (SUBLANES, tq), 0)
    adm_bits = jnp.zeros((SUBLANES, tq), I32)
    for j in range(SEL_BITS):
        adm = ((k_row + SUBLANES * j) >> CHUNK_SHIFT) <= q_chunk
        adm_bits = adm_bits | jnp.where(adm, jnp.int32(1) << (SEL_BITS - 1 - j), 0)
    alive_ref[i] = adm_bits

    def popcount_rows(x):
        return jnp.sum(lax.population_count(x), axis=0)

    def count_alive(plane_of):
        def body(g, acc):
            return acc + popcount_rows(plane_of(g) & alive_ref[pl.ds(g * GU_SEL, GU_SEL)])
        acc = lax.fori_loop(0, n_groups, body, jnp.zeros((SUBLANES, tq), I32))
        return jnp.sum(acc, axis=0, keepdims=True)

    def bit_pass(r, need):
        ones_here = count_alive(lambda g: planes_ref[r, pl.ds(g * GU_SEL, GU_SEL)])
        take = ones_here >= need

        def update(g, carry):
            grp = pl.ds(g * GU_SEL, GU_SEL)
            alive = alive_ref[grp]
            with_bit = alive & planes_ref[r, grp]
            alive_ref[grp] = jnp.where(take, with_bit, alive ^ with_bit)
            gt_ref[grp] = jnp.where(take, gt_ref[grp], gt_ref[grp] | with_bit)
            return carry
        lax.fori_loop(0, n_groups, update, 0)
        return jnp.where(take, need, need - ones_here)

    need = lax.fori_loop(0, SEL_BITS, bit_pass, jnp.full((1, tq), topk, I32))
    n_equal = count_alive(lambda g: full_group)
    any_tie = jnp.max(jnp.where(n_equal > need, 1, 0)) > 0

    @pl.when(jnp.logical_not(any_tie))
    def _():
        def body(g, carry):
            grp = pl.ds(g * GU_SEL, GU_SEL)
            sel_ref[grp] = gt_ref[grp] | alive_ref[grp]
            return carry
        lax.fori_loop(0, n_groups, body, 0)

    @pl.when(any_tie)
    def _():
        n_take = need.astype(F32)
        ltri = (lax.broadcasted_iota(I32, (KU_SEL, KU_SEL), 0)
                >= lax.broadcasted_iota(I32, (KU_SEL, KU_SEL), 1)).astype(BF16)

        def body(u, seen):
            eqf = _expand_bits(alive_ref[u]).astype(F32)
            rank = _dot(ltri, eqf.astype(BF16)) + seen
            first = jnp.logical_and(eqf != 0.0, rank <= n_take)
            words = gt_ref[u]
            for j in range(SEL_BITS):
                words = words | jnp.where(first[SUBLANES * j:SUBLANES * (j + 1), :],
                                          jnp.int32(1) << (SEL_BITS - 1 - j), 0)
            sel_ref[u] = words
            return seen + jnp.sum(eqf, axis=0, keepdims=True)
        lax.fori_loop(0, n_units, body, jnp.zeros((1, tq), F32))

        def clear(u, carry):
            sel_ref[u] = jnp.zeros((SUBLANES, tq), I32)
            return carry
        lax.fori_loop(n_units, n_groups * GU_SEL, clear, 0)

    def fill(g, carry):
        sel_ref[pl.ds(g * GU_SEL, GU_SEL)] = zeros_group
        return carry
    lax.fori_loop(n_groups, sel_ref.shape[0] // GU_SEL, fill, 0)


def _select_topk(qi, wi, kiT, topk):
    B, S, _ = qi.shape
    tq = TQ_SEL
    n_units, dims, unit = kiT.shape[1:]
    assert unit == KU_SEL == tq == SEL_BITS * SUBLANES and n_units * unit == S
    n_alloc = -(-n_units // GU_SEL) * GU_SEL
    return pl.pallas_call(
        functools.partial(_select_kernel, topk=topk),
        grid=(B, S // tq),
        in_specs=[
            pl.BlockSpec((None, tq, IDX_HEADS * HP), lambda b, i: (b, i, 0)),
            pl.BlockSpec((None, tq, LANES), lambda b, i: (b, i, 0)),
            pl.BlockSpec((None, n_units, dims, unit), lambda b, i: (b, 0, 0, 0)),
        ],
        out_specs=pl.BlockSpec((None, n_alloc, SUBLANES, tq), lambda b, i: (b, 0, 0, i)),
        out_shape=jax.ShapeDtypeStruct((B, n_alloc, SUBLANES, S), I32),
        scratch_shapes=[pltpu.VMEM((SEL_BITS, n_alloc + 1, SUBLANES, tq), I32),
                        pltpu.VMEM((n_alloc, SUBLANES, tq), I32),
                        pltpu.VMEM((n_alloc, SUBLANES, tq), I32),
                        pltpu.VMEM((tq, KU_SEL), F32), pltpu.VMEM((tq, KU_SEL), F32),
                        pltpu.VMEM((IDX_HEADS, tq, LANES), F32)],
        compiler_params=_cparams(("parallel", "parallel")),
        name="select_topk",
    )(qi, wi, kiT)


def _attend_kernel(ti_ref, tj_ref, safe_ref, *refs, heads, with_mask):
    if with_mask:
        qT_ref, k_ref, vT_ref, selbits_ref, bias_ref, o_ref, acc_ref, s_ref, p_ref, m_ref, add_ref = refs
    else:
        qT_ref, k_ref, vT_ref, diag_ref, o_ref, acc_ref, s_ref, p_ref, m_ref = refs
    s_idx = pl.program_id(1)
    i = ti_ref[s_idx]
    jj = tj_ref[s_idx]
    bounded = safe_ref[0] == 1
    tk = TK_ATT
    n_ch = tk // CH_ATT

    @pl.when(jj == 0)
    def _():
        acc_ref[...] = jnp.zeros(acc_ref.shape, F32)
        m_ref[...] = jnp.full(m_ref.shape, NEG, F32)

    def chunks():
        return [slice(CH_ATT * c, CH_ATT * (c + 1)) for c in range(n_ch)]

    def key_block(sb, carry):
        j = jj * NSUB_ATT + sb
        k0 = pl.multiple_of(sb * tk, tk)

        def logits(hh):
            s_ref[hh % 2] = _dot(k_ref[pl.ds(k0, tk), HP * hh:HP * (hh + 1)],
                                 qT_ref[HP * hh:HP * (hh + 1), :])

        def pv(hh, buf):
            return _dot(vT_ref[VP * hh:VP * (hh + 1), pl.ds(k0, tk)], p_ref[buf])

        def sweep_bounded(extra):
            logits(0)
            for hh in range(heads):
                buf = hh % 2
                if hh + 1 < heads:
                    logits(hh + 1)
                for rows in chunks():
                    t = s_ref[buf, rows, :]
                    e = extra(hh, rows)
                    if e is not None:
                        t = t + e
                    p_ref[buf, rows, :] = jnp.exp2(t).astype(BF16)
                acc_ref[VP * hh:VP * (hh + 1), :] += pv(hh, buf)

        def sweep_running_max(extra):
            for hh in range(heads):
                buf = hh % 2
                logits(hh)
                m_blk = jnp.full((SUBLANES, s_ref.shape[2]), NEG, F32)
                for rows in chunks():
                    t = s_ref[buf, rows, :]
                    e = extra(hh, rows)
                    if e is not None:
                        t = t + e
                        s_ref[buf, rows, :] = t
                    for r in range(CH_ATT // SUBLANES):
                        m_blk = jnp.maximum(m_blk, t[SUBLANES * r:SUBLANES * (r + 1), :])
                m_old = m_ref[hh:hh + 1, :]
                m_new = jnp.maximum(m_old, jnp.max(m_blk, axis=0, keepdims=True))
                m_ref[hh:hh + 1, :] = m_new
                for rows in chunks():
                    p_ref[buf, rows, :] = jnp.exp2(s_ref[buf, rows, :] - m_new).astype(BF16)
                acc_ref[VP * hh:VP * (hh + 1), :] = (
                    jnp.exp2(m_old - m_new) * acc_ref[VP * hh:VP * (hh + 1), :] + pv(hh, buf))

        def both(extra):
            @pl.when(bounded)
            def _():
                sweep_bounded(extra)

            @pl.when(jnp.logical_not(bounded))
            def _():
                sweep_running_max(extra)

        if with_mask:
            @pl.when(j <= i)
            def _():
                for c in range(tk // KU_SEL):
                    bits = _expand_bits(selbits_ref[sb * (tk // KU_SEL) + c])
                    add_ref[KU_SEL * c:KU_SEL * (c + 1), :] = jnp.where(bits != 0, 0.0, NEG)

            @pl.when(j < i - 1)
            def _():
                both(lambda hh, rows: add_ref[rows, :])

            @pl.when(jnp.logical_and(j >= i - 1, j <= i))
            def _():
                near = i - j
                both(lambda hh, rows: add_ref[rows, :] + bias_ref[hh, near, rows, :].astype(F32))
        else:
            @pl.when(j < i)
            def _():
                both(lambda hh, rows: None)

            @pl.when(j == i)
            def _():
                both(lambda hh, rows: diag_ref[rows, :])
        return carry

    lax.fori_loop(0, NSUB_ATT, key_block, 0)

    @pl.when(jj == i // NSUB_ATT)
    def _():
        dv = VP - BF16_ROWS
        for hh in range(heads):
            inv = 1.0 / acc_ref[VP * hh + dv:VP * hh + dv + 1, :]
            s_ref[0, dv * hh:dv * (hh + 1), :] = acc_ref[VP * hh:VP * hh + dv, :] * inv
        o_ref[...] = s_ref[0, 0:dv * heads, :].T.astype(o_ref.dtype)


def _pair_tables(S):
    n = S // TQ_ATT
    ti = np.array([i for i in range(n) for _ in range(i // NSUB_ATT + 1)], np.int32)
    tj = np.array([j for i in range(n) for j in range(i // NSUB_ATT + 1)], np.int32)
    return jnp.asarray(ti), jnp.asarray(tj)


def _attend(qT, k, vT, safe, extra_inputs, extra_specs, extra_scratch, with_mask, name):
    B, S, _ = k.shape
    tq, tk = TQ_ATT, TK_ATT
    tg = tk * NSUB_ATT
    assert tq == tk and tk >= B_V * A_HEADS and S % tg == 0
    ti, tj = _pair_tables(S)
    heads = A_HEADS
    kern = functools.partial(_attend_kernel, heads=heads, with_mask=with_mask)
    grid_spec = pltpu.PrefetchScalarGridSpec(
        num_scalar_prefetch=3,
        grid=(B, ti.shape[0]),
        in_specs=[
            pl.BlockSpec((None, heads * HP, tq), lambda b, s, ti, tj, sf: (b, 0, ti[s])),
            pl.BlockSpec((None, tg, heads * HP), lambda b, s, ti, tj, sf: (b, tj[s], 0)),
            pl.BlockSpec((None, heads * VP, tg), lambda b, s, ti, tj, sf: (b, 0, tj[s])),
        ] + extra_specs,
        out_specs=pl.BlockSpec((None, tq, heads * B_V), lambda b, s, ti, tj, sf: (b, ti[s], 0)),
        scratch_shapes=[pltpu.VMEM((heads * VP, tq), F32),
                        pltpu.VMEM((2, tk, tq), F32),
                        pltpu.VMEM((2, tk, tq), BF16),
                        pltpu.VMEM((heads, tq), F32)] + extra_scratch,
    )
    return pl.pallas_call(
        kern, grid_spec=grid_spec,
        out_shape=jax.ShapeDtypeStruct((B, S, heads * B_V), BF16),
        compiler_params=_cparams(("parallel", "arbitrary")),
        name=name,
    )(ti, tj, safe, qT, k, vT, *extra_inputs)


def _attend_a(qaT, ka, vaT, selbits, bias_tiles, safe):
    tq, tk = TQ_ATT, TK_ATT
    specs = [
        pl.BlockSpec((None, tk * NSUB_ATT // KU_SEL, SUBLANES, tq),
                     lambda b, s, ti, tj, sf: (b, tj[s], 0, ti[s])),
        pl.BlockSpec(bias_tiles.shape, lambda b, s, ti, tj, sf: (0, 0, 0, 0)),
    ]
    return _attend(qaT, ka, vaT, safe, (selbits, bias_tiles), specs, [pltpu.VMEM((tk, tq), F32)],
                   True, "attend_a")


def _attend_b(qbT, kb, vbT, diag_tile, safe):
    tq, tk = TQ_ATT, TK_ATT
    specs = [pl.BlockSpec((tk, tq), lambda b, s, ti, tj, sf: (0, 0))]
    return _attend(qbT, kb, vbT, safe, (diag_tile,), specs, [], False, "attend_b")


def _merge_kernel(x_ref, ya_ref, yb_ref, g_ref, wga_ref, wgb_ref, bga_ref, bgb_ref,
                  wpa_ref, wpb_ref, wo_ref, o_ref):
    x = x_ref[...]
    h = (x * lax.rsqrt(jnp.mean(x * x, axis=-1, keepdims=True) + EPS) * g_ref[...]).astype(BF16)
    gate_a = jax.nn.sigmoid(_dot(h, wga_ref[...]) + bga_ref[...])
    gate_b = jax.nn.sigmoid(_dot(h, wgb_ref[...]) + bgb_ref[...])
    merged = gate_a * _dot(ya_ref[...], wpa_ref[...]) + gate_b * _dot(yb_ref[...], wpb_ref[...])
    o_ref[...] = x + _dot(merged.astype(BF16), wo_ref[...])


def _merge_out(x2, ya2, yb2, g, wga, wgb, bga, bgb, wpa, wpb, wo):
    R, D = x2.shape
    tm = TM_MERGE
    row = lambda w: pl.BlockSpec((tm, w), lambda i: (i, 0))
    full = lambda a: pl.BlockSpec(a.shape, lambda i: (0,) * a.ndim)
    consts = (g, wga, wgb, bga, bgb, wpa, wpb, wo)
    return pl.pallas_call(
        _merge_kernel,
        grid=(R // tm,),
        in_specs=[row(D), row(A_WIDTH), row(B_WIDTH)] + [full(a) for a in consts],
        out_specs=row(D),
        out_shape=jax.ShapeDtypeStruct((R, D), F32),
        compiler_params=_cparams(("parallel",)),
        name="merge_out",
    )(x2, ya2, yb2, *consts)


def _ffn_kernel(x_ref, xp_ref, g_ref, wuv_ref, wug_ref, cwv_ref, cwg_ref, cbv_ref, cbg_ref, wd_ref,
                o_ref, uv_ref, ug_ref, acc_ref, *, tiles_per_seq):
    i = pl.program_id(0)
    f = pl.program_id(1)
    tm = x_ref.shape[0]
    halo = SUBLANES
    g = g_ref[...]

    def normed(v):
        return (v * lax.rsqrt(jnp.mean(v * v, axis=-1, keepdims=True) + EPS) * g).astype(BF16)

    h = normed(x_ref[...])
    keep = jnp.where(i % tiles_per_seq == 0, 0.0, 1.0)
    hp = normed(xp_ref[...])

    def conv(u_ref, w_ref, cw_ref, cb_ref):
        u_ref[0:halo, :] = _dot(hp, w_ref[...]) * keep
        u_ref[halo:halo + tm, :] = _dot(h, w_ref[...])
        out = cb_ref[...]
        for t in range(CONV_W):
            lo = halo - (CONV_W - 1) + t
            out = out + cw_ref[t:t + 1, :] * u_ref[lo:lo + tm, :]
        return out

    val = conv(uv_ref, wuv_ref, cwv_ref, cbv_ref)
    gat = conv(ug_ref, wug_ref, cwg_ref, cbg_ref)
    act = (gat * jax.nn.sigmoid(gat) * val).astype(BF16)
    part = _dot(act, wd_ref[...])

    @pl.when(f == 0)
    def _():
        acc_ref[...] = part

    @pl.when(f > 0)
    def _():
        acc_ref[...] = acc_ref[...] + part

    @pl.when(f == pl.num_programs(1) - 1)
    def _():
        o_ref[...] = x_ref[...] + acc_ref[...]


def _conv_ffn(x2, S, g, wu, cw, cb, wd):
    R, D = x2.shape
    tm, tf = TM_FFN, TF_FFN
    nf = D_FF // tf
    halo_blocks = tm // SUBLANES
    kern = functools.partial(_ffn_kernel, tiles_per_seq=S // tm)
    return pl.pallas_call(
        kern,
        grid=(R // tm, nf),
        in_specs=[
            pl.BlockSpec((tm, D), lambda i, f: (i, 0)),
            pl.BlockSpec((SUBLANES, D), lambda i, f: (jnp.maximum(i * halo_blocks - 1, 0), 0)),
            pl.BlockSpec((1, D), lambda i, f: (0, 0)),
            pl.BlockSpec((D, tf), lambda i, f: (0, f)),
            pl.BlockSpec((D, tf), lambda i, f: (0, nf + f)),
            pl.BlockSpec((CONV_W, tf), lambda i, f: (0, f)),
            pl.BlockSpec((CONV_W, tf), lambda i, f: (0, nf + f)),
            pl.BlockSpec((1, tf), lambda i, f: (0, f)),
            pl.BlockSpec((1, tf), lambda i, f: (0, nf + f)),
            pl.BlockSpec((tf, D), lambda i, f: (f, 0)),
        ],
        out_specs=pl.BlockSpec((tm, D), lambda i, f: (i, 0)),
        out_shape=jax.ShapeDtypeStruct((R, D), F32),
        scratch_shapes=[pltpu.VMEM((tm + SUBLANES, tf), F32), pltpu.VMEM((tm + SUBLANES, tf), F32),
                        pltpu.VMEM((tm, D), F32)],
        compiler_params=_cparams(("parallel", "arbitrary")),
        name="conv_ffn",
    )(x2, x2, g, wu, wu, cw, cw, cb, cb, wd)


def _t5_bucket(rel):
    nb = REL_BUCKETS // 2
    max_exact = nb // 2
    side = jnp.where(rel > 0, nb, 0)
    n = jnp.abs(rel)
    nf = jnp.maximum(n, 1).astype(F32)
    large = max_exact + (jnp.log(nf / max_exact) / math.log(REL_MAX_DIST / max_exact)
                         * (nb - max_exact)).astype(I32)
    large = jnp.minimum(large, nb - 1)
    return side + jnp.where(n < max_exact, n, large)


def _bias_tiles(rel_bias):
    tq, tk = TQ_ATT, TK_ATT
    assert tq == tk and tk >= REL_MAX_DIST
    kk = jnp.arange(tk, dtype=I32)[:, None]
    qq = jnp.arange(tq, dtype=I32)[None, :]
    bucket = _t5_bucket(jnp.stack([kk - qq, kk - qq - tk])).reshape(1, 2 * tk * tq)
    onehot = (jnp.arange(REL_BUCKETS, dtype=I32)[:, None] == bucket).astype(F32)
    far = rel_bias[_t5_bucket(jnp.asarray(-REL_MAX_DIST, I32))]
    table = ((rel_bias - far[None, :]) * LOG2E).T
    tiles = jnp.dot(table, onehot, precision=lax.Precision.HIGHEST)
    return tiles.reshape(A_HEADS, 2, tk, tq).astype(BF16), jnp.max(jnp.abs(table), axis=1)


def _diag_tile():
    kk = np.arange(TK_ATT)[:, None] // CHUNK
    qq = np.arange(TQ_ATT)[None, :] // CHUNK
    return jnp.asarray(np.where(kk <= qq, 0.0, NEG).astype(np.float32))


def _rope_tables(S):
    half = B_ROPE // 2
    inv = ROPE_BASE ** (-jnp.arange(half, dtype=F32) / half)
    ang = jnp.arange(S, dtype=I32).astype(F32)[:, None] * inv[None, :]
    cos, sin = jnp.cos(ang), jnp.sin(ang)
    ones = jnp.ones((S, B_NOPE), F32)
    zeros_n = jnp.zeros((S, B_NOPE), F32)
    zeros_p = jnp.zeros((S, HP - B_QK), F32)
    cos_t = jnp.concatenate([ones, cos, cos, zeros_p], axis=1)
    sin_t = jnp.concatenate([zeros_n, -sin, sin, zeros_p], axis=1)
    return cos_t, sin_t


def _pad_cols(w, width):
    return jnp.pad(w, ((0, 0), (0, width - w.shape[1])))


def _pad_heads(w, heads, width):
    rows = w.shape[0]
    w = w.reshape(rows, heads, -1)
    return jnp.pad(w, ((0, 0), (0, 0), (0, width - w.shape[2]))).reshape(rows, heads * width)


def _swap_halves(w):
    half = w.shape[-1] // 2
    return jnp.concatenate([w[..., half:], w[..., :half]], axis=-1)


def _layer_weights(l, w_in, b_w_uq, b_w_ukv, b_q_norm, b_k_norm, a_q_norm, a_k_norm, bias_max):
    w = w_in[l]
    o = np.cumsum([0, A_WIDTH, A_WIDTH, A_WIDTH, IDX_HEADS * IDX_DIM, IDX_DIM, IDX_HEADS,
                   B_Q_LORA, B_KV_LORA, B_ROPE, D_MODEL, D_MODEL])
    seg = [w[:, o[t]:o[t + 1]] for t in range(11)]
    zn = jnp.zeros((D_MODEL, B_NOPE), F32)
    kr = seg[8]
    w1 = jnp.concatenate([
        _pad_heads(seg[0], A_HEADS, HP), _pad_heads(seg[1], A_HEADS, HP), seg[2],
        _pad_heads(seg[3], IDX_HEADS, HP),
        _pad_cols(seg[4], LANES), _pad_cols(seg[5], LANES), seg[6], seg[7],
        _pad_cols(jnp.concatenate([zn, kr], axis=1), HP),
        _pad_cols(jnp.concatenate([zn, _swap_halves(kr)], axis=1), HP),
    ], axis=1).astype(BF16)
    assert w1.shape[1] == _C_END

    uq = b_w_uq[l].reshape(B_Q_LORA, B_HEADS, B_QK)
    zq = jnp.zeros((B_Q_LORA, B_HEADS, B_NOPE), F32)
    pq = jnp.zeros((B_Q_LORA, B_HEADS, HP - B_QK), F32)
    wqm = jnp.concatenate([uq, pq], axis=-1).reshape(B_Q_LORA, B_HEADS * HP).astype(BF16)
    wqs = jnp.concatenate([zq, _swap_halves(uq[..., B_NOPE:]), pq], axis=-1)
    wqs = wqs.reshape(B_Q_LORA, B_HEADS * HP).astype(BF16)

    ukv = b_w_ukv[l].reshape(B_KV_LORA, B_HEADS, B_NOPE + B_V)
    pk = jnp.zeros((B_KV_LORA, B_HEADS, HP - B_NOPE), F32)
    wkm = jnp.concatenate([ukv[..., :B_NOPE], pk], axis=-1).reshape(B_KV_LORA, B_HEADS * HP).astype(BF16)
    wv = ukv[..., B_NOPE:].reshape(B_KV_LORA, B_WIDTH).astype(BF16)

    gqa = _pad_cols(a_q_norm[l][None, :], HP)
    gka = _pad_cols(a_k_norm[l][None, :], HP)
    gqb = _pad_cols(b_q_norm[l][None, :], HP)
    gkb = _pad_cols(b_k_norm[l][None, :], HP)

    ka_bound = math.sqrt(A_HEAD_DIM) * jnp.max(jnp.abs(a_k_norm[l]))
    kb_bound = math.sqrt(B_QK) * jnp.max(jnp.abs(b_k_norm[l]))
    bnd = jnp.zeros((SUBLANES, LANES), F32)
    bnd = bnd.at[0, :].set(ka_bound).at[1, :].set(kb_bound).at[2, :A_HEADS].set(bias_max)
    qa_bound = math.sqrt(A_HEAD_DIM) * jnp.max(jnp.abs(a_q_norm[l])) * (A_HEAD_DIM ** -0.5 * LOG2E)
    qb_bound = math.sqrt(B_QK) * jnp.max(jnp.abs(b_q_norm[l])) * (B_QK ** -0.5 * LOG2E)
    range_a = (qa_bound * ka_bound + jnp.max(bias_max)) * BOUND_SLACK + jnp.max(bias_max)
    range_b = qb_bound * kb_bound * BOUND_SLACK
    safe_a = (range_a <= MAX_LOG2_RANGE).astype(I32).reshape(1)
    safe_b = (range_b <= MAX_LOG2_RANGE).astype(I32).reshape(1)
    return (w1, wqm, wqs, wkm, wv, gqa, gka, gqb, gkb, bnd, safe_a, safe_b,
            seg[9].astype(BF16), seg[10].astype(BF16))


def kernel(x, rel_bias, norm_mix, w_in, a_q_norm, a_k_norm, b_cq_norm, b_ckv_norm, b_w_uq, b_w_ukv,
           b_q_norm, b_k_norm, w_proj_a, w_proj_b, b_gate, w_out, norm_ffn, w_up, conv_w, conv_b, w_down):
    B, S, D = x.shape
    assert D == D_MODEL and S % TQ_ATT == 0 and S % TM_FFN == 0
    topk = min(TOPK_MAX, S // 4)
    assert TQ_SEL >= topk
    depth = w_in.shape[0]

    cos_t, sin_t = _rope_tables(S)
    bias_tiles, bias_max = _bias_tiles(rel_bias)
    diag_tile = _diag_tile()

    for l in range(depth):
        (w1, wqm, wqs, wkm, wv, gqa, gka, gqb, gkb, bnd, safe_a, safe_b, wga, wgb) = _layer_weights(
            l, w_in, b_w_uq, b_w_ukv, b_q_norm, b_k_norm, a_q_norm, a_k_norm, bias_max)
        qaT, ka, vaT, qi, kiT, wi, qbT, kb, vbT = _token_prep(
            x, norm_mix[l][None, :], w1, wqm, wqs, wkm, wv, gqa, gka,
            b_cq_norm[l][None, :], b_ckv_norm[l][None, :], gqb, gkb, bnd,
            cos_t, sin_t)
        selbits = _select_topk(qi, wi, kiT, topk)
        y_a = _attend_a(qaT, ka, vaT, selbits, bias_tiles, safe_a)
        y_b = _attend_b(qbT, kb, vbT, diag_tile, safe_b)
        x2 = _merge_out(
            x.reshape(B * S, D), y_a.reshape(B * S, A_WIDTH), y_b.reshape(B * S, B_WIDTH),
            norm_mix[l][None, :], wga, wgb, b_gate[l][None, :D_MODEL], b_gate[l][None, D_MODEL:],
            w_proj_a[l].astype(BF16), w_proj_b[l].astype(BF16), w_out[l].astype(BF16))
        x2 = _conv_ffn(x2, S, norm_ffn[l][None, :], w_up[l].astype(BF16), conv_w[l], conv_b[l][None, :],
                       w_down[l].astype(BF16))
        x = x2.reshape(B, S, D)
    return x
```

```python
import functools
import math

import numpy as np
import jax
import jax.numpy as jnp
from jax import lax
from jax.experimental import pallas as pl
from jax.experimental.pallas import tpu as pltpu

F32 = jnp.float32
BF16 = jnp.bfloat16
I32 = jnp.int32

D_MODEL = 1024
CHUNK = 64
CHUNK_SHIFT = 6
assert 1 << CHUNK_SHIFT == CHUNK
A_HEADS = 8
A_HEAD_DIM = 64
A_WIDTH = A_HEADS * A_HEAD_DIM
IDX_HEADS = 8
IDX_DIM = 64
TOPK_MAX = 256
B_HEADS = 8
B_Q_LORA = 384
B_KV_LORA = 256
B_NOPE = 64
B_ROPE = 32
B_QK = B_NOPE + B_ROPE
B_V = 64
B_WIDTH = B_HEADS * B_V
ROPE_BASE = 10000.0
REL_BUCKETS = 32
REL_MAX_DIST = 128
D_FF = 2816
CONV_W = 3
EPS = 1e-6

LANES = 128
SUBLANES = 8
BF16_ROWS = 16
VMEM_LIMIT = 56 * 1024 * 1024

TM_PREP = 512
TQ_SEL = 256
KU_SEL = 256
CH_SEL = 64
SEL_BITS = 32
GU_SEL = 16
TQ_ATT = 512
TK_ATT = 512
NSUB_ATT = 4
CH_ATT = 64
TM_MERGE = 512
TM_FFN = 512
TF_FFN = 1408

HP = 128
VP = B_V + BF16_ROWS
NEG = -1e30
SIGN_BIT = -2147483648
LOG2E = 1.4426950408889634
BOUND_SLACK = 1.01
MAX_LOG2_RANGE = 60.0


def _cparams(sem):
    return pltpu.CompilerParams(dimension_semantics=sem, vmem_limit_bytes=VMEM_LIMIT)


def _dot(a, b):
    return jnp.dot(a, b, preferred_element_type=F32)


_C_QA = 0
_C_KA = _C_QA + A_HEADS * HP
_C_VA = _C_KA + A_HEADS * HP
_C_QI = _C_VA + A_WIDTH
_C_KI = _C_QI + IDX_HEADS * HP
_C_WI = _C_KI + LANES
_C_CQ = _C_WI + LANES
_C_CKV = _C_CQ + B_Q_LORA
_C_KR = _C_CKV + B_KV_LORA
_C_KRS = _C_KR + HP
_C_END = _C_KRS + HP


def _token_prep_kernel(x_ref, g_ref, w1_ref, wqm_ref, wqs_ref, wkm_ref, wv_ref,
                       gqa_ref, gka_ref, gcq_ref, gckv_ref, gqb_ref, gkb_ref, bnd_ref,
                       cos_ref, sin_ref,
                       qaT_ref, ka_ref, vaT_ref, qi_ref, kiT_ref, wi_ref,
                       qbT_ref, kb_ref, vbT_ref):
    x = x_ref[...]
    tm = x.shape[0]
    h = (x * lax.rsqrt(jnp.mean(x * x, axis=-1, keepdims=True) + EPS) * g_ref[...]).astype(BF16)

    def proj(lo, hi):
        return _dot(h, w1_ref[:, lo:hi])

    def head_norm(zh, gain):
        ss = jnp.sum(zh * zh, axis=-1, keepdims=True) * (1.0 / A_HEAD_DIM)
        return zh * lax.rsqrt(ss + EPS) * gain

    ones_rows = jnp.where(lax.broadcasted_iota(I32, (BF16_ROWS, tm), 0) == 0, 1.0, 0.0).astype(BF16)

    def store_vT(ref, v):
        vT = v.T
        for hh in range(A_HEADS):
            ref[VP * hh:VP * hh + B_V, :] = vT[B_V * hh:B_V * (hh + 1), :].astype(BF16)
            ref[VP * hh + B_V:VP * (hh + 1), :] = ones_rows

    spare_a = jnp.where(lax.broadcasted_iota(I32, (1, HP), 1) == A_HEAD_DIM, 1.0, 0.0)
    qa = proj(_C_QA, _C_KA)
    ka = proj(_C_KA, _C_VA)
    gqa = gqa_ref[...] * (A_HEAD_DIM ** -0.5 * LOG2E)
    gka = gka_ref[...]
    for hh in range(A_HEADS):
        blk = slice(HP * hh, HP * (hh + 1))
        qh = head_norm(qa[:, blk], gqa)
        qnorm = jnp.sqrt(jnp.sum(qh * qh, axis=-1, keepdims=True))
        m_a = (qnorm * bnd_ref[0:1, :] + bnd_ref[2:3, hh:hh + 1]) * BOUND_SLACK
        qaT_ref[blk, :] = (qh - m_a * spare_a).T.astype(BF16)
        ka_ref[:, blk] = (head_norm(ka[:, blk], gka) + spare_a).astype(BF16)
    store_vT(vaT_ref, proj(_C_VA, _C_QI))

    qi_ref[...] = proj(_C_QI, _C_KI).astype(BF16)
    kiT = proj(_C_KI, _C_WI).T.astype(BF16)
    for c in range(tm // KU_SEL):
        kiT_ref[c] = kiT[:, KU_SEL * c:KU_SEL * (c + 1)]
    wi_ref[...] = proj(_C_WI, _C_CQ) * ((IDX_HEADS ** -0.5) * (IDX_DIM ** -0.5))

    cos = cos_ref[...]
    sin = sin_ref[...]
    spare = jnp.where(lax.broadcasted_iota(I32, (1, HP), 1) == B_QK, 1.0, 0.0)
    cq = proj(_C_CQ, _C_CKV)
    cqn = (cq * lax.rsqrt(jnp.mean(cq * cq, axis=-1, keepdims=True) + EPS) * gcq_ref[...]).astype(BF16)
    qm = _dot(cqn, wqm_ref[...])
    qs = _dot(cqn, wqs_ref[...])
    gqb = gqb_ref[...] * (B_QK ** -0.5 * LOG2E)
    kbound = bnd_ref[1:2, :]
    for hh in range(B_HEADS):
        blk = slice(HP * hh, HP * (hh + 1))
        qh = qm[:, blk] * cos + qs[:, blk] * sin
        ss = jnp.sum(qh * qh, axis=-1, keepdims=True) * (1.0 / B_QK)
        qh = qh * lax.rsqrt(ss + EPS) * gqb
        m_b = jnp.sqrt(jnp.sum(qh * qh, axis=-1, keepdims=True)) * kbound * BOUND_SLACK
        qbT_ref[blk, :] = (qh - m_b * spare).T.astype(BF16)

    ckv = proj(_C_CKV, _C_KR)
    ckvn = (ckv * lax.rsqrt(jnp.mean(ckv * ckv, axis=-1, keepdims=True) + EPS) * gckv_ref[...]).astype(BF16)
    km = _dot(ckvn, wkm_ref[...])
    store_vT(vbT_ref, _dot(ckvn, wv_ref[...]))
    krot = proj(_C_KR, _C_KRS) * cos + proj(_C_KRS, _C_END) * sin
    gkb = gkb_ref[...]
    for hh in range(B_HEADS):
        blk = slice(HP * hh, HP * (hh + 1))
        kh = km[:, blk] + krot
        ss = jnp.sum(kh * kh, axis=-1, keepdims=True) * (1.0 / B_QK)
        kb_ref[:, blk] = (kh * lax.rsqrt(ss + EPS) * gkb + spare).astype(BF16)


def _token_prep(x, g, w1, wqm, wqs, wkm, wv, gqa, gka, gcq, gckv, gqb, gkb, bnd,
                cos_t, sin_t):
    B, S, D = x.shape
    tm = TM_PREP
    nt = S // tm
    row3 = lambda w: pl.BlockSpec((None, tm, w), lambda b, i: (b, i, 0))
    colT = lambda r: pl.BlockSpec((None, r, tm), lambda b, i: (b, 0, i))
    full = lambda a: pl.BlockSpec(a.shape, lambda b, i: (0,) * a.ndim)
    tab = pl.BlockSpec((tm, LANES), lambda b, i: (i, 0))
    consts = (g, w1, wqm, wqs, wkm, wv, gqa, gka, gcq, gckv, gqb, gkb, bnd)
    out_shape = (
        jax.ShapeDtypeStruct((B, A_HEADS * HP, S), BF16),
        jax.ShapeDtypeStruct((B, S, A_HEADS * HP), BF16),
        jax.ShapeDtypeStruct((B, A_HEADS * VP, S), BF16),
        jax.ShapeDtypeStruct((B, S, IDX_HEADS * HP), BF16),
        jax.ShapeDtypeStruct((B, S // KU_SEL, LANES, KU_SEL), BF16),
        jax.ShapeDtypeStruct((B, S, LANES), F32),
        jax.ShapeDtypeStruct((B, B_HEADS * HP, S), BF16),
        jax.ShapeDtypeStruct((B, S, B_HEADS * HP), BF16),
        jax.ShapeDtypeStruct((B, B_HEADS * VP, S), BF16),
    )
    kiT_spec = pl.BlockSpec((None, tm // KU_SEL, LANES, KU_SEL), lambda b, i: (b, i, 0, 0))
    out_specs = (colT(A_HEADS * HP), row3(A_HEADS * HP), colT(A_HEADS * VP), row3(IDX_HEADS * HP),
                 kiT_spec, row3(LANES), colT(B_HEADS * HP), row3(B_HEADS * HP), colT(B_HEADS * VP))
    return pl.pallas_call(
        _token_prep_kernel,
        grid=(B, nt),
        in_specs=[row3(D)] + [full(a) for a in consts] + [tab, tab],
        out_specs=out_specs,
        out_shape=out_shape,
        compiler_params=_cparams(("parallel", "parallel")),
        name="token_prep",
    )(x, *consts, cos_t, sin_t)


def _bit_transpose32(words):
    a = list(words)
    j, m = 16, 0x0000FFFF
    while j:
        k = 0
        while k < 32:
            t = (a[k] ^ lax.shift_right_logical(a[k + j], jnp.int32(j))) & m
            a[k] = a[k] ^ t
            a[k + j] = a[k + j] ^ (t << j)
            k = (k + j + 1) & ~j
        j >>= 1
        m = m ^ (m << j)
    return a


def _expand_bits(word):
    return jnp.concatenate(
        [lax.shift_right_logical(word, jnp.int32(SEL_BITS - 1 - j)) & 1 for j in range(SEL_BITS)], axis=0)


def _select_kernel(qi_ref, wi_ref, kiT_ref, sel_ref, planes_ref, alive_ref, gt_ref,
                   acca_ref, accb_ref, wb_ref, *, topk):
    tq = TQ_SEL
    i = pl.program_id(1)
    n_keys = (i + 1) * tq
    n_groups = (i + GU_SEL) // GU_SEL

    for hh in range(IDX_HEADS):
        wb_ref[hh] = jnp.broadcast_to(wi_ref[:, hh:hh + 1], (tq, LANES))

    zeros_group = jnp.zeros((GU_SEL, SUBLANES, tq), I32)

    @pl.when(i == 0)
    def _():
        planes_ref[...] = jnp.zeros(planes_ref.shape, I32)

    n_units = n_keys // KU_SEL
    accb_ref[...] = jnp.zeros(accb_ref.shape, F32)

    def heads(u, dst_ref):
        kT = kiT_ref[jnp.minimum(u, n_units - 1)]
        for hh in range(IDX_HEADS):
            d = _dot(qi_ref[:, HP * hh:HP * (hh + 1)], kT)
            for c in range(tq // CH_SEL):
                rows = slice(CH_SEL * c, CH_SEL * (c + 1))
                w = wb_ref[hh, rows, :]
                for lanes in (slice(0, LANES), slice(LANES, 2 * LANES)):
                    t = jnp.maximum(d[rows, lanes], 0.0) * w
                    if hh == 0:
                        dst_ref[rows, lanes] = t
                    else:
                        dst_ref[rows, lanes] += t

    def convert(src_ref, u):
        accT = src_ref[...].T
        words = []
        for j in range(SEL_BITS):
            acc = accT[SUBLANES * j:SUBLANES * (j + 1), :] + 0.0
            bits = pltpu.bitcast(acc, I32)
            words.append(bits ^ ((bits >> 31) | SIGN_BIT))
        unit = jnp.maximum(u, 0)
        for r, plane in enumerate(_bit_transpose32(words)):
            planes_ref[r, unit] = plane

    def unit_pair(t, carry):
        heads(2 * t, acca_ref)
        convert(accb_ref, 2 * t - 1)
        heads(2 * t + 1, accb_ref)
        convert(acca_ref, 2 * t)
        return carry

    lax.fori_loop(0, n_units // 2 + 1, unit_pair, 0)

    full_group = jnp.full((GU_SEL, SUBLANES, tq), -1, I32)

    def init(g, carry):
        unit = g * GU_SEL + lax.broadcasted_iota(I32, (GU_SEL, SUBLANES, tq), 0)
        alive_ref[pl.ds(g * GU_SEL, GU_SEL)] = jnp.where(unit < i, full_group, zeros_group)
        gt_ref[pl.ds(g * GU_SEL, GU_SEL)] = zeros_group
        return carry
    lax.fori_loop(0, n_groups, init, 0)

    q_chunk = (i * tq + lax.broadcasted_iota(I32, (SUBLANES, tq), 1)) >> CHUNK_SHIFT
    k_row = i * tq + lax.broadcasted_iota(I32, (SUBLANES, tq), 0)
    adm_bits = jnp.zeros((SUBLANES, tq), I32)
    for j in range(SEL_BITS):
        adm = ((k_row + SUBLANES * j) >> CHUNK_SHIFT) <= q_chunk
        adm_bits = adm_bits | jnp.where(adm, jnp.int32(1) << (SEL_BITS - 1 - j), 0)
    alive_ref[i] = adm_bits

    def popcount_rows(x):
        return jnp.sum(lax.population_count(x), axis=0)

    def count_alive(plane_of):
        def body(g, acc):
            return acc + popcount_rows(plane_of(g) & alive_ref[pl.ds(g * GU_SEL, GU_SEL)])
        acc = lax.fori_loop(0, n_groups, body, jnp.zeros((SUBLANES, tq), I32))
        return jnp.sum(acc, axis=0, keepdims=True)

    def bit_pass(r, need):
        ones_here = count_alive(lambda g: planes_ref[r, pl.ds(g * GU_SEL, GU_SEL)])
        take = ones_here >= need

        def update(g, carry):
            grp = pl.ds(g * GU_SEL, GU_SEL)
            alive = alive_ref[grp]
            with_bit = alive & planes_ref[r, grp]
            alive_ref[grp] = jnp.where(take, with_bit, alive ^ with_bit)
            gt_ref[grp] = jnp.where(take, gt_ref[grp], gt_ref[grp] | with_bit)
            return carry
        lax.fori_loop(0, n_groups, update, 0)
        return jnp.where(take, need, need - ones_here)

    need = lax.fori_loop(0, SEL_BITS, bit_pass, jnp.full((1, tq), topk, I32))
    n_equal = count_alive(lambda g: full_group)
    any_tie = jnp.max(jnp.where(n_equal > need, 1, 0)) > 0

    @pl.when(jnp.logical_not(any_tie))
    def _():
        def body(g, carry):
            grp = pl.ds(g * GU_SEL, GU_SEL)
            sel_ref[grp] = gt_ref[grp] | alive_ref[grp]
            return carry
        lax.fori_loop(0, n_groups, body, 0)

    @pl.when(any_tie)
    def _():
        n_take = need.astype(F32)
        ltri = (lax.broadcasted_iota(I32, (KU_SEL, KU_SEL), 0)
                >= lax.broadcasted_iota(I32, (KU_SEL, KU_SEL), 1)).astype(BF16)

        def body(u, seen):
            eqf = _expand_bits(alive_ref[u]).astype(F32)
            rank = _dot(ltri, eqf.astype(BF16)) + seen
            first = jnp.logical_and(eqf != 0.0, rank <= n_take)
            words = gt_ref[u]
            for j in range(SEL_BITS):
                words = words | jnp.where(first[SUBLANES * j:SUBLANES * (j + 1), :],
                                          jnp.int32(1) << (SEL_BITS - 1 - j), 0)
            sel_ref[u] = words
            return seen + jnp.sum(eqf, axis=0, keepdims=True)
        lax.fori_loop(0, n_units, body, jnp.zeros((1, tq), F32))

        def clear(u, carry):
            sel_ref[u] = jnp.zeros((SUBLANES, tq), I32)
            return carry
        lax.fori_loop(n_units, n_groups * GU_SEL, clear, 0)

    def fill(g, carry):
        sel_ref[pl.ds(g * GU_SEL, GU_SEL)] = zeros_group
        return carry
    lax.fori_loop(n_groups, sel_ref.shape[0] // GU_SEL, fill, 0)


def _select_topk(qi, wi, kiT, topk):
    B, S, _ = qi.shape
    tq = TQ_SEL
    n_units, dims, unit = kiT.shape[1:]
    assert unit == KU_SEL == tq == SEL_BITS * SUBLANES and n_units * unit == S
    n_alloc = -(-n_units // GU_SEL) * GU_SEL
    return pl.pallas_call(
        functools.partial(_select_kernel, topk=topk),
        grid=(B, S // tq),
        in_specs=[
            pl.BlockSpec((None, tq, IDX_HEADS * HP), lambda b, i: (b, i, 0)),
            pl.BlockSpec((None, tq, LANES), lambda b, i: (b, i, 0)),
            pl.BlockSpec((None, n_units, dims, unit), lambda b, i: (b, 0, 0, 0)),
        ],
        out_specs=pl.BlockSpec((None, n_alloc, SUBLANES, tq), lambda b, i: (b, 0, 0, i)),
        out_shape=jax.ShapeDtypeStruct((B, n_alloc, SUBLANES, S), I32),
        scratch_shapes=[pltpu.VMEM((SEL_BITS, n_alloc + 1, SUBLANES, tq), I32),
                        pltpu.VMEM((n_alloc, SUBLANES, tq), I32),
                        pltpu.VMEM((n_alloc, SUBLANES, tq), I32),
                        pltpu.VMEM((tq, KU_SEL), F32), pltpu.VMEM((tq, KU_SEL), F32),
                        pltpu.VMEM((IDX_HEADS, tq, LANES), F32)],
        compiler_params=_cparams(("parallel", "arbitrary")),
        name="select_topk",
    )(qi, wi, kiT)


def _attend_kernel(ti_ref, tj_ref, safe_ref, *refs, heads, with_mask):
    if with_mask:
        qT_ref, k_ref, vT_ref, selbits_ref, bias_ref, o_ref, acc_ref, s_ref, p_ref, m_ref, add_ref = refs
    else:
        qT_ref, k_ref, vT_ref, diag_ref, o_ref, acc_ref, s_ref, p_ref, m_ref = refs
    s_idx = pl.program_id(1)
    i = ti_ref[s_idx]
    jj = tj_ref[s_idx]
    bounded = safe_ref[0] == 1
    tk = TK_ATT
    n_ch = tk // CH_ATT

    @pl.when(jj == 0)
    def _():
        acc_ref[...] = jnp.zeros(acc_ref.shape, F32)
        m_ref[...] = jnp.full(m_ref.shape, NEG, F32)

    def chunks():
        return [slice(CH_ATT * c, CH_ATT * (c + 1)) for c in range(n_ch)]

    def key_block(sb, carry):
        j = jj * NSUB_ATT + sb
        k0 = pl.multiple_of(sb * tk, tk)

        def logits(hh):
            s_ref[hh % 2] = _dot(k_ref[pl.ds(k0, tk), HP * hh:HP * (hh + 1)],
                                 qT_ref[HP * hh:HP * (hh + 1), :])

        def pv(hh, buf):
            return _dot(vT_ref[VP * hh:VP * (hh + 1), pl.ds(k0, tk)], p_ref[buf])

        def sweep_bounded(extra):
            logits(0)
            for hh in range(heads):
                buf = hh % 2
                if hh + 1 < heads:
                    logits(hh + 1)
                for rows in chunks():
                    t = s_ref[buf, rows, :]
                    e = extra(hh, rows)
                    if e is not None:
                        t = t + e
                    p_ref[buf, rows, :] = jnp.exp2(t).astype(BF16)
                acc_ref[VP * hh:VP * (hh + 1), :] += pv(hh, buf)

        def sweep_running_max(extra):
            for hh in range(heads):
                buf = hh % 2
                logits(hh)
                m_blk = jnp.full((SUBLANES, s_ref.shape[2]), NEG, F32)
                for rows in chunks():
                    t = s_ref[buf, rows, :]
                    e = extra(hh, rows)
                    if e is not None:
                        t = t + e
                        s_ref[buf, rows, :] = t
                    for r in range(CH_ATT // SUBLANES):
                        m_blk = jnp.maximum(m_blk, t[SUBLANES * r:SUBLANES * (r + 1), :])
                m_old = m_ref[hh:hh + 1, :]
                m_new = jnp.maximum(m_old, jnp.max(m_blk, axis=0, keepdims=True))
                m_ref[hh:hh + 1, :] = m_new
                for rows in chunks():
                    p_ref[buf, rows, :] = jnp.exp2(s_ref[buf, rows, :] - m_new).astype(BF16)
                acc_ref[VP * hh:VP * (hh + 1), :] = (
                    jnp.exp2(m_old - m_new) * acc_ref[VP * hh:VP * (hh + 1), :] + pv(hh, buf))

        def both(extra):
            @pl.when(bounded)
            def _():
                sweep_bounded(extra)

            @pl.when(jnp.logical_not(bounded))
            def _():
                sweep_running_max(extra)

        if with_mask:
            @pl.when(j <= i)
            def _():
                for c in range(tk // KU_SEL):
                    bits = _expand_bits(selbits_ref[sb * (tk // KU_SEL) + c])
                    add_ref[KU_SEL * c:KU_SEL * (c + 1), :] = jnp.where(bits != 0, 0.0, NEG)

            @pl.when(j < i - 1)
            def _():
                both(lambda hh, rows: add_ref[rows, :])

            @pl.when(jnp.logical_and(j >= i - 1, j <= i))
            def _():
                near = i - j
                both(lambda hh, rows: add_ref[rows, :] + bias_ref[hh, near, rows, :].astype(F32))
        else:
            @pl.when(j < i)
            def _():
                both(lambda hh, rows: None)

            @pl.when(j == i)
            def _():
                both(lambda hh, rows: diag_ref[rows, :])
        return carry

    lax.fori_loop(0, NSUB_ATT, key_block, 0)

    @pl.when(jj == i // NSUB_ATT)
    def _():
        dv = VP - BF16_ROWS
        for hh in range(heads):
            inv = 1.0 / acc_ref[VP * hh + dv:VP * hh + dv + 1, :]
            s_ref[0, dv * hh:dv * (hh + 1), :] = acc_ref[VP * hh:VP * hh + dv, :] * inv
        o_ref[...] = s_ref[0, 0:dv * heads, :].T.astype(o_ref.dtype)


def _pair_tables(S):
    n = S // TQ_ATT
    ti = np.array([i for i in range(n) for _ in range(i // NSUB_ATT + 1)], np.int32)
    tj = np.array([j for i in range(n) for j in range(i // NSUB_ATT + 1)], np.int32)
    return jnp.asarray(ti), jnp.asarray(tj)


def _attend(qT, k, vT, safe, extra_inputs, extra_specs, extra_scratch, with_mask, name):
    B, S, _ = k.shape
    tq, tk = TQ_ATT, TK_ATT
    tg = tk * NSUB_ATT
    assert tq == tk and tk >= B_V * A_HEADS and S % tg == 0
    ti, tj = _pair_tables(S)
    heads = A_HEADS
    kern = functools.partial(_attend_kernel, heads=heads, with_mask=with_mask)
    grid_spec = pltpu.PrefetchScalarGridSpec(
        num_scalar_prefetch=3,
        grid=(B, ti.shape[0]),
        in_specs=[
            pl.BlockSpec((None, heads * HP, tq), lambda b, s, ti, tj, sf: (b, 0, ti[s])),
            pl.BlockSpec((None, tg, heads * HP), lambda b, s, ti, tj, sf: (b, tj[s], 0)),
            pl.BlockSpec((None, heads * VP, tg), lambda b, s, ti, tj, sf: (b, 0, tj[s])),
        ] + extra_specs,
        out_specs=pl.BlockSpec((None, tq, heads * B_V), lambda b, s, ti, tj, sf: (b, ti[s], 0)),
        scratch_shapes=[pltpu.VMEM((heads * VP, tq), F32),
                        pltpu.VMEM((2, tk, tq), F32),
                        pltpu.VMEM((2, tk, tq), BF16),
                        pltpu.VMEM((heads, tq), F32)] + extra_scratch,
    )
    return pl.pallas_call(
        kern, grid_spec=grid_spec,
        out_shape=jax.ShapeDtypeStruct((B, S, heads * B_V), BF16),
        compiler_params=_cparams(("parallel", "arbitrary")),
        name=name,
    )(ti, tj, safe, qT, k, vT, *extra_inputs)


def _attend_a(qaT, ka, vaT, selbits, bias_tiles, safe):
    tq, tk = TQ_ATT, TK_ATT
    specs = [
        pl.BlockSpec((None, tk * NSUB_ATT // KU_SEL, SUBLANES, tq),
                     lambda b, s, ti, tj, sf: (b, tj[s], 0, ti[s])),
        pl.BlockSpec(bias_tiles.shape, lambda b, s, ti, tj, sf: (0, 0, 0, 0)),
    ]
    return _attend(qaT, ka, vaT, safe, (selbits, bias_tiles), specs, [pltpu.VMEM((tk, tq), F32)],
                   True, "attend_a")


def _attend_b(qbT, kb, vbT, diag_tile, safe):
    tq, tk = TQ_ATT, TK_ATT
    specs = [pl.BlockSpec((tk, tq), lambda b, s, ti, tj, sf: (0, 0))]
    return _attend(qbT, kb, vbT, safe, (diag_tile,), specs, [], False, "attend_b")


def _merge_kernel(x_ref, ya_ref, yb_ref, g_ref, wga_ref, wgb_ref, bga_ref, bgb_ref,
                  wpa_ref, wpb_ref, wo_ref, o_ref):
    x = x_ref[...]
    h = (x * lax.rsqrt(jnp.mean(x * x, axis=-1, keepdims=True) + EPS) * g_ref[...]).astype(BF16)
    gate_a = jax.nn.sigmoid(_dot(h, wga_ref[...]) + bga_ref[...])
    gate_b = jax.nn.sigmoid(_dot(h, wgb_ref[...]) + bgb_ref[...])
    merged = gate_a * _dot(ya_ref[...], wpa_ref[...]) + gate_b * _dot(yb_ref[...], wpb_ref[...])
    o_ref[...] = x + _dot(merged.astype(BF16), wo_ref[...])


def _merge_out(x2, ya2, yb2, g, wga, wgb, bga, bgb, wpa, wpb, wo):
    R, D = x2.shape
    tm = TM_MERGE
    row = lambda w: pl.BlockSpec((tm, w), lambda i: (i, 0))
    full = lambda a: pl.BlockSpec(a.shape, lambda i: (0,) * a.ndim)
    consts = (g, wga, wgb, bga, bgb, wpa, wpb, wo)
    return pl.pallas_call(
        _merge_kernel,
        grid=(R // tm,),
        in_specs=[row(D), row(A_WIDTH), row(B_WIDTH)] + [full(a) for a in consts],
        out_specs=row(D),
        out_shape=jax.ShapeDtypeStruct((R, D), F32),
        compiler_params=_cparams(("parallel",)),
        name="merge_out",
    )(x2, ya2, yb2, *consts)


def _ffn_kernel(x_ref, xp_ref, g_ref, wuv_ref, wug_ref, cwv_ref, cwg_ref, cbv_ref, cbg_ref, wd_ref,
                o_ref, uv_ref, ug_ref, acc_ref, *, tiles_per_seq):
    i = pl.program_id(0)
    f = pl.program_id(1)
    tm = x_ref.shape[0]
    halo = SUBLANES
    g = g_ref[...]

    def normed(v):
        return (v * lax.rsqrt(jnp.mean(v * v, axis=-1, keepdims=True) + EPS) * g).astype(BF16)

    h = normed(x_ref[...])
    keep = jnp.where(i % tiles_per_seq == 0, 0.0, 1.0)
    hp = normed(xp_ref[...])

    def conv(u_ref, w_ref, cw_ref, cb_ref):
        u_ref[0:halo, :] = _dot(hp, w_ref[...]) * keep
        u_ref[halo:halo + tm, :] = _dot(h, w_ref[...])
        out = cb_ref[...]
        for t in range(CONV_W):
            lo = halo - (CONV_W - 1) + t
            out = out + cw_ref[t:t + 1, :] * u_ref[lo:lo + tm, :]
        return out

    val = conv(uv_ref, wuv_ref, cwv_ref, cbv_ref)
    gat = conv(ug_ref, wug_ref, cwg_ref, cbg_ref)
    act = (gat * jax.nn.sigmoid(gat) * val).astype(BF16)
    part = _dot(act, wd_ref[...])

    @pl.when(f == 0)
    def _():
        acc_ref[...] = part

    @pl.when(f > 0)
    def _():
        acc_ref[...] = acc_ref[...] + part

    @pl.when(f == pl.num_programs(1) - 1)
    def _():
        o_ref[...] = x_ref[...] + acc_ref[...]


def _conv_ffn(x2, S, g, wu, cw, cb, wd):
    R, D = x2.shape
    tm, tf = TM_FFN, TF_FFN
    nf = D_FF // tf
    halo_blocks = tm // SUBLANES
    kern = functools.partial(_ffn_kernel, tiles_per_seq=S // tm)
    return pl.pallas_call(
        kern,
        grid=(R // tm, nf),
        in_specs=[
            pl.BlockSpec((tm, D), lambda i, f: (i, 0)),
            pl.BlockSpec((SUBLANES, D), lambda i, f: (jnp.maximum(i * halo_blocks - 1, 0), 0)),
            pl.BlockSpec((1, D), lambda i, f: (0, 0)),
            pl.BlockSpec((D, tf), lambda i, f: (0, f)),
            pl.BlockSpec((D, tf), lambda i, f: (0, nf + f)),
            pl.BlockSpec((CONV_W, tf), lambda i, f: (0, f)),
            pl.BlockSpec((CONV_W, tf), lambda i, f: (0, nf + f)),
            pl.BlockSpec((1, tf), lambda i, f: (0, f)),
            pl.BlockSpec((1, tf), lambda i, f: (0, nf + f)),
            pl.BlockSpec((tf, D), lambda i, f: (f, 0)),
        ],
        out_specs=pl.BlockSpec((tm, D), lambda i, f: (i, 0)),
        out_shape=jax.ShapeDtypeStruct((R, D), F32),
        scratch_shapes=[pltpu.VMEM((tm + SUBLANES, tf), F32), pltpu.VMEM((tm + SUBLANES, tf), F32),
                        pltpu.VMEM((tm, D), F32)],
        compiler_params=_cparams(("parallel", "arbitrary")),
        name="conv_ffn",
    )(x2, x2, g, wu, wu, cw, cw, cb, cb, wd)


def _t5_bucket(rel):
    nb = REL_BUCKETS // 2
    max_exact = nb // 2
    side = jnp.where(rel > 0, nb, 0)
    n = jnp.abs(rel)
    nf = jnp.maximum(n, 1).astype(F32)
    large = max_exact + (jnp.log(nf / max_exact) / math.log(REL_MAX_DIST / max_exact)
                         * (nb - max_exact)).astype(I32)
    large = jnp.minimum(large, nb - 1)
    return side + jnp.where(n < max_exact, n, large)


def _bias_tiles(rel_bias):
    tq, tk = TQ_ATT, TK_ATT
    assert tq == tk and tk >= REL_MAX_DIST
    kk = jnp.arange(tk, dtype=I32)[:, None]
    qq = jnp.arange(tq, dtype=I32)[None, :]
    bucket = _t5_bucket(jnp.stack([kk - qq, kk - qq - tk])).reshape(1, 2 * tk * tq)
    onehot = (jnp.arange(REL_BUCKETS, dtype=I32)[:, None] == bucket).astype(F32)
    far = rel_bias[_t5_bucket(jnp.asarray(-REL_MAX_DIST, I32))]
    table = ((rel_bias - far[None, :]) * LOG2E).T
    tiles = jnp.dot(table, onehot, precision=lax.Precision.HIGHEST)
    return tiles.reshape(A_HEADS, 2, tk, tq).astype(BF16), jnp.max(jnp.abs(table), axis=1)


def _diag_tile():
    kk = np.arange(TK_ATT)[:, None] // CHUNK
    qq = np.arange(TQ_ATT)[None, :] // CHUNK
    return jnp.asarray(np.where(kk <= qq, 0.0, NEG).astype(np.float32))


def _rope_tables(S):
    half = B_ROPE // 2
    inv = ROPE_BASE ** (-jnp.arange(half, dtype=F32) / half)
    ang = jnp.arange(S, dtype=I32).astype(F32)[:, None] * inv[None, :]
    cos, sin = jnp.cos(ang), jnp.sin(ang)
    ones = jnp.ones((S, B_NOPE), F32)
    zeros_n = jnp.zeros((S, B_NOPE), F32)
    zeros_p = jnp.zeros((S, HP - B_QK), F32)
    cos_t = jnp.concatenate([ones, cos, cos, zeros_p], axis=1)
    sin_t = jnp.concatenate([zeros_n, -sin, sin, zeros_p], axis=1)
    return cos_t, sin_t


def _pad_cols(w, width):
    return jnp.pad(w, ((0, 0), (0, width - w.shape[1])))


def _pad_heads(w, heads, width):
    rows = w.shape[0]
    w = w.reshape(rows, heads, -1)
    return jnp.pad(w, ((0, 0), (0, 0), (0, width - w.shape[2]))).reshape(rows, heads * width)


def _swap_halves(w):
    half = w.shape[-1] // 2
    return jnp.concatenate([w[..., half:], w[..., :half]], axis=-1)


def _layer_weights(l, w_in, b_w_uq, b_w_ukv, b_q_norm, b_k_norm, a_q_norm, a_k_norm, bias_max):
    w = w_in[l]
    o = np.cumsum([0, A_WIDTH, A_WIDTH, A_WIDTH, IDX_HEADS * IDX_DIM, IDX_DIM, IDX_HEADS,
                   B_Q_LORA, B_KV_LORA, B_ROPE, D_MODEL, D_MODEL])
    seg = [w[:, o[t]:o[t + 1]] for t in range(11)]
    zn = jnp.zeros((D_MODEL, B_NOPE), F32)
    kr = seg[8]
    w1 = jnp.concatenate([
        _pad_heads(seg[0], A_HEADS, HP), _pad_heads(seg[1], A_HEADS, HP), seg[2],
        _pad_heads(seg[3], IDX_HEADS, HP),
        _pad_cols(seg[4], LANES), _pad_cols(seg[5], LANES), seg[6], seg[7],
        _pad_cols(jnp.concatenate([zn, kr], axis=1), HP),
        _pad_cols(jnp.concatenate([zn, _swap_halves(kr)], axis=1), HP),
    ], axis=1).astype(BF16)
    assert w1.shape[1] == _C_END

    uq = b_w_uq[l].reshape(B_Q_LORA, B_HEADS, B_QK)
    zq = jnp.zeros((B_Q_LORA, B_HEADS, B_NOPE), F32)
    pq = jnp.zeros((B_Q_LORA, B_HEADS, HP - B_QK), F32)
    wqm = jnp.concatenate([uq, pq], axis=-1).reshape(B_Q_LORA, B_HEADS * HP).astype(BF16)
    wqs = jnp.concatenate([zq, _swap_halves(uq[..., B_NOPE:]), pq], axis=-1)
    wqs = wqs.reshape(B_Q_LORA, B_HEADS * HP).astype(BF16)

    ukv = b_w_ukv[l].reshape(B_KV_LORA, B_HEADS, B_NOPE + B_V)
    pk = jnp.zeros((B_KV_LORA, B_HEADS, HP - B_NOPE), F32)
    wkm = jnp.concatenate([ukv[..., :B_NOPE], pk], axis=-1).reshape(B_KV_LORA, B_HEADS * HP).astype(BF16)
    wv = ukv[..., B_NOPE:].reshape(B_KV_LORA, B_WIDTH).astype(BF16)

    gqa = _pad_cols(a_q_norm[l][None, :], HP)
    gka = _pad_cols(a_k_norm[l][None, :], HP)
    gqb = _pad_cols(b_q_norm[l][None, :], HP)
    gkb = _pad_cols(b_k_norm[l][None, :], HP)

    ka_bound = math.sqrt(A_HEAD_DIM) * jnp.max(jnp.abs(a_k_norm[l]))
    kb_bound = math.sqrt(B_QK) * jnp.max(jnp.abs(b_k_norm[l]))
    bnd = jnp.zeros((SUBLANES, LANES), F32)
    bnd = bnd.at[0, :].set(ka_bound).at[1, :].set(kb_bound).at[2, :A_HEADS].set(bias_max)
    qa_bound = math.sqrt(A_HEAD_DIM) * jnp.max(jnp.abs(a_q_norm[l])) * (A_HEAD_DIM ** -0.5 * LOG2E)
    qb_bound = math.sqrt(B_QK) * jnp.max(jnp.abs(b_q_norm[l])) * (B_QK ** -0.5 * LOG2E)
    range_a = (qa_bound * ka_bound + jnp.max(bias_max)) * BOUND_SLACK + jnp.max(bias_max)
    range_b = qb_bound * kb_bound * BOUND_SLACK
    safe_a = (range_a <= MAX_LOG2_RANGE).astype(I32).reshape(1)
    safe_b = (range_b <= MAX_LOG2_RANGE).astype(I32).reshape(1)
    return (w1, wqm, wqs, wkm, wv, gqa, gka, gqb, gkb, bnd, safe_a, safe_b,
            seg[9].astype(BF16), seg[10].astype(BF16))


def kernel(x, rel_bias, norm_mix, w_in, a_q_norm, a_k_norm, b_cq_norm, b_ckv_norm, b_w_uq, b_w_ukv,
           b_q_norm, b_k_norm, w_proj_a, w_proj_b, b_gate, w_out, norm_ffn, w_up, conv_w, conv_b, w_down):
    B, S, D = x.shape
    assert D == D_MODEL and S % TQ_ATT == 0 and S % TM_FFN == 0
    topk = min(TOPK_MAX, S // 4)
    assert TQ_SEL >= topk
    depth = w_in.shape[0]

    cos_t, sin_t = _rope_tables(S)
    bias_tiles, bias_max = _bias_tiles(rel_bias)
    diag_tile = _diag_tile()

    for l in range(depth):
        (w1, wqm, wqs, wkm, wv, gqa, gka, gqb, gkb, bnd, safe_a, safe_b, wga, wgb) = _layer_weights(
            l, w_in, b_w_uq, b_w_ukv, b_q_norm, b_k_norm, a_q_norm, a_k_norm, bias_max)
        qaT, ka, vaT, qi, kiT, wi, qbT, kb, vbT = _token_prep(
            x, norm_mix[l][None, :], w1, wqm, wqs, wkm, wv, gqa, gka,
            b_cq_norm[l][None, :], b_ckv_norm[l][None, :], gqb, gkb, bnd,
            cos_t, sin_t)
        selbits = _select_topk(qi, wi, kiT, topk)
        y_a = _attend_a(qaT, ka, vaT, selbits, bias_tiles, safe_a)
        y_b = _attend_b(qbT, kb, vbT, diag_tile, safe_b)
        x2 = _merge_out(
            x.reshape(B * S, D), y_a.reshape(B * S, A_WIDTH), y_b.reshape(B * S, B_WIDTH),
            norm_mix[l][None, :], wga, wgb, b_gate[l][None, :D_MODEL], b_gate[l][None, D_MODEL:],
            w_proj_a[l].astype(BF16), w_proj_b[l].astype(BF16), w_out[l].astype(BF16))
        x2 = _conv_ffn(x2, S, norm_ffn[l][None, :], w_up[l].astype(BF16), conv_w[l], conv_b[l][None, :],
                       w_down[l].astype(BF16))
        x = x2.reshape(B, S, D)
    return x
```

```python
import functools
import math

import numpy as np
import jax
import jax.numpy as jnp
from jax import lax
from jax.experimental import pallas as pl
from jax.experimental.pallas import tpu as pltpu

F32 = jnp.float32
BF16 = jnp.bfloat16
I32 = jnp.int32

D_MODEL = 1024
CHUNK = 64
CHUNK_SHIFT = 6
assert 1 << CHUNK_SHIFT == CHUNK
A_HEADS = 8
A_HEAD_DIM = 64
A_WIDTH = A_HEADS * A_HEAD_DIM
IDX_HEADS = 8
IDX_DIM = 64
TOPK_MAX = 256
B_HEADS = 8
B_Q_LORA = 384
B_KV_LORA = 256
B_NOPE = 64
B_ROPE = 32
B_QK = B_NOPE + B_ROPE
B_V = 64
B_WIDTH = B_HEADS * B_V
ROPE_BASE = 10000.0
REL_BUCKETS = 32
REL_MAX_DIST = 128
D_FF = 2816
CONV_W = 3
EPS = 1e-6

LANES = 128
SUBLANES = 8
BF16_ROWS = 16
VMEM_LIMIT = 56 * 1024 * 1024

TM_PREP = 512
TQ_SEL = 256
KU_SEL = 256
CH_SEL = 64
SEL_BITS = 32
GU_SEL = 16
TQ_ATT = 512
TK_ATT = 512
NSUB_ATT = 4
CH_ATT = 64
TM_MERGE = 512
TM_FFN = 512
TF_FFN = 1408

HP = 128
VP = B_V + BF16_ROWS
NEG = -1e30
SIGN_BIT = -2147483648
LOG2E = 1.4426950408889634
BOUND_SLACK = 1.01
MAX_LOG2_RANGE = 60.0


def _cparams(sem):
    return pltpu.CompilerParams(dimension_semantics=sem, vmem_limit_bytes=VMEM_LIMIT)


def _dot(a, b):
    return jnp.dot(a, b, preferred_element_type=F32)


_C_QA = 0
_C_KA = _C_QA + A_HEADS * HP
_C_VA = _C_KA + A_HEADS * HP
_C_QI = _C_VA + A_WIDTH
_C_KI = _C_QI + IDX_HEADS * HP
_C_WI = _C_KI + LANES
_C_CQ = _C_WI + LANES
_C_CKV = _C_CQ + B_Q_LORA
_C_KR = _C_CKV + B_KV_LORA
_C_KRS = _C_KR + HP
_C_END = _C_KRS + HP


def _token_prep_kernel(x_ref, g_ref, w1_ref, wqm_ref, wqs_ref, wkm_ref, wv_ref,
                       gqa_ref, gka_ref, gcq_ref, gckv_ref, gqb_ref, gkb_ref, bnd_ref,
                       cos_ref, sin_ref,
                       qaT_ref, ka_ref, vaT_ref, qi_ref, kiT_ref, wi_ref,
                       qbT_ref, kb_ref, vbT_ref):
    x = x_ref[...]
    tm = x.shape[0]
    h = (x * lax.rsqrt(jnp.mean(x * x, axis=-1, keepdims=True) + EPS) * g_ref[...]).astype(BF16)

    def proj(lo, hi):
        return _dot(h, w1_ref[:, lo:hi])

    def head_norm(zh, gain):
        ss = jnp.sum(zh * zh, axis=-1, keepdims=True) * (1.0 / A_HEAD_DIM)
        return zh * lax.rsqrt(ss + EPS) * gain

    ones_rows = jnp.where(lax.broadcasted_iota(I32, (BF16_ROWS, tm), 0) == 0, 1.0, 0.0).astype(BF16)

    def store_vT(ref, v):
        vT = v.T
        for hh in range(A_HEADS):
            ref[VP * hh:VP * hh + B_V, :] = vT[B_V * hh:B_V * (hh + 1), :].astype(BF16)
            ref[VP * hh + B_V:VP * (hh + 1), :] = ones_rows

    spare_a = jnp.where(lax.broadcasted_iota(I32, (1, HP), 1) == A_HEAD_DIM, 1.0, 0.0)
    qa = proj(_C_QA, _C_KA)
    ka = proj(_C_KA, _C_VA)
    gqa = gqa_ref[...] * (A_HEAD_DIM ** -0.5 * LOG2E)
    gka = gka_ref[...]
    for hh in range(A_HEADS):
        blk = slice(HP * hh, HP * (hh + 1))
        qh = head_norm(qa[:, blk], gqa)
        qnorm = jnp.sqrt(jnp.sum(qh * qh, axis=-1, keepdims=True))
        m_a = (qnorm * bnd_ref[0:1, :] + bnd_ref[2:3, hh:hh + 1]) * BOUND_SLACK
        qaT_ref[blk, :] = (qh - m_a * spare_a).T.astype(BF16)
        ka_ref[:, blk] = (head_norm(ka[:, blk], gka) + spare_a).astype(BF16)
    store_vT(vaT_ref, proj(_C_VA, _C_QI))

    qi_ref[...] = proj(_C_QI, _C_KI).astype(BF16)
    kiT = proj(_C_KI, _C_WI).T.astype(BF16)
    for c in range(tm // KU_SEL):
        kiT_ref[c] = kiT[:, KU_SEL * c:KU_SEL * (c + 1)]
    wi_ref[...] = proj(_C_WI, _C_CQ) * ((IDX_HEADS ** -0.5) * (IDX_DIM ** -0.5))

    cos = cos_ref[...]
    sin = sin_ref[...]
    spare = jnp.where(lax.broadcasted_iota(I32, (1, HP), 1) == B_QK, 1.0, 0.0)
    cq = proj(_C_CQ, _C_CKV)
    cqn = (cq * lax.rsqrt(jnp.mean(cq * cq, axis=-1, keepdims=True) + EPS) * gcq_ref[...]).astype(BF16)
    qm = _dot(cqn, wqm_ref[...])
    qs = _dot(cqn, wqs_ref[...])
    gqb = gqb_ref[...] * (B_QK ** -0.5 * LOG2E)
    kbound = bnd_ref[1:2, :]
    for hh in range(B_HEADS):
        blk = slice(HP * hh, HP * (hh + 1))
        qh = qm[:, blk] * cos + qs[:, blk] * sin
        ss = jnp.sum(qh * qh, axis=-1, keepdims=True) * (1.0 / B_QK)
        qh = qh * lax.rsqrt(ss + EPS) * gqb
        m_b = jnp.sqrt(jnp.sum(qh * qh, axis=-1, keepdims=True)) * kbound * BOUND_SLACK
        qbT_ref[blk, :] = (qh - m_b * spare).T.astype(BF16)

    ckv = proj(_C_CKV, _C_KR)
    ckvn = (ckv * lax.rsqrt(jnp.mean(ckv * ckv, axis=-1, keepdims=True) + EPS) * gckv_ref[...]).astype(BF16)
    km = _dot(ckvn, wkm_ref[...])
    store_vT(vbT_ref, _dot(ckvn, wv_ref[...]))
    krot = proj(_C_KR, _C_KRS) * cos + proj(_C_KRS, _C_END) * sin
    gkb = gkb_ref[...]
    for hh in range(B_HEADS):
        blk = slice(HP * hh, HP * (hh + 1))
        kh = km[:, blk] + krot
        ss = jnp.sum(kh * kh, axis=-1, keepdims=True) * (1.0 / B_QK)
        kb_ref[:, blk] = (kh * lax.rsqrt(ss + EPS) * gkb + spare).astype(BF16)


def _token_prep(x, g, w1, wqm, wqs, wkm, wv, gqa, gka, gcq, gckv, gqb, gkb, bnd,
                cos_t, sin_t):
    B, S, D = x.shape
    tm = TM_PREP
    nt = S // tm
    row3 = lambda w: pl.BlockSpec((None, tm, w), lambda b, i: (b, i, 0))
    colT = lambda r: pl.BlockSpec((None, r, tm), lambda b, i: (b, 0, i))
    full = lambda a: pl.BlockSpec(a.shape, lambda b, i: (0,) * a.ndim)
    tab = pl.BlockSpec((tm, LANES), lambda b, i: (i, 0))
    consts = (g, w1, wqm, wqs, wkm, wv, gqa, gka, gcq, gckv, gqb, gkb, bnd)
    out_shape = (
        jax.ShapeDtypeStruct((B, A_HEADS * HP, S), BF16),
        jax.ShapeDtypeStruct((B, S, A_HEADS * HP), BF16),
        jax.ShapeDtypeStruct((B, A_HEADS * VP, S), BF16),
        jax.ShapeDtypeStruct((B, S, IDX_HEADS * HP), BF16),
        jax.ShapeDtypeStruct((B, S // KU_SEL, LANES, KU_SEL), BF16),
        jax.ShapeDtypeStruct((B, S, LANES), F32),
        jax.ShapeDtypeStruct((B, B_HEADS * HP, S), BF16),
        jax.ShapeDtypeStruct((B, S, B_HEADS * HP), BF16),
        jax.ShapeDtypeStruct((B, B_HEADS * VP, S), BF16),
    )
    kiT_spec = pl.BlockSpec((None, tm // KU_SEL, LANES, KU_SEL), lambda b, i: (b, i, 0, 0))
    out_specs = (colT(A_HEADS * HP), row3(A_HEADS * HP), colT(A_HEADS * VP), row3(IDX_HEADS * HP),
                 kiT_spec, row3(LANES), colT(B_HEADS * HP), row3(B_HEADS * HP), colT(B_HEADS * VP))
    return pl.pallas_call(
        _token_prep_kernel,
        grid=(B, nt),
        in_specs=[row3(D)] + [full(a) for a in consts] + [tab, tab],
        out_specs=out_specs,
        out_shape=out_shape,
        compiler_params=_cparams(("parallel", "parallel")),
        name="token_prep",
    )(x, *consts, cos_t, sin_t)


def _bit_transpose32(words):
    a = list(words)
    j, m = 16, 0x0000FFFF
    while j:
        k = 0
        while k < 32:
            t = (a[k] ^ lax.shift_right_logical(a[k + j], jnp.int32(j))) & m
            a[k] = a[k] ^ t
            a[k + j] = a[k + j] ^ (t << j)
            k = (k + j + 1) & ~j
        j >>= 1
        m = m ^ (m << j)
    return a


def _expand_bits(word):
    return jnp.concatenate(
        [lax.shift_right_logical(word, jnp.int32(SEL_BITS - 1 - j)) & 1 for j in range(SEL_BITS)], axis=0)


def _select_kernel(qi_ref, wi_ref, kiT_ref, sel_ref, planes_ref, alive_ref, gt_ref,
                   acca_ref, accb_ref, wb_ref, *, topk):
    tq = TQ_SEL
    i = pl.program_id(1)
    n_keys = (i + 1) * tq
    n_groups = (i + GU_SEL) // GU_SEL

    for hh in range(IDX_HEADS):
        wb_ref[hh] = jnp.broadcast_to(wi_ref[:, hh:hh + 1], (tq, LANES))

    zeros_group = jnp.zeros((GU_SEL, SUBLANES, tq), I32)

    @pl.when(i == 0)
    def _():
        planes_ref[...] = jnp.zeros(planes_ref.shape, I32)

    n_units = n_keys // KU_SEL
    accb_ref[...] = jnp.zeros(accb_ref.shape, F32)

    def heads(u, dst_ref):
        kT = kiT_ref[jnp.minimum(u, n_units - 1)]
        for hh in range(IDX_HEADS):
            d = _dot(qi_ref[:, HP * hh:HP * (hh + 1)], kT)
            for c in range(tq // CH_SEL):
                rows = slice(CH_SEL * c, CH_SEL * (c + 1))
                w = wb_ref[hh, rows, :]
                for lanes in (slice(0, LANES), slice(LANES, 2 * LANES)):
                    t = jnp.maximum(d[rows, lanes], 0.0) * w
                    if hh == 0:
                        dst_ref[rows, lanes] = t
                    else:
                        dst_ref[rows, lanes] += t

    def convert(src_ref, u):
        accT = src_ref[...].T
        words = []
        for j in range(SEL_BITS):
            acc = accT[SUBLANES * j:SUBLANES * (j + 1), :] + 0.0
            bits = pltpu.bitcast(acc, I32)
            words.append(bits ^ ((bits >> 31) | SIGN_BIT))
        unit = jnp.maximum(u, 0)
        for r, plane in enumerate(_bit_transpose32(words)):
            planes_ref[r, unit] = plane

    def unit_pair(t, carry):
        heads(2 * t, acca_ref)
        convert(accb_ref, 2 * t - 1)
        heads(2 * t + 1, accb_ref)
        convert(acca_ref, 2 * t)
        return carry

    lax.fori_loop(0, n_units // 2 + 1, unit_pair, 0)

    full_group = jnp.full((GU_SEL, SUBLANES, tq), -1, I32)

    def init(g, carry):
        unit = g * GU_SEL + lax.broadcasted_iota(I32, (GU_SEL, SUBLANES, tq), 0)
        alive_ref[pl.ds(g * GU_SEL, GU_SEL)] = jnp.where(unit < i, full_group, zeros_group)
        gt_ref[pl.ds(g * GU_SEL, GU_SEL)] = zeros_group
        return carry
    lax.fori_loop(0, n_groups, init, 0)

    q_chunk = (i * tq + lax.broadcasted_iota(I32, (SUBLANES, tq), 1)) >> CHUNK_SHIFT
    k_row = i * tq + lax.broadcasted_iota(I32, (SUBLANES, tq), 0)
    adm_bits = jnp.zeros((SUBLANES, tq), I32)
    for j in range(SEL_BITS):
        adm = ((k_row + SUBLANES * j) >> CHUNK_SHIFT) <= q_chunk
        adm_bits = adm_bits | jnp.where(adm, jnp.int32(1) << (SEL_BITS - 1 - j), 0)
    alive_ref[i] = adm_bits

    def popcount_rows(x):
        return jnp.sum(lax.population_count(x), axis=0)

    def count_alive(plane_of):
        def body(g, acc):
            return acc + popcount_rows(plane_of(g) & alive_ref[pl.ds(g * GU_SEL, GU_SEL)])
        acc = lax.fori_loop(0, n_groups, body, jnp.zeros((SUBLANES, tq), I32))
        return jnp.sum(acc, axis=0, keepdims=True)

    def bit_pass(r, carry):
        need, ones_here = carry
        take = ones_here >= need
        last = r == SEL_BITS - 1
        nxt = jnp.minimum(r + 1, SEL_BITS - 1)

        def sweep(g, acc):
            grp = pl.ds(g * GU_SEL, GU_SEL)
            alive = alive_ref[grp]
            with_bit = alive & planes_ref[r, grp]
            alive = jnp.where(take, with_bit, alive ^ with_bit)
            alive_ref[grp] = alive
            gt_ref[grp] = jnp.where(take, gt_ref[grp], gt_ref[grp] | with_bit)
            return acc + popcount_rows(alive & jnp.where(last, full_group, planes_ref[nxt, grp]))
        acc = lax.fori_loop(0, n_groups, sweep, jnp.zeros((SUBLANES, tq), I32))
        return jnp.where(take, need, need - ones_here), jnp.sum(acc, axis=0, keepdims=True)

    ones_top = count_alive(lambda g: planes_ref[0, pl.ds(g * GU_SEL, GU_SEL)])
    need, n_equal = lax.fori_loop(0, SEL_BITS, bit_pass, (jnp.full((1, tq), topk, I32), ones_top))
    any_tie = jnp.max(jnp.where(n_equal > need, 1, 0)) > 0

    @pl.when(jnp.logical_not(any_tie))
    def _():
        def body(g, carry):
            grp = pl.ds(g * GU_SEL, GU_SEL)
            sel_ref[grp] = gt_ref[grp] | alive_ref[grp]
            return carry
        lax.fori_loop(0, n_groups, body, 0)

    @pl.when(any_tie)
    def _():
        n_take = need.astype(F32)
        ltri = (lax.broadcasted_iota(I32, (KU_SEL, KU_SEL), 0)
                >= lax.broadcasted_iota(I32, (KU_SEL, KU_SEL), 1)).astype(BF16)

        def body(u, seen):
            eqf = _expand_bits(alive_ref[u]).astype(F32)
            rank = _dot(ltri, eqf.astype(BF16)) + seen
            first = jnp.logical_and(eqf != 0.0, rank <= n_take)
            words = gt_ref[u]
            for j in range(SEL_BITS):
                words = words | jnp.where(first[SUBLANES * j:SUBLANES * (j + 1), :],
                                          jnp.int32(1) << (SEL_BITS - 1 - j), 0)
            sel_ref[u] = words
            return seen + jnp.sum(eqf, axis=0, keepdims=True)
        lax.fori_loop(0, n_units, body, jnp.zeros((1, tq), F32))

        def clear(u, carry):
            sel_ref[u] = jnp.zeros((SUBLANES, tq), I32)
            return carry
        lax.fori_loop(n_units, n_groups * GU_SEL, clear, 0)

    def fill(g, carry):
        sel_ref[pl.ds(g * GU_SEL, GU_SEL)] = zeros_group
        return carry
    lax.fori_loop(n_groups, sel_ref.shape[0] // GU_SEL, fill, 0)


def _select_topk(qi, wi, kiT, topk):
    B, S, _ = qi.shape
    tq = TQ_SEL
    n_units, dims, unit = kiT.shape[1:]
    assert unit == KU_SEL == tq == SEL_BITS * SUBLANES and n_units * unit == S
    n_alloc = -(-n_units // GU_SEL) * GU_SEL
    return pl.pallas_call(
        functools.partial(_select_kernel, topk=topk),
        grid=(B, S // tq),
        in_specs=[
            pl.BlockSpec((None, tq, IDX_HEADS * HP), lambda b, i: (b, i, 0)),
            pl.BlockSpec((None, tq, LANES), lambda b, i: (b, i, 0)),
            pl.BlockSpec((None, n_units, dims, unit), lambda b, i: (b, 0, 0, 0)),
        ],
        out_specs=pl.BlockSpec((None, n_alloc, SUBLANES, tq), lambda b, i: (b, 0, 0, i)),
        out_shape=jax.ShapeDtypeStruct((B, n_alloc, SUBLANES, S), I32),
        scratch_shapes=[pltpu.VMEM((SEL_BITS, n_alloc + 1, SUBLANES, tq), I32),
                        pltpu.VMEM((n_alloc, SUBLANES, tq), I32),
                        pltpu.VMEM((n_alloc, SUBLANES, tq), I32),
                        pltpu.VMEM((tq, KU_SEL), F32), pltpu.VMEM((tq, KU_SEL), F32),
                        pltpu.VMEM((IDX_HEADS, tq, LANES), F32)],
        compiler_params=_cparams(("parallel", "arbitrary")),
        name="select_topk",
    )(qi, wi, kiT)


def _attend_kernel(ti_ref, tj_ref, safe_ref, *refs, heads, with_mask):
    if with_mask:
        qT_ref, k_ref, vT_ref, selbits_ref, bias_ref, o_ref, acc_ref, s_ref, p_ref, m_ref, add_ref = refs
    else:
        qT_ref, k_ref, vT_ref, diag_ref, o_ref, acc_ref, s_ref, p_ref, m_ref = refs
    s_idx = pl.program_id(1)
    i = ti_ref[s_idx]
    jj = tj_ref[s_idx]
    bounded = safe_ref[0] == 1
    tk = TK_ATT
    n_ch = tk // CH_ATT

    @pl.when(jj == 0)
    def _():
        acc_ref[...] = jnp.zeros(acc_ref.shape, F32)
        m_ref[...] = jnp.full(m_ref.shape, NEG, F32)

    def chunks():
        return [slice(CH_ATT * c, CH_ATT * (c + 1)) for c in range(n_ch)]

    def key_block(sb, carry):
        j = jj * NSUB_ATT + sb
        k0 = pl.multiple_of(sb * tk, tk)

        def logits(hh):
            s_ref[hh % 2] = _dot(k_ref[pl.ds(k0, tk), HP * hh:HP * (hh + 1)],
                                 qT_ref[HP * hh:HP * (hh + 1), :])

        def pv(hh, buf):
            return _dot(vT_ref[VP * hh:VP * (hh + 1), pl.ds(k0, tk)], p_ref[buf])

        def sweep_bounded(extra):
            logits(0)
            for hh in range(heads):
                buf = hh % 2
                if hh + 1 < heads:
                    logits(hh + 1)
                for rows in chunks():
                    t = s_ref[buf, rows, :]
                    e = extra(hh, rows)
                    if e is not None:
                        t = t + e
                    p_ref[buf, rows, :] = jnp.exp2(t).astype(BF16)
                acc_ref[VP * hh:VP * (hh + 1), :] += pv(hh, buf)

        def sweep_running_max(extra):
            for hh in range(heads):
                buf = hh % 2
                logits(hh)
                m_blk = jnp.full((SUBLANES, s_ref.shape[2]), NEG, F32)
                for rows in chunks():
                    t = s_ref[buf, rows, :]
                    e = extra(hh, rows)
                    if e is not None:
                        t = t + e
                        s_ref[buf, rows, :] = t
                    for r in range(CH_ATT // SUBLANES):
                        m_blk = jnp.maximum(m_blk, t[SUBLANES * r:SUBLANES * (r + 1), :])
                m_old = m_ref[hh:hh + 1, :]
                m_new = jnp.maximum(m_old, jnp.max(m_blk, axis=0, keepdims=True))
                m_ref[hh:hh + 1, :] = m_new
                for rows in chunks():
                    p_ref[buf, rows, :] = jnp.exp2(s_ref[buf, rows, :] - m_new).astype(BF16)
                acc_ref[VP * hh:VP * (hh + 1), :] = (
                    jnp.exp2(m_old - m_new) * acc_ref[VP * hh:VP * (hh + 1), :] + pv(hh, buf))

        def both(extra):
            @pl.when(bounded)
            def _():
                sweep_bounded(extra)

            @pl.when(jnp.logical_not(bounded))
            def _():
                sweep_running_max(extra)

        if with_mask:
            @pl.when(j <= i)
            def _():
                for c in range(tk // KU_SEL):
                    bits = _expand_bits(selbits_ref[sb * (tk // KU_SEL) + c])
                    add_ref[KU_SEL * c:KU_SEL * (c + 1), :] = jnp.where(bits != 0, 0.0, NEG)

            @pl.when(j < i - 1)
            def _():
                both(lambda hh, rows: add_ref[rows, :])

            @pl.when(jnp.logical_and(j >= i - 1, j <= i))
            def _():
                near = i - j
                both(lambda hh, rows: add_ref[rows, :] + bias_ref[hh, near, rows, :].astype(F32))
        else:
            @pl.when(j < i)
            def _():
                both(lambda hh, rows: None)

            @pl.when(j == i)
            def _():
                both(lambda hh, rows: diag_ref[rows, :])
        return carry

    lax.fori_loop(0, NSUB_ATT, key_block, 0)

    @pl.when(jj == i // NSUB_ATT)
    def _():
        dv = VP - BF16_ROWS
        for hh in range(heads):
            inv = 1.0 / acc_ref[VP * hh + dv:VP * hh + dv + 1, :]
            s_ref[0, dv * hh:dv * (hh + 1), :] = acc_ref[VP * hh:VP * hh + dv, :] * inv
        o_ref[...] = s_ref[0, 0:dv * heads, :].T.astype(o_ref.dtype)


def _pair_tables(S):
    n = S // TQ_ATT
    ti = np.array([i for i in range(n) for _ in range(i // NSUB_ATT + 1)], np.int32)
    tj = np.array([j for i in range(n) for j in range(i // NSUB_ATT + 1)], np.int32)
    return jnp.asarray(ti), jnp.asarray(tj)


def _attend(qT, k, vT, safe, extra_inputs, extra_specs, extra_scratch, with_mask, name):
    B, S, _ = k.shape
    tq, tk = TQ_ATT, TK_ATT
    tg = tk * NSUB_ATT
    assert tq == tk and tk >= B_V * A_HEADS and S % tg == 0
    ti, tj = _pair_tables(S)
    heads = A_HEADS
    kern = functools.partial(_attend_kernel, heads=heads, with_mask=with_mask)
    grid_spec = pltpu.PrefetchScalarGridSpec(
        num_scalar_prefetch=3,
        grid=(B, ti.shape[0]),
        in_specs=[
            pl.BlockSpec((None, heads * HP, tq), lambda b, s, ti, tj, sf: (b, 0, ti[s])),
            pl.BlockSpec((None, tg, heads * HP), lambda b, s, ti, tj, sf: (b, tj[s], 0)),
            pl.BlockSpec((None, heads * VP, tg), lambda b, s, ti, tj, sf: (b, 0, tj[s])),
        ] + extra_specs,
        out_specs=pl.BlockSpec((None, tq, heads * B_V), lambda b, s, ti, tj, sf: (b, ti[s], 0)),
        scratch_shapes=[pltpu.VMEM((heads * VP, tq), F32),
                        pltpu.VMEM((2, tk, tq), F32),
                        pltpu.VMEM((2, tk, tq), BF16),
                        pltpu.VMEM((heads, tq), F32)] + extra_scratch,
    )
    return pl.pallas_call(
        kern, grid_spec=grid_spec,
        out_shape=jax.ShapeDtypeStruct((B, S, heads * B_V), BF16),
        compiler_params=_cparams(("parallel", "arbitrary")),
        name=name,
    )(ti, tj, safe, qT, k, vT, *extra_inputs)


def _attend_a(qaT, ka, vaT, selbits, bias_tiles, safe):
    tq, tk = TQ_ATT, TK_ATT
    specs = [
        pl.BlockSpec((None, tk * NSUB_ATT // KU_SEL, SUBLANES, tq),
                     lambda b, s, ti, tj, sf: (b, tj[s], 0, ti[s])),
        pl.BlockSpec(bias_tiles.shape, lambda b, s, ti, tj, sf: (0, 0, 0, 0)),
    ]
    return _attend(qaT, ka, vaT, safe, (selbits, bias_tiles), specs, [pltpu.VMEM((tk, tq), F32)],
                   True, "attend_a")


def _attend_b(qbT, kb, vbT, diag_tile, safe):
    tq, tk = TQ_ATT, TK_ATT
    specs = [pl.BlockSpec((tk, tq), lambda b, s, ti, tj, sf: (0, 0))]
    return _attend(qbT, kb, vbT, safe, (diag_tile,), specs, [], False, "attend_b")


def _merge_kernel(x_ref, ya_ref, yb_ref, g_ref, wga_ref, wgb_ref, bga_ref, bgb_ref,
                  wpa_ref, wpb_ref, wo_ref, o_ref):
    x = x_ref[...]
    h = (x * lax.rsqrt(jnp.mean(x * x, axis=-1, keepdims=True) + EPS) * g_ref[...]).astype(BF16)
    gate_a = jax.nn.sigmoid(_dot(h, wga_ref[...]) + bga_ref[...])
    gate_b = jax.nn.sigmoid(_dot(h, wgb_ref[...]) + bgb_ref[...])
    merged = gate_a * _dot(ya_ref[...], wpa_ref[...]) + gate_b * _dot(yb_ref[...], wpb_ref[...])
    o_ref[...] = x + _dot(merged.astype(BF16), wo_ref[...])


def _merge_out(x2, ya2, yb2, g, wga, wgb, bga, bgb, wpa, wpb, wo):
    R, D = x2.shape
    tm = TM_MERGE
    row = lambda w: pl.BlockSpec((tm, w), lambda i: (i, 0))
    full = lambda a: pl.BlockSpec(a.shape, lambda i: (0,) * a.ndim)
    consts = (g, wga, wgb, bga, bgb, wpa, wpb, wo)
    return pl.pallas_call(
        _merge_kernel,
        grid=(R // tm,),
        in_specs=[row(D), row(A_WIDTH), row(B_WIDTH)] + [full(a) for a in consts],
        out_specs=row(D),
        out_shape=jax.ShapeDtypeStruct((R, D), F32),
        compiler_params=_cparams(("parallel",)),
        name="merge_out",
    )(x2, ya2, yb2, *consts)


def _ffn_kernel(x_ref, xp_ref, g_ref, wuv_ref, wug_ref, cwv_ref, cwg_ref, cbv_ref, cbg_ref, wd_ref,
                o_ref, uv_ref, ug_ref, acc_ref, *, tiles_per_seq):
    i = pl.program_id(0)
    f = pl.program_id(1)
    tm = x_ref.shape[0]
    halo = SUBLANES
    g = g_ref[...]

    def normed(v):
        return (v * lax.rsqrt(jnp.mean(v * v, axis=-1, keepdims=True) + EPS) * g).astype(BF16)

    h = normed(x_ref[...])
    keep = jnp.where(i % tiles_per_seq == 0, 0.0, 1.0)
    hp = normed(xp_ref[...])

    def conv(u_ref, w_ref, cw_ref, cb_ref):
        u_ref[0:halo, :] = _dot(hp, w_ref[...]) * keep
        u_ref[halo:halo + tm, :] = _dot(h, w_ref[...])
        out = cb_ref[...]
        for t in range(CONV_W):
            lo = halo - (CONV_W - 1) + t
            out = out + cw_ref[t:t + 1, :] * u_ref[lo:lo + tm, :]
        return out

    val = conv(uv_ref, wuv_ref, cwv_ref, cbv_ref)
    gat = conv(ug_ref, wug_ref, cwg_ref, cbg_ref)
    act = (gat * jax.nn.sigmoid(gat) * val).astype(BF16)
    part = _dot(act, wd_ref[...])

    @pl.when(f == 0)
    def _():
        acc_ref[...] = part

    @pl.when(f > 0)
    def _():
        acc_ref[...] = acc_ref[...] + part

    @pl.when(f == pl.num_programs(1) - 1)
    def _():
        o_ref[...] = x_ref[...] + acc_ref[...]


def _conv_ffn(x2, S, g, wu, cw, cb, wd):
    R, D = x2.shape
    tm, tf = TM_FFN, TF_FFN
    nf = D_FF // tf
    halo_blocks = tm // SUBLANES
    kern = functools.partial(_ffn_kernel, tiles_per_seq=S // tm)
    return pl.pallas_call(
        kern,
        grid=(R // tm, nf),
        in_specs=[
            pl.BlockSpec((tm, D), lambda i, f: (i, 0)),
            pl.BlockSpec((SUBLANES, D), lambda i, f: (jnp.maximum(i * halo_blocks - 1, 0), 0)),
            pl.BlockSpec((1, D), lambda i, f: (0, 0)),
            pl.BlockSpec((D, tf), lambda i, f: (0, f)),
            pl.BlockSpec((D, tf), lambda i, f: (0, nf + f)),
            pl.BlockSpec((CONV_W, tf), lambda i, f: (0, f)),
            pl.BlockSpec((CONV_W, tf), lambda i, f: (0, nf + f)),
            pl.BlockSpec((1, tf), lambda i, f: (0, f)),
            pl.BlockSpec((1, tf), lambda i, f: (0, nf + f)),
            pl.BlockSpec((tf, D), lambda i, f: (f, 0)),
        ],
        out_specs=pl.BlockSpec((tm, D), lambda i, f: (i, 0)),
        out_shape=jax.ShapeDtypeStruct((R, D), F32),
        scratch_shapes=[pltpu.VMEM((tm + SUBLANES, tf), F32), pltpu.VMEM((tm + SUBLANES, tf), F32),
                        pltpu.VMEM((tm, D), F32)],
        compiler_params=_cparams(("parallel", "arbitrary")),
        name="conv_ffn",
    )(x2, x2, g, wu, wu, cw, cw, cb, cb, wd)


def _t5_bucket(rel):
    nb = REL_BUCKETS // 2
    max_exact = nb // 2
    side = jnp.where(rel > 0, nb, 0)
    n = jnp.abs(rel)
    nf = jnp.maximum(n, 1).astype(F32)
    large = max_exact + (jnp.log(nf / max_exact) / math.log(REL_MAX_DIST / max_exact)
                         * (nb - max_exact)).astype(I32)
    large = jnp.minimum(large, nb - 1)
    return side + jnp.where(n < max_exact, n, large)


def _bias_tiles(rel_bias):
    tq, tk = TQ_ATT, TK_ATT
    assert tq == tk and tk >= REL_MAX_DIST
    kk = jnp.arange(tk, dtype=I32)[:, None]
    qq = jnp.arange(tq, dtype=I32)[None, :]
    bucket = _t5_bucket(jnp.stack([kk - qq, kk - qq - tk])).reshape(1, 2 * tk * tq)
    onehot = (jnp.arange(REL_BUCKETS, dtype=I32)[:, None] == bucket).astype(F32)
    far = rel_bias[_t5_bucket(jnp.asarray(-REL_MAX_DIST, I32))]
    table = ((rel_bias - far[None, :]) * LOG2E).T
    tiles = jnp.dot(table, onehot, precision=lax.Precision.HIGHEST)
    return tiles.reshape(A_HEADS, 2, tk, tq).astype(BF16), jnp.max(jnp.abs(table), axis=1)


def _diag_tile():
    kk = np.arange(TK_ATT)[:, None] // CHUNK
    qq = np.arange(TQ_ATT)[None, :] // CHUNK
    return jnp.asarray(np.where(kk <= qq, 0.0, NEG).astype(np.float32))


def _rope_tables(S):
    half = B_ROPE // 2
    inv = ROPE_BASE ** (-jnp.arange(half, dtype=F32) / half)
    ang = jnp.arange(S, dtype=I32).astype(F32)[:, None] * inv[None, :]
    cos, sin = jnp.cos(ang), jnp.sin(ang)
    ones = jnp.ones((S, B_NOPE), F32)
    zeros_n = jnp.zeros((S, B_NOPE), F32)
    zeros_p = jnp.zeros((S, HP - B_QK), F32)
    cos_t = jnp.concatenate([ones, cos, cos, zeros_p], axis=1)
    sin_t = jnp.concatenate([zeros_n, -sin, sin, zeros_p], axis=1)
    return cos_t, sin_t


def _pad_cols(w, width):
    return jnp.pad(w, ((0, 0), (0, width - w.shape[1])))


def _pad_heads(w, heads, width):
    rows = w.shape[0]
    w = w.reshape(rows, heads, -1)
    return jnp.pad(w, ((0, 0), (0, 0), (0, width - w.shape[2]))).reshape(rows, heads * width)


def _swap_halves(w):
    half = w.shape[-1] // 2
    return jnp.concatenate([w[..., half:], w[..., :half]], axis=-1)


def _layer_weights(l, w_in, b_w_uq, b_w_ukv, b_q_norm, b_k_norm, a_q_norm, a_k_norm, bias_max):
    w = w_in[l]
    o = np.cumsum([0, A_WIDTH, A_WIDTH, A_WIDTH, IDX_HEADS * IDX_DIM, IDX_DIM, IDX_HEADS,
                   B_Q_LORA, B_KV_LORA, B_ROPE, D_MODEL, D_MODEL])
    seg = [w[:, o[t]:o[t + 1]] for t in range(11)]
    zn = jnp.zeros((D_MODEL, B_NOPE), F32)
    kr = seg[8]
    w1 = jnp.concatenate([
        _pad_heads(seg[0], A_HEADS, HP), _pad_heads(seg[1], A_HEADS, HP), seg[2],
        _pad_heads(seg[3], IDX_HEADS, HP),
        _pad_cols(seg[4], LANES), _pad_cols(seg[5], LANES), seg[6], seg[7],
        _pad_cols(jnp.concatenate([zn, kr], axis=1), HP),
        _pad_cols(jnp.concatenate([zn, _swap_halves(kr)], axis=1), HP),
    ], axis=1).astype(BF16)
    assert w1.shape[1] == _C_END

    uq = b_w_uq[l].reshape(B_Q_LORA, B_HEADS, B_QK)
    zq = jnp.zeros((B_Q_LORA, B_HEADS, B_NOPE), F32)
    pq = jnp.zeros((B_Q_LORA, B_HEADS, HP - B_QK), F32)
    wqm = jnp.concatenate([uq, pq], axis=-1).reshape(B_Q_LORA, B_HEADS * HP).astype(BF16)
    wqs = jnp.concatenate([zq, _swap_halves(uq[..., B_NOPE:]), pq], axis=-1)
    wqs = wqs.reshape(B_Q_LORA, B_HEADS * HP).astype(BF16)

    ukv = b_w_ukv[l].reshape(B_KV_LORA, B_HEADS, B_NOPE + B_V)
    pk = jnp.zeros((B_KV_LORA, B_HEADS, HP - B_NOPE), F32)
    wkm = jnp.concatenate([ukv[..., :B_NOPE], pk], axis=-1).reshape(B_KV_LORA, B_HEADS * HP).astype(BF16)
    wv = ukv[..., B_NOPE:].reshape(B_KV_LORA, B_WIDTH).astype(BF16)

    gqa = _pad_cols(a_q_norm[l][None, :], HP)
    gka = _pad_cols(a_k_norm[l][None, :], HP)
    gqb = _pad_cols(b_q_norm[l][None, :], HP)
    gkb = _pad_cols(b_k_norm[l][None, :], HP)

    ka_bound = math.sqrt(A_HEAD_DIM) * jnp.max(jnp.abs(a_k_norm[l]))
    kb_bound = math.sqrt(B_QK) * jnp.max(jnp.abs(b_k_norm[l]))
    bnd = jnp.zeros((SUBLANES, LANES), F32)
    bnd = bnd.at[0, :].set(ka_bound).at[1, :].set(kb_bound).at[2, :A_HEADS].set(bias_max)
    qa_bound = math.sqrt(A_HEAD_DIM) * jnp.max(jnp.abs(a_q_norm[l])) * (A_HEAD_DIM ** -0.5 * LOG2E)
    qb_bound = math.sqrt(B_QK) * jnp.max(jnp.abs(b_q_norm[l])) * (B_QK ** -0.5 * LOG2E)
    range_a = (qa_bound * ka_bound + jnp.max(bias_max)) * BOUND_SLACK + jnp.max(bias_max)
    range_b = qb_bound * kb_bound * BOUND_SLACK
    safe_a = (range_a <= MAX_LOG2_RANGE).astype(I32).reshape(1)
    safe_b = (range_b <= MAX_LOG2_RANGE).astype(I32).reshape(1)
    return (w1, wqm, wqs, wkm, wv, gqa, gka, gqb, gkb, bnd, safe_a, safe_b,
            seg[9].astype(BF16), seg[10].astype(BF16))


def kernel(x, rel_bias, norm_mix, w_in, a_q_norm, a_k_norm, b_cq_norm, b_ckv_norm, b_w_uq, b_w_ukv,
           b_q_norm, b_k_norm, w_proj_a, w_proj_b, b_gate, w_out, norm_ffn, w_up, conv_w, conv_b, w_down):
    B, S, D = x.shape
    assert D == D_MODEL and S % TQ_ATT == 0 and S % TM_FFN == 0
    topk = min(TOPK_MAX, S // 4)
    assert TQ_SEL >= topk
    depth = w_in.shape[0]

    cos_t, sin_t = _rope_tables(S)
    bias_tiles, bias_max = _bias_tiles(rel_bias)
    diag_tile = _diag_tile()

    for l in range(depth):
        (w1, wqm, wqs, wkm, wv, gqa, gka, gqb, gkb, bnd, safe_a, safe_b, wga, wgb) = _layer_weights(
            l, w_in, b_w_uq, b_w_ukv, b_q_norm, b_k_norm, a_q_norm, a_k_norm, bias_max)
        qaT, ka, vaT, qi, kiT, wi, qbT, kb, vbT = _token_prep(
            x, norm_mix[l][None, :], w1, wqm, wqs, wkm, wv, gqa, gka,
            b_cq_norm[l][None, :], b_ckv_norm[l][None, :], gqb, gkb, bnd,
            cos_t, sin_t)
        selbits = _select_topk(qi, wi, kiT, topk)
        y_a = _attend_a(qaT, ka, vaT, selbits, bias_tiles, safe_a)
        y_b = _attend_b(qbT, kb, vbT, diag_tile, safe_b)
        x2 = _merge_out(
            x.reshape(B * S, D), y_a.reshape(B * S, A_WIDTH), y_b.reshape(B * S, B_WIDTH),
            norm_mix[l][None, :], wga, wgb, b_gate[l][None, :D_MODEL], b_gate[l][None, D_MODEL:],
            w_proj_a[l].astype(BF16), w_proj_b[l].astype(BF16), w_out[l].astype(BF16))
        x2 = _conv_ffn(x2, S, norm_ffn[l][None, :], w_up[l].astype(BF16), conv_w[l], conv_b[l][None, :],
                       w_down[l].astype(BF16))
        x = x2.reshape(B, S, D)
    return x
```

```python
import functools
import math

import numpy as np
import jax
import jax.numpy as jnp
from jax import lax
from jax.experimental import pallas as pl
from jax.experimental.pallas import tpu as pltpu

F32 = jnp.float32
BF16 = jnp.bfloat16
I32 = jnp.int32

D_MODEL = 1024
CHUNK = 64
CHUNK_SHIFT = 6
assert 1 << CHUNK_SHIFT == CHUNK
A_HEADS = 8
A_HEAD_DIM = 64
A_WIDTH = A_HEADS * A_HEAD_DIM
IDX_HEADS = 8
IDX_DIM = 64
TOPK_MAX = 256
B_HEADS = 8
B_Q_LORA = 384
B_KV_LORA = 256
B_NOPE = 64
B_ROPE = 32
B_QK = B_NOPE + B_ROPE
B_V = 64
B_WIDTH = B_HEADS * B_V
ROPE_BASE = 10000.0
REL_BUCKETS = 32
REL_MAX_DIST = 128
D_FF = 2816
CONV_W = 3
EPS = 1e-6

LANES = 128
SUBLANES = 8
BF16_ROWS = 16
VMEM_LIMIT = 56 * 1024 * 1024

TM_PREP = 512
TQ_SEL = 256
KU_SEL = 256
CH_SEL = 64
SEL_BITS = 32
GU_SEL = 16
TQ_ATT = 512
TK_ATT = 512
NSUB_ATT = 4
CH_ATT = 64
TM_MERGE = 512
TM_FFN = 512
TF_FFN = 1408
FFN_GROUPS = 4

HP = 128
VP = B_V + BF16_ROWS
NEG = -1e30
SIGN_BIT = -2147483648
LOG2E = 1.4426950408889634
BOUND_SLACK = 1.01
MAX_LOG2_RANGE = 60.0


def _cparams(sem):
    return pltpu.CompilerParams(dimension_semantics=sem, vmem_limit_bytes=VMEM_LIMIT)


def _dot(a, b):
    return jnp.dot(a, b, preferred_element_type=F32)


_C_QA = 0
_C_KA = _C_QA + A_HEADS * HP
_C_VA = _C_KA + A_HEADS * HP
_C_QI = _C_VA + A_WIDTH
_C_KI = _C_QI + IDX_HEADS * HP
_C_WI = _C_KI + LANES
_C_CQ = _C_WI + LANES
_C_CKV = _C_CQ + B_Q_LORA
_C_KR = _C_CKV + B_KV_LORA
_C_KRS = _C_KR + HP
_C_END = _C_KRS + HP


def _token_prep_kernel(x_ref, g_ref, w1_ref, wqm_ref, wqs_ref, wkm_ref, wv_ref,
                       gqa_ref, gka_ref, gcq_ref, gckv_ref, gqb_ref, gkb_ref, bnd_ref,
                       cos_ref, sin_ref,
                       qaT_ref, ka_ref, vaT_ref, qi_ref, kiT_ref, wi_ref,
                       qbT_ref, kb_ref, vbT_ref):
    x = x_ref[...]
    tm = x.shape[0]
    h = (x * lax.rsqrt(jnp.mean(x * x, axis=-1, keepdims=True) + EPS) * g_ref[...]).astype(BF16)

    def proj(lo, hi):
        return _dot(h, w1_ref[:, lo:hi])

    def head_norm(zh, gain):
        ss = jnp.sum(zh * zh, axis=-1, keepdims=True) * (1.0 / A_HEAD_DIM)
        return zh * lax.rsqrt(ss + EPS) * gain

    ones_rows = jnp.where(lax.broadcasted_iota(I32, (BF16_ROWS, tm), 0) == 0, 1.0, 0.0).astype(BF16)

    def store_vT(ref, v):
        vT = v.T
        for hh in range(A_HEADS):
            ref[VP * hh:VP * hh + B_V, :] = vT[B_V * hh:B_V * (hh + 1), :].astype(BF16)
            ref[VP * hh + B_V:VP * (hh + 1), :] = ones_rows

    spare_a = jnp.where(lax.broadcasted_iota(I32, (1, HP), 1) == A_HEAD_DIM, 1.0, 0.0)
    qa = proj(_C_QA, _C_KA)
    ka = proj(_C_KA, _C_VA)
    gqa = gqa_ref[...] * (A_HEAD_DIM ** -0.5 * LOG2E)
    gka = gka_ref[...]
    for hh in range(A_HEADS):
        blk = slice(HP * hh, HP * (hh + 1))
        qh = head_norm(qa[:, blk], gqa)
        qnorm = jnp.sqrt(jnp.sum(qh * qh, axis=-1, keepdims=True))
        m_a = (qnorm * bnd_ref[0:1, :] + bnd_ref[2:3, hh:hh + 1]) * BOUND_SLACK
        qaT_ref[blk, :] = (qh - m_a * spare_a).T.astype(BF16)
        ka_ref[:, blk] = (head_norm(ka[:, blk], gka) + spare_a).astype(BF16)
    store_vT(vaT_ref, proj(_C_VA, _C_QI))

    qi_ref[...] = proj(_C_QI, _C_KI).astype(BF16)
    kiT = proj(_C_KI, _C_WI).T.astype(BF16)
    for c in range(tm // KU_SEL):
        kiT_ref[c] = kiT[:, KU_SEL * c:KU_SEL * (c + 1)]
    wi_ref[...] = proj(_C_WI, _C_CQ) * ((IDX_HEADS ** -0.5) * (IDX_DIM ** -0.5))

    cos = cos_ref[...]
    sin = sin_ref[...]
    spare = jnp.where(lax.broadcasted_iota(I32, (1, HP), 1) == B_QK, 1.0, 0.0)
    cq = proj(_C_CQ, _C_CKV)
    cqn = (cq * lax.rsqrt(jnp.mean(cq * cq, axis=-1, keepdims=True) + EPS) * gcq_ref[...]).astype(BF16)
    qm = _dot(cqn, wqm_ref[...])
    qs = _dot(cqn, wqs_ref[...])
    gqb = gqb_ref[...] * (B_QK ** -0.5 * LOG2E)
    kbound = bnd_ref[1:2, :]
    for hh in range(B_HEADS):
        blk = slice(HP * hh, HP * (hh + 1))
        qh = qm[:, blk] * cos + qs[:, blk] * sin
        ss = jnp.sum(qh * qh, axis=-1, keepdims=True) * (1.0 / B_QK)
        qh = qh * lax.rsqrt(ss + EPS) * gqb
        m_b = jnp.sqrt(jnp.sum(qh * qh, axis=-1, keepdims=True)) * kbound * BOUND_SLACK
        qbT_ref[blk, :] = (qh - m_b * spare).T.astype(BF16)

    ckv = proj(_C_CKV, _C_KR)
    ckvn = (ckv * lax.rsqrt(jnp.mean(ckv * ckv, axis=-1, keepdims=True) + EPS) * gckv_ref[...]).astype(BF16)
    km = _dot(ckvn, wkm_ref[...])
    store_vT(vbT_ref, _dot(ckvn, wv_ref[...]))
    krot = proj(_C_KR, _C_KRS) * cos + proj(_C_KRS, _C_END) * sin
    gkb = gkb_ref[...]
    for hh in range(B_HEADS):
        blk = slice(HP * hh, HP * (hh + 1))
        kh = km[:, blk] + krot
        ss = jnp.sum(kh * kh, axis=-1, keepdims=True) * (1.0 / B_QK)
        kb_ref[:, blk] = (kh * lax.rsqrt(ss + EPS) * gkb + spare).astype(BF16)


def _token_prep(x, g, w1, wqm, wqs, wkm, wv, gqa, gka, gcq, gckv, gqb, gkb, bnd,
                cos_t, sin_t):
    B, S, D = x.shape
    tm = TM_PREP
    nt = S // tm
    row3 = lambda w: pl.BlockSpec((None, tm, w), lambda b, i: (b, i, 0))
    colT = lambda r: pl.BlockSpec((None, r, tm), lambda b, i: (b, 0, i))
    full = lambda a: pl.BlockSpec(a.shape, lambda b, i: (0,) * a.ndim)
    tab = pl.BlockSpec((tm, LANES), lambda b, i: (i, 0))
    consts = (g, w1, wqm, wqs, wkm, wv, gqa, gka, gcq, gckv, gqb, gkb, bnd)
    out_shape = (
        jax.ShapeDtypeStruct((B, A_HEADS * HP, S), BF16),
        jax.ShapeDtypeStruct((B, S, A_HEADS * HP), BF16),
        jax.ShapeDtypeStruct((B, A_HEADS * VP, S), BF16),
        jax.ShapeDtypeStruct((B, S, IDX_HEADS * HP), BF16),
        jax.ShapeDtypeStruct((B, S // KU_SEL, LANES, KU_SEL), BF16),
        jax.ShapeDtypeStruct((B, S, LANES), F32),
        jax.ShapeDtypeStruct((B, B_HEADS * HP, S), BF16),
        jax.ShapeDtypeStruct((B, S, B_HEADS * HP), BF16),
        jax.ShapeDtypeStruct((B, B_HEADS * VP, S), BF16),
    )
    kiT_spec = pl.BlockSpec((None, tm // KU_SEL, LANES, KU_SEL), lambda b, i: (b, i, 0, 0))
    out_specs = (colT(A_HEADS * HP), row3(A_HEADS * HP), colT(A_HEADS * VP), row3(IDX_HEADS * HP),
                 kiT_spec, row3(LANES), colT(B_HEADS * HP), row3(B_HEADS * HP), colT(B_HEADS * VP))
    return pl.pallas_call(
        _token_prep_kernel,
        grid=(B, nt),
        in_specs=[row3(D)] + [full(a) for a in consts] + [tab, tab],
        out_specs=out_specs,
        out_shape=out_shape,
        compiler_params=_cparams(("parallel", "parallel")),
        name="token_prep",
    )(x, *consts, cos_t, sin_t)


def _bit_transpose32(words):
    a = list(words)
    j, m = 16, 0x0000FFFF
    while j:
        k = 0
        while k < 32:
            t = (a[k] ^ lax.shift_right_logical(a[k + j], jnp.int32(j))) & m
            a[k] = a[k] ^ t
            a[k + j] = a[k + j] ^ (t << j)
            k = (k + j + 1) & ~j
        j >>= 1
        m = m ^ (m << j)
    return a


def _expand_bits(word):
    return jnp.concatenate(
        [lax.shift_right_logical(word, jnp.int32(SEL_BITS - 1 - j)) & 1 for j in range(SEL_BITS)], axis=0)


def _select_kernel(qi_ref, wi_ref, kiT_ref, sel_ref, planes_ref, alive_ref, gt_ref,
                   acca_ref, accb_ref, wb_ref, *, topk):
    tq = TQ_SEL
    i = pl.program_id(1)
    n_keys = (i + 1) * tq
    n_groups = (i + GU_SEL) // GU_SEL

    for hh in range(IDX_HEADS):
        wb_ref[hh] = jnp.broadcast_to(wi_ref[:, hh:hh + 1], (tq, LANES))

    zeros_group = jnp.zeros((GU_SEL, SUBLANES, tq), I32)

    @pl.when(i == 0)
    def _():
        planes_ref[...] = jnp.zeros(planes_ref.shape, I32)

    n_units = n_keys // KU_SEL
    accb_ref[...] = jnp.zeros(accb_ref.shape, F32)

    def heads(u, dst_ref):
        kT = kiT_ref[jnp.minimum(u, n_units - 1)]
        for hh in range(IDX_HEADS):
            d = _dot(qi_ref[:, HP * hh:HP * (hh + 1)], kT)
            for c in range(tq // CH_SEL):
                rows = slice(CH_SEL * c, CH_SEL * (c + 1))
                w = wb_ref[hh, rows, :]
                for lanes in (slice(0, LANES), slice(LANES, 2 * LANES)):
                    t = jnp.maximum(d[rows, lanes], 0.0) * w
                    if hh == 0:
                        dst_ref[rows, lanes] = t
                    else:
                        dst_ref[rows, lanes] += t

    def convert(src_ref, u):
        accT = src_ref[...].T
        words = []
        for j in range(SEL_BITS):
            acc = accT[SUBLANES * j:SUBLANES * (j + 1), :] + 0.0
            bits = pltpu.bitcast(acc, I32)
            words.append(bits ^ ((bits >> 31) | SIGN_BIT))
        unit = jnp.maximum(u, 0)
        for r, plane in enumerate(_bit_transpose32(words)):
            planes_ref[r, unit] = plane

    def unit_pair(t, carry):
        heads(2 * t, acca_ref)
        convert(accb_ref, 2 * t - 1)
        heads(2 * t + 1, accb_ref)
        convert(acca_ref, 2 * t)
        return carry

    lax.fori_loop(0, n_units // 2 + 1, unit_pair, 0)

    full_group = jnp.full((GU_SEL, SUBLANES, tq), -1, I32)

    def init(g, carry):
        unit = g * GU_SEL + lax.broadcasted_iota(I32, (GU_SEL, SUBLANES, tq), 0)
        alive_ref[pl.ds(g * GU_SEL, GU_SEL)] = jnp.where(unit < i, full_group, zeros_group)
        gt_ref[pl.ds(g * GU_SEL, GU_SEL)] = zeros_group
        return carry
    lax.fori_loop(0, n_groups, init, 0)

    q_chunk = (i * tq + lax.broadcasted_iota(I32, (SUBLANES, tq), 1)) >> CHUNK_SHIFT
    k_row = i * tq + lax.broadcasted_iota(I32, (SUBLANES, tq), 0)
    adm_bits = jnp.zeros((SUBLANES, tq), I32)
    for j in range(SEL_BITS):
        adm = ((k_row + SUBLANES * j) >> CHUNK_SHIFT) <= q_chunk
        adm_bits = adm_bits | jnp.where(adm, jnp.int32(1) << (SEL_BITS - 1 - j), 0)
    alive_ref[i] = adm_bits

    def popcount_rows(x):
        return jnp.sum(lax.population_count(x), axis=0)

    def count_alive(plane_of):
        def body(g, acc):
            return acc + popcount_rows(plane_of(g) & alive_ref[pl.ds(g * GU_SEL, GU_SEL)])
        acc = lax.fori_loop(0, n_groups, body, jnp.zeros((SUBLANES, tq), I32))
        return jnp.sum(acc, axis=0, keepdims=True)

    def bit_pass(r, carry):
        need, ones_here = carry
        take = ones_here >= need
        last = r == SEL_BITS - 1
        nxt = jnp.minimum(r + 1, SEL_BITS - 1)

        def sweep(g, acc):
            grp = pl.ds(g * GU_SEL, GU_SEL)
            alive = alive_ref[grp]
            with_bit = alive & planes_ref[r, grp]
            alive = jnp.where(take, with_bit, alive ^ with_bit)
            alive_ref[grp] = alive
            gt_ref[grp] = jnp.where(take, gt_ref[grp], gt_ref[grp] | with_bit)
            return acc + popcount_rows(alive & jnp.where(last, full_group, planes_ref[nxt, grp]))
        acc = lax.fori_loop(0, n_groups, sweep, jnp.zeros((SUBLANES, tq), I32))
        return jnp.where(take, need, need - ones_here), jnp.sum(acc, axis=0, keepdims=True)

    ones_top = count_alive(lambda g: planes_ref[0, pl.ds(g * GU_SEL, GU_SEL)])
    need, n_equal = lax.fori_loop(0, SEL_BITS, bit_pass, (jnp.full((1, tq), topk, I32), ones_top))
    any_tie = jnp.max(jnp.where(n_equal > need, 1, 0)) > 0

    @pl.when(jnp.logical_not(any_tie))
    def _():
        def body(g, carry):
            grp = pl.ds(g * GU_SEL, GU_SEL)
            sel_ref[grp] = gt_ref[grp] | alive_ref[grp]
            return carry
        lax.fori_loop(0, n_groups, body, 0)

    @pl.when(any_tie)
    def _():
        n_take = need.astype(F32)
        ltri = (lax.broadcasted_iota(I32, (KU_SEL, KU_SEL), 0)
                >= lax.broadcasted_iota(I32, (KU_SEL, KU_SEL), 1)).astype(BF16)

        def body(u, seen):
            eqf = _expand_bits(alive_ref[u]).astype(F32)
            rank = _dot(ltri, eqf.astype(BF16)) + seen
            first = jnp.logical_and(eqf != 0.0, rank <= n_take)
            words = gt_ref[u]
            for j in range(SEL_BITS):
                words = words | jnp.where(first[SUBLANES * j:SUBLANES * (j + 1), :],
                                          jnp.int32(1) << (SEL_BITS - 1 - j), 0)
            sel_ref[u] = words
            return seen + jnp.sum(eqf, axis=0, keepdims=True)
        lax.fori_loop(0, n_units, body, jnp.zeros((1, tq), F32))

        def clear(u, carry):
            sel_ref[u] = jnp.zeros((SUBLANES, tq), I32)
            return carry
        lax.fori_loop(n_units, n_groups * GU_SEL, clear, 0)

    def fill(g, carry):
        sel_ref[pl.ds(g * GU_SEL, GU_SEL)] = zeros_group
        return carry
    lax.fori_loop(n_groups, sel_ref.shape[0] // GU_SEL, fill, 0)


def _select_topk(qi, wi, kiT, topk):
    B, S, _ = qi.shape
    tq = TQ_SEL
    n_units, dims, unit = kiT.shape[1:]
    assert unit == KU_SEL == tq == SEL_BITS * SUBLANES and n_units * unit == S
    n_alloc = -(-n_units // GU_SEL) * GU_SEL
    return pl.pallas_call(
        functools.partial(_select_kernel, topk=topk),
        grid=(B, S // tq),
        in_specs=[
            pl.BlockSpec((None, tq, IDX_HEADS * HP), lambda b, i: (b, i, 0)),
            pl.BlockSpec((None, tq, LANES), lambda b, i: (b, i, 0)),
            pl.BlockSpec((None, n_units, dims, unit), lambda b, i: (b, 0, 0, 0)),
        ],
        out_specs=pl.BlockSpec((None, n_alloc, SUBLANES, tq), lambda b, i: (b, 0, 0, i)),
        out_shape=jax.ShapeDtypeStruct((B, n_alloc, SUBLANES, S), I32),
        scratch_shapes=[pltpu.VMEM((SEL_BITS, n_alloc + 1, SUBLANES, tq), I32),
                        pltpu.VMEM((n_alloc, SUBLANES, tq), I32),
                        pltpu.VMEM((n_alloc, SUBLANES, tq), I32),
                        pltpu.VMEM((tq, KU_SEL), F32), pltpu.VMEM((tq, KU_SEL), F32),
                        pltpu.VMEM((IDX_HEADS, tq, LANES), F32)],
        compiler_params=_cparams(("parallel", "arbitrary")),
        name="select_topk",
    )(qi, wi, kiT)


def _attend_kernel(ti_ref, tj_ref, safe_ref, *refs, heads, with_mask):
    if with_mask:
        qT_ref, k_ref, vT_ref, selbits_ref, bias_ref, o_ref, acc_ref, s_ref, p_ref, m_ref, add_ref = refs
    else:
        qT_ref, k_ref, vT_ref, diag_ref, o_ref, acc_ref, s_ref, p_ref, m_ref = refs
    s_idx = pl.program_id(1)
    i = ti_ref[s_idx]
    jj = tj_ref[s_idx]
    bounded = safe_ref[0] == 1
    tk = TK_ATT
    n_ch = tk // CH_ATT

    @pl.when(jj == 0)
    def _():
        acc_ref[...] = jnp.zeros(acc_ref.shape, F32)
        m_ref[...] = jnp.full(m_ref.shape, NEG, F32)

    def chunks():
        return [slice(CH_ATT * c, CH_ATT * (c + 1)) for c in range(n_ch)]

    def key_block(sb, carry):
        j = jj * NSUB_ATT + sb
        k0 = pl.multiple_of(sb * tk, tk)

        def logits(hh):
            s_ref[hh % 2] = _dot(k_ref[pl.ds(k0, tk), HP * hh:HP * (hh + 1)],
                                 qT_ref[HP * hh:HP * (hh + 1), :])

        def pv(hh, buf):
            return _dot(vT_ref[VP * hh:VP * (hh + 1), pl.ds(k0, tk)], p_ref[buf])

        def sweep_bounded(extra):
            logits(0)
            for hh in range(heads):
                buf = hh % 2
                if hh + 1 < heads:
                    logits(hh + 1)
                for rows in chunks():
                    t = s_ref[buf, rows, :]
                    e = extra(hh, rows)
                    if e is not None:
                        t = t + e
                    p_ref[buf, rows, :] = jnp.exp2(t).astype(BF16)
                acc_ref[VP * hh:VP * (hh + 1), :] += pv(hh, buf)

        def sweep_running_max(extra):
            for hh in range(heads):
                buf = hh % 2
                logits(hh)
                m_blk = jnp.full((SUBLANES, s_ref.shape[2]), NEG, F32)
                for rows in chunks():
                    t = s_ref[buf, rows, :]
                    e = extra(hh, rows)
                    if e is not None:
                        t = t + e
                        s_ref[buf, rows, :] = t
                    for r in range(CH_ATT // SUBLANES):
                        m_blk = jnp.maximum(m_blk, t[SUBLANES * r:SUBLANES * (r + 1), :])
                m_old = m_ref[hh:hh + 1, :]
                m_new = jnp.maximum(m_old, jnp.max(m_blk, axis=0, keepdims=True))
                m_ref[hh:hh + 1, :] = m_new
                for rows in chunks():
                    p_ref[buf, rows, :] = jnp.exp2(s_ref[buf, rows, :] - m_new).astype(BF16)
                acc_ref[VP * hh:VP * (hh + 1), :] = (
                    jnp.exp2(m_old - m_new) * acc_ref[VP * hh:VP * (hh + 1), :] + pv(hh, buf))

        def both(extra):
            @pl.when(bounded)
            def _():
                sweep_bounded(extra)

            @pl.when(jnp.logical_not(bounded))
            def _():
                sweep_running_max(extra)

        if with_mask:
            @pl.when(j <= i)
            def _():
                for c in range(tk // KU_SEL):
                    bits = _expand_bits(selbits_ref[sb * (tk // KU_SEL) + c])
                    add_ref[KU_SEL * c:KU_SEL * (c + 1), :] = jnp.where(bits != 0, 0.0, NEG)

            @pl.when(j < i - 1)
            def _():
                both(lambda hh, rows: add_ref[rows, :])

            @pl.when(jnp.logical_and(j >= i - 1, j <= i))
            def _():
                near = i - j
                both(lambda hh, rows: add_ref[rows, :] + bias_ref[hh, near, rows, :].astype(F32))
        else:
            @pl.when(j < i)
            def _():
                both(lambda hh, rows: None)

            @pl.when(j == i)
            def _():
                both(lambda hh, rows: diag_ref[rows, :])
        return carry

    lax.fori_loop(0, NSUB_ATT, key_block, 0)

    @pl.when(jj == i // NSUB_ATT)
    def _():
        dv = VP - BF16_ROWS
        for hh in range(heads):
            inv = 1.0 / acc_ref[VP * hh + dv:VP * hh + dv + 1, :]
            s_ref[0, dv * hh:dv * (hh + 1), :] = acc_ref[VP * hh:VP * hh + dv, :] * inv
        o_ref[...] = s_ref[0, 0:dv * heads, :].T.astype(o_ref.dtype)


def _pair_tables(S):
    n = S // TQ_ATT
    ti = np.array([i for i in range(n) for _ in range(i // NSUB_ATT + 1)], np.int32)
    tj = np.array([j for i in range(n) for j in range(i // NSUB_ATT + 1)], np.int32)
    return jnp.asarray(ti), jnp.asarray(tj)


def _attend(qT, k, vT, safe, extra_inputs, extra_specs, extra_scratch, with_mask, name):
    B, S, _ = k.shape
    tq, tk = TQ_ATT, TK_ATT
    tg = tk * NSUB_ATT
    assert tq == tk and tk >= B_V * A_HEADS and S % tg == 0
    ti, tj = _pair_tables(S)
    heads = A_HEADS
    kern = functools.partial(_attend_kernel, heads=heads, with_mask=with_mask)
    grid_spec = pltpu.PrefetchScalarGridSpec(
        num_scalar_prefetch=3,
        grid=(B, ti.shape[0]),
        in_specs=[
            pl.BlockSpec((None, heads * HP, tq), lambda b, s, ti, tj, sf: (b, 0, ti[s])),
            pl.BlockSpec((None, tg, heads * HP), lambda b, s, ti, tj, sf: (b, tj[s], 0)),
            pl.BlockSpec((None, heads * VP, tg), lambda b, s, ti, tj, sf: (b, 0, tj[s])),
        ] + extra_specs,
        out_specs=pl.BlockSpec((None, tq, heads * B_V), lambda b, s, ti, tj, sf: (b, ti[s], 0)),
        scratch_shapes=[pltpu.VMEM((heads * VP, tq), F32),
                        pltpu.VMEM((2, tk, tq), F32),
                        pltpu.VMEM((2, tk, tq), BF16),
                        pltpu.VMEM((heads, tq), F32)] + extra_scratch,
    )
    return pl.pallas_call(
        kern, grid_spec=grid_spec,
        out_shape=jax.ShapeDtypeStruct((B, S, heads * B_V), BF16),
        compiler_params=_cparams(("parallel", "arbitrary")),
        name=name,
    )(ti, tj, safe, qT, k, vT, *extra_inputs)


def _attend_a(qaT, ka, vaT, selbits, bias_tiles, safe):
    tq, tk = TQ_ATT, TK_ATT
    specs = [
        pl.BlockSpec((None, tk * NSUB_ATT // KU_SEL, SUBLANES, tq),
                     lambda b, s, ti, tj, sf: (b, tj[s], 0, ti[s])),
        pl.BlockSpec(bias_tiles.shape, lambda b, s, ti, tj, sf: (0, 0, 0, 0)),
    ]
    return _attend(qaT, ka, vaT, safe, (selbits, bias_tiles), specs, [pltpu.VMEM((tk, tq), F32)],
                   True, "attend_a")


def _attend_b(qbT, kb, vbT, diag_tile, safe):
    tq, tk = TQ_ATT, TK_ATT
    specs = [pl.BlockSpec((tk, tq), lambda b, s, ti, tj, sf: (0, 0))]
    return _attend(qbT, kb, vbT, safe, (diag_tile,), specs, [], False, "attend_b")


def _merge_kernel(x_ref, ya_ref, yb_ref, g_ref, wga_ref, wgb_ref, bga_ref, bgb_ref,
                  wpa_ref, wpb_ref, wo_ref, o_ref):
    x = x_ref[...]
    h = (x * lax.rsqrt(jnp.mean(x * x, axis=-1, keepdims=True) + EPS) * g_ref[...]).astype(BF16)
    gate_a = jax.nn.sigmoid(_dot(h, wga_ref[...]) + bga_ref[...])
    gate_b = jax.nn.sigmoid(_dot(h, wgb_ref[...]) + bgb_ref[...])
    merged = gate_a * _dot(ya_ref[...], wpa_ref[...]) + gate_b * _dot(yb_ref[...], wpb_ref[...])
    o_ref[...] = x + _dot(merged.astype(BF16), wo_ref[...])


def _merge_out(x2, ya2, yb2, g, wga, wgb, bga, bgb, wpa, wpb, wo):
    R, D = x2.shape
    tm = TM_MERGE
    row = lambda w: pl.BlockSpec((tm, w), lambda i: (i, 0))
    full = lambda a: pl.BlockSpec(a.shape, lambda i: (0,) * a.ndim)
    consts = (g, wga, wgb, bga, bgb, wpa, wpb, wo)
    return pl.pallas_call(
        _merge_kernel,
        grid=(R // tm,),
        in_specs=[row(D), row(A_WIDTH), row(B_WIDTH)] + [full(a) for a in consts],
        out_specs=row(D),
        out_shape=jax.ShapeDtypeStruct((R, D), F32),
        compiler_params=_cparams(("parallel",)),
        name="merge_out",
    )(x2, ya2, yb2, *consts)


def _ffn_kernel(x_ref, xp_ref, g_ref, wuv_ref, wug_ref, cwv_ref, cwg_ref, cbv_ref, cbg_ref, wd_ref,
                o_ref, uv_ref, ug_ref, acc_ref, *, tiles_per_seq):
    i = pl.program_id(0)
    f = pl.program_id(1)
    tm = x_ref.shape[0]
    halo = SUBLANES
    g = g_ref[...]

    def normed(v):
        return (v * lax.rsqrt(jnp.mean(v * v, axis=-1, keepdims=True) + EPS) * g).astype(BF16)

    h = normed(x_ref[...])
    keep = jnp.where(i % tiles_per_seq == 0, 0.0, 1.0)
    hp = normed(xp_ref[...])

    uv_ref[0:halo, :] = _dot(hp, wuv_ref[...]) * keep
    ug_ref[0:halo, :] = _dot(hp, wug_ref[...]) * keep
    uv_ref[halo:halo + tm, :] = _dot(h, wuv_ref[...])
    ug_ref[halo:halo + tm, :] = _dot(h, wug_ref[...])

    def conv(u_ref, cw_ref, cb_ref, r0, nr):
        out = cb_ref[...]
        for t in range(CONV_W):
            lo = halo - (CONV_W - 1) + t + r0
            out = out + cw_ref[t:t + 1, :] * u_ref[lo:lo + nr, :]
        return out

    nr = tm // FFN_GROUPS
    parts = []
    for gi in range(FFN_GROUPS):
        val = conv(uv_ref, cwv_ref, cbv_ref, gi * nr, nr)
        gat = conv(ug_ref, cwg_ref, cbg_ref, gi * nr, nr)
        act = (gat * jax.nn.sigmoid(gat) * val).astype(BF16)
        parts.append(_dot(act, wd_ref[...]))
    part = jnp.concatenate(parts, axis=0)

    @pl.when(f == 0)
    def _():
        acc_ref[...] = part

    @pl.when(f > 0)
    def _():
        acc_ref[...] = acc_ref[...] + part

    @pl.when(f == pl.num_programs(1) - 1)
    def _():
        o_ref[...] = x_ref[...] + acc_ref[...]


def _conv_ffn(x2, S, g, wu, cw, cb, wd):
    R, D = x2.shape
    tm, tf = TM_FFN, TF_FFN
    nf = D_FF // tf
    halo_blocks = tm // SUBLANES
    kern = functools.partial(_ffn_kernel, tiles_per_seq=S // tm)
    return pl.pallas_call(
        kern,
        grid=(R // tm, nf),
        in_specs=[
            pl.BlockSpec((tm, D), lambda i, f: (i, 0)),
            pl.BlockSpec((SUBLANES, D), lambda i, f: (jnp.maximum(i * halo_blocks - 1, 0), 0)),
            pl.BlockSpec((1, D), lambda i, f: (0, 0)),
            pl.BlockSpec((D, tf), lambda i, f: (0, f)),
            pl.BlockSpec((D, tf), lambda i, f: (0, nf + f)),
            pl.BlockSpec((CONV_W, tf), lambda i, f: (0, f)),
            pl.BlockSpec((CONV_W, tf), lambda i, f: (0, nf + f)),
            pl.BlockSpec((1, tf), lambda i, f: (0, f)),
            pl.BlockSpec((1, tf), lambda i, f: (0, nf + f)),
            pl.BlockSpec((tf, D), lambda i, f: (f, 0)),
        ],
        out_specs=pl.BlockSpec((tm, D), lambda i, f: (i, 0)),
        out_shape=jax.ShapeDtypeStruct((R, D), F32),
        scratch_shapes=[pltpu.VMEM((tm + SUBLANES, tf), F32), pltpu.VMEM((tm + SUBLANES, tf), F32),
                        pltpu.VMEM((tm, D), F32)],
        compiler_params=_cparams(("parallel", "arbitrary")),
        name="conv_ffn",
    )(x2, x2, g, wu, wu, cw, cw, cb, cb, wd)


def _t5_bucket(rel):
    nb = REL_BUCKETS // 2
    max_exact = nb // 2
    side = jnp.where(rel > 0, nb, 0)
    n = jnp.abs(rel)
    nf = jnp.maximum(n, 1).astype(F32)
    large = max_exact + (jnp.log(nf / max_exact) / math.log(REL_MAX_DIST / max_exact)
                         * (nb - max_exact)).astype(I32)
    large = jnp.minimum(large, nb - 1)
    return side + jnp.where(n < max_exact, n, large)


def _bias_tiles(rel_bias):
    tq, tk = TQ_ATT, TK_ATT
    assert tq == tk and tk >= REL_MAX_DIST
    kk = jnp.arange(tk, dtype=I32)[:, None]
    qq = jnp.arange(tq, dtype=I32)[None, :]
    bucket = _t5_bucket(jnp.stack([kk - qq, kk - qq - tk])).reshape(1, 2 * tk * tq)
    onehot = (jnp.arange(REL_BUCKETS, dtype=I32)[:, None] == bucket).astype(F32)
    far = rel_bias[_t5_bucket(jnp.asarray(-REL_MAX_DIST, I32))]
    table = ((rel_bias - far[None, :]) * LOG2E).T
    tiles = jnp.dot(table, onehot, precision=lax.Precision.HIGHEST)
    return tiles.reshape(A_HEADS, 2, tk, tq).astype(BF16), jnp.max(jnp.abs(table), axis=1)


def _diag_tile():
    kk = np.arange(TK_ATT)[:, None] // CHUNK
    qq = np.arange(TQ_ATT)[None, :] // CHUNK
    return jnp.asarray(np.where(kk <= qq, 0.0, NEG).astype(np.float32))


def _rope_tables(S):
    half = B_ROPE // 2
    inv = ROPE_BASE ** (-jnp.arange(half, dtype=F32) / half)
    ang = jnp.arange(S, dtype=I32).astype(F32)[:, None] * inv[None, :]
    cos, sin = jnp.cos(ang), jnp.sin(ang)
    ones = jnp.ones((S, B_NOPE), F32)
    zeros_n = jnp.zeros((S, B_NOPE), F32)
    zeros_p = jnp.zeros((S, HP - B_QK), F32)
    cos_t = jnp.concatenate([ones, cos, cos, zeros_p], axis=1)
    sin_t = jnp.concatenate([zeros_n, -sin, sin, zeros_p], axis=1)
    return cos_t, sin_t


def _pad_cols(w, width):
    return jnp.pad(w, ((0, 0), (0, width - w.shape[1])))


def _pad_heads(w, heads, width):
    rows = w.shape[0]
    w = w.reshape(rows, heads, -1)
    return jnp.pad(w, ((0, 0), (0, 0), (0, width - w.shape[2]))).reshape(rows, heads * width)


def _swap_halves(w):
    half = w.shape[-1] // 2
    return jnp.concatenate([w[..., half:], w[..., :half]], axis=-1)


def _layer_weights(l, w_in, b_w_uq, b_w_ukv, b_q_norm, b_k_norm, a_q_norm, a_k_norm, bias_max):
    w = w_in[l]
    o = np.cumsum([0, A_WIDTH, A_WIDTH, A_WIDTH, IDX_HEADS * IDX_DIM, IDX_DIM, IDX_HEADS,
                   B_Q_LORA, B_KV_LORA, B_ROPE, D_MODEL, D_MODEL])
    seg = [w[:, o[t]:o[t + 1]] for t in range(11)]
    zn = jnp.zeros((D_MODEL, B_NOPE), F32)
    kr = seg[8]
    w1 = jnp.concatenate([
        _pad_heads(seg[0], A_HEADS, HP), _pad_heads(seg[1], A_HEADS, HP), seg[2],
        _pad_heads(seg[3], IDX_HEADS, HP),
        _pad_cols(seg[4], LANES), _pad_cols(seg[5], LANES), seg[6], seg[7],
        _pad_cols(jnp.concatenate([zn, kr], axis=1), HP),
        _pad_cols(jnp.concatenate([zn, _swap_halves(kr)], axis=1), HP),
    ], axis=1).astype(BF16)
    assert w1.shape[1] == _C_END

    uq = b_w_uq[l].reshape(B_Q_LORA, B_HEADS, B_QK)
    zq = jnp.zeros((B_Q_LORA, B_HEADS, B_NOPE), F32)
    pq = jnp.zeros((B_Q_LORA, B_HEADS, HP - B_QK), F32)
    wqm = jnp.concatenate([uq, pq], axis=-1).reshape(B_Q_LORA, B_HEADS * HP).astype(BF16)
    wqs = jnp.concatenate([zq, _swap_halves(uq[..., B_NOPE:]), pq], axis=-1)
    wqs = wqs.reshape(B_Q_LORA, B_HEADS * HP).astype(BF16)

    ukv = b_w_ukv[l].reshape(B_KV_LORA, B_HEADS, B_NOPE + B_V)
    pk = jnp.zeros((B_KV_LORA, B_HEADS, HP - B_NOPE), F32)
    wkm = jnp.concatenate([ukv[..., :B_NOPE], pk], axis=-1).reshape(B_KV_LORA, B_HEADS * HP).astype(BF16)
    wv = ukv[..., B_NOPE:].reshape(B_KV_LORA, B_WIDTH).astype(BF16)

    gqa = _pad_cols(a_q_norm[l][None, :], HP)
    gka = _pad_cols(a_k_norm[l][None, :], HP)
    gqb = _pad_cols(b_q_norm[l][None, :], HP)
    gkb = _pad_cols(b_k_norm[l][None, :], HP)

    ka_bound = math.sqrt(A_HEAD_DIM) * jnp.max(jnp.abs(a_k_norm[l]))
    kb_bound = math.sqrt(B_QK) * jnp.max(jnp.abs(b_k_norm[l]))
    bnd = jnp.zeros((SUBLANES, LANES), F32)
    bnd = bnd.at[0, :].set(ka_bound).at[1, :].set(kb_bound).at[2, :A_HEADS].set(bias_max)
    qa_bound = math.sqrt(A_HEAD_DIM) * jnp.max(jnp.abs(a_q_norm[l])) * (A_HEAD_DIM ** -0.5 * LOG2E)
    qb_bound = math.sqrt(B_QK) * jnp.max(jnp.abs(b_q_norm[l])) * (B_QK ** -0.5 * LOG2E)
    range_a = (qa_bound * ka_bound + jnp.max(bias_max)) * BOUND_SLACK + jnp.max(bias_max)
    range_b = qb_bound * kb_bound * BOUND_SLACK
    safe_a = (range_a <= MAX_LOG2_RANGE).astype(I32).reshape(1)
    safe_b = (range_b <= MAX_LOG2_RANGE).astype(I32).reshape(1)
    return (w1, wqm, wqs, wkm, wv, gqa, gka, gqb, gkb, bnd, safe_a, safe_b,
            seg[9].astype(BF16), seg[10].astype(BF16))


def kernel(x, rel_bias, norm_mix, w_in, a_q_norm, a_k_norm, b_cq_norm, b_ckv_norm, b_w_uq, b_w_ukv,
           b_q_norm, b_k_norm, w_proj_a, w_proj_b, b_gate, w_out, norm_ffn, w_up, conv_w, conv_b, w_down):
    B, S, D = x.shape
    assert D == D_MODEL and S % TQ_ATT == 0 and S % TM_FFN == 0
    topk = min(TOPK_MAX, S // 4)
    assert TQ_SEL >= topk
    depth = w_in.shape[0]

    cos_t, sin_t = _rope_tables(S)
    bias_tiles, bias_max = _bias_tiles(rel_bias)
    diag_tile = _diag_tile()

    for l in range(depth):
        (w1, wqm, wqs, wkm, wv, gqa, gka, gqb, gkb, bnd, safe_a, safe_b, wga, wgb) = _layer_weights(
            l, w_in, b_w_uq, b_w_ukv, b_q_norm, b_k_norm, a_q_norm, a_k_norm, bias_max)
        qaT, ka, vaT, qi, kiT, wi, qbT, kb, vbT = _token_prep(
            x, norm_mix[l][None, :], w1, wqm, wqs, wkm, wv, gqa, gka,
            b_cq_norm[l][None, :], b_ckv_norm[l][None, :], gqb, gkb, bnd,
            cos_t, sin_t)
        selbits = _select_topk(qi, wi, kiT, topk)
        y_a = _attend_a(qaT, ka, vaT, selbits, bias_tiles, safe_a)
        y_b = _attend_b(qbT, kb, vbT, diag_tile, safe_b)
        x2 = _merge_out(
            x.reshape(B * S, D), y_a.reshape(B * S, A_WIDTH), y_b.reshape(B * S, B_WIDTH),
            norm_mix[l][None, :], wga, wgb, b_gate[l][None, :D_MODEL], b_gate[l][None, D_MODEL:],
            w_proj_a[l].astype(BF16), w_proj_b[l].astype(BF16), w_out[l].astype(BF16))
        x2 = _conv_ffn(x2, S, norm_ffn[l][None, :], w_up[l].astype(BF16), conv_w[l], conv_b[l][None, :],
                       w_down[l].astype(BF16))
        x = x2.reshape(B, S, D)
    return x
```
